```python
import math
import jax
import jax.numpy as jnp
from jax import lax
import numpy as np

D_MODEL = 1024
BATCH = 4
SEQ = 4096
DEPTH = 1

GRID_W = 64
CTX_LEN = 256
SSD_EXPAND = 2
D_SSD = SSD_EXPAND * D_MODEL
SSD_HEADDIM = 64
SSD_HEADS = D_SSD // SSD_HEADDIM
SSD_GROUPS = 8
HEADS_PER_GROUP = SSD_HEADS // SSD_GROUPS
SSD_STATE = 128
SSD_CHUNK = 128
CONV_K = 5
D_XBC = D_SSD + 2 * SSD_GROUPS * SSD_STATE
NA_HEADDIM = 64
NA_HEADS = D_MODEL // NA_HEADDIM
D_NA = NA_HEADS * NA_HEADDIM
NA_KH = 8
NA_KW = 16
ROPE_THETA = 10000.0
N_GROUPS = 4
EXPERTS_PER_GROUP = 8
N_EXPERTS = N_GROUPS * EXPERTS_PER_GROUP
TOP_K = 2
D_EXPERT = D_MODEL // 2
MOE_BLOCK = 128
N_MOD = 6
EPS = 1e-6
COL_SPLITS = (D_SSD, D_SSD + D_XBC, D_SSD + D_XBC + 2 * SSD_HEADS, D_SSD + D_XBC + 2 * SSD_HEADS + 3 * D_NA)
D_IN = COL_SPLITS[-1] + 2 * D_MODEL

kernel_name = 'hybrid_ssd_natten_hmoe_dit_block'


def rmsnorm(x, w):
    xf = x.astype(jnp.float32)
    y = xf * lax.rsqrt(jnp.mean(xf * xf, axis=-1, keepdims=True) + EPS)
    return (y * w.astype(jnp.float32)).astype(x.dtype)


def dwconv_centred(x, w, b):
    ch = x.shape[-1]
    y = lax.conv_general_dilated(x, w[:, None, :], window_strides=(1,),
                                 padding=((CONV_K // 2, CONV_K // 2),),
                                 dimension_numbers=('NWC', 'WIO', 'NWC'), feature_group_count=ch)
    return y + b


def axial_rope_tables(length):
    t = jnp.arange(length, dtype=jnp.int32)
    row = (t // GRID_W).astype(jnp.float32)
    col = (t % GRID_W).astype(jnp.float32)
    half = SSD_STATE // 2
    inv = ROPE_THETA ** (-jnp.arange(0, half, 2, dtype=jnp.float32) / half)
    ar = row[:, None] * inv
    ac = col[:, None] * inv
    return (jnp.cos(ar), jnp.sin(ar), jnp.cos(ac), jnp.sin(ac))


def _rotate(x, cos, sin):
    x1, x2 = jnp.split(x, 2, axis=-1)
    cos = cos[None, :, None, :].astype(x.dtype)
    sin = sin[None, :, None, :].astype(x.dtype)
    return jnp.concatenate([x1 * cos - x2 * sin, x2 * cos + x1 * sin], axis=-1)


def apply_axial_rope(x, cos_r, sin_r, cos_c, sin_c):
    xr, xc = jnp.split(x, 2, axis=-1)
    return jnp.concatenate([_rotate(xr, cos_r, sin_r), _rotate(xc, cos_c, sin_c)], axis=-1)


def ssd_inputs(xbc, dt_raw, conv_w, conv_b, dt_bias_f, dt_bias_b, rope):
    b, l, _ = xbc.shape
    xbc = jax.nn.silu(dwconv_centred(xbc, conv_w, conv_b))
    xs, bm, cm = jnp.split(xbc, (D_SSD, D_SSD + SSD_GROUPS * SSD_STATE), axis=-1)
    xs = xs.reshape(b, l, SSD_HEADS, SSD_HEADDIM)
    bm = bm.reshape(b, l, SSD_GROUPS, SSD_STATE)
    cm = cm.reshape(b, l, SSD_GROUPS, SSD_STATE)
    if rope is not None:
        bm = apply_axial_rope(bm, *rope)
        cm = apply_axial_rope(cm, *rope)
    dtf = jax.nn.softplus((dt_raw[..., :SSD_HEADS] + dt_bias_f).astype(jnp.float32))
    dtb = jax.nn.softplus((dt_raw[..., SSD_HEADS:] + dt_bias_b).astype(jnp.float32))
    return xs, bm, cm, dtf, dtb


def _ssd_prepare(xs, dt, a, bm):
    b, l, h, p = xs.shape
    nc = l // SSD_CHUNK
    xd = (xs * dt[..., None].astype(xs.dtype)).reshape(b, nc, SSD_CHUNK, SSD_GROUPS, HEADS_PER_GROUP, p)
    a_cum = jnp.cumsum((dt * a).reshape(b, nc, SSD_CHUNK, SSD_GROUPS, HEADS_PER_GROUP), axis=2)
    bc = bm.reshape(b, nc, SSD_CHUNK, SSD_GROUPS, SSD_STATE)
    return xd, a_cum, bc


def _ssd_chunk_states(xd, a_cum, bc, h0):
    nc = xd.shape[1]
    decay_in = jnp.exp(a_cum[:, :, -1:] - a_cum).astype(xd.dtype)
    local = jnp.einsum('bcsgn,bcsgr,bcsgrp->bcgrpn', bc, decay_in, xd)
    states = jnp.concatenate([h0[:, None], local], axis=1)
    chunk_cum = jnp.cumsum(jnp.pad(a_cum[:, :, -1], ((0, 0), (1, 0), (0, 0), (0, 0))), axis=1)
    seg = chunk_cum[:, :, None] - chunk_cum[:, None]
    scan_order = jnp.tril(jnp.ones((nc + 1, nc + 1), dtype=bool))[None, :, :, None, None]
    decay_chunk = jnp.exp(jnp.where(scan_order, seg, -jnp.inf)).astype(xd.dtype)
    return jnp.einsum('bzcgr,bcgrpn->bzgrpn', decay_chunk, states)


def _ssd_outputs(xd, a_cum, bc, cc, start_states):
    b, nc = xd.shape[:2]
    seg = a_cum[:, :, :, None] - a_cum[:, :, None]
    scan_order = jnp.tril(jnp.ones((SSD_CHUNK, SSD_CHUNK), dtype=bool))[None, None, :, :, None, None]
    decay = jnp.exp(jnp.where(scan_order, seg, -jnp.inf)).astype(xd.dtype)
    cb = jnp.einsum('bclgn,bcsgn->bclsg', cc, bc)
    y_diag = jnp.einsum('bclsg,bclsgr,bcsgrp->bclgrp', cb, decay, xd)
    y_off = jnp.einsum('bclgn,bcgrpn,bclgr->bclgrp', cc, start_states, jnp.exp(a_cum).astype(xd.dtype))
    return (y_diag + y_off).reshape(b, nc * SSD_CHUNK, SSD_HEADS, SSD_HEADDIM)


def ssd_scan(xs, dt, a, bm, cm, h0):
    xd, a_cum, bc = _ssd_prepare(xs, dt, a, bm)
    states = _ssd_chunk_states(xd, a_cum, bc, h0)
    cc = cm.reshape(bc.shape)
    return _ssd_outputs(xd, a_cum, bc, cc, states[:, :-1])


def ssd_final_state(xs, dt, a, bm, h0):
    xd, a_cum, bc = _ssd_prepare(xs, dt, a, bm)
    return _ssd_chunk_states(xd, a_cum, bc, h0)[:, -1]


def ssd_bidirectional(xs, bm, cm, dtf, dtb, a_f, a_b, h0f, h0b, d_skip, z, norm_w):
    b, l = xs.shape[:2]
    flip = lambda t: jnp.flip(t, axis=1)
    yf = ssd_scan(xs, dtf, a_f, bm, cm, h0f)
    yb = flip(ssd_scan(flip(xs), flip(dtb), a_b, flip(bm), flip(cm), h0b))
    y = (yf + yb + xs * d_skip[:, None]).reshape(b, l, D_SSD)
    return rmsnorm(y * jax.nn.silu(z), norm_w)


def na_qkv(qkv, q_norm_w, k_norm_w):
    b, l, _ = qkv.shape
    q, k, v = [t.reshape(b, l, NA_HEADS, NA_HEADDIM) for t in jnp.split(qkv, 3, axis=-1)]
    return rmsnorm(q, q_norm_w), rmsnorm(k, k_norm_w), v


def neighbourhood_attention(q, k, v, kc, vc, rpb):
    b, s = q.shape[:2]
    rows = s // GRID_W
    kh = min(NA_KH, rows)
    scale = NA_HEADDIM ** -0.5
    qg = q.reshape(b, rows, GRID_W, NA_HEADS, NA_HEADDIM)
    kg = k.reshape(qg.shape)
    vg = v.reshape(qg.shape)
    col = jnp.arange(GRID_W, dtype=jnp.int32)
    cs = jnp.clip(col - NA_KW // 2, 0, GRID_W - NA_KW)
    col_idx = cs[:, None] + jnp.arange(NA_KW, dtype=jnp.int32)
    dc = col_idx - col[:, None] + (NA_KW - 1)

    def row_block(r):
        rs = jnp.clip(r - kh // 2, 0, rows - kh)
        qb = lax.dynamic_index_in_dim(qg, r, axis=1, keepdims=False)
        kband = lax.dynamic_slice_in_dim(kg, rs, kh, axis=1)
        vband = lax.dynamic_slice_in_dim(vg, rs, kh, axis=1)
        kw = kband[:, :, col_idx]
        vw = vband[:, :, col_idx]
        dr = rs + jnp.arange(kh, dtype=jnp.int32) - r + (NA_KH - 1)
        bias = rpb[:, dr][:, :, dc].transpose(0, 2, 1, 3)
        s_loc = jnp.einsum('bqhd,biqjhd->bhqij', qb, kw) * scale + bias[None]
        s_ctx = jnp.einsum('bqhd,bkhd->bhqk', qb, kc) * scale
        scores = jnp.concatenate([s_loc.reshape(b, NA_HEADS, GRID_W, kh * NA_KW), s_ctx], axis=-1)
        p = jax.nn.softmax(scores.astype(jnp.float32), axis=-1).astype(v.dtype)
        p_loc = p[..., :kh * NA_KW].reshape(b, NA_HEADS, GRID_W, kh, NA_KW)
        p_ctx = p[..., kh * NA_KW:]
        return jnp.einsum('bhqij,biqjhd->bqhd', p_loc, vw) + jnp.einsum('bhqk,bkhd->bqhd', p_ctx, vc)

    out = lax.map(row_block, jnp.arange(rows, dtype=jnp.int32))
    return out.transpose(1, 0, 2, 3, 4).reshape(b, s, D_NA)


def context_attention(qc, kc, vc):
    b, l = qc.shape[:2]
    scores = jnp.einsum('bqhd,bkhd->bhqk', qc, kc) * (NA_HEADDIM ** -0.5)
    p = jax.nn.softmax(scores.astype(jnp.float32), axis=-1).astype(vc.dtype)
    return jnp.einsum('bhqk,bkhd->bqhd', p, vc).reshape(b, l, D_NA)


def merge_branches(y_ssd, y_na, gates, w_br_ssd, w_br_na, w_out):
    g_ssd, g_na = jnp.split(gates, 2, axis=-1)
    merged = jax.nn.sigmoid(g_ssd) * (y_ssd @ w_br_ssd) + jax.nn.sigmoid(g_na) * (y_na @ w_br_na)
    return merged @ w_out


def hierarchical_moe(h, w_grp, b_grp, w_rt, b_rt, w1, w3, w2):
    t, d = h.shape
    grp_logits = (h @ w_grp + b_grp).astype(jnp.float32)
    grp = jnp.argmax(grp_logits, axis=-1)
    p_grp = jnp.take_along_axis(jax.nn.softmax(grp_logits, axis=-1), grp[:, None], axis=1)[:, 0]
    exp_logits = (h @ w_rt + b_rt).astype(jnp.float32).reshape(t, N_GROUPS, EXPERTS_PER_GROUP)
    in_grp = jnp.take_along_axis(exp_logits, grp[:, None, None], axis=1)[:, 0]
    top_v, top_i = lax.top_k(in_grp, TOP_K)
    gate = (jax.nn.softmax(top_v, axis=-1) * p_grp[:, None]).astype(h.dtype)
    expert = grp[:, None] * EXPERTS_PER_GROUP + top_i
    n_assign = t * TOP_K
    a_exp = expert.reshape(-1)
    a_tok = jnp.arange(n_assign, dtype=jnp.int32) // TOP_K
    a_gate = gate.reshape(-1)
    order = jnp.argsort(a_exp)
    se = a_exp[order]
    counts = jnp.bincount(a_exp, length=N_EXPERTS)
    padded = (counts + MOE_BLOCK - 1) // MOE_BLOCK * MOE_BLOCK
    start = jnp.cumsum(counts) - counts
    pend = jnp.cumsum(padded)
    pstart = pend - padded
    dest = pstart[se] + jnp.arange(n_assign, dtype=jnp.int32) - start[se]
    n_blk = (n_assign + N_EXPERTS * (MOE_BLOCK - 1) + MOE_BLOCK - 1) // MOE_BLOCK
    buf_tok = jnp.full((n_blk * MOE_BLOCK,), t, dtype=jnp.int32).at[dest].set(a_tok[order])
    buf_gate = jnp.zeros((n_blk * MOE_BLOCK,), h.dtype).at[dest].set(a_gate[order])
    blk_exp = jnp.minimum(jnp.searchsorted(pend, jnp.arange(n_blk, dtype=jnp.int32) * MOE_BLOCK, side='right'), N_EXPERTS - 1)
    h_pad = jnp.concatenate([h, jnp.zeros((1, d), h.dtype)], axis=0)
    xb = h_pad[buf_tok].reshape(n_blk, MOE_BLOCK, d)

    def expert_block(args):
        xblk, e = args
        return (jax.nn.silu(xblk @ w1[e]) * (xblk @ w3[e])) @ w2[e]

    yb = lax.map(expert_block, (xb, blk_exp)).reshape(-1, d)
    return jax.ops.segment_sum(yb * buf_gate[:, None], buf_tok, num_segments=t + 1)[:t]


def hybrid_layer(x, ctx, c, c_ctx, w_mod, b_mod, norm1_w, w_in, conv_w, conv_b, a_log_f, a_log_b,
                 dt_bias_f, dt_bias_b, d_skip, ssd_norm_w, q_norm_w, k_norm_w, rpb, w_br_ssd, w_br_na,
                 w_out, norm2_w, w_grp, b_grp, w_rt, b_rt, w1, w3, w2, update_ctx):
    b, s, d = x.shape
    mod = jax.nn.silu(c) @ w_mod + b_mod
    sh1, sc1, g1, sh2, sc2, g2 = [m[:, None] for m in jnp.split(mod, N_MOD, axis=-1)]
    modc = jax.nn.silu(c_ctx) @ w_mod + b_mod
    sh1c, sc1c, g1c, sh2c, sc2c, g2c = jnp.split(modc, N_MOD, axis=-1)

    h = rmsnorm(x, norm1_w) * (1.0 + sc1) + sh1
    hc = rmsnorm(ctx, norm1_w) * (1.0 + sc1c) + sh1c
    z, xbc, dtr, qkv, gates = jnp.split(h @ w_in, COL_SPLITS, axis=-1)
    zc, xbcc, dtrc, qkvc, gatesc = jnp.split(hc @ w_in, COL_SPLITS, axis=-1)

    a_f = -jnp.exp(a_log_f.astype(jnp.float32))
    a_b = -jnp.exp(a_log_b.astype(jnp.float32))
    xs, bm, cm, dtf, dtb = ssd_inputs(xbc, dtr, conv_w, conv_b, dt_bias_f, dt_bias_b, axial_rope_tables(s))
    xsc, bmc, cmc, dtfc, dtbc = ssd_inputs(xbcc, dtrc, conv_w, conv_b, dt_bias_f, dt_bias_b, None)
    h0 = jnp.zeros((b, SSD_GROUPS, HEADS_PER_GROUP, SSD_HEADDIM, SSD_STATE), x.dtype)
    flip = lambda t_: jnp.flip(t_, axis=1)
    h_ctx_f = ssd_final_state(xsc, dtfc, a_f, bmc, h0)
    h_ctx_b = ssd_final_state(flip(xsc), flip(dtbc), a_b, flip(bmc), h0)
    y_ssd = ssd_bidirectional(xs, bm, cm, dtf, dtb, a_f, a_b, h_ctx_f, h_ctx_b, d_skip, z, ssd_norm_w)

    q, k, v = na_qkv(qkv, q_norm_w, k_norm_w)
    qc, kc, vc = na_qkv(qkvc, q_norm_w, k_norm_w)
    y_na = neighbourhood_attention(q, k, v, kc, vc, rpb)

    x = x + g1 * merge_branches(y_ssd, y_na, gates, w_br_ssd, w_br_na, w_out)
    if update_ctx:
        yc_ssd = ssd_bidirectional(xsc, bmc, cmc, dtfc, dtbc, a_f, a_b, h0, h0, d_skip, zc, ssd_norm_w)
        yc_na = context_attention(qc, kc, vc)
        ctx = ctx + g1c * merge_branches(yc_ssd, yc_na, gatesc, w_br_ssd, w_br_na, w_out)

    h2 = rmsnorm(x, norm2_w) * (1.0 + sc2) + sh2
    x = x + g2 * hierarchical_moe(h2.reshape(-1, d), w_grp, b_grp, w_rt, b_rt, w1, w3, w2).reshape(x.shape)
    if update_ctx:
        h2c = rmsnorm(ctx, norm2_w) * (1.0 + sc2c) + sh2c
        ctx = ctx + g2c * hierarchical_moe(h2c.reshape(-1, d), w_grp, b_grp, w_rt, b_rt, w1, w3, w2).reshape(ctx.shape)
    return x, ctx


def setup_inputs(seed: int = 0) -> dict:
    key = jax.random.key(seed)
    ks = jax.random.split(key, 32)
    f32 = jnp.float32
    L = DEPTH

    def nrm(k, shape, scale):
        return jax.random.normal(k, shape, f32) * scale

    def gain(k, shape, scale=0.02):
        return 1.0 + scale * jax.random.normal(k, shape, f32)

    dt_f = jnp.exp(jax.random.uniform(ks[9], (L, SSD_HEADS), f32, math.log(1e-3), math.log(1e-1)))
    dt_b = jnp.exp(jax.random.uniform(ks[10], (L, SSD_HEADS), f32, math.log(1e-3), math.log(1e-1)))
    return {
        'x': nrm(ks[0], (BATCH, SEQ, D_MODEL), 1.0),
        'c': nrm(ks[1], (BATCH, D_MODEL), 1.0),
        'ctx': nrm(ks[2], (BATCH, CTX_LEN, D_MODEL), 1.0),
        'c_ctx': nrm(ks[3], (D_MODEL,), 1.0),
        'w_mod': nrm(ks[4], (L, D_MODEL, N_MOD * D_MODEL), 0.5 * D_MODEL ** -0.5),
        'b_mod': nrm(ks[5], (L, N_MOD * D_MODEL), 0.02),
        'norm1_w': gain(ks[6], (L, D_MODEL)),
        'w_in': nrm(ks[7], (L, D_MODEL, D_IN), D_MODEL ** -0.5),
        'conv_w': nrm(ks[8], (L, CONV_K, D_XBC), CONV_K ** -0.5),
        'conv_b': nrm(ks[11], (L, D_XBC), 0.02),
        'a_log_f': jnp.log(jax.random.uniform(ks[12], (L, SSD_HEADS), f32, 1.0, 16.0)),
        'a_log_b': jnp.log(jax.random.uniform(ks[13], (L, SSD_HEADS), f32, 1.0, 16.0)),
        'dt_bias_f': dt_f + jnp.log(-jnp.expm1(-dt_f)),
        'dt_bias_b': dt_b + jnp.log(-jnp.expm1(-dt_b)),
        'd_skip': gain(ks[14], (L, SSD_HEADS), 0.1),
        'ssd_norm_w': gain(ks[15], (L, D_SSD)),
        'q_norm_w': gain(ks[16], (L, NA_HEADDIM)),
        'k_norm_w': gain(ks[17], (L, NA_HEADDIM)),
        'rpb': nrm(ks[18], (L, NA_HEADS, 2 * NA_KH - 1, 2 * NA_KW - 1), 0.1),
        'w_br_ssd': nrm(ks[19], (L, D_SSD, D_MODEL), D_SSD ** -0.5),
        'w_br_na': nrm(ks[20], (L, D_NA, D_MODEL), D_NA ** -0.5),
        'w_out': nrm(ks[21], (L, D_MODEL, D_MODEL), D_MODEL ** -0.5),
        'norm2_w': gain(ks[22], (L, D_MODEL)),
        'w_grp': nrm(ks[23], (L, D_MODEL, N_GROUPS), D_MODEL ** -0.5),
        'b_grp': nrm(ks[24], (L, N_GROUPS), 0.01),
        'w_rt': nrm(ks[25], (L, D_MODEL, N_EXPERTS), D_MODEL ** -0.5),
        'b_rt': nrm(ks[26], (L, N_EXPERTS), 0.01),
        'w1': nrm(ks[27], (L, N_EXPERTS, D_MODEL, D_EXPERT), D_MODEL ** -0.5),
        'w3': nrm(ks[28], (L, N_EXPERTS, D_MODEL, D_EXPERT), D_MODEL ** -0.5),
        'w2': nrm(ks[29], (L, N_EXPERTS, D_EXPERT, D_MODEL), D_EXPERT ** -0.5),
    }


def reference(x, c, ctx, c_ctx, w_mod, b_mod, norm1_w, w_in, conv_w, conv_b, a_log_f, a_log_b,
              dt_bias_f, dt_bias_b, d_skip, ssd_norm_w, q_norm_w, k_norm_w, rpb, w_br_ssd, w_br_na,
              w_out, norm2_w, w_grp, b_grp, w_rt, b_rt, w1, w3, w2):
    for l in range(DEPTH):
        x, ctx = hybrid_layer(x, ctx, c, c_ctx, w_mod[l], b_mod[l], norm1_w[l], w_in[l], conv_w[l], conv_b[l],
                              a_log_f[l], a_log_b[l], dt_bias_f[l], dt_bias_b[l], d_skip[l], ssd_norm_w[l],
                              q_norm_w[l], k_norm_w[l], rpb[l], w_br_ssd[l], w_br_na[l], w_out[l], norm2_w[l],
                              w_grp[l], b_grp[l], w_rt[l], b_rt[l], w1[l], w3[l], w2[l],
                              update_ctx=(l < DEPTH - 1))
    return x
```

```python
import functools
import math

import jax
import jax.numpy as jnp
from jax import lax
from jax.experimental import pallas as pl
from jax.experimental.pallas import tpu as pltpu

F32 = jnp.float32
BF16 = jnp.bfloat16
I32 = jnp.int32
U32 = jnp.uint32

D_MODEL = 1024
GRID_W = 64
D_SSD = 2048
SSD_HEADDIM = 64
SSD_HEADS = 32
SSD_GROUPS = 8
HEADS_PER_GROUP = 4
SSD_STATE = 128
SSD_CHUNK = 128
CONV_K = 5
D_BC = SSD_GROUPS * SSD_STATE
D_XBC = D_SSD + 2 * D_BC
NA_HEADDIM = 64
NA_HEADS = 16
D_NA = 1024
NA_KH = 8
NA_KW = 16
ROPE_THETA = 10000.0
N_GROUPS = 4
EXPERTS_PER_GROUP = 8
N_EXPERTS = 32
D_EXPERT = 512
N_MOD = 6
EPS = 1e-6
NEG_BIG = -1e30

COLS_LAT = D_XBC + D_SSD + 2 * D_MODEL + 3 * D_NA
COLS_CTX = D_XBC + 2 * D_NA
LANE = 128

V7X_VMEM_BYTES = 64 * 1024 * 1024


def _cparams(sem, vmem_mb):
    return pltpu.CompilerParams(dimension_semantics=sem, vmem_limit_bytes=vmem_mb * 1024 * 1024)


def _sigmoid(x):
    return 1.0 / (1.0 + jnp.exp(-x))


def _split2(x):
    hi = x.astype(BF16)
    lo = (x - hi.astype(F32)).astype(BF16)
    return hi, lo


def _split3(x):
    hi = x.astype(BF16)
    r = x - hi.astype(F32)
    mid = r.astype(BF16)
    lo = (r - mid.astype(F32)).astype(BF16)
    return hi, mid, lo


def _dot(a, b):
    return jnp.dot(a, b, preferred_element_type=F32)


def _dot_nt(a, b):
    return lax.dot_general(a, b, (((1,), (1,)), ((), ())), preferred_element_type=F32)


def _dot_tn(a, b):
    return lax.dot_general(a, b, (((0,), (0,)), ((), ())), preferred_element_type=F32)


def _mod_kernel(c_ref, w_ref, b_ref, o_ref):
    c = c_ref[...]
    a = (c * _sigmoid(c)).astype(BF16)
    o_ref[...] = _dot(a, w_ref[...].astype(BF16)) + b_ref[...]


def _modulation(cin, w_mod, b_mod):
    n = w_mod.shape[1]
    tn = 1536
    return pl.pallas_call(
        _mod_kernel,
        grid=(n // tn,),
        in_specs=[pl.BlockSpec((8, D_MODEL), lambda j: (0, 0)),
                  pl.BlockSpec((D_MODEL, tn), lambda j: (0, j)),
                  pl.BlockSpec((1, tn), lambda j: (0, j))],
        out_specs=pl.BlockSpec((8, tn), lambda j: (0, j)),
        out_shape=jax.ShapeDtypeStruct((8, n), F32),
        compiler_params=_cparams(("arbitrary",), 40),
        name="modulation",
    )(cin, w_mod, b_mod.reshape(1, n))


def _inproj_kernel(x_ref, sh_ref, sc_ref, nw_ref, w_ref, wdt_ref, o_ref, dt_ref, h_scr, *, tm):
    j = pl.program_id(1)

    @pl.when(j == 0)
    def _():
        scale = 1.0 + sc_ref[0]
        shift = sh_ref[0]
        nw = nw_ref[...]

        def body(r, carry):
            rows = pl.ds(pl.multiple_of(r * 128, 128), 128)
            x = x_ref[rows, :]
            ms = jnp.mean(x * x, axis=-1, keepdims=True)
            h = x * lax.rsqrt(ms + EPS) * nw * scale + shift
            h_scr[rows, :] = h.astype(BF16)
            return carry

        lax.fori_loop(0, tm // 128, body, 0)
        dt_ref[...] = _dot(h_scr[...], wdt_ref[...])

    o_ref[...] = _dot(h_scr[...], w_ref[...]).astype(BF16)


def _inproj(x2, mod3, mod_row_fn, norm_w, w, wdt, tm, tn):
    t = x2.shape[0]
    n = w.shape[1]
    kern = functools.partial(_inproj_kernel, tm=tm)
    return pl.pallas_call(
        kern,
        grid=(t // tm, n // tn),
        in_specs=[pl.BlockSpec((tm, D_MODEL), lambda i, j: (i, 0)),
                  pl.BlockSpec((1, 1, D_MODEL), lambda i, j: (mod_row_fn(i) * N_MOD + 0, 0, 0)),
                  pl.BlockSpec((1, 1, D_MODEL), lambda i, j: (mod_row_fn(i) * N_MOD + 1, 0, 0)),
                  pl.BlockSpec((1, D_MODEL), lambda i, j: (0, 0)),
                  pl.BlockSpec((D_MODEL, tn), lambda i, j: (0, j)),
                  pl.BlockSpec((D_MODEL, LANE), lambda i, j: (0, 0))],
        out_specs=[pl.BlockSpec((tm, tn), lambda i, j: (i, j)),
                   pl.BlockSpec((tm, LANE), lambda i, j: (i, 0))],
        out_shape=[jax.ShapeDtypeStruct((t, n), BF16),
                   jax.ShapeDtypeStruct((t, LANE), F32)],
        scratch_shapes=[pltpu.VMEM((tm, D_MODEL), BF16)],
        compiler_params=_cparams(("arbitrary", "arbitrary"), 48),
        name="inproj",
    )(x2, mod3, mod3, norm_w, w, wdt)


def _ssd_prep_kernel(prev_ref, cur_ref, next_ref, dt_ref, cw_ref, cb_ref, dtbias_ref, cos_ref, sin_ref,
                     xs_ref, bc_ref, dto_ref, ext_scr, *, tl, tiles_per_seq):
    i = pl.program_id(0)
    first = (i % tiles_per_seq) == 0
    last = (i % tiles_per_seq) == tiles_per_seq - 1
    ext_scr[0:16, :] = jnp.where(first, 0.0, prev_ref[...].astype(F32))
    ext_scr[16 + tl:32 + tl, :] = jnp.where(last, 0.0, next_ref[...].astype(F32))
    rc = 64
    for r in range(tl // rc):
        ext_scr[16 + r * rc:16 + (r + 1) * rc, :] = cur_ref[r * rc:(r + 1) * rc, :].astype(F32)

    lane = lax.broadcasted_iota(I32, (rc, LANE), 1)
    lo_half = ((lane // 32) % 2) == 0
    cw_all = cw_ref[...]
    cwid = 512
    for c in range(D_XBC // cwid):
        cs = slice(c * cwid, (c + 1) * cwid)
        w = cw_all[:, cs]
        b = cb_ref[:, cs]
        for r in range(tl // rc):
            acc = b + ext_scr[14 + r * rc:14 + (r + 1) * rc, cs] * w[0:1, :]
            for k in range(1, CONV_K):
                acc = acc + ext_scr[14 + k + r * rc:14 + k + (r + 1) * rc, cs] * w[k:k + 1, :]
            y = acc * _sigmoid(acc)
            rows = slice(r * rc, (r + 1) * rc)
            if c * cwid < D_SSD:
                xs_ref[rows, cs] = y.astype(BF16)
            else:
                cosv = cos_ref[rows, :]
                sinv = sin_ref[rows, :]
                for g in range(cwid // LANE):
                    yg = y[:, g * LANE:(g + 1) * LANE]
                    partner = jnp.where(lo_half, pltpu.roll(yg, 96, 1), pltpu.roll(yg, 32, 1))
                    o = yg * cosv + partner * sinv
                    c0 = c * cwid - D_SSD + g * LANE
                    bc_ref[rows, c0:c0 + LANE] = o.astype(BF16)

    v = dt_ref[...] + dtbias_ref[...]
    dto_ref[...] = jnp.maximum(v, 0.0) + jnp.log(1.0 + jnp.exp(-jnp.abs(v)))


def _ssd_prep(big, dt_raw, conv_w, conv_b, dtbias, cos_t, sin_t, seq_len, tl):
    t = big.shape[0]
    tps = seq_len // tl
    hb = tl // 16
    nhalo = t // 16
    kern = functools.partial(_ssd_prep_kernel, tl=tl, tiles_per_seq=tps)
    return pl.pallas_call(
        kern,
        grid=(t // tl,),
        in_specs=[pl.BlockSpec((16, D_XBC), lambda i: (jnp.maximum(i * hb - 1, 0), 0)),
                  pl.BlockSpec((tl, D_XBC), lambda i: (i, 0)),
                  pl.BlockSpec((16, D_XBC), lambda i: (jnp.minimum((i + 1) * hb, nhalo - 1), 0)),
                  pl.BlockSpec((tl, LANE), lambda i: (i, 0)),
                  pl.BlockSpec((CONV_K, D_XBC), lambda i: (0, 0)),
                  pl.BlockSpec((1, D_XBC), lambda i: (0, 0)),
                  pl.BlockSpec((1, LANE), lambda i: (0, 0)),
                  pl.BlockSpec((tl, LANE), lambda i: (i % tps, 0)),
                  pl.BlockSpec((tl, LANE), lambda i: (i % tps, 0))],
        out_specs=[pl.BlockSpec((tl, D_SSD), lambda i: (i, 0)),
                   pl.BlockSpec((tl, 2 * D_BC), lambda i: (i, 0)),
                   pl.BlockSpec((tl, LANE), lambda i: (i, 0))],
        out_shape=[jax.ShapeDtypeStruct((t, D_SSD), BF16),
                   jax.ShapeDtypeStruct((t, 2 * D_BC), BF16),
                   jax.ShapeDtypeStruct((t, LANE), F32)],
        scratch_shapes=[pltpu.VMEM((tl + 32, D_XBC), F32)],
        compiler_params=_cparams(("arbitrary",), 40),
        name="ssd_prep",
    )(big, big, big, dt_raw, conv_w, conv_b, dtbias, cos_t, sin_t)


def _ssd_scan_kernel(xs_ref, bc_ref, dt_ref, alog_ref, dskip_ref, tri_ref, ee_ref, h0_ref,
                     y_ref, hfin_ref, h_scr, *, rev, nc):
    k = pl.program_id(1)
    L = SSD_CHUNK

    @pl.when(k == 0)
    def _():
        h_scr[...] = h0_ref[0]

    col = SSD_HEADS if rev else 0
    edge = 0 if rev else L - 1
    a_row = -jnp.exp(alog_ref[...])
    dt = dt_ref[...]
    trib = tri_ref[...]
    tri = trib > 0.5
    d_hi, d_mid, d_lo = _split3(dt * a_row)
    cum = _dot(trib, d_hi) + _dot(trib, d_mid) + _dot(trib, d_lo)
    tot = cum[edge:edge + 1, :]
    cum_t = cum.T
    dt_t = dt.T

    ee = ee_ref[...]
    o_hi, o_lo = _split2(jnp.exp(cum))
    oscale = _dot(o_hi, ee) + _dot(o_lo, ee)
    w_hi, w_lo = _split2(jnp.exp(tot - cum) * dt)
    wscale = _dot(w_hi, ee) + _dot(w_lo, ee)
    etot = oscale[edge:edge + 1, :]

    gw = HEADS_PER_GROUP * SSD_HEADDIM
    lane_head = lax.broadcasted_iota(I32, (L, gw), 1) // SSD_HEADDIM
    for g in range(SSD_GROUPS):
        gs = slice(g * gw, (g + 1) * gw)
        b_g = bc_ref[:, g * SSD_STATE:(g + 1) * SSD_STATE]
        c_g = bc_ref[:, D_BC + g * SSD_STATE:D_BC + (g + 1) * SSD_STATE]
        xs_g = xs_ref[:, gs]
        cb = _dot_nt(c_g, b_g)
        ms = []
        for r in range(HEADS_PER_GROUP):
            hh = col + g * HEADS_PER_GROUP + r
            seg = cum[:, hh:hh + 1] - cum_t[hh:hh + 1, :]
            lm = jnp.exp(jnp.where(tri, seg, NEG_BIG)) * dt_t[hh:hh + 1, :]
            ms.append((cb * lm).astype(BF16))
        mcat = jnp.concatenate(ms, axis=1)
        zero = jnp.zeros_like(xs_g)
        bd = jnp.concatenate([jnp.where(lane_head == r, xs_g, zero) for r in range(HEADS_PER_GROUP)], axis=0)
        y_diag = _dot(mcat, bd)
        h_g = h_scr[:, gs]
        y_off = _dot(c_g, h_g.astype(BF16))
        y = y_diag + oscale[:, gs] * y_off
        xs_f = xs_g.astype(F32)
        if not rev:
            y = y + xs_f * dskip_ref[:, gs]
        y_ref[:, gs] = y.astype(BF16)
        xw = (xs_f * wscale[:, gs]).astype(BF16)
        h_scr[:, gs] = etot[:, gs] * h_g + _dot_tn(b_g, xw)

    @pl.when(k == nc - 1)
    def _():
        hfin_ref[0] = h_scr[...]


def _ssd_scan(xs, bc, dt, alog_row, dskip_row, tri, ee, h0, nb, rev):
    t = xs.shape[0]
    nc = t // nb // SSD_CHUNK
    kern = functools.partial(_ssd_scan_kernel, rev=rev, nc=nc)

    def rowmap(b, k):
        c = (nc - 1 - k) if rev else k
        return (b * nc + c, 0)

    const2 = lambda b, k: (0, 0)
    return pl.pallas_call(
        kern,
        grid=(nb, nc),
        in_specs=[pl.BlockSpec((SSD_CHUNK, D_SSD), rowmap),
                  pl.BlockSpec((SSD_CHUNK, 2 * D_BC), rowmap),
                  pl.BlockSpec((SSD_CHUNK, LANE), rowmap),
                  pl.BlockSpec((1, LANE), const2),
                  pl.BlockSpec((1, D_SSD), const2),
                  pl.BlockSpec((SSD_CHUNK, SSD_CHUNK), const2),
                  pl.BlockSpec((LANE, D_SSD), const2),
                  pl.BlockSpec((1, SSD_STATE, D_SSD), lambda b, k: (b, 0, 0))],
        out_specs=[pl.BlockSpec((SSD_CHUNK, D_SSD), rowmap),
                   pl.BlockSpec((1, SSD_STATE, D_SSD), lambda b, k: (b, 0, 0))],
        out_shape=[jax.ShapeDtypeStruct((t, D_SSD), BF16),
                   jax.ShapeDtypeStruct((nb, SSD_STATE, D_SSD), F32)],
        scratch_shapes=[pltpu.VMEM((SSD_STATE, D_SSD), F32)],
        compiler_params=_cparams(("arbitrary", "arbitrary"), 40),
        name="ssd_scan_bwd" if rev else "ssd_scan_fwd",
    )(xs, bc, dt, alog_row, dskip_row, tri, ee, h0)


def _headnorm_cols(src_ref, w_ref, g, dst_ref, scale):
    for c in range(D_NA // LANE):
        cs = slice(c * LANE, (c + 1) * LANE)
        x = src_ref[:, cs].astype(F32)
        hi, lo = _split2(x * x)
        ms = _dot(hi, g) + _dot(lo, g)
        y = x * lax.rsqrt(ms + EPS) * w_ref[:, cs]
        if scale is not None:
            y = y * scale
        dst_ref[:, cs] = y.astype(BF16)


def _na_prep_qk_kernel(q_ref, k_ref, qw_ref, kw_ref, g_ref, qo_ref, ko_ref):
    g = g_ref[...]
    _headnorm_cols(q_ref, qw_ref, g, qo_ref, NA_HEADDIM ** -0.5)
    _headnorm_cols(k_ref, kw_ref, g, ko_ref, None)


def _na_prep_k_kernel(k_ref, kw_ref, g_ref, ko_ref):
    _headnorm_cols(k_ref, kw_ref, g_ref[...], ko_ref, None)


def _na_prep(big, qcol, kcol, qw, kw, gmat, tm):
    t = big.shape[0]
    blk = lambda cidx: pl.BlockSpec((tm, D_NA), lambda i: (i, cidx))
    row = pl.BlockSpec((1, D_NA), lambda i: (0, 0))
    gspec = pl.BlockSpec((LANE, LANE), lambda i: (0, 0))
    out = pl.BlockSpec((tm, D_NA), lambda i: (i, 0))
    if qcol is None:
        return pl.pallas_call(
            _na_prep_k_kernel, grid=(t // tm,),
            in_specs=[blk(kcol), row, gspec], out_specs=out,
            out_shape=jax.ShapeDtypeStruct((t, D_NA), BF16),
            compiler_params=_cparams(("arbitrary",), 32), name="na_prep_ctx",
        )(big, kw, gmat)
    return pl.pallas_call(
        _na_prep_qk_kernel, grid=(t // tm,),
        in_specs=[blk(qcol), blk(kcol), row, row, gspec], out_specs=[out, out],
        out_shape=[jax.ShapeDtypeStruct((t, D_NA), BF16)] * 2,
        compiler_params=_cparams(("arbitrary",), 32), name="na_prep",
    )(big, big, qw, kw, gmat)


def _na_kernel(q_ref, k_ref, v_ref, kc_ref, vc_ref, bias_ref, o_ref, *, rows):
    r = pl.program_id(1)
    rs = jnp.clip(r - NA_KH // 2, 0, rows - NA_KH)
    start = pl.multiple_of(rs * GRID_W, GRID_W)
    nk = NA_KH * GRID_W
    lane = lax.broadcasted_iota(I32, (GRID_W, LANE), 1)
    first_head = lane < NA_HEADDIM
    for j in range(NA_HEADS // 2):
        cs = slice(j * LANE, (j + 1) * LANE)
        qp = q_ref[:, cs]
        zero = jnp.zeros_like(qp)
        qs = jnp.concatenate([jnp.where(first_head, qp, zero), jnp.where(first_head, zero, qp)], axis=0)
        kb = k_ref[pl.ds(start, nk), cs]
        vb = v_ref[pl.ds(start, nk), cs]
        s_loc = _dot_nt(qs, kb) + bias_ref[0, j * LANE:(j + 1) * LANE, :]
        s_ctx = _dot_nt(qs, kc_ref[:, cs])
        m = jnp.maximum(jnp.max(s_loc, axis=-1, keepdims=True), jnp.max(s_ctx, axis=-1, keepdims=True))
        p_loc = jnp.exp(s_loc - m)
        p_ctx = jnp.exp(s_ctx - m)
        den = jnp.sum(p_loc, axis=-1, keepdims=True) + jnp.sum(p_ctx, axis=-1, keepdims=True)
        o = _dot(p_loc.astype(BF16), vb) + _dot(p_ctx.astype(BF16), vc_ref[:, cs])
        o = o / den
        o_ref[:, cs] = jnp.where(first_head, o[:GRID_W], o[GRID_W:]).astype(BF16)


def _neigh_attention(qn, kn, big, kcn, bigc, bias_tab, nb, seq_len, ctx_len, vcol, vccol):
    t = qn.shape[0]
    rows = seq_len // GRID_W
    kern = functools.partial(_na_kernel, rows=rows)

    def pat(b, r):
        return (r - jnp.clip(r - NA_KH // 2, 0, rows - NA_KH), 0, 0)

    return pl.pallas_call(
        kern,
        grid=(nb, rows),
        in_specs=[pl.BlockSpec((GRID_W, D_NA), lambda b, r: (b * rows + r, 0)),
                  pl.BlockSpec((seq_len, D_NA), lambda b, r: (b, 0)),
                  pl.BlockSpec((seq_len, D_NA), lambda b, r: (b, vcol)),
                  pl.BlockSpec((ctx_len, D_NA), lambda b, r: (b, 0)),
                  pl.BlockSpec((ctx_len, D_NA), lambda b, r: (b, vccol)),
                  pl.BlockSpec((1, NA_HEADS * GRID_W, NA_KH * GRID_W), pat)],
        out_specs=pl.BlockSpec((GRID_W, D_NA), lambda b, r: (b * rows + r, 0)),
        out_shape=jax.ShapeDtypeStruct((t, D_NA), BF16),
        compiler_params=_cparams(("arbitrary", "arbitrary"), 56),
        name="neigh_attention",
    )(qn, kn, big, kcn, bigc, bias_tab)


def _na_bias_table(rpb):
    p = jnp.arange(NA_KH)[:, None]
    i = jnp.arange(NA_KH)[None, :]
    dr = i - p + (NA_KH - 1)
    c = jnp.arange(GRID_W)[:, None]
    kc = jnp.arange(GRID_W)[None, :]
    cs = jnp.clip(c - NA_KW // 2, 0, GRID_W - NA_KW)
    valid = (kc >= cs) & (kc < cs + NA_KW)
    dc = jnp.clip(kc - c + (NA_KW - 1), 0, 2 * NA_KW - 2)
    t = rpb[:, dr[:, :, None, None], dc[None, None, :, :]]
    t = jnp.where(valid[None, None, None], t, NEG_BIG)
    t = t.transpose(1, 0, 3, 2, 4)
    return t.reshape(NA_KH, NA_HEADS * GRID_W, NA_KH * GRID_W)


def _merge_kernel(yf_ref, yb_ref, z_ref, gt_ref, yna_ref, x_ref, g1_ref, sh2_ref, sc2_ref, snw_ref, n2w_ref,
                  wbs_ref, wbn_ref, wo_ref, wrh_ref, wrl_ref, br_ref, x1_ref, h2p_ref, lg_ref):
    z = z_ref[...].astype(F32)
    y = (yf_ref[...].astype(F32) + yb_ref[...].astype(F32)) * (z * _sigmoid(z))
    ms = jnp.mean(y * y, axis=-1, keepdims=True)
    yn = (y * lax.rsqrt(ms + EPS) * snw_ref[...]).astype(BF16)
    a = _dot(yn, wbs_ref[...])
    b = _dot(yna_ref[...], wbn_ref[...])
    g_ssd = gt_ref[:, :D_MODEL].astype(F32)
    g_na = gt_ref[:, D_MODEL:].astype(F32)
    merged = (_sigmoid(g_ssd) * a + _sigmoid(g_na) * b).astype(BF16)
    x1 = x_ref[...] + g1_ref[0] * _dot(merged, wo_ref[...])
    x1_ref[...] = x1
    ms2 = jnp.mean(x1 * x1, axis=-1, keepdims=True)
    h2 = x1 * lax.rsqrt(ms2 + EPS) * n2w_ref[...] * (1.0 + sc2_ref[0]) + sh2_ref[0]
    h_hi, h_lo = _split2(h2)
    lg_ref[...] = (_dot(h_hi, wrh_ref[...]) + _dot(h_lo, wrh_ref[...]) + _dot(h_hi, wrl_ref[...])) + br_ref[...]
    bits = pltpu.bitcast(h_hi.astype(F32), U32)
    half = D_MODEL // 2
    h2p_ref[...] = (bits[:, :half] >> 16) | bits[:, half:]


def _merge(yf, yb, big, yna, x2, mod3, snw, n2w, wbs, wbn, wo, wrh, wrl, br, seq_len, tm, zcol, gcol):
    t = x2.shape[0]
    tiles_per_seq = seq_len // tm
    modspec = lambda kidx: pl.BlockSpec((1, 1, D_MODEL), lambda i: ((i // tiles_per_seq) * N_MOD + kidx, 0, 0))
    full = lambda shp: pl.BlockSpec(shp, lambda i: (0,) * len(shp))
    return pl.pallas_call(
        _merge_kernel,
        grid=(t // tm,),
        in_specs=[pl.BlockSpec((tm, D_SSD), lambda i: (i, 0)),
                  pl.BlockSpec((tm, D_SSD), lambda i: (i, 0)),
                  pl.BlockSpec((tm, D_SSD), lambda i: (i, zcol)),
                  pl.BlockSpec((tm, 2 * D_MODEL), lambda i: (i, gcol)),
                  pl.BlockSpec((tm, D_NA), lambda i: (i, 0)),
                  pl.BlockSpec((tm, D_MODEL), lambda i: (i, 0)),
                  modspec(2), modspec(3), modspec(4),
                  full((1, D_SSD)), full((1, D_MODEL)),
                  full((D_SSD, D_MODEL)), full((D_NA, D_MODEL)), full((D_MODEL, D_MODEL)),
                  full((D_MODEL, LANE)), full((D_MODEL, LANE)), full((1, LANE))],
        out_specs=[pl.BlockSpec((tm, D_MODEL), lambda i: (i, 0)),
                   pl.BlockSpec((tm, D_MODEL // 2), lambda i: (i, 0)),
                   pl.BlockSpec((tm, LANE), lambda i: (i, 0))],
        out_shape=[jax.ShapeDtypeStruct((t, D_MODEL), F32),
                   jax.ShapeDtypeStruct((t, D_MODEL // 2), U32),
                   jax.ShapeDtypeStruct((t, LANE), F32)],
        compiler_params=_cparams(("arbitrary",), 48),
        name="merge",
    )(yf, yb, big, big, yna, x2, mod3, mod3, mod3, snw, n2w, wbs, wbn, wo, wrh, wrl, br)


ROUTE_TM = 512
GRP_LANE0 = N_EXPERTS


def _route_kernel(lg_ref, stri_ref, gs_ref, rt_ref, cnt_ref, run_scr):
    i = pl.program_id(0)

    @pl.when(i == 0)
    def _():
        run_scr[...] = jnp.zeros_like(run_scr)

    tm = ROUTE_TM
    lg = lg_ref[...]
    lane = lax.broadcasted_iota(I32, (tm, LANE), 1)
    neg_inf = jnp.float32(-jnp.inf)
    big_lane = jnp.int32(4 * LANE)
    is_grp = (lane >= GRP_LANE0) & (lane < GRP_LANE0 + N_GROUPS)
    gl = jnp.where(is_grp, lg, neg_inf)
    gmax = jnp.max(gl, axis=-1, keepdims=True)
    grp = jnp.min(jnp.where(gl == gmax, lane, big_lane), axis=-1, keepdims=True) - GRP_LANE0
    psum = jnp.sum(jnp.where(is_grp, jnp.exp(lg - gmax), 0.0), axis=-1, keepdims=True)
    p_grp = 1.0 / psum
    in_g = (lane < N_EXPERTS) & ((lane // EXPERTS_PER_GROUP) == grp)
    el = jnp.where(in_g, lg, neg_inf)
    v1 = jnp.max(el, axis=-1, keepdims=True)
    i1 = jnp.min(jnp.where(el == v1, lane, big_lane), axis=-1, keepdims=True)
    el2 = jnp.where(lane == i1, neg_inf, el)
    v2 = jnp.max(el2, axis=-1, keepdims=True)
    i2 = jnp.min(jnp.where(el2 == v2, lane, big_lane), axis=-1, keepdims=True)
    tt = jnp.exp(v2 - v1)
    den = 1.0 + tt
    ga = p_grp / den
    gb = p_grp * tt / den

    sel1 = lane == i1
    sel2 = lane == i2
    onehot = jnp.where(sel1 | sel2, 1.0, 0.0)
    cume = _dot(stri_ref[...], onehot.astype(BF16)) + run_scr[...]
    r1 = jnp.sum(jnp.where(sel1, cume, 0.0), axis=-1, keepdims=True)
    r2 = jnp.sum(jnp.where(sel2, cume, 0.0), axis=-1, keepdims=True)
    run_scr[...] = run_scr[...] + jnp.sum(onehot, axis=0, keepdims=True)
    cnt_ref[...] = run_scr[...].astype(I32)

    slab = jnp.where(lane == 0, ga, 0.0)
    slab = jnp.where(lane == 1, gb, slab)
    slab = jnp.where(lane == 2, i1.astype(F32), slab)
    slab = jnp.where(lane == 3, i2.astype(F32), slab)
    slab = jnp.where(lane == 4, r1, slab)
    slab = jnp.where(lane == 5, r2, slab)
    gs_ref[...] = slab
    for q in range(tm // LANE):
        blk_t = slab[q * LANE:(q + 1) * LANE, :].T
        rt_ref[:, q * LANE:(q + 1) * LANE] = blk_t[0:8, :].astype(I32)


def _route(logits, stri):
    t = logits.shape[0]
    tm = ROUTE_TM
    return pl.pallas_call(
        _route_kernel,
        grid=(t // tm,),
        in_specs=[pl.BlockSpec((tm, LANE), lambda i: (i, 0)),
                  pl.BlockSpec((tm, tm), lambda i: (0, 0))],
        out_specs=[pl.BlockSpec((tm, LANE), lambda i: (i, 0)),
                   pl.BlockSpec((8, tm), lambda i: (0, i)),
                   pl.BlockSpec((1, LANE), lambda i: (0, 0))],
        out_shape=[jax.ShapeDtypeStruct((t, LANE), F32),
                   jax.ShapeDtypeStruct((8, t), I32),
                   jax.ShapeDtypeStruct((1, LANE), I32)],
        scratch_shapes=[pltpu.VMEM((1, LANE), F32)],
        compiler_params=_cparams(("arbitrary",), 32),
        name="route",
    )(logits, stri)


MOE_BLOCK = 128
DISPATCH_TM = 256
COMBINE_TM = 256


def _dispatch_kernel(pstart_ref, e1_ref, e2_ref, r1_ref, r2_ref, h_hbm, xb_in_hbm, xb_hbm, sem):
    del xb_in_hbm
    i = pl.program_id(0)
    tm = DISPATCH_TM

    def row_copy(tok, dst):
        return pltpu.make_async_copy(h_hbm.at[pl.ds(tok, 1)], xb_hbm.at[pl.ds(dst, 1)], sem)

    def body(tt, carry):
        tok = i * tm + tt
        d1 = pstart_ref[e1_ref[0, 0, tt]] + r1_ref[0, 0, tt]
        d2 = pstart_ref[e2_ref[0, 0, tt]] + r2_ref[0, 0, tt]
        row_copy(tok, d1).start()
        row_copy(tok, d2).start()
        return carry

    lax.fori_loop(0, tm, body, 0)
    for _ in range(2):
        pltpu.make_async_copy(h_hbm.at[pl.ds(0, tm)], xb_hbm.at[pl.ds(0, tm)], sem).wait()


def _dispatch(pstart, e1, e2, r1, r2, h2p, n_rows):
    t = h2p.shape[0]
    tm = DISPATCH_TM
    w = h2p.shape[1]
    smem = pl.BlockSpec((1, 1, tm), lambda i, ps: (i, 0, 0), memory_space=pltpu.SMEM)
    anyspec = pl.BlockSpec(memory_space=pl.ANY)
    xb0 = jnp.zeros((n_rows, w), U32)
    grid_spec = pltpu.PrefetchScalarGridSpec(
        num_scalar_prefetch=1, grid=(t // tm,),
        in_specs=[smem, smem, smem, smem, anyspec, anyspec],
        out_specs=anyspec,
        scratch_shapes=[pltpu.SemaphoreType.DMA(())])
    return pl.pallas_call(
        _dispatch_kernel, grid_spec=grid_spec,
        out_shape=jax.ShapeDtypeStruct((n_rows, w), U32),
        input_output_aliases={6: 0},
        compiler_params=pltpu.CompilerParams(dimension_semantics=("arbitrary",), has_side_effects=True),
        name="moe_dispatch",
    )(pstart, e1, e2, r1, r2, h2p, xb0)


def _ffn_kernel(be_ref, nu_ref, xb_ref, w1_ref, w3_ref, w2_ref, yb_ref, w13_scr, w2_scr):
    i = pl.program_id(0)
    prev = be_ref[jnp.maximum(i - 1, 0)]
    changed = (i == 0) | (be_ref[i] != prev)
    used = i < nu_ref[0]

    @pl.when(changed & used)
    def _():
        w13_scr[:, :D_EXPERT] = w1_ref[0].astype(BF16)
        w13_scr[:, D_EXPERT:] = w3_ref[0].astype(BF16)
        w2_scr[...] = w2_ref[0].astype(BF16)

    @pl.when(used)
    def _():
        xw = xb_ref[...]
        lo = pltpu.bitcast(xw << 16, F32)
        hi = pltpu.bitcast(xw & jnp.uint32(0xFFFF0000), F32)
        x = jnp.concatenate([lo, hi], axis=1).astype(BF16)
        h = _dot(x, w13_scr[...])
        h1 = h[:, :D_EXPERT]
        h3 = h[:, D_EXPERT:]
        a = (h1 * _sigmoid(h1) * h3).astype(BF16)
        yb_ref[...] = _dot(a, w2_scr[...])

    @pl.when(jnp.logical_not(used))
    def _():
        yb_ref[...] = jnp.zeros_like(yb_ref)


def _ffn(blk_exp, n_used, xb, w1, w3, w2):
    n_rows = xb.shape[0]
    nblk = n_rows // MOE_BLOCK
    wmap = lambda i, be, nu: (be[i], 0, 0)
    grid_spec = pltpu.PrefetchScalarGridSpec(
        num_scalar_prefetch=2, grid=(nblk,),
        in_specs=[pl.BlockSpec((MOE_BLOCK, D_MODEL // 2), lambda i, be, nu: (i, 0)),
                  pl.BlockSpec((1, D_MODEL, D_EXPERT), wmap),
                  pl.BlockSpec((1, D_MODEL, D_EXPERT), wmap),
                  pl.BlockSpec((1, D_EXPERT, D_MODEL), wmap)],
        out_specs=pl.BlockSpec((MOE_BLOCK, D_MODEL), lambda i, be, nu: (i, 0)),
        scratch_shapes=[pltpu.VMEM((D_MODEL, 2 * D_EXPERT), BF16),
                        pltpu.VMEM((D_EXPERT, D_MODEL), BF16)])
    return pl.pallas_call(
        _ffn_kernel, grid_spec=grid_spec,
        out_shape=jax.ShapeDtypeStruct((n_rows, D_MODEL), F32),
        compiler_params=_cparams(("arbitrary",), 40),
        name="moe_ffn",
    )(blk_exp, n_used, xb, w1, w3, w2)


def _combine_kernel(pstart_ref, e1_ref, e2_ref, r1_ref, r2_ref, yb_hbm, gs_ref, x1_ref, g2_ref, o_ref,
                    ya_scr, yb_scr, sem):
    tm = COMBINE_TM

    def body(tt, carry):
        d1 = pstart_ref[e1_ref[0, 0, tt]] + r1_ref[0, 0, tt]
        d2 = pstart_ref[e2_ref[0, 0, tt]] + r2_ref[0, 0, tt]
        pltpu.make_async_copy(yb_hbm.at[pl.ds(d1, 1)], ya_scr.at[pl.ds(tt, 1)], sem).start()
        pltpu.make_async_copy(yb_hbm.at[pl.ds(d2, 1)], yb_scr.at[pl.ds(tt, 1)], sem).start()
        return carry

    lax.fori_loop(0, tm, body, 0)
    pltpu.make_async_copy(yb_hbm.at[pl.ds(0, tm)], ya_scr, sem).wait()
    pltpu.make_async_copy(yb_hbm.at[pl.ds(0, tm)], yb_scr, sem).wait()
    ga = gs_ref[:, 0:1]
    gb = gs_ref[:, 1:2]
    o_ref[...] = x1_ref[...] + g2_ref[0] * (ga * ya_scr[...] + gb * yb_scr[...])


def _combine(pstart, e1, e2, r1, r2, ybuf, gs, x1, mod3, seq_len):
    t = x1.shape[0]
    tm = COMBINE_TM
    tiles_per_seq = seq_len // tm
    smem = pl.BlockSpec((1, 1, tm), lambda i, ps: (i, 0, 0), memory_space=pltpu.SMEM)
    grid_spec = pltpu.PrefetchScalarGridSpec(
        num_scalar_prefetch=1, grid=(t // tm,),
        in_specs=[smem, smem, smem, smem,
                  pl.BlockSpec(memory_space=pl.ANY),
                  pl.BlockSpec((tm, LANE), lambda i, ps: (i, 0)),
                  pl.BlockSpec((tm, D_MODEL), lambda i, ps: (i, 0)),
                  pl.BlockSpec((1, 1, D_MODEL), lambda i, ps: ((i // tiles_per_seq) * N_MOD + 5, 0, 0))],
        out_specs=pl.BlockSpec((tm, D_MODEL), lambda i, ps: (i, 0)),
        scratch_shapes=[pltpu.VMEM((tm, D_MODEL), F32), pltpu.VMEM((tm, D_MODEL), F32),
                        pltpu.SemaphoreType.DMA(())])
    return pl.pallas_call(
        _combine_kernel, grid_spec=grid_spec,
        out_shape=jax.ShapeDtypeStruct((t, D_MODEL), F32),
        compiler_params=_cparams(("arbitrary",), 32),
        name="moe_combine",
    )(pstart, e1, e2, r1, r2, ybuf, gs, x1, mod3)


def _rope_tables(seq_len):
    t = jnp.arange(seq_len, dtype=I32)
    row = (t // GRID_W).astype(F32)
    colp = (t % GRID_W).astype(F32)
    half = SSD_STATE // 2
    inv = ROPE_THETA ** (-jnp.arange(0, half, 2, dtype=F32) / half)
    ar = row[:, None] * inv
    ac = colp[:, None] * inv
    cos_t = jnp.concatenate([jnp.cos(ar), jnp.cos(ar), jnp.cos(ac), jnp.cos(ac)], axis=-1)
    sin_t = jnp.concatenate([-jnp.sin(ar), jnp.sin(ar), -jnp.sin(ac), jnp.sin(ac)], axis=-1)
    return cos_t, sin_t


def _scan_tables(rev):
    li = jnp.arange(SSD_CHUNK)[:, None]
    ui = jnp.arange(SSD_CHUNK)[None, :]
    tri = ((ui >= li) if rev else (ui <= li)).astype(BF16)
    col = SSD_HEADS if rev else 0
    j = jnp.arange(LANE)[:, None]
    c = jnp.arange(D_SSD)[None, :]
    ee = (j == col + c // SSD_HEADDIM).astype(BF16)
    return tri, ee


def _pad_lanes(v, width=LANE):
    v = v.reshape(1, -1)
    return jnp.pad(v, ((0, 0), (0, width - v.shape[1])))


def kernel(x, c, ctx, c_ctx, w_mod, b_mod, norm1_w, w_in, conv_w, conv_b, a_log_f, a_log_b, dt_bias_f, dt_bias_b, d_skip, ssd_norm_w, q_norm_w, k_norm_w, rpb, w_br_ssd, w_br_na, w_out, norm2_w, w_grp, b_grp, w_rt, b_rt, w1, w3, w2):
    nb, seq_len, d = x.shape
    ctx_len = ctx.shape[1]
    t = nb * seq_len
    tc = nb * ctx_len
    assert w_mod.shape[0] == 1 and d == D_MODEL and nb <= 7
    assert seq_len % 256 == 0 and ctx_len % SSD_CHUNK == 0 and seq_len // GRID_W >= NA_KH

    cin = jnp.concatenate([c, c_ctx[None, :], jnp.zeros((8 - nb - 1, d), F32)], axis=0)
    mod = _modulation(cin, w_mod[0], b_mod[0])
    mod3 = mod.reshape(8 * N_MOD, 1, D_MODEL)

    wi = w_in[0]
    o_z, o_xbc, o_dt = 0, D_SSD, D_SSD + D_XBC
    o_qkv = o_dt + 2 * SSD_HEADS
    o_g = o_qkv + 3 * D_NA
    w_z = wi[:, o_z:o_xbc]
    w_xbc = wi[:, o_xbc:o_dt]
    w_dt = wi[:, o_dt:o_qkv]
    w_qkv = wi[:, o_qkv:o_g]
    w_g = wi[:, o_g:]
    w_lat = jnp.concatenate([w_xbc, w_z, w_g, w_qkv], axis=1).astype(BF16)
    w_ctx = jnp.concatenate([w_xbc, w_qkv[:, D_NA:]], axis=1).astype(BF16)
    w_dtp = jnp.pad(w_dt, ((0, 0), (0, LANE - 2 * SSD_HEADS))).astype(BF16)
    n1w = norm1_w[0].reshape(1, D_MODEL)

    x2 = x.reshape(t, D_MODEL)
    ctx2 = ctx.reshape(tc, D_MODEL)
    tm_in = 1024 if seq_len % 1024 == 0 else 256
    tiles = seq_len // tm_in
    big, dt_raw = _inproj(x2, mod3, lambda i: i // tiles, n1w, w_lat, w_dtp, tm_in, 1024)
    tm_c = 256 if tc % 256 == 0 else SSD_CHUNK
    bigc, dtc_raw = _inproj(ctx2, mod3, lambda i: nb, n1w, w_ctx, w_dtp, tm_c, 1024)
    ZCOL, GCOL, QCOL, KCOL, VCOL = 2, 3, 8, 9, 10
    KC_COL, VC_COL = 4, 5

    dtbias = _pad_lanes(jnp.concatenate([dt_bias_f[0], dt_bias_b[0]]))
    cw = conv_w[0]
    cbias = conv_b[0].reshape(1, D_XBC)
    cos_t, sin_t = _rope_tables(seq_len)
    xs, bc, dts = _ssd_prep(big, dt_raw, cw, cbias, dtbias, cos_t, sin_t, seq_len, 256)
    ctl = 256 if ctx_len % 256 == 0 else SSD_CHUNK
    ones_t = jnp.ones((ctx_len, LANE), F32)
    zeros_t = jnp.zeros((ctx_len, LANE), F32)
    xsc, bcc, dtsc = _ssd_prep(bigc, dtc_raw, cw, cbias, dtbias, ones_t, zeros_t, ctx_len, ctl)

    alog = _pad_lanes(jnp.concatenate([a_log_f[0], a_log_b[0]]))
    dskip = jnp.repeat(d_skip[0], SSD_HEADDIM).reshape(1, D_SSD)
    tri_f, ee_f = _scan_tables(False)
    tri_b, ee_b = _scan_tables(True)
    h_zero = jnp.zeros((nb, SSD_STATE, D_SSD), F32)
    _, hcf = _ssd_scan(xsc, bcc, dtsc, alog, dskip, tri_f, ee_f, h_zero, nb, False)
    _, hcb = _ssd_scan(xsc, bcc, dtsc, alog, dskip, tri_b, ee_b, h_zero, nb, True)
    yf, _ = _ssd_scan(xs, bc, dts, alog, dskip, tri_f, ee_f, hcf, nb, False)
    yb, _ = _ssd_scan(xs, bc, dts, alog, dskip, tri_b, ee_b, hcb, nb, True)

    qw = jnp.tile(q_norm_w[0], NA_HEADS).reshape(1, D_NA)
    kw = jnp.tile(k_norm_w[0], NA_HEADS).reshape(1, D_NA)
    gi = jnp.arange(LANE)
    gmat = ((gi[:, None] // NA_HEADDIM) == (gi[None, :] // NA_HEADDIM)).astype(BF16) * (1.0 / NA_HEADDIM)
    gmat = gmat.astype(BF16)
    qn, kn = _na_prep(big, QCOL, KCOL, qw, kw, gmat, 512 if t % 512 == 0 else 256)
    kcn = _na_prep(bigc, None, KC_COL, qw, kw, gmat, 256 if tc % 256 == 0 else SSD_CHUNK)
    bias_tab = _na_bias_table(rpb[0])
    y_na = _neigh_attention(qn, kn, big, kcn, bigc, bias_tab, nb, seq_len, ctx_len, VCOL, VC_COL)

    w_r = jnp.pad(jnp.concatenate([w_rt[0], w_grp[0]], axis=1), ((0, 0), (0, LANE - N_EXPERTS - N_GROUPS)))
    wrh = w_r.astype(BF16)
    wrl = (w_r - wrh.astype(F32)).astype(BF16)
    br = _pad_lanes(jnp.concatenate([b_rt[0], b_grp[0]]))
    x1, h2p, logits = _merge(yf, yb, big, y_na, x2, mod3,
                             ssd_norm_w[0].reshape(1, D_SSD), norm2_w[0].reshape(1, D_MODEL),
                             w_br_ssd[0].astype(BF16), w_br_na[0].astype(BF16), w_out[0].astype(BF16),
                             wrh, wrl, br, seq_len, 256, ZCOL, GCOL)

    si = jnp.arange(ROUTE_TM)
    stri = (si[None, :] < si[:, None]).astype(BF16)
    gs, rt, cnt = _route(logits, stri)

    counts = cnt[0, :N_EXPERTS]
    padded = (counts + MOE_BLOCK - 1) // MOE_BLOCK * MOE_BLOCK
    pend = jnp.cumsum(padded)
    pstart = (pend - padded).astype(I32)
    nblk = (2 * t + N_EXPERTS * (MOE_BLOCK - 1) + MOE_BLOCK - 1) // MOE_BLOCK
    n_used = (pend[-1] // MOE_BLOCK).astype(I32).reshape(1)
    blk_exp = jnp.minimum(jnp.searchsorted(pend, jnp.arange(nblk, dtype=I32) * MOE_BLOCK, side='right'),
                          N_EXPERTS - 1).astype(I32)

    def tok_tiles(row, tm):
        return rt[row].reshape(t // tm, 1, tm)

    e1d, e2d, r1d, r2d = [tok_tiles(k, DISPATCH_TM) for k in (2, 3, 4, 5)]
    xb = _dispatch(pstart, e1d, e2d, r1d, r2d, h2p, nblk * MOE_BLOCK)
    ybuf = _ffn(blk_exp, n_used, xb, w1[0], w3[0], w2[0])
    e1c, e2c, r1c, r2c = [tok_tiles(k, COMBINE_TM) for k in (2, 3, 4, 5)]
    out = _combine(pstart, e1c, e2c, r1c, r2c, ybuf, gs, x1, mod3, seq_len)
    return out.reshape(nb, seq_len, D_MODEL)
```

```python
import functools
import math

import jax
import jax.numpy as jnp
from jax import lax
from jax.experimental import pallas as pl
from jax.experimental.pallas import tpu as pltpu

F32 = jnp.float32
BF16 = jnp.bfloat16
I32 = jnp.int32
U32 = jnp.uint32

D_MODEL = 1024
GRID_W = 64
D_SSD = 2048
SSD_HEADDIM = 64
SSD_HEADS = 32
SSD_GROUPS = 8
HEADS_PER_GROUP = 4
SSD_STATE = 128
SSD_CHUNK = 128
CONV_K = 5
D_BC = SSD_GROUPS * SSD_STATE
D_XBC = D_SSD + 2 * D_BC
NA_HEADDIM = 64
NA_HEADS = 16
D_NA = 1024
NA_KH = 8
NA_KW = 16
ROPE_THETA = 10000.0
N_GROUPS = 4
EXPERTS_PER_GROUP = 8
N_EXPERTS = 32
D_EXPERT = 512
N_MOD = 6
EPS = 1e-6
NEG_BIG = -1e30

COLS_LAT = D_XBC + D_SSD + 2 * D_MODEL + 3 * D_NA
COLS_CTX = D_XBC + 2 * D_NA
LANE = 128

V7X_VMEM_BYTES = 64 * 1024 * 1024


def _cparams(sem, vmem_mb):
    return pltpu.CompilerParams(dimension_semantics=sem, vmem_limit_bytes=vmem_mb * 1024 * 1024)


def _sigmoid(x):
    return 1.0 / (1.0 + jnp.exp(-x))


def _split2(x):
    hi = x.astype(BF16)
    lo = (x - hi.astype(F32)).astype(BF16)
    return hi, lo


def _split3(x):
    hi = x.astype(BF16)
    r = x - hi.astype(F32)
    mid = r.astype(BF16)
    lo = (r - mid.astype(F32)).astype(BF16)
    return hi, mid, lo


def _dot(a, b):
    return jnp.dot(a, b, preferred_element_type=F32)


def _dot_nt(a, b):
    return lax.dot_general(a, b, (((1,), (1,)), ((), ())), preferred_element_type=F32)


def _dot_tn(a, b):
    return lax.dot_general(a, b, (((0,), (0,)), ((), ())), preferred_element_type=F32)


def _mod_kernel(c_ref, w_ref, b_ref, o_ref):
    c = c_ref[...]
    a = (c * _sigmoid(c)).astype(BF16)
    o_ref[...] = _dot(a, w_ref[...].astype(BF16)) + b_ref[...]


def _modulation(cin, w_mod, b_mod):
    n = w_mod.shape[1]
    tn = 1536
    return pl.pallas_call(
        _mod_kernel,
        grid=(n // tn,),
        in_specs=[pl.BlockSpec((8, D_MODEL), lambda j: (0, 0)),
                  pl.BlockSpec((D_MODEL, tn), lambda j: (0, j)),
                  pl.BlockSpec((1, tn), lambda j: (0, j))],
        out_specs=pl.BlockSpec((8, tn), lambda j: (0, j)),
        out_shape=jax.ShapeDtypeStruct((8, n), F32),
        compiler_params=_cparams(("arbitrary",), 40),
        name="modulation",
    )(cin, w_mod, b_mod.reshape(1, n))


def _inproj_kernel(x_ref, sh_ref, sc_ref, nw_ref, w_ref, wdt_ref, o_ref, dt_ref, h_scr, *, tm):
    j = pl.program_id(1)

    @pl.when(j == 0)
    def _():
        scale = 1.0 + sc_ref[0]
        shift = sh_ref[0]
        nw = nw_ref[...]

        def body(r, carry):
            rows = pl.ds(pl.multiple_of(r * 128, 128), 128)
            x = x_ref[rows, :]
            ms = jnp.mean(x * x, axis=-1, keepdims=True)
            h = x * lax.rsqrt(ms + EPS) * nw * scale + shift
            h_scr[rows, :] = h.astype(BF16)
            return carry

        lax.fori_loop(0, tm // 128, body, 0)
        dt_ref[...] = _dot(h_scr[...], wdt_ref[...])

    o_ref[...] = _dot(h_scr[...], w_ref[...]).astype(BF16)


def _inproj(x2, mod3, mod_row_fn, norm_w, w, wdt, tm, tn):
    t = x2.shape[0]
    n = w.shape[1]
    kern = functools.partial(_inproj_kernel, tm=tm)
    return pl.pallas_call(
        kern,
        grid=(t // tm, n // tn),
        in_specs=[pl.BlockSpec((tm, D_MODEL), lambda i, j: (i, 0)),
                  pl.BlockSpec((1, 1, D_MODEL), lambda i, j: (mod_row_fn(i) * N_MOD + 0, 0, 0)),
                  pl.BlockSpec((1, 1, D_MODEL), lambda i, j: (mod_row_fn(i) * N_MOD + 1, 0, 0)),
                  pl.BlockSpec((1, D_MODEL), lambda i, j: (0, 0)),
                  pl.BlockSpec((D_MODEL, tn), lambda i, j: (0, j)),
                  pl.BlockSpec((D_MODEL, LANE), lambda i, j: (0, 0))],
        out_specs=[pl.BlockSpec((tm, tn), lambda i, j: (i, j)),
                   pl.BlockSpec((tm, LANE), lambda i, j: (i, 0))],
        out_shape=[jax.ShapeDtypeStruct((t, n), BF16),
                   jax.ShapeDtypeStruct((t, LANE), F32)],
        scratch_shapes=[pltpu.VMEM((tm, D_MODEL), BF16)],
        compiler_params=_cparams(("arbitrary", "arbitrary"), 48),
        name="inproj",
    )(x2, mod3, mod3, norm_w, w, wdt)


def _ssd_prep_kernel(prev_ref, cur_ref, next_ref, dt_ref, cw_ref, cb_ref, dtbias_ref, cos_ref, sin_ref,
                     xs_ref, bc_ref, dto_ref, ext_scr, *, tl, tiles_per_seq):
    i = pl.program_id(0)
    first = (i % tiles_per_seq) == 0
    last = (i % tiles_per_seq) == tiles_per_seq - 1
    ext_scr[0:16, :] = jnp.where(first, 0.0, prev_ref[...].astype(F32))
    ext_scr[16 + tl:32 + tl, :] = jnp.where(last, 0.0, next_ref[...].astype(F32))
    rc = 64
    for r in range(tl // rc):
        ext_scr[16 + r * rc:16 + (r + 1) * rc, :] = cur_ref[r * rc:(r + 1) * rc, :].astype(F32)

    lane = lax.broadcasted_iota(I32, (rc, LANE), 1)
    lo_half = ((lane // 32) % 2) == 0
    cw_all = cw_ref[...]
    cwid = 512
    for c in range(D_XBC // cwid):
        cs = slice(c * cwid, (c + 1) * cwid)
        w = cw_all[:, cs]
        b = cb_ref[:, cs]
        for r in range(tl // rc):
            acc = b + ext_scr[14 + r * rc:14 + (r + 1) * rc, cs] * w[0:1, :]
            for k in range(1, CONV_K):
                acc = acc + ext_scr[14 + k + r * rc:14 + k + (r + 1) * rc, cs] * w[k:k + 1, :]
            y = acc * _sigmoid(acc)
            rows = slice(r * rc, (r + 1) * rc)
            if c * cwid < D_SSD:
                xs_ref[rows, cs] = y.astype(BF16)
            else:
                cosv = cos_ref[rows, :]
                sinv = sin_ref[rows, :]
                for g in range(cwid // LANE):
                    yg = y[:, g * LANE:(g + 1) * LANE]
                    partner = jnp.where(lo_half, pltpu.roll(yg, 96, 1), pltpu.roll(yg, 32, 1))
                    o = yg * cosv + partner * sinv
                    c0 = c * cwid - D_SSD + g * LANE
                    bc_ref[rows, c0:c0 + LANE] = o.astype(BF16)

    v = dt_ref[...] + dtbias_ref[...]
    dto_ref[...] = jnp.maximum(v, 0.0) + jnp.log(1.0 + jnp.exp(-jnp.abs(v)))


def _ssd_prep(big, dt_raw, conv_w, conv_b, dtbias, cos_t, sin_t, seq_len, tl):
    t = big.shape[0]
    tps = seq_len // tl
    hb = tl // 16
    nhalo = t // 16
    kern = functools.partial(_ssd_prep_kernel, tl=tl, tiles_per_seq=tps)
    return pl.pallas_call(
        kern,
        grid=(t // tl,),
        in_specs=[pl.BlockSpec((16, D_XBC), lambda i: (jnp.maximum(i * hb - 1, 0), 0)),
                  pl.BlockSpec((tl, D_XBC), lambda i: (i, 0)),
                  pl.BlockSpec((16, D_XBC), lambda i: (jnp.minimum((i + 1) * hb, nhalo - 1), 0)),
                  pl.BlockSpec((tl, LANE), lambda i: (i, 0)),
                  pl.BlockSpec((CONV_K, D_XBC), lambda i: (0, 0)),
                  pl.BlockSpec((1, D_XBC), lambda i: (0, 0)),
                  pl.BlockSpec((1, LANE), lambda i: (0, 0)),
                  pl.BlockSpec((tl, LANE), lambda i: (i % tps, 0)),
                  pl.BlockSpec((tl, LANE), lambda i: (i % tps, 0))],
        out_specs=[pl.BlockSpec((tl, D_SSD), lambda i: (i, 0)),
                   pl.BlockSpec((tl, 2 * D_BC), lambda i: (i, 0)),
                   pl.BlockSpec((tl, LANE), lambda i: (i, 0))],
        out_shape=[jax.ShapeDtypeStruct((t, D_SSD), BF16),
                   jax.ShapeDtypeStruct((t, 2 * D_BC), BF16),
                   jax.ShapeDtypeStruct((t, LANE), F32)],
        scratch_shapes=[pltpu.VMEM((tl + 32, D_XBC), F32)],
        compiler_params=_cparams(("arbitrary",), 40),
        name="ssd_prep",
    )(big, big, big, dt_raw, conv_w, conv_b, dtbias, cos_t, sin_t)


def _ssd_scan_kernel(xs_ref, bc_ref, dt_ref, alog_ref, dskip_ref, tri_ref, ee_ref, h0_ref,
                     y_ref, hfin_ref, h_scr, *, rev, nc):
    k = pl.program_id(1)
    L = SSD_CHUNK

    @pl.when(k == 0)
    def _():
        h_scr[...] = h0_ref[0]

    col = SSD_HEADS if rev else 0
    edge = 0 if rev else L - 1
    a_row = -jnp.exp(alog_ref[...])
    dt = dt_ref[...]
    trib = tri_ref[...]
    tri = trib > 0.5
    d_hi, d_mid, d_lo = _split3(dt * a_row)
    cum = _dot(trib, d_hi) + _dot(trib, d_mid) + _dot(trib, d_lo)
    tot = cum[edge:edge + 1, :]
    cum_t = cum.T
    dt_t = dt.T

    ee = ee_ref[...]
    o_hi, o_lo = _split2(jnp.exp(cum))
    oscale = _dot(o_hi, ee) + _dot(o_lo, ee)
    w_hi, w_lo = _split2(jnp.exp(tot - cum) * dt)
    wscale = _dot(w_hi, ee) + _dot(w_lo, ee)
    etot = oscale[edge:edge + 1, :]

    gw = HEADS_PER_GROUP * SSD_HEADDIM
    lane_head = lax.broadcasted_iota(I32, (L, gw), 1) // SSD_HEADDIM
    for g in range(SSD_GROUPS):
        gs = slice(g * gw, (g + 1) * gw)
        b_g = bc_ref[:, g * SSD_STATE:(g + 1) * SSD_STATE]
        c_g = bc_ref[:, D_BC + g * SSD_STATE:D_BC + (g + 1) * SSD_STATE]
        xs_g = xs_ref[:, gs]
        cb = _dot_nt(c_g, b_g)
        ms = []
        for r in range(HEADS_PER_GROUP):
            hh = col + g * HEADS_PER_GROUP + r
            seg = cum[:, hh:hh + 1] - cum_t[hh:hh + 1, :]
            lm = jnp.exp(jnp.where(tri, seg, NEG_BIG)) * dt_t[hh:hh + 1, :]
            ms.append((cb * lm).astype(BF16))
        mcat = jnp.concatenate(ms, axis=1)
        zero = jnp.zeros_like(xs_g)
        bd = jnp.concatenate([jnp.where(lane_head == r, xs_g, zero) for r in range(HEADS_PER_GROUP)], axis=0)
        y_diag = _dot(mcat, bd)
        h_g = h_scr[:, gs]
        y_off = _dot(c_g, h_g.astype(BF16))
        y = y_diag + oscale[:, gs] * y_off
        xs_f = xs_g.astype(F32)
        if not rev:
            y = y + xs_f * dskip_ref[:, gs]
        y_ref[:, gs] = y.astype(BF16)
        xw = (xs_f * wscale[:, gs]).astype(BF16)
        h_scr[:, gs] = etot[:, gs] * h_g + _dot_tn(b_g, xw)

    @pl.when(k == nc - 1)
    def _():
        hfin_ref[0] = h_scr[...]


def _ssd_scan(xs, bc, dt, alog_row, dskip_row, tri, ee, h0, nb, rev):
    t = xs.shape[0]
    nc = t // nb // SSD_CHUNK
    kern = functools.partial(_ssd_scan_kernel, rev=rev, nc=nc)

    def rowmap(b, k):
        c = (nc - 1 - k) if rev else k
        return (b * nc + c, 0)

    const2 = lambda b, k: (0, 0)
    return pl.pallas_call(
        kern,
        grid=(nb, nc),
        in_specs=[pl.BlockSpec((SSD_CHUNK, D_SSD), rowmap),
                  pl.BlockSpec((SSD_CHUNK, 2 * D_BC), rowmap),
                  pl.BlockSpec((SSD_CHUNK, LANE), rowmap),
                  pl.BlockSpec((1, LANE), const2),
                  pl.BlockSpec((1, D_SSD), const2),
                  pl.BlockSpec((SSD_CHUNK, SSD_CHUNK), const2),
                  pl.BlockSpec((LANE, D_SSD), const2),
                  pl.BlockSpec((1, SSD_STATE, D_SSD), lambda b, k: (b, 0, 0))],
        out_specs=[pl.BlockSpec((SSD_CHUNK, D_SSD), rowmap),
                   pl.BlockSpec((1, SSD_STATE, D_SSD), lambda b, k: (b, 0, 0))],
        out_shape=[jax.ShapeDtypeStruct((t, D_SSD), BF16),
                   jax.ShapeDtypeStruct((nb, SSD_STATE, D_SSD), F32)],
        scratch_shapes=[pltpu.VMEM((SSD_STATE, D_SSD), F32)],
        compiler_params=_cparams(("arbitrary", "arbitrary"), 40),
        name="ssd_scan_bwd" if rev else "ssd_scan_fwd",
    )(xs, bc, dt, alog_row, dskip_row, tri, ee, h0)


def _headnorm_cols(src_ref, w_ref, g, dst_ref, scale):
    for c in range(D_NA // LANE):
        cs = slice(c * LANE, (c + 1) * LANE)
        x = src_ref[:, cs].astype(F32)
        hi, lo = _split2(x * x)
        ms = _dot(hi, g) + _dot(lo, g)
        y = x * lax.rsqrt(ms + EPS) * w_ref[:, cs]
        if scale is not None:
            y = y * scale
        dst_ref[:, cs] = y.astype(BF16)


def _na_prep_qk_kernel(q_ref, k_ref, qw_ref, kw_ref, g_ref, qo_ref, ko_ref):
    g = g_ref[...]
    _headnorm_cols(q_ref, qw_ref, g, qo_ref, NA_HEADDIM ** -0.5)
    _headnorm_cols(k_ref, kw_ref, g, ko_ref, None)


def _na_prep_k_kernel(k_ref, kw_ref, g_ref, ko_ref):
    _headnorm_cols(k_ref, kw_ref, g_ref[...], ko_ref, None)


def _na_prep(big, qcol, kcol, qw, kw, gmat, tm):
    t = big.shape[0]
    blk = lambda cidx: pl.BlockSpec((tm, D_NA), lambda i: (i, cidx))
    row = pl.BlockSpec((1, D_NA), lambda i: (0, 0))
    gspec = pl.BlockSpec((LANE, LANE), lambda i: (0, 0))
    out = pl.BlockSpec((tm, D_NA), lambda i: (i, 0))
    if qcol is None:
        return pl.pallas_call(
            _na_prep_k_kernel, grid=(t // tm,),
            in_specs=[blk(kcol), row, gspec], out_specs=out,
            out_shape=jax.ShapeDtypeStruct((t, D_NA), BF16),
            compiler_params=_cparams(("arbitrary",), 32), name="na_prep_ctx",
        )(big, kw, gmat)
    return pl.pallas_call(
        _na_prep_qk_kernel, grid=(t // tm,),
        in_specs=[blk(qcol), blk(kcol), row, row, gspec], out_specs=[out, out],
        out_shape=[jax.ShapeDtypeStruct((t, D_NA), BF16)] * 2,
        compiler_params=_cparams(("arbitrary",), 32), name="na_prep",
    )(big, big, qw, kw, gmat)


NA_LOOKAHEAD = 3


def _na_kernel(q_ref, k_ref, v_ref, kc_ref, vc_ref, bias_ref, o_ref, *, rows):
    r = pl.program_id(1)
    rs = jnp.clip(r - NA_KH // 2, 0, rows - NA_KH)
    start = pl.multiple_of(rs * GRID_W, GRID_W)
    nk = NA_KH * GRID_W
    lane = lax.broadcasted_iota(I32, (GRID_W, LANE), 1)
    first_head = lane < NA_HEADDIM
    npair = NA_HEADS // 2

    def scores(j):
        cs = slice(j * LANE, (j + 1) * LANE)
        qp = q_ref[:, cs]
        zero = jnp.zeros_like(qp)
        qs = jnp.concatenate([jnp.where(first_head, qp, zero), jnp.where(first_head, zero, qp)], axis=0)
        kb = k_ref[pl.ds(start, nk), cs]
        s_loc = _dot_nt(qs, kb) + bias_ref[0, j * LANE:(j + 1) * LANE, :]
        s_ctx = _dot_nt(qs, kc_ref[:, cs])
        return s_loc, s_ctx

    pending = [scores(j) for j in range(NA_LOOKAHEAD)]
    for j in range(npair):
        cs = slice(j * LANE, (j + 1) * LANE)
        s_loc, s_ctx = pending.pop(0)
        if j + NA_LOOKAHEAD < npair:
            pending.append(scores(j + NA_LOOKAHEAD))
        vb = v_ref[pl.ds(start, nk), cs]
        m = jnp.maximum(jnp.max(s_loc, axis=-1, keepdims=True), jnp.max(s_ctx, axis=-1, keepdims=True))
        p_loc = jnp.exp(s_loc - m)
        p_ctx = jnp.exp(s_ctx - m)
        den = jnp.sum(p_loc, axis=-1, keepdims=True) + jnp.sum(p_ctx, axis=-1, keepdims=True)
        o = _dot(p_loc.astype(BF16), vb) + _dot(p_ctx.astype(BF16), vc_ref[:, cs])
        o = o / den
        o_ref[:, cs] = jnp.where(first_head, o[:GRID_W], o[GRID_W:]).astype(BF16)


def _neigh_attention(qn, kn, big, kcn, bigc, bias_tab, nb, seq_len, ctx_len, vcol, vccol):
    t = qn.shape[0]
    rows = seq_len // GRID_W
    kern = functools.partial(_na_kernel, rows=rows)

    def pat(b, r):
        return (r - jnp.clip(r - NA_KH // 2, 0, rows - NA_KH), 0, 0)

    return pl.pallas_call(
        kern,
        grid=(nb, rows),
        in_specs=[pl.BlockSpec((GRID_W, D_NA), lambda b, r: (b * rows + r, 0)),
                  pl.BlockSpec((seq_len, D_NA), lambda b, r: (b, 0)),
                  pl.BlockSpec((seq_len, D_NA), lambda b, r: (b, vcol)),
                  pl.BlockSpec((ctx_len, D_NA), lambda b, r: (b, 0)),
                  pl.BlockSpec((ctx_len, D_NA), lambda b, r: (b, vccol)),
                  pl.BlockSpec((1, NA_HEADS * GRID_W, NA_KH * GRID_W), pat)],
        out_specs=pl.BlockSpec((GRID_W, D_NA), lambda b, r: (b * rows + r, 0)),
        out_shape=jax.ShapeDtypeStruct((t, D_NA), BF16),
        compiler_params=_cparams(("arbitrary", "arbitrary"), 56),
        name="neigh_attention",
    )(qn, kn, big, kcn, bigc, bias_tab)


def _na_bias_kernel(rpb_ref, o_ref):
    rp = rpb_ref[0]
    r64 = pltpu.roll(rp, GRID_W, 1)
    c = lax.broadcasted_iota(I32, (GRID_W, LANE), 0)
    kc = lax.broadcasted_iota(I32, (GRID_W, LANE), 1) % GRID_W
    cs = jnp.clip(c - NA_KW // 2, 0, GRID_W - NA_KW)
    valid = (kc >= cs) & (kc < cs + NA_KW)
    pair = []
    for d in range(2 * NA_KH - 2):
        vec = rp[d:d + 1, :] + r64[d + 1:d + 2, :]
        w = pltpu.roll(jnp.broadcast_to(vec, (GRID_W, LANE)), LANE - (NA_KW - 1), 1, stride=1, stride_axis=0)
        pair.append(jnp.where(valid, w, NEG_BIG))
    for p in range(NA_KH):
        for ii in range(NA_KH // 2):
            o_ref[p, :, ii * LANE:(ii + 1) * LANE] = pair[2 * ii - p + NA_KH - 1]


def _na_bias_table(rpb):
    rp = jnp.pad(rpb, ((0, 0), (0, 1), (0, LANE - (2 * NA_KW - 1))))
    return pl.pallas_call(
        _na_bias_kernel,
        grid=(NA_HEADS,),
        in_specs=[pl.BlockSpec((1, 2 * NA_KH, LANE), lambda h: (h, 0, 0))],
        out_specs=pl.BlockSpec((NA_KH, GRID_W, NA_KH * GRID_W), lambda h: (0, h, 0)),
        out_shape=jax.ShapeDtypeStruct((NA_KH, NA_HEADS * GRID_W, NA_KH * GRID_W), F32),
        compiler_params=_cparams(("arbitrary",), 32),
        name="na_bias",
    )(rp)


def _merge_kernel(yf_ref, yb_ref, z_ref, gt_ref, yna_ref, x_ref, g1_ref, sh2_ref, sc2_ref, snw_ref, n2w_ref,
                  wbs_ref, wbn_ref, wo_ref, wrh_ref, wrl_ref, br_ref, x1_ref, h2p_ref, lg_ref):
    z = z_ref[...].astype(F32)
    y = (yf_ref[...].astype(F32) + yb_ref[...].astype(F32)) * (z * _sigmoid(z))
    ms = jnp.mean(y * y, axis=-1, keepdims=True)
    yn = (y * lax.rsqrt(ms + EPS) * snw_ref[...]).astype(BF16)
    a = _dot(yn, wbs_ref[...])
    b = _dot(yna_ref[...], wbn_ref[...])
    g_ssd = gt_ref[:, :D_MODEL].astype(F32)
    g_na = gt_ref[:, D_MODEL:].astype(F32)
    merged = (_sigmoid(g_ssd) * a + _sigmoid(g_na) * b).astype(BF16)
    x1 = x_ref[...] + g1_ref[0] * _dot(merged, wo_ref[...])
    x1_ref[...] = x1
    ms2 = jnp.mean(x1 * x1, axis=-1, keepdims=True)
    h2 = x1 * lax.rsqrt(ms2 + EPS) * n2w_ref[...] * (1.0 + sc2_ref[0]) + sh2_ref[0]
    h_hi, h_lo = _split2(h2)
    lg_ref[...] = (_dot(h_hi, wrh_ref[...]) + _dot(h_lo, wrh_ref[...]) + _dot(h_hi, wrl_ref[...])) + br_ref[...]
    bits = pltpu.bitcast(h_hi.astype(F32), U32)
    half = D_MODEL // 2
    h2p_ref[...] = (bits[:, :half] >> 16) | bits[:, half:]


def _merge(yf, yb, big, yna, x2, mod3, snw, n2w, wbs, wbn, wo, wrh, wrl, br, seq_len, tm, zcol, gcol):
    t = x2.shape[0]
    tiles_per_seq = seq_len // tm
    modspec = lambda kidx: pl.BlockSpec((1, 1, D_MODEL), lambda i: ((i // tiles_per_seq) * N_MOD + kidx, 0, 0))
    full = lambda shp: pl.BlockSpec(shp, lambda i: (0,) * len(shp))
    return pl.pallas_call(
        _merge_kernel,
        grid=(t // tm,),
        in_specs=[pl.BlockSpec((tm, D_SSD), lambda i: (i, 0)),
                  pl.BlockSpec((tm, D_SSD), lambda i: (i, 0)),
                  pl.BlockSpec((tm, D_SSD), lambda i: (i, zcol)),
                  pl.BlockSpec((tm, 2 * D_MODEL), lambda i: (i, gcol)),
                  pl.BlockSpec((tm, D_NA), lambda i: (i, 0)),
                  pl.BlockSpec((tm, D_MODEL), lambda i: (i, 0)),
                  modspec(2), modspec(3), modspec(4),
                  full((1, D_SSD)), full((1, D_MODEL)),
                  full((D_SSD, D_MODEL)), full((D_NA, D_MODEL)), full((D_MODEL, D_MODEL)),
                  full((D_MODEL, LANE)), full((D_MODEL, LANE)), full((1, LANE))],
        out_specs=[pl.BlockSpec((tm, D_MODEL), lambda i: (i, 0)),
                   pl.BlockSpec((tm, D_MODEL // 2), lambda i: (i, 0)),
                   pl.BlockSpec((tm, LANE), lambda i: (i, 0))],
        out_shape=[jax.ShapeDtypeStruct((t, D_MODEL), F32),
                   jax.ShapeDtypeStruct((t, D_MODEL // 2), U32),
                   jax.ShapeDtypeStruct((t, LANE), F32)],
        compiler_params=_cparams(("arbitrary",), 48),
        name="merge",
    )(yf, yb, big, big, yna, x2, mod3, mod3, mod3, snw, n2w, wbs, wbn, wo, wrh, wrl, br)


ROUTE_TM = 512
GRP_LANE0 = N_EXPERTS


def _route_topk(lg):
    tm = lg.shape[0]
    lane = lax.broadcasted_iota(I32, (tm, LANE), 1)
    neg_inf = jnp.float32(-jnp.inf)
    big_lane = jnp.int32(4 * LANE)
    is_grp = (lane >= GRP_LANE0) & (lane < GRP_LANE0 + N_GROUPS)
    gl = jnp.where(is_grp, lg, neg_inf)
    gmax = jnp.max(gl, axis=-1, keepdims=True)
    grp = jnp.min(jnp.where(gl == gmax, lane, big_lane), axis=-1, keepdims=True) - GRP_LANE0
    psum = jnp.sum(jnp.where(is_grp, jnp.exp(lg - gmax), 0.0), axis=-1, keepdims=True)
    p_grp = 1.0 / psum
    in_g = (lane < N_EXPERTS) & ((lane // EXPERTS_PER_GROUP) == grp)
    el = jnp.where(in_g, lg, neg_inf)
    v1 = jnp.max(el, axis=-1, keepdims=True)
    i1 = jnp.min(jnp.where(el == v1, lane, big_lane), axis=-1, keepdims=True)
    el2 = jnp.where(lane == i1, neg_inf, el)
    v2 = jnp.max(el2, axis=-1, keepdims=True)
    i2 = jnp.min(jnp.where(el2 == v2, lane, big_lane), axis=-1, keepdims=True)
    tt = jnp.exp(v2 - v1)
    den = 1.0 + tt
    ga = p_grp / den
    gb = p_grp * tt / den

    sel1 = lane == i1
    sel2 = lane == i2
    return lane, sel1, sel2, ga, gb


def _route_count_kernel(lg_ref, cnt_ref, run_scr):
    @pl.when(pl.program_id(0) == 0)
    def _():
        run_scr[...] = jnp.zeros_like(run_scr)

    _, sel1, sel2, _, _ = _route_topk(lg_ref[...])
    onehot = jnp.where(sel1 | sel2, 1.0, 0.0)
    run_scr[...] = run_scr[...] + jnp.sum(onehot, axis=0, keepdims=True)
    cnt_ref[...] = run_scr[...].astype(I32)


def _route_dest_kernel(lg_ref, stri_ref, pstart_ref, gs_ref, rt_ref, run_scr):
    @pl.when(pl.program_id(0) == 0)
    def _():
        run_scr[...] = pstart_ref[...]

    lane, sel1, sel2, ga, gb = _route_topk(lg_ref[...])
    onehot = jnp.where(sel1 | sel2, 1.0, 0.0)
    pos = _dot(stri_ref[...], onehot.astype(BF16)) + run_scr[...]
    d1 = jnp.sum(jnp.where(sel1, pos, 0.0), axis=-1, keepdims=True)
    d2 = jnp.sum(jnp.where(sel2, pos, 0.0), axis=-1, keepdims=True)
    run_scr[...] = run_scr[...] + jnp.sum(onehot, axis=0, keepdims=True)

    slab = jnp.where(lane == 0, ga, 0.0)
    slab = jnp.where(lane == 1, gb, slab)
    slab = jnp.where(lane == 2, d1, slab)
    slab = jnp.where(lane == 3, d2, slab)
    gs_ref[...] = slab
    for q in range(ROUTE_TM // LANE):
        blk_t = slab[q * LANE:(q + 1) * LANE, :].T
        rt_ref[:, q * LANE:(q + 1) * LANE] = blk_t[0:8, :].astype(I32)


def _route_count(logits):
    t = logits.shape[0]
    tm = ROUTE_TM
    return pl.pallas_call(
        _route_count_kernel,
        grid=(t // tm,),
        in_specs=[pl.BlockSpec((tm, LANE), lambda i: (i, 0))],
        out_specs=pl.BlockSpec((1, LANE), lambda i: (0, 0)),
        out_shape=jax.ShapeDtypeStruct((1, LANE), I32),
        scratch_shapes=[pltpu.VMEM((1, LANE), F32)],
        compiler_params=_cparams(("arbitrary",), 32),
        name="route_count",
    )(logits)


def _route_dest(logits, stri, pstart_row):
    t = logits.shape[0]
    tm = ROUTE_TM
    return pl.pallas_call(
        _route_dest_kernel,
        grid=(t // tm,),
        in_specs=[pl.BlockSpec((tm, LANE), lambda i: (i, 0)),
                  pl.BlockSpec((tm, tm), lambda i: (0, 0)),
                  pl.BlockSpec((1, LANE), lambda i: (0, 0))],
        out_specs=[pl.BlockSpec((tm, LANE), lambda i: (i, 0)),
                   pl.BlockSpec((8, tm), lambda i: (0, i))],
        out_shape=[jax.ShapeDtypeStruct((t, LANE), F32),
                   jax.ShapeDtypeStruct((8, t), I32)],
        scratch_shapes=[pltpu.VMEM((1, LANE), F32)],
        compiler_params=_cparams(("arbitrary",), 32),
        name="route_dest",
    )(logits, stri, pstart_row)


MOE_BLOCK = 256
DISPATCH_TM = 512
COMBINE_TM = 256


ROW_UNROLL = 8


def _dispatch_kernel(d1_ref, d2_ref, h_ref, xb_in_hbm, xb_hbm, sem):
    del xb_in_hbm
    tm = DISPATCH_TM

    def body(g, carry):
        for u in range(ROW_UNROLL):
            tt = g * ROW_UNROLL + u
            src = h_ref.at[pl.ds(tt, 1)]
            pltpu.make_async_copy(src, xb_hbm.at[pl.ds(d1_ref[0, 0, tt], 1)], sem).start()
            pltpu.make_async_copy(src, xb_hbm.at[pl.ds(d2_ref[0, 0, tt], 1)], sem).start()
        return carry

    lax.fori_loop(0, tm // ROW_UNROLL, body, 0)
    for _ in range(2):
        pltpu.make_async_copy(h_ref, xb_hbm.at[pl.ds(0, tm)], sem).wait()


def _dispatch(d1, d2, h2p, n_rows):
    t = h2p.shape[0]
    tm = DISPATCH_TM
    w = h2p.shape[1]
    smem = pl.BlockSpec((1, 1, tm), lambda i: (i, 0, 0), memory_space=pltpu.SMEM)
    anyspec = pl.BlockSpec(memory_space=pl.ANY)
    xb0 = jnp.zeros((n_rows, w), U32)
    return pl.pallas_call(
        _dispatch_kernel,
        grid=(t // tm,),
        in_specs=[smem, smem, pl.BlockSpec((tm, w), lambda i: (i, 0)), anyspec],
        out_specs=anyspec,
        out_shape=jax.ShapeDtypeStruct((n_rows, w), U32),
        scratch_shapes=[pltpu.SemaphoreType.DMA(())],
        input_output_aliases={3: 0},
        compiler_params=pltpu.CompilerParams(dimension_semantics=("arbitrary",), has_side_effects=True),
        name="moe_dispatch",
    )(d1, d2, h2p, xb0)


def _ffn_kernel(be_ref, nu_ref, xb_ref, w1_ref, w3_ref, w2_ref, yb_ref, w13_scr, w2_scr):
    i = pl.program_id(0)
    prev = be_ref[jnp.maximum(i - 1, 0)]
    changed = (i == 0) | (be_ref[i] != prev)
    used = i < nu_ref[0]

    @pl.when(changed & used)
    def _():
        w13_scr[:, :D_EXPERT] = w1_ref[0].astype(BF16)
        w13_scr[:, D_EXPERT:] = w3_ref[0].astype(BF16)
        w2_scr[...] = w2_ref[0].astype(BF16)

    @pl.when(used)
    def _():
        xw = xb_ref[...]
        lo = pltpu.bitcast(xw << 16, F32)
        hi = pltpu.bitcast(xw & jnp.uint32(0xFFFF0000), F32)
        x = jnp.concatenate([lo, hi], axis=1).astype(BF16)
        h = _dot(x, w13_scr[...])
        h1 = h[:, :D_EXPERT]
        h3 = h[:, D_EXPERT:]
        a = (h1 * _sigmoid(h1) * h3).astype(BF16)
        yb_ref[...] = _dot(a, w2_scr[...])

    @pl.when(jnp.logical_not(used))
    def _():
        yb_ref[...] = jnp.zeros_like(yb_ref)


def _ffn(blk_exp, n_used, xb, w1, w3, w2):
    n_rows = xb.shape[0]
    nblk = n_rows // MOE_BLOCK
    wmap = lambda i, be, nu: (be[i], 0, 0)
    grid_spec = pltpu.PrefetchScalarGridSpec(
        num_scalar_prefetch=2, grid=(nblk,),
        in_specs=[pl.BlockSpec((MOE_BLOCK, D_MODEL // 2), lambda i, be, nu: (i, 0)),
                  pl.BlockSpec((1, D_MODEL, D_EXPERT), wmap),
                  pl.BlockSpec((1, D_MODEL, D_EXPERT), wmap),
                  pl.BlockSpec((1, D_EXPERT, D_MODEL), wmap)],
        out_specs=pl.BlockSpec((MOE_BLOCK, D_MODEL), lambda i, be, nu: (i, 0)),
        scratch_shapes=[pltpu.VMEM((D_MODEL, 2 * D_EXPERT), BF16),
                        pltpu.VMEM((D_EXPERT, D_MODEL), BF16)])
    return pl.pallas_call(
        _ffn_kernel, grid_spec=grid_spec,
        out_shape=jax.ShapeDtypeStruct((n_rows, D_MODEL), F32),
        compiler_params=_cparams(("arbitrary",), 40),
        name="moe_ffn",
    )(blk_exp, n_used, xb, w1, w3, w2)


def _combine_kernel(d1_ref, d2_ref, d1n_ref, d2n_ref, yb_hbm, gs_ref, x1_ref, g2_ref, o_ref,
                    ya_scr, yb_scr, sem):
    tm = COMBINE_TM
    i = pl.program_id(0)
    slot = i % 2

    def issue(da_ref, db_ref, s):
        def body(g, carry):
            for u in range(ROW_UNROLL):
                tt = g * ROW_UNROLL + u
                pltpu.make_async_copy(yb_hbm.at[pl.ds(da_ref[0, 0, tt], 1)],
                                      ya_scr.at[s, pl.ds(tt, 1)], sem.at[s]).start()
                pltpu.make_async_copy(yb_hbm.at[pl.ds(db_ref[0, 0, tt], 1)],
                                      yb_scr.at[s, pl.ds(tt, 1)], sem.at[s]).start()
            return carry

        lax.fori_loop(0, tm // ROW_UNROLL, body, 0)

    @pl.when(i == 0)
    def _():
        issue(d1_ref, d2_ref, 0)

    @pl.when(i + 1 < pl.num_programs(0))
    def _():
        issue(d1n_ref, d2n_ref, 1 - slot)

    pltpu.make_async_copy(yb_hbm.at[pl.ds(0, tm)], ya_scr.at[slot], sem.at[slot]).wait()
    pltpu.make_async_copy(yb_hbm.at[pl.ds(0, tm)], yb_scr.at[slot], sem.at[slot]).wait()
    ga = gs_ref[:, 0:1]
    gb = gs_ref[:, 1:2]
    o_ref[...] = x1_ref[...] + g2_ref[0] * (ga * ya_scr[slot] + gb * yb_scr[slot])


def _combine(d1, d2, ybuf, gs, x1, mod3, seq_len):
    t = x1.shape[0]
    tm = COMBINE_TM
    nt = t // tm
    tiles_per_seq = seq_len // tm
    smem = pl.BlockSpec((1, 1, tm), lambda i: (i, 0, 0), memory_space=pltpu.SMEM)
    smem_next = pl.BlockSpec((1, 1, tm), lambda i: (jnp.minimum(i + 1, nt - 1), 0, 0), memory_space=pltpu.SMEM)
    return pl.pallas_call(
        _combine_kernel,
        grid=(nt,),
        in_specs=[smem, smem, smem_next, smem_next,
                  pl.BlockSpec(memory_space=pl.ANY),
                  pl.BlockSpec((tm, LANE), lambda i: (i, 0)),
                  pl.BlockSpec((tm, D_MODEL), lambda i: (i, 0)),
                  pl.BlockSpec((1, 1, D_MODEL), lambda i: ((i // tiles_per_seq) * N_MOD + 5, 0, 0))],
        out_specs=pl.BlockSpec((tm, D_MODEL), lambda i: (i, 0)),
        out_shape=jax.ShapeDtypeStruct((t, D_MODEL), F32),
        scratch_shapes=[pltpu.VMEM((2, tm, D_MODEL), F32), pltpu.VMEM((2, tm, D_MODEL), F32),
                        pltpu.SemaphoreType.DMA((2,))],
        compiler_params=_cparams(("arbitrary",), 32),
        name="moe_combine",
    )(d1, d2, d1, d2, ybuf, gs, x1, mod3)


def _rope_tables(seq_len):
    t = jnp.arange(seq_len, dtype=I32)
    row = (t // GRID_W).astype(F32)
    colp = (t % GRID_W).astype(F32)
    half = SSD_STATE // 2
    inv = ROPE_THETA ** (-jnp.arange(0, half, 2, dtype=F32) / half)
    ar = row[:, None] * inv
    ac = colp[:, None] * inv
    cos_t = jnp.concatenate([jnp.cos(ar), jnp.cos(ar), jnp.cos(ac), jnp.cos(ac)], axis=-1)
    sin_t = jnp.concatenate([-jnp.sin(ar), jnp.sin(ar), -jnp.sin(ac), jnp.sin(ac)], axis=-1)
    return cos_t, sin_t


def _scan_tables(rev):
    li = jnp.arange(SSD_CHUNK)[:, None]
    ui = jnp.arange(SSD_CHUNK)[None, :]
    tri = ((ui >= li) if rev else (ui <= li)).astype(BF16)
    col = SSD_HEADS if rev else 0
    j = jnp.arange(LANE)[:, None]
    c = jnp.arange(D_SSD)[None, :]
    ee = (j == col + c // SSD_HEADDIM).astype(BF16)
    return tri, ee


def _pad_lanes(v, width=LANE):
    v = v.reshape(1, -1)
    return jnp.pad(v, ((0, 0), (0, width - v.shape[1])))


def kernel(x, c, ctx, c_ctx, w_mod, b_mod, norm1_w, w_in, conv_w, conv_b, a_log_f, a_log_b, dt_bias_f, dt_bias_b, d_skip, ssd_norm_w, q_norm_w, k_norm_w, rpb, w_br_ssd, w_br_na, w_out, norm2_w, w_grp, b_grp, w_rt, b_rt, w1, w3, w2):
    nb, seq_len, d = x.shape
    ctx_len = ctx.shape[1]
    t = nb * seq_len
    tc = nb * ctx_len
    assert w_mod.shape[0] == 1 and d == D_MODEL and nb <= 7
    assert seq_len % 256 == 0 and ctx_len % SSD_CHUNK == 0 and seq_len // GRID_W >= NA_KH

    cin = jnp.concatenate([c, c_ctx[None, :], jnp.zeros((8 - nb - 1, d), F32)], axis=0)
    mod = _modulation(cin, w_mod[0], b_mod[0])
    mod3 = mod.reshape(8 * N_MOD, 1, D_MODEL)

    wi = w_in[0]
    o_z, o_xbc, o_dt = 0, D_SSD, D_SSD + D_XBC
    o_qkv = o_dt + 2 * SSD_HEADS
    o_g = o_qkv + 3 * D_NA
    w_z = wi[:, o_z:o_xbc]
    w_xbc = wi[:, o_xbc:o_dt]
    w_dt = wi[:, o_dt:o_qkv]
    w_qkv = wi[:, o_qkv:o_g]
    w_g = wi[:, o_g:]
    w_lat = jnp.concatenate([w_xbc, w_z, w_g, w_qkv], axis=1).astype(BF16)
    w_ctx = jnp.concatenate([w_xbc, w_qkv[:, D_NA:]], axis=1).astype(BF16)
    w_dtp = jnp.pad(w_dt, ((0, 0), (0, LANE - 2 * SSD_HEADS))).astype(BF16)
    n1w = norm1_w[0].reshape(1, D_MODEL)

    x2 = x.reshape(t, D_MODEL)
    ctx2 = ctx.reshape(tc, D_MODEL)
    tm_in = 1024 if seq_len % 1024 == 0 else 256
    tiles = seq_len // tm_in
    big, dt_raw = _inproj(x2, mod3, lambda i: i // tiles, n1w, w_lat, w_dtp, tm_in, 1024)
    tm_c = 256 if tc % 256 == 0 else SSD_CHUNK
    bigc, dtc_raw = _inproj(ctx2, mod3, lambda i: nb, n1w, w_ctx, w_dtp, tm_c, 1024)
    ZCOL, GCOL, QCOL, KCOL, VCOL = 2, 3, 8, 9, 10
    KC_COL, VC_COL = 4, 5

    dtbias = _pad_lanes(jnp.concatenate([dt_bias_f[0], dt_bias_b[0]]))
    cw = conv_w[0]
    cbias = conv_b[0].reshape(1, D_XBC)
    cos_t, sin_t = _rope_tables(seq_len)
    xs, bc, dts = _ssd_prep(big, dt_raw, cw, cbias, dtbias, cos_t, sin_t, seq_len, 256)
    ctl = 256 if ctx_len % 256 == 0 else SSD_CHUNK
    ones_t = jnp.ones((ctx_len, LANE), F32)
    zeros_t = jnp.zeros((ctx_len, LANE), F32)
    xsc, bcc, dtsc = _ssd_prep(bigc, dtc_raw, cw, cbias, dtbias, ones_t, zeros_t, ctx_len, ctl)

    alog = _pad_lanes(jnp.concatenate([a_log_f[0], a_log_b[0]]))
    dskip = jnp.repeat(d_skip[0], SSD_HEADDIM).reshape(1, D_SSD)
    tri_f, ee_f = _scan_tables(False)
    tri_b, ee_b = _scan_tables(True)
    h_zero = jnp.zeros((nb, SSD_STATE, D_SSD), F32)
    _, hcf = _ssd_scan(xsc, bcc, dtsc, alog, dskip, tri_f, ee_f, h_zero, nb, False)
    _, hcb = _ssd_scan(xsc, bcc, dtsc, alog, dskip, tri_b, ee_b, h_zero, nb, True)
    yf, _ = _ssd_scan(xs, bc, dts, alog, dskip, tri_f, ee_f, hcf, nb, False)
    yb, _ = _ssd_scan(xs, bc, dts, alog, dskip, tri_b, ee_b, hcb, nb, True)

    qw = jnp.tile(q_norm_w[0], NA_HEADS).reshape(1, D_NA)
    kw = jnp.tile(k_norm_w[0], NA_HEADS).reshape(1, D_NA)
    gi = jnp.arange(LANE)
    gmat = ((gi[:, None] // NA_HEADDIM) == (gi[None, :] // NA_HEADDIM)).astype(BF16) * (1.0 / NA_HEADDIM)
    gmat = gmat.astype(BF16)
    qn, kn = _na_prep(big, QCOL, KCOL, qw, kw, gmat, 512 if t % 512 == 0 else 256)
    kcn = _na_prep(bigc, None, KC_COL, qw, kw, gmat, 256 if tc % 256 == 0 else SSD_CHUNK)
    bias_tab = _na_bias_table(rpb[0])
    y_na = _neigh_attention(qn, kn, big, kcn, bigc, bias_tab, nb, seq_len, ctx_len, VCOL, VC_COL)

    w_r = jnp.pad(jnp.concatenate([w_rt[0], w_grp[0]], axis=1), ((0, 0), (0, LANE - N_EXPERTS - N_GROUPS)))
    wrh = w_r.astype(BF16)
    wrl = (w_r - wrh.astype(F32)).astype(BF16)
    br = _pad_lanes(jnp.concatenate([b_rt[0], b_grp[0]]))
    x1, h2p, logits = _merge(yf, yb, big, y_na, x2, mod3,
                             ssd_norm_w[0].reshape(1, D_SSD), norm2_w[0].reshape(1, D_MODEL),
                             w_br_ssd[0].astype(BF16), w_br_na[0].astype(BF16), w_out[0].astype(BF16),
                             wrh, wrl, br, seq_len, 256, ZCOL, GCOL)

    si = jnp.arange(ROUTE_TM)
    stri = (si[None, :] < si[:, None]).astype(BF16)
    cnt = _route_count(logits)

    counts = cnt[0, :N_EXPERTS]
    padded = (counts + MOE_BLOCK - 1) // MOE_BLOCK * MOE_BLOCK
    pend = jnp.cumsum(padded)
    pstart = pend - padded
    nblk = (2 * t + N_EXPERTS * (MOE_BLOCK - 1) + MOE_BLOCK - 1) // MOE_BLOCK
    n_used = (pend[-1] // MOE_BLOCK).astype(I32).reshape(1)
    blk_row0 = jnp.arange(nblk, dtype=I32) * MOE_BLOCK
    blk_exp = jnp.minimum(jnp.sum((pend[None, :] <= blk_row0[:, None]).astype(I32), axis=1), N_EXPERTS - 1)
    gs, rt = _route_dest(logits, stri, _pad_lanes(pstart.astype(F32)))

    def tok_tiles(row, tm):
        return rt[row].reshape(t // tm, 1, tm)

    xb = _dispatch(tok_tiles(2, DISPATCH_TM), tok_tiles(3, DISPATCH_TM), h2p, nblk * MOE_BLOCK)
    ybuf = _ffn(blk_exp, n_used, xb, w1[0], w3[0], w2[0])
    out = _combine(tok_tiles(2, COMBINE_TM), tok_tiles(3, COMBINE_TM), ybuf, gs, x1, mod3, seq_len)
    return out.reshape(nb, seq_len, D_MODEL)
```

```python
import functools
import math

import jax
import jax.numpy as jnp
from jax import lax
from jax.experimental import pallas as pl
from jax.experimental.pallas import tpu as pltpu

F32 = jnp.float32
BF16 = jnp.bfloat16
I32 = jnp.int32
U32 = jnp.uint32

D_MODEL = 1024
GRID_W = 64
D_SSD = 2048
SSD_HEADDIM = 64
SSD_HEADS = 32
SSD_GROUPS = 8
HEADS_PER_GROUP = 4
SSD_STATE = 128
SSD_CHUNK = 128
CONV_K = 5
D_BC = SSD_GROUPS * SSD_STATE
D_XBC = D_SSD + 2 * D_BC
NA_HEADDIM = 64
NA_HEADS = 16
D_NA = 1024
NA_KH = 8
NA_KW = 16
ROPE_THETA = 10000.0
N_GROUPS = 4
EXPERTS_PER_GROUP = 8
N_EXPERTS = 32
D_EXPERT = 512
N_MOD = 6
EPS = 1e-6
NEG_BIG = -1e30

COLS_LAT = D_XBC + D_SSD + 2 * D_MODEL + 3 * D_NA
COLS_CTX = D_XBC + 2 * D_NA
LANE = 128

V7X_VMEM_BYTES = 64 * 1024 * 1024


def _cparams(sem, vmem_mb):
    return pltpu.CompilerParams(dimension_semantics=sem, vmem_limit_bytes=vmem_mb * 1024 * 1024)


def _sigmoid(x):
    return 1.0 / (1.0 + jnp.exp(-x))


def _split2(x):
    hi = x.astype(BF16)
    lo = (x - hi.astype(F32)).astype(BF16)
    return hi, lo


def _split3(x):
    hi = x.astype(BF16)
    r = x - hi.astype(F32)
    mid = r.astype(BF16)
    lo = (r - mid.astype(F32)).astype(BF16)
    return hi, mid, lo


def _dot(a, b):
    return jnp.dot(a, b, preferred_element_type=F32)


def _dot_nt(a, b):
    return lax.dot_general(a, b, (((1,), (1,)), ((), ())), preferred_element_type=F32)


def _dot_tn(a, b):
    return lax.dot_general(a, b, (((0,), (0,)), ((), ())), preferred_element_type=F32)


def _mod_kernel(c_ref, w_ref, b_ref, o_ref):
    c = c_ref[...]
    a = (c * _sigmoid(c)).astype(BF16)
    o_ref[...] = _dot(a, w_ref[...].astype(BF16)) + b_ref[...]


def _modulation(cin, w_mod, b_mod):
    n = w_mod.shape[1]
    tn = 1536
    return pl.pallas_call(
        _mod_kernel,
        grid=(n // tn,),
        in_specs=[pl.BlockSpec((8, D_MODEL), lambda j: (0, 0)),
                  pl.BlockSpec((D_MODEL, tn), lambda j: (0, j)),
                  pl.BlockSpec((1, tn), lambda j: (0, j))],
        out_specs=pl.BlockSpec((8, tn), lambda j: (0, j)),
        out_shape=jax.ShapeDtypeStruct((8, n), F32),
        compiler_params=_cparams(("arbitrary",), 40),
        name="modulation",
    )(cin, w_mod, b_mod.reshape(1, n))


def _inproj_kernel(x_ref, sh_ref, sc_ref, nw_ref, w_ref, wdt_ref, o_ref, dt_ref, h_scr, *, tm):
    j = pl.program_id(1)

    @pl.when(j == 0)
    def _():
        scale = 1.0 + sc_ref[0]
        shift = sh_ref[0]
        nw = nw_ref[...]

        def body(r, carry):
            rows = pl.ds(pl.multiple_of(r * 128, 128), 128)
            x = x_ref[rows, :]
            ms = jnp.mean(x * x, axis=-1, keepdims=True)
            h = x * lax.rsqrt(ms + EPS) * nw * scale + shift
            h_scr[rows, :] = h.astype(BF16)
            return carry

        lax.fori_loop(0, tm // 128, body, 0)
        dt_ref[...] = _dot(h_scr[...], wdt_ref[...])

    o_ref[...] = _dot(h_scr[...], w_ref[...]).astype(BF16)


def _inproj(x2, mod3, mod_row_fn, norm_w, w, wdt, tm, tn):
    t = x2.shape[0]
    n = w.shape[1]
    kern = functools.partial(_inproj_kernel, tm=tm)
    return pl.pallas_call(
        kern,
        grid=(t // tm, n // tn),
        in_specs=[pl.BlockSpec((tm, D_MODEL), lambda i, j: (i, 0)),
                  pl.BlockSpec((1, 1, D_MODEL), lambda i, j: (mod_row_fn(i) * N_MOD + 0, 0, 0)),
                  pl.BlockSpec((1, 1, D_MODEL), lambda i, j: (mod_row_fn(i) * N_MOD + 1, 0, 0)),
                  pl.BlockSpec((1, D_MODEL), lambda i, j: (0, 0)),
                  pl.BlockSpec((D_MODEL, tn), lambda i, j: (0, j)),
                  pl.BlockSpec((D_MODEL, LANE), lambda i, j: (0, 0))],
        out_specs=[pl.BlockSpec((tm, tn), lambda i, j: (i, j)),
                   pl.BlockSpec((tm, LANE), lambda i, j: (i, 0))],
        out_shape=[jax.ShapeDtypeStruct((t, n), BF16),
                   jax.ShapeDtypeStruct((t, LANE), F32)],
        scratch_shapes=[pltpu.VMEM((tm, D_MODEL), BF16)],
        compiler_params=_cparams(("arbitrary", "arbitrary"), 48),
        name="inproj",
    )(x2, mod3, mod3, norm_w, w, wdt)


CONV_OFFSETS = (-2, -1, 1, 2)


def _ssd_prep_kernel(prev_ref, cur_ref, next_ref, dt_ref, cw_ref, cb_ref, dtbias_ref, cos_ref, sin_ref,
                     shift_ref, etop_ref, ebot_ref, xs_ref, bc_ref, dto_ref, *, tl, tiles_per_seq):
    i = pl.program_id(0)
    first = (i % tiles_per_seq) == 0
    last = (i % tiles_per_seq) == tiles_per_seq - 1
    lane = lax.broadcasted_iota(I32, (tl, LANE), 1)
    lo_half = ((lane // 32) % 2) == 0
    shift = shift_ref[...]
    etop = etop_ref[...]
    ebot = ebot_ref[...]
    cosv = cos_ref[...]
    sinv = sin_ref[...]
    cwid = 256
    for c in range(D_XBC // cwid):
        cs = slice(c * cwid, (c + 1) * cwid)
        w = cw_ref[:, cs]
        xc = cur_ref[:, cs]
        sh = _dot(shift, xc).reshape(tl // 8, len(CONV_OFFSETS), 8, cwid)
        tap = lambda k: sh[:, k].reshape(tl, cwid)
        acc = cb_ref[:, cs] + xc.astype(F32) * w[2:3, :]
        acc = acc + tap(0) * w[0:1, :] + tap(1) * w[1:2, :] + tap(2) * w[3:4, :] + tap(3) * w[4:5, :]
        top = _dot(etop, prev_ref[:, cs])
        bot = _dot(ebot, next_ref[:, cs])
        top_c = jnp.where(first, 0.0, top[0:8] * w[0:1, :] + top[8:16] * w[1:2, :])
        bot_c = jnp.where(last, 0.0, bot[0:8] * w[3:4, :] + bot[8:16] * w[4:5, :])
        acc = jnp.concatenate([acc[0:8] + top_c, acc[8:tl - 8], acc[tl - 8:tl] + bot_c], axis=0)
        y = acc * _sigmoid(acc)
        if c * cwid < D_SSD:
            xs_ref[:, cs] = y.astype(BF16)
        else:
            for g in range(cwid // LANE):
                yg = y[:, g * LANE:(g + 1) * LANE]
                partner = jnp.where(lo_half, pltpu.roll(yg, 96, 1), pltpu.roll(yg, 32, 1))
                o = yg * cosv + partner * sinv
                c0 = c * cwid - D_SSD + g * LANE
                bc_ref[:, c0:c0 + LANE] = o.astype(BF16)

    v = dt_ref[...] + dtbias_ref[...]
    dto_ref[...] = jnp.maximum(v, 0.0) + jnp.log(1.0 + jnp.exp(-jnp.abs(v)))


def _conv_shift_tables(tl):
    t = jnp.arange(tl)[:, None]
    u = jnp.arange(tl)[None, :]
    shift = jnp.stack([(u == t + off) for off in CONV_OFFSETS], axis=0).reshape(len(CONV_OFFSETS), tl // 8, 8, tl)
    shift = shift.transpose(1, 0, 2, 3).reshape(len(CONV_OFFSETS) * tl, tl).astype(BF16)
    r = jnp.arange(16)[:, None]
    q = jnp.arange(16)[None, :]
    etop = (((r < 8) & (q == r + 14)) | ((r >= 8) & (q == r - 8 + 15))).astype(BF16)
    ebot = (((r < 8) & (q == r - 7)) | ((r >= 8) & (q == r - 8 - 6))).astype(BF16)
    return shift, etop, ebot


def _ssd_prep(big, dt_raw, conv_w, conv_b, dtbias, cos_t, sin_t, seq_len, tl):
    t = big.shape[0]
    tps = seq_len // tl
    hb = tl // 16
    nhalo = t // 16
    kern = functools.partial(_ssd_prep_kernel, tl=tl, tiles_per_seq=tps)
    shift, etop, ebot = _conv_shift_tables(tl)
    const = lambda i: (0, 0)
    return pl.pallas_call(
        kern,
        grid=(t // tl,),
        in_specs=[pl.BlockSpec((16, D_XBC), lambda i: (jnp.maximum(i * hb - 1, 0), 0)),
                  pl.BlockSpec((tl, D_XBC), lambda i: (i, 0)),
                  pl.BlockSpec((16, D_XBC), lambda i: (jnp.minimum((i + 1) * hb, nhalo - 1), 0)),
                  pl.BlockSpec((tl, LANE), lambda i: (i, 0)),
                  pl.BlockSpec((CONV_K, D_XBC), const),
                  pl.BlockSpec((1, D_XBC), const),
                  pl.BlockSpec((1, LANE), const),
                  pl.BlockSpec((tl, LANE), lambda i: (i % tps, 0)),
                  pl.BlockSpec((tl, LANE), lambda i: (i % tps, 0)),
                  pl.BlockSpec((len(CONV_OFFSETS) * tl, tl), const),
                  pl.BlockSpec((16, 16), const),
                  pl.BlockSpec((16, 16), const)],
        out_specs=[pl.BlockSpec((tl, D_SSD), lambda i: (i, 0)),
                   pl.BlockSpec((tl, 2 * D_BC), lambda i: (i, 0)),
                   pl.BlockSpec((tl, LANE), lambda i: (i, 0))],
        out_shape=[jax.ShapeDtypeStruct((t, D_SSD), BF16),
                   jax.ShapeDtypeStruct((t, 2 * D_BC), BF16),
                   jax.ShapeDtypeStruct((t, LANE), F32)],
        compiler_params=_cparams(("arbitrary",), 40),
        name="ssd_prep",
    )(big, big, big, dt_raw, conv_w, conv_b, dtbias, cos_t, sin_t, shift, etop, ebot)


def _ssd_scan_kernel(xs_ref, bc_ref, dt_ref, alog_ref, dskip_ref, tri_ref, ee_ref, h0_ref,
                     y_ref, hfin_ref, h_scr, *, rev, nc):
    k = pl.program_id(1)
    L = SSD_CHUNK

    @pl.when(k == 0)
    def _():
        h_scr[...] = h0_ref[0]

    col = SSD_HEADS if rev else 0
    edge = 0 if rev else L - 1
    a_row = -jnp.exp(alog_ref[...])
    dt = dt_ref[...]
    trib = tri_ref[...]
    tri = trib > 0.5
    d_hi, d_mid, d_lo = _split3(dt * a_row)
    cum = _dot(trib, d_hi) + _dot(trib, d_mid) + _dot(trib, d_lo)
    tot = cum[edge:edge + 1, :]
    cum_t = cum.T
    dt_t = dt.T

    ee = ee_ref[...]
    oscale = _dot(jnp.exp(cum).astype(BF16), ee)
    wscale = _dot((jnp.exp(tot - cum) * dt).astype(BF16), ee)
    t_hi, t_lo = _split2(jnp.broadcast_to(jnp.exp(tot), (8, LANE)))
    etot = (_dot(t_hi, ee) + _dot(t_lo, ee))[0:1, :]

    gw = HEADS_PER_GROUP * SSD_HEADDIM
    lane_head = lax.broadcasted_iota(I32, (L, gw), 1) // SSD_HEADDIM

    def b_of(g):
        return bc_ref[:, g * SSD_STATE:(g + 1) * SSD_STATE]

    def c_of(g):
        return bc_ref[:, D_BC + g * SSD_STATE:D_BC + (g + 1) * SSD_STATE]

    def operands(g, cb):
        gs = slice(g * gw, (g + 1) * gw)
        xs_g = xs_ref[:, gs]
        ms = []
        for r in range(HEADS_PER_GROUP):
            hh = col + g * HEADS_PER_GROUP + r
            seg = cum[:, hh:hh + 1] - cum_t[hh:hh + 1, :]
            lm = jnp.exp(jnp.where(tri, seg, NEG_BIG)) * dt_t[hh:hh + 1, :]
            ms.append((cb * lm).astype(BF16))
        mcat = jnp.concatenate(ms, axis=1)
        zero = jnp.zeros_like(xs_g)
        bd = jnp.concatenate([jnp.where(lane_head == r, xs_g, zero) for r in range(HEADS_PER_GROUP)], axis=0)
        xs_f = xs_g.astype(F32)
        xw = (xs_f * wscale[:, gs]).astype(BF16)
        return mcat, bd, xs_f, xw

    cbs = [_dot_nt(c_of(g), b_of(g)) for g in range(SSD_GROUPS)]
    nxt = operands(0, cbs[0])
    for g in range(SSD_GROUPS):
        gs = slice(g * gw, (g + 1) * gw)
        mcat, bd, xs_f, xw = nxt
        h_g = h_scr[:, gs]
        y_diag = _dot(mcat, bd)
        y_off = _dot(c_of(g), h_g.astype(BF16))
        loc = _dot_tn(b_of(g), xw)
        if g + 1 < SSD_GROUPS:
            nxt = operands(g + 1, cbs[g + 1])
        y = y_diag + oscale[:, gs] * y_off
        if not rev:
            y = y + xs_f * dskip_ref[:, gs]
        y_ref[:, gs] = y.astype(BF16)
        h_scr[:, gs] = etot[:, gs] * h_g + loc

    @pl.when(k == nc - 1)
    def _():
        hfin_ref[0] = h_scr[...]


def _ssd_scan(xs, bc, dt, alog_row, dskip_row, tri, ee, h0, nb, rev):
    t = xs.shape[0]
    nc = t // nb // SSD_CHUNK
    kern = functools.partial(_ssd_scan_kernel, rev=rev, nc=nc)

    def rowmap(b, k):
        c = (nc - 1 - k) if rev else k
        return (b * nc + c, 0)

    const2 = lambda b, k: (0, 0)
    return pl.pallas_call(
        kern,
        grid=(nb, nc),
        in_specs=[pl.BlockSpec((SSD_CHUNK, D_SSD), rowmap),
                  pl.BlockSpec((SSD_CHUNK, 2 * D_BC), rowmap),
                  pl.BlockSpec((SSD_CHUNK, LANE), rowmap),
                  pl.BlockSpec((1, LANE), const2),
                  pl.BlockSpec((1, D_SSD), const2),
                  pl.BlockSpec((SSD_CHUNK, SSD_CHUNK), const2),
                  pl.BlockSpec((LANE, D_SSD), const2),
                  pl.BlockSpec((1, SSD_STATE, D_SSD), lambda b, k: (b, 0, 0))],
        out_specs=[pl.BlockSpec((SSD_CHUNK, D_SSD), rowmap),
                   pl.BlockSpec((1, SSD_STATE, D_SSD), lambda b, k: (b, 0, 0))],
        out_shape=[jax.ShapeDtypeStruct((t, D_SSD), BF16),
                   jax.ShapeDtypeStruct((nb, SSD_STATE, D_SSD), F32)],
        scratch_shapes=[pltpu.VMEM((SSD_STATE, D_SSD), F32)],
        compiler_params=_cparams(("arbitrary", "arbitrary"), 40),
        name="ssd_scan_bwd" if rev else "ssd_scan_fwd",
    )(xs, bc, dt, alog_row, dskip_row, tri, ee, h0)


def _headnorm_cols(src_ref, w_ref, g, dst_ref, scale):
    for c in range(D_NA // LANE):
        cs = slice(c * LANE, (c + 1) * LANE)
        x = src_ref[:, cs].astype(F32)
        hi, lo = _split2(x * x)
        ms = _dot(hi, g) + _dot(lo, g)
        y = x * lax.rsqrt(ms + EPS) * w_ref[:, cs]
        if scale is not None:
            y = y * scale
        dst_ref[:, cs] = y.astype(BF16)


def _na_prep_qk_kernel(q_ref, k_ref, qw_ref, kw_ref, g_ref, qo_ref, ko_ref):
    g = g_ref[...]
    _headnorm_cols(q_ref, qw_ref, g, qo_ref, NA_HEADDIM ** -0.5)
    _headnorm_cols(k_ref, kw_ref, g, ko_ref, None)


def _na_prep_k_kernel(k_ref, kw_ref, g_ref, ko_ref):
    _headnorm_cols(k_ref, kw_ref, g_ref[...], ko_ref, None)


def _na_prep(big, qcol, kcol, qw, kw, gmat, tm):
    t = big.shape[0]
    blk = lambda cidx: pl.BlockSpec((tm, D_NA), lambda i: (i, cidx))
    row = pl.BlockSpec((1, D_NA), lambda i: (0, 0))
    gspec = pl.BlockSpec((LANE, LANE), lambda i: (0, 0))
    out = pl.BlockSpec((tm, D_NA), lambda i: (i, 0))
    if qcol is None:
        return pl.pallas_call(
            _na_prep_k_kernel, grid=(t // tm,),
            in_specs=[blk(kcol), row, gspec], out_specs=out,
            out_shape=jax.ShapeDtypeStruct((t, D_NA), BF16),
            compiler_params=_cparams(("arbitrary",), 32), name="na_prep_ctx",
        )(big, kw, gmat)
    return pl.pallas_call(
        _na_prep_qk_kernel, grid=(t // tm,),
        in_specs=[blk(qcol), blk(kcol), row, row, gspec], out_specs=[out, out],
        out_shape=[jax.ShapeDtypeStruct((t, D_NA), BF16)] * 2,
        compiler_params=_cparams(("arbitrary",), 32), name="na_prep",
    )(big, big, qw, kw, gmat)


NA_LOOKAHEAD = 3


def _na_kernel(q_ref, k_ref, v_ref, kc_ref, vc_ref, bias_ref, o_ref, *, rows):
    r = pl.program_id(1)
    rs = jnp.clip(r - NA_KH // 2, 0, rows - NA_KH)
    start = pl.multiple_of(rs * GRID_W, GRID_W)
    nk = NA_KH * GRID_W
    lane = lax.broadcasted_iota(I32, (GRID_W, LANE), 1)
    first_head = lane < NA_HEADDIM
    npair = NA_HEADS // 2

    def scores(j):
        cs = slice(j * LANE, (j + 1) * LANE)
        qp = q_ref[:, cs]
        zero = jnp.zeros_like(qp)
        qs = jnp.concatenate([jnp.where(first_head, qp, zero), jnp.where(first_head, zero, qp)], axis=0)
        kb = k_ref[pl.ds(start, nk), cs]
        s_loc = _dot_nt(qs, kb) + bias_ref[0, j * LANE:(j + 1) * LANE, :]
        s_ctx = _dot_nt(qs, kc_ref[:, cs])
        return s_loc, s_ctx

    pending = [scores(j) for j in range(NA_LOOKAHEAD)]
    for j in range(npair):
        cs = slice(j * LANE, (j + 1) * LANE)
        s_loc, s_ctx = pending.pop(0)
        if j + NA_LOOKAHEAD < npair:
            pending.append(scores(j + NA_LOOKAHEAD))
        vb = v_ref[pl.ds(start, nk), cs]
        m = jnp.maximum(jnp.max(s_loc, axis=-1, keepdims=True), jnp.max(s_ctx, axis=-1, keepdims=True))
        p_loc = jnp.exp(s_loc - m)
        p_ctx = jnp.exp(s_ctx - m)
        den = jnp.sum(p_loc, axis=-1, keepdims=True) + jnp.sum(p_ctx, axis=-1, keepdims=True)
        o = _dot(p_loc.astype(BF16), vb) + _dot(p_ctx.astype(BF16), vc_ref[:, cs])
        o = o / den
        o_ref[:, cs] = jnp.where(first_head, o[:GRID_W], o[GRID_W:]).astype(BF16)


def _neigh_attention(qn, kn, big, kcn, bigc, bias_tab, nb, seq_len, ctx_len, vcol, vccol):
    t = qn.shape[0]
    rows = seq_len // GRID_W
    kern = functools.partial(_na_kernel, rows=rows)

    def pat(b, r):
        return (r - jnp.clip(r - NA_KH // 2, 0, rows - NA_KH), 0, 0)

    return pl.pallas_call(
        kern,
        grid=(nb, rows),
        in_specs=[pl.BlockSpec((GRID_W, D_NA), lambda b, r: (b * rows + r, 0)),
                  pl.BlockSpec((seq_len, D_NA), lambda b, r: (b, 0)),
                  pl.BlockSpec((seq_len, D_NA), lambda b, r: (b, vcol)),
                  pl.BlockSpec((ctx_len, D_NA), lambda b, r: (b, 0)),
                  pl.BlockSpec((ctx_len, D_NA), lambda b, r: (b, vccol)),
                  pl.BlockSpec((1, NA_HEADS * GRID_W, NA_KH * GRID_W), pat)],
        out_specs=pl.BlockSpec((GRID_W, D_NA), lambda b, r: (b * rows + r, 0)),
        out_shape=jax.ShapeDtypeStruct((t, D_NA), BF16),
        compiler_params=_cparams(("arbitrary", "arbitrary"), 56),
        name="neigh_attention",
    )(qn, kn, big, kcn, bigc, bias_tab)


def _na_bias_kernel(rpb_ref, o_ref):
    rp = rpb_ref[0]
    r64 = pltpu.roll(rp, GRID_W, 1)
    c = lax.broadcasted_iota(I32, (GRID_W, LANE), 0)
    kc = lax.broadcasted_iota(I32, (GRID_W, LANE), 1) % GRID_W
    cs = jnp.clip(c - NA_KW // 2, 0, GRID_W - NA_KW)
    valid = (kc >= cs) & (kc < cs + NA_KW)
    pair = []
    for d in range(2 * NA_KH - 2):
        vec = rp[d:d + 1, :] + r64[d + 1:d + 2, :]
        w = pltpu.roll(jnp.broadcast_to(vec, (GRID_W, LANE)), LANE - (NA_KW - 1), 1, stride=1, stride_axis=0)
        pair.append(jnp.where(valid, w, NEG_BIG))
    for p in range(NA_KH):
        for ii in range(NA_KH // 2):
            o_ref[p, :, ii * LANE:(ii + 1) * LANE] = pair[2 * ii - p + NA_KH - 1]


def _na_bias_table(rpb):
    rp = jnp.pad(rpb, ((0, 0), (0, 1), (0, LANE - (2 * NA_KW - 1))))
    return pl.pallas_call(
        _na_bias_kernel,
        grid=(NA_HEADS,),
        in_specs=[pl.BlockSpec((1, 2 * NA_KH, LANE), lambda h: (h, 0, 0))],
        out_specs=pl.BlockSpec((NA_KH, GRID_W, NA_KH * GRID_W), lambda h: (0, h, 0)),
        out_shape=jax.ShapeDtypeStruct((NA_KH, NA_HEADS * GRID_W, NA_KH * GRID_W), F32),
        compiler_params=_cparams(("arbitrary",), 32),
        name="na_bias",
    )(rp)


MERGE_SUB = 128


def _merge_kernel(yf_ref, yb_ref, z_ref, gt_ref, yna_ref, x_ref, g1_ref, sh2_ref, sc2_ref, snw_ref, n2w_ref,
                  wbs_ref, wbn_ref, wo_ref, wrh_ref, wrl_ref, br_ref, x1_ref, h2p_ref, lg_ref):
    tm = x_ref.shape[0]
    subs = [slice(r * MERGE_SUB, (r + 1) * MERGE_SUB) for r in range(tm // MERGE_SUB)]
    half = D_MODEL // 2
    yn = []
    for rs in subs:
        z = z_ref[rs, :].astype(F32)
        y = (yf_ref[rs, :] + yb_ref[rs, :]).astype(F32) * (z * _sigmoid(z))
        ms = jnp.mean(y * y, axis=-1, keepdims=True)
        yn.append((y * lax.rsqrt(ms + EPS) * snw_ref[...]).astype(BF16))
    ab = [(_dot(yn[r], wbs_ref[...]), _dot(yna_ref[rs, :], wbn_ref[...])) for r, rs in enumerate(subs)]
    merged = []
    for r, rs in enumerate(subs):
        g_ssd = gt_ref[rs, :D_MODEL].astype(F32)
        g_na = gt_ref[rs, D_MODEL:].astype(F32)
        merged.append((_sigmoid(g_ssd) * ab[r][0] + _sigmoid(g_na) * ab[r][1]).astype(BF16))
    mo = [_dot(m, wo_ref[...]) for m in merged]
    hs = []
    for r, rs in enumerate(subs):
        x1 = x_ref[rs, :] + g1_ref[0] * mo[r]
        x1_ref[rs, :] = x1
        ms2 = jnp.mean(x1 * x1, axis=-1, keepdims=True)
        h2 = x1 * lax.rsqrt(ms2 + EPS) * n2w_ref[...] * (1.0 + sc2_ref[0]) + sh2_ref[0]
        h_hi, h_lo = _split2(h2)
        hs.append((h_hi, h_lo))
        bits = pltpu.bitcast(h_hi.astype(F32), U32)
        h2p_ref[rs, :] = (bits[:, :half] >> 16) | bits[:, half:]
    for r, rs in enumerate(subs):
        h_hi, h_lo = hs[r]
        lg_ref[rs, :] = (_dot(h_hi, wrh_ref[...]) + _dot(h_lo, wrh_ref[...]) + _dot(h_hi, wrl_ref[...])) + br_ref[...]


def _merge(yf, yb, big, yna, x2, mod3, snw, n2w, wbs, wbn, wo, wrh, wrl, br, seq_len, tm, zcol, gcol):
    t = x2.shape[0]
    tiles_per_seq = seq_len // tm
    modspec = lambda kidx: pl.BlockSpec((1, 1, D_MODEL), lambda i: ((i // tiles_per_seq) * N_MOD + kidx, 0, 0))
    full = lambda shp: pl.BlockSpec(shp, lambda i: (0,) * len(shp))
    return pl.pallas_call(
        _merge_kernel,
        grid=(t // tm,),
        in_specs=[pl.BlockSpec((tm, D_SSD), lambda i: (i, 0)),
                  pl.BlockSpec((tm, D_SSD), lambda i: (i, 0)),
                  pl.BlockSpec((tm, D_SSD), lambda i: (i, zcol)),
                  pl.BlockSpec((tm, 2 * D_MODEL), lambda i: (i, gcol)),
                  pl.BlockSpec((tm, D_NA), lambda i: (i, 0)),
                  pl.BlockSpec((tm, D_MODEL), lambda i: (i, 0)),
                  modspec(2), modspec(3), modspec(4),
                  full((1, D_SSD)), full((1, D_MODEL)),
                  full((D_SSD, D_MODEL)), full((D_NA, D_MODEL)), full((D_MODEL, D_MODEL)),
                  full((D_MODEL, LANE)), full((D_MODEL, LANE)), full((1, LANE))],
        out_specs=[pl.BlockSpec((tm, D_MODEL), lambda i: (i, 0)),
                   pl.BlockSpec((tm, D_MODEL // 2), lambda i: (i, 0)),
                   pl.BlockSpec((tm, LANE), lambda i: (i, 0))],
        out_shape=[jax.ShapeDtypeStruct((t, D_MODEL), F32),
                   jax.ShapeDtypeStruct((t, D_MODEL // 2), U32),
                   jax.ShapeDtypeStruct((t, LANE), F32)],
        compiler_params=_cparams(("arbitrary",), 48),
        name="merge",
    )(yf, yb, big, big, yna, x2, mod3, mod3, mod3, snw, n2w, wbs, wbn, wo, wrh, wrl, br)


ROUTE_TM = 512
GRP_LANE0 = N_EXPERTS


def _route_topk(lg):
    tm = lg.shape[0]
    lane = lax.broadcasted_iota(I32, (tm, LANE), 1)
    neg_inf = jnp.float32(-jnp.inf)
    big_lane = jnp.int32(4 * LANE)
    is_grp = (lane >= GRP_LANE0) & (lane < GRP_LANE0 + N_GROUPS)
    gl = jnp.where(is_grp, lg, neg_inf)
    gmax = jnp.max(gl, axis=-1, keepdims=True)
    grp = jnp.min(jnp.where(gl == gmax, lane, big_lane), axis=-1, keepdims=True) - GRP_LANE0
    psum = jnp.sum(jnp.where(is_grp, jnp.exp(lg - gmax), 0.0), axis=-1, keepdims=True)
    p_grp = 1.0 / psum
    in_g = (lane < N_EXPERTS) & ((lane // EXPERTS_PER_GROUP) == grp)
    el = jnp.where(in_g, lg, neg_inf)
    v1 = jnp.max(el, axis=-1, keepdims=True)
    i1 = jnp.min(jnp.where(el == v1, lane, big_lane), axis=-1, keepdims=True)
    el2 = jnp.where(lane == i1, neg_inf, el)
    v2 = jnp.max(el2, axis=-1, keepdims=True)
    i2 = jnp.min(jnp.where(el2 == v2, lane, big_lane), axis=-1, keepdims=True)
    tt = jnp.exp(v2 - v1)
    den = 1.0 + tt
    ga = p_grp / den
    gb = p_grp * tt / den

    sel1 = lane == i1
    sel2 = lane == i2
    return lane, sel1, sel2, ga, gb


def _route_count_kernel(lg_ref, cnt_ref, run_scr):
    @pl.when(pl.program_id(0) == 0)
    def _():
        run_scr[...] = jnp.zeros_like(run_scr)

    _, sel1, sel2, _, _ = _route_topk(lg_ref[...])
    onehot = jnp.where(sel1 | sel2, 1.0, 0.0)
    run_scr[...] = run_scr[...] + jnp.sum(onehot, axis=0, keepdims=True)
    cnt_ref[...] = run_scr[...].astype(I32)


def _route_dest_kernel(lg_ref, stri_ref, pstart_ref, gs_ref, rt_ref, run_scr):
    @pl.when(pl.program_id(0) == 0)
    def _():
        run_scr[...] = pstart_ref[...]

    lane, sel1, sel2, ga, gb = _route_topk(lg_ref[...])
    onehot = jnp.where(sel1 | sel2, 1.0, 0.0)
    pos = _dot(stri_ref[...], onehot.astype(BF16)) + run_scr[...]
    d1 = jnp.sum(jnp.where(sel1, pos, 0.0), axis=-1, keepdims=True)
    d2 = jnp.sum(jnp.where(sel2, pos, 0.0), axis=-1, keepdims=True)
    run_scr[...] = run_scr[...] + jnp.sum(onehot, axis=0, keepdims=True)

    slab = jnp.where(lane == 0, ga, 0.0)
    slab = jnp.where(lane == 1, gb, slab)
    slab = jnp.where(lane == 2, d1, slab)
    slab = jnp.where(lane == 3, d2, slab)
    gs_ref[...] = slab
    for q in range(ROUTE_TM // LANE):
        blk_t = slab[q * LANE:(q + 1) * LANE, :].T
        rt_ref[:, q * LANE:(q + 1) * LANE] = blk_t[0:8, :].astype(I32)


def _route_count(logits):
    t = logits.shape[0]
    tm = ROUTE_TM
    return pl.pallas_call(
        _route_count_kernel,
        grid=(t // tm,),
        in_specs=[pl.BlockSpec((tm, LANE), lambda i: (i, 0))],
        out_specs=pl.BlockSpec((1, LANE), lambda i: (0, 0)),
        out_shape=jax.ShapeDtypeStruct((1, LANE), I32),
        scratch_shapes=[pltpu.VMEM((1, LANE), F32)],
        compiler_params=_cparams(("arbitrary",), 32),
        name="route_count",
    )(logits)


def _route_dest(logits, stri, pstart_row):
    t = logits.shape[0]
    tm = ROUTE_TM
    return pl.pallas_call(
        _route_dest_kernel,
        grid=(t // tm,),
        in_specs=[pl.BlockSpec((tm, LANE), lambda i: (i, 0)),
                  pl.BlockSpec((tm, tm), lambda i: (0, 0)),
                  pl.BlockSpec((1, LANE), lambda i: (0, 0))],
        out_specs=[pl.BlockSpec((tm, LANE), lambda i: (i, 0)),
                   pl.BlockSpec((8, tm), lambda i: (0, i))],
        out_shape=[jax.ShapeDtypeStruct((t, LANE), F32),
                   jax.ShapeDtypeStruct((8, t), I32)],
        scratch_shapes=[pltpu.VMEM((1, LANE), F32)],
        compiler_params=_cparams(("arbitrary",), 32),
        name="route_dest",
    )(logits, stri, pstart_row)


MOE_BLOCK = 256
DISPATCH_TM = 512
COMBINE_TM = 256


ROW_UNROLL = 8


def _dispatch_kernel(d1_ref, d2_ref, h_ref, xb_in_hbm, xb_hbm, sem):
    del xb_in_hbm
    tm = DISPATCH_TM

    def body(g, carry):
        for u in range(ROW_UNROLL):
            tt = g * ROW_UNROLL + u
            src = h_ref.at[pl.ds(tt, 1)]
            pltpu.make_async_copy(src, xb_hbm.at[pl.ds(d1_ref[0, 0, tt], 1)], sem).start()
            pltpu.make_async_copy(src, xb_hbm.at[pl.ds(d2_ref[0, 0, tt], 1)], sem).start(priority=1)
        return carry

    lax.fori_loop(0, tm // ROW_UNROLL, body, 0)
    for _ in range(2):
        pltpu.make_async_copy(h_ref, xb_hbm.at[pl.ds(0, tm)], sem).wait()


def _dispatch(d1, d2, h2p, n_rows):
    t = h2p.shape[0]
    tm = DISPATCH_TM
    w = h2p.shape[1]
    smem = pl.BlockSpec((1, 1, tm), lambda i: (i, 0, 0), memory_space=pltpu.SMEM)
    anyspec = pl.BlockSpec(memory_space=pl.ANY)
    xb0 = jnp.zeros((n_rows, w), U32)
    return pl.pallas_call(
        _dispatch_kernel,
        grid=(t // tm,),
        in_specs=[smem, smem, pl.BlockSpec((tm, w), lambda i: (i, 0)), anyspec],
        out_specs=anyspec,
        out_shape=jax.ShapeDtypeStruct((n_rows, w), U32),
        scratch_shapes=[pltpu.SemaphoreType.DMA(())],
        input_output_aliases={3: 0},
        compiler_params=pltpu.CompilerParams(dimension_semantics=("arbitrary",), has_side_effects=True),
        name="moe_dispatch",
    )(d1, d2, h2p, xb0)


def _ffn_kernel(be_ref, nu_ref, xb_ref, w1_ref, w3_ref, w2_ref, yb_ref, w13_scr, w2_scr):
    i = pl.program_id(0)
    prev = be_ref[jnp.maximum(i - 1, 0)]
    changed = (i == 0) | (be_ref[i] != prev)
    used = i < nu_ref[0]

    @pl.when(changed & used)
    def _():
        w13_scr[:, :D_EXPERT] = w1_ref[0].astype(BF16)
        w13_scr[:, D_EXPERT:] = w3_ref[0].astype(BF16)
        w2_scr[...] = w2_ref[0].astype(BF16)

    @pl.when(used)
    def _():
        xw = xb_ref[...]
        lo = pltpu.bitcast(xw << 16, F32)
        hi = pltpu.bitcast(xw & jnp.uint32(0xFFFF0000), F32)
        x = jnp.concatenate([lo, hi], axis=1).astype(BF16)
        h = _dot(x, w13_scr[...])
        h1 = h[:, :D_EXPERT]
        h3 = h[:, D_EXPERT:]
        a = (h1 * _sigmoid(h1) * h3).astype(BF16)
        yb_ref[...] = _dot(a, w2_scr[...])

    @pl.when(jnp.logical_not(used))
    def _():
        yb_ref[...] = jnp.zeros_like(yb_ref)


def _ffn(blk_exp, n_used, xb, w1, w3, w2):
    n_rows = xb.shape[0]
    nblk = n_rows // MOE_BLOCK
    wmap = lambda i, be, nu: (be[i], 0, 0)
    grid_spec = pltpu.PrefetchScalarGridSpec(
        num_scalar_prefetch=2, grid=(nblk,),
        in_specs=[pl.BlockSpec((MOE_BLOCK, D_MODEL // 2), lambda i, be, nu: (i, 0)),
                  pl.BlockSpec((1, D_MODEL, D_EXPERT), wmap),
                  pl.BlockSpec((1, D_MODEL, D_EXPERT), wmap),
                  pl.BlockSpec((1, D_EXPERT, D_MODEL), wmap)],
        out_specs=pl.BlockSpec((MOE_BLOCK, D_MODEL), lambda i, be, nu: (i, 0)),
        scratch_shapes=[pltpu.VMEM((D_MODEL, 2 * D_EXPERT), BF16),
                        pltpu.VMEM((D_EXPERT, D_MODEL), BF16)])
    return pl.pallas_call(
        _ffn_kernel, grid_spec=grid_spec,
        out_shape=jax.ShapeDtypeStruct((n_rows, D_MODEL), F32),
        compiler_params=_cparams(("arbitrary",), 40),
        name="moe_ffn",
    )(blk_exp, n_used, xb, w1, w3, w2)


def _combine_kernel(d1_ref, d2_ref, d1n_ref, d2n_ref, yb_hbm, gs_ref, x1_ref, g2_ref, o_ref,
                    ya_scr, yb_scr, sem):
    tm = COMBINE_TM
    i = pl.program_id(0)
    slot = i % 2

    def issue(da_ref, db_ref, s):
        def body(g, carry):
            for u in range(ROW_UNROLL):
                tt = g * ROW_UNROLL + u
                pltpu.make_async_copy(yb_hbm.at[pl.ds(da_ref[0, 0, tt], 1)],
                                      ya_scr.at[s, pl.ds(tt, 1)], sem.at[s]).start()
                pltpu.make_async_copy(yb_hbm.at[pl.ds(db_ref[0, 0, tt], 1)],
                                      yb_scr.at[s, pl.ds(tt, 1)], sem.at[s]).start(priority=1)
            return carry

        lax.fori_loop(0, tm // ROW_UNROLL, body, 0)

    @pl.when(i == 0)
    def _():
        issue(d1_ref, d2_ref, 0)

    @pl.when(i + 1 < pl.num_programs(0))
    def _():
        issue(d1n_ref, d2n_ref, 1 - slot)

    pltpu.make_async_copy(yb_hbm.at[pl.ds(0, tm)], ya_scr.at[slot], sem.at[slot]).wait()
    pltpu.make_async_copy(yb_hbm.at[pl.ds(0, tm)], yb_scr.at[slot], sem.at[slot]).wait()
    ga = gs_ref[:, 0:1]
    gb = gs_ref[:, 1:2]
    o_ref[...] = x1_ref[...] + g2_ref[0] * (ga * ya_scr[slot] + gb * yb_scr[slot])


def _combine(d1, d2, ybuf, gs, x1, mod3, seq_len):
    t = x1.shape[0]
    tm = COMBINE_TM
    nt = t // tm
    tiles_per_seq = seq_len // tm
    smem = pl.BlockSpec((1, 1, tm), lambda i: (i, 0, 0), memory_space=pltpu.SMEM)
    smem_next = pl.BlockSpec((1, 1, tm), lambda i: (jnp.minimum(i + 1, nt - 1), 0, 0), memory_space=pltpu.SMEM)
    return pl.pallas_call(
        _combine_kernel,
        grid=(nt,),
        in_specs=[smem, smem, smem_next, smem_next,
                  pl.BlockSpec(memory_space=pl.ANY),
                  pl.BlockSpec((tm, LANE), lambda i: (i, 0)),
                  pl.BlockSpec((tm, D_MODEL), lambda i: (i, 0)),
                  pl.BlockSpec((1, 1, D_MODEL), lambda i: ((i // tiles_per_seq) * N_MOD + 5, 0, 0))],
        out_specs=pl.BlockSpec((tm, D_MODEL), lambda i: (i, 0)),
        out_shape=jax.ShapeDtypeStruct((t, D_MODEL), F32),
        scratch_shapes=[pltpu.VMEM((2, tm, D_MODEL), F32), pltpu.VMEM((2, tm, D_MODEL), F32),
                        pltpu.SemaphoreType.DMA((2,))],
        compiler_params=_cparams(("arbitrary",), 32),
        name="moe_combine",
    )(d1, d2, d1, d2, ybuf, gs, x1, mod3)


def _rope_tables(seq_len):
    t = jnp.arange(seq_len, dtype=I32)
    row = (t // GRID_W).astype(F32)
    colp = (t % GRID_W).astype(F32)
    half = SSD_STATE // 2
    inv = ROPE_THETA ** (-jnp.arange(0, half, 2, dtype=F32) / half)
    ar = row[:, None] * inv
    ac = colp[:, None] * inv
    cos_t = jnp.concatenate([jnp.cos(ar), jnp.cos(ar), jnp.cos(ac), jnp.cos(ac)], axis=-1)
    sin_t = jnp.concatenate([-jnp.sin(ar), jnp.sin(ar), -jnp.sin(ac), jnp.sin(ac)], axis=-1)
    return cos_t, sin_t


def _scan_tables(rev):
    li = jnp.arange(SSD_CHUNK)[:, None]
    ui = jnp.arange(SSD_CHUNK)[None, :]
    tri = ((ui >= li) if rev else (ui <= li)).astype(BF16)
    col = SSD_HEADS if rev else 0
    j = jnp.arange(LANE)[:, None]
    c = jnp.arange(D_SSD)[None, :]
    ee = (j == col + c // SSD_HEADDIM).astype(BF16)
    return tri, ee


def _pad_lanes(v, width=LANE):
    v = v.reshape(1, -1)
    return jnp.pad(v, ((0, 0), (0, width - v.shape[1])))


def kernel(x, c, ctx, c_ctx, w_mod, b_mod, norm1_w, w_in, conv_w, conv_b, a_log_f, a_log_b, dt_bias_f, dt_bias_b, d_skip, ssd_norm_w, q_norm_w, k_norm_w, rpb, w_br_ssd, w_br_na, w_out, norm2_w, w_grp, b_grp, w_rt, b_rt, w1, w3, w2):
    nb, seq_len, d = x.shape
    ctx_len = ctx.shape[1]
    t = nb * seq_len
    tc = nb * ctx_len
    assert w_mod.shape[0] == 1 and d == D_MODEL and nb <= 7
    assert seq_len % 256 == 0 and ctx_len % SSD_CHUNK == 0 and seq_len // GRID_W >= NA_KH

    cin = jnp.concatenate([c, c_ctx[None, :], jnp.zeros((8 - nb - 1, d), F32)], axis=0)
    mod = _modulation(cin, w_mod[0], b_mod[0])
    mod3 = mod.reshape(8 * N_MOD, 1, D_MODEL)

    wi = w_in[0]
    o_z, o_xbc, o_dt = 0, D_SSD, D_SSD + D_XBC
    o_qkv = o_dt + 2 * SSD_HEADS
    o_g = o_qkv + 3 * D_NA
    w_z = wi[:, o_z:o_xbc]
    w_xbc = wi[:, o_xbc:o_dt]
    w_dt = wi[:, o_dt:o_qkv]
    w_qkv = wi[:, o_qkv:o_g]
    w_g = wi[:, o_g:]
    w_lat = jnp.concatenate([w_xbc, w_z, w_g, w_qkv], axis=1).astype(BF16)
    w_ctx = jnp.concatenate([w_xbc, w_qkv[:, D_NA:]], axis=1).astype(BF16)
    w_dtp = jnp.pad(w_dt, ((0, 0), (0, LANE - 2 * SSD_HEADS))).astype(BF16)
    n1w = norm1_w[0].reshape(1, D_MODEL)

    x2 = x.reshape(t, D_MODEL)
    ctx2 = ctx.reshape(tc, D_MODEL)
    tm_in = 2048 if seq_len % 2048 == 0 else 256
    tiles = seq_len // tm_in
    big, dt_raw = _inproj(x2, mod3, lambda i: i // tiles, n1w, w_lat, w_dtp, tm_in, 1024)
    tm_c = 256 if tc % 256 == 0 else SSD_CHUNK
    bigc, dtc_raw = _inproj(ctx2, mod3, lambda i: nb, n1w, w_ctx, w_dtp, tm_c, 1024)
    ZCOL, GCOL, QCOL, KCOL, VCOL = 2, 3, 8, 9, 10
    KC_COL, VC_COL = 4, 5

    dtbias = _pad_lanes(jnp.concatenate([dt_bias_f[0], dt_bias_b[0]]))
    cw = conv_w[0]
    cbias = conv_b[0].reshape(1, D_XBC)
    cos_t, sin_t = _rope_tables(seq_len)
    xs, bc, dts = _ssd_prep(big, dt_raw, cw, cbias, dtbias, cos_t, sin_t, seq_len, 256)
    ctl = 256 if ctx_len % 256 == 0 else SSD_CHUNK
    ones_t = jnp.ones((ctx_len, LANE), F32)
    zeros_t = jnp.zeros((ctx_len, LANE), F32)
    xsc, bcc, dtsc = _ssd_prep(bigc, dtc_raw, cw, cbias, dtbias, ones_t, zeros_t, ctx_len, ctl)

    alog = _pad_lanes(jnp.concatenate([a_log_f[0], a_log_b[0]]))
    dskip = jnp.repeat(d_skip[0], SSD_HEADDIM).reshape(1, D_SSD)
    tri_f, ee_f = _scan_tables(False)
    tri_b, ee_b = _scan_tables(True)
    h_zero = jnp.zeros((nb, SSD_STATE, D_SSD), F32)
    _, hcf = _ssd_scan(xsc, bcc, dtsc, alog, dskip, tri_f, ee_f, h_zero, nb, False)
    _, hcb = _ssd_scan(xsc, bcc, dtsc, alog, dskip, tri_b, ee_b, h_zero, nb, True)
    yf, _ = _ssd_scan(xs, bc, dts, alog, dskip, tri_f, ee_f, hcf, nb, False)
    yb, _ = _ssd_scan(xs, bc, dts, alog, dskip, tri_b, ee_b, hcb, nb, True)

    qw = jnp.tile(q_norm_w[0], NA_HEADS).reshape(1, D_NA)
    kw = jnp.tile(k_norm_w[0], NA_HEADS).reshape(1, D_NA)
    gi = jnp.arange(LANE)
    gmat = ((gi[:, None] // NA_HEADDIM) == (gi[None, :] // NA_HEADDIM)).astype(BF16) * (1.0 / NA_HEADDIM)
    gmat = gmat.astype(BF16)
    qn, kn = _na_prep(big, QCOL, KCOL, qw, kw, gmat, 512 if t % 512 == 0 else 256)
    kcn = _na_prep(bigc, None, KC_COL, qw, kw, gmat, 256 if tc % 256 == 0 else SSD_CHUNK)
    bias_tab = _na_bias_table(rpb[0])
    y_na = _neigh_attention(qn, kn, big, kcn, bigc, bias_tab, nb, seq_len, ctx_len, VCOL, VC_COL)

    w_r = jnp.pad(jnp.concatenate([w_rt[0], w_grp[0]], axis=1), ((0, 0), (0, LANE - N_EXPERTS - N_GROUPS)))
    wrh = w_r.astype(BF16)
    wrl = (w_r - wrh.astype(F32)).astype(BF16)
    br = _pad_lanes(jnp.concatenate([b_rt[0], b_grp[0]]))
    x1, h2p, logits = _merge(yf, yb, big, y_na, x2, mod3,
                             ssd_norm_w[0].reshape(1, D_SSD), norm2_w[0].reshape(1, D_MODEL),
                             w_br_ssd[0].astype(BF16), w_br_na[0].astype(BF16), w_out[0].astype(BF16),
                             wrh, wrl, br, seq_len, 256, ZCOL, GCOL)

    si = jnp.arange(ROUTE_TM)
    stri = (si[None, :] < si[:, None]).astype(BF16)
    cnt = _route_count(logits)

    counts = cnt[0, :N_EXPERTS]
    padded = (counts + MOE_BLOCK - 1) // MOE_BLOCK * MOE_BLOCK
    pend = jnp.cumsum(padded)
    pstart = pend - padded
    nblk = (2 * t + N_EXPERTS * (MOE_BLOCK - 1) + MOE_BLOCK - 1) // MOE_BLOCK
    n_used = (pend[-1] // MOE_BLOCK).astype(I32).reshape(1)
    blk_row0 = jnp.arange(nblk, dtype=I32) * MOE_BLOCK
    blk_exp = jnp.minimum(jnp.sum((pend[None, :] <= blk_row0[:, None]).astype(I32), axis=1), N_EXPERTS - 1)
    gs, rt = _route_dest(logits, stri, _pad_lanes(pstart.astype(F32)))

    def tok_tiles(row, tm):
        return rt[row].reshape(t // tm, 1, tm)

    xb = _dispatch(tok_tiles(2, DISPATCH_TM), tok_tiles(3, DISPATCH_TM), h2p, nblk * MOE_BLOCK)
    ybuf = _ffn(blk_exp, n_used, xb, w1[0], w3[0], w2[0])
    out = _combine(tok_tiles(2, COMBINE_TM), tok_tiles(3, COMBINE_TM), ybuf, gs, x1, mod3, seq_len)
    return out.reshape(nb, seq_len, D_MODEL)
```

```python
import functools
import math

import jax
import jax.numpy as jnp
from jax import lax
from jax.experimental import pallas as pl
from jax.experimental.pallas import tpu as pltpu

F32 = jnp.float32
BF16 = jnp.bfloat16
I32 = jnp.int32
U32 = jnp.uint32

D_MODEL = 1024
GRID_W = 64
D_SSD = 2048
SSD_HEADDIM = 64
SSD_HEADS = 32
SSD_GROUPS = 8
HEADS_PER_GROUP = 4
SSD_STATE = 128
SSD_CHUNK = 128
CONV_K = 5
D_BC = SSD_GROUPS * SSD_STATE
D_XBC = D_SSD + 2 * D_BC
NA_HEADDIM = 64
NA_HEADS = 16
D_NA = 1024
NA_KH = 8
NA_KW = 16
ROPE_THETA = 10000.0
N_GROUPS = 4
EXPERTS_PER_GROUP = 8
N_EXPERTS = 32
D_EXPERT = 512
N_MOD = 6
EPS = 1e-6
NEG_BIG = -1e30

COLS_LAT = D_XBC + D_SSD + 2 * D_MODEL + 3 * D_NA
COLS_CTX = D_XBC + 2 * D_NA
LANE = 128

V7X_VMEM_BYTES = 64 * 1024 * 1024


def _cparams(sem, vmem_mb):
    return pltpu.CompilerParams(dimension_semantics=sem, vmem_limit_bytes=vmem_mb * 1024 * 1024)


def _sigmoid(x):
    return 1.0 / (1.0 + jnp.exp(-x))


def _split2(x):
    hi = x.astype(BF16)
    lo = (x - hi.astype(F32)).astype(BF16)
    return hi, lo


def _split3(x):
    hi = x.astype(BF16)
    r = x - hi.astype(F32)
    mid = r.astype(BF16)
    lo = (r - mid.astype(F32)).astype(BF16)
    return hi, mid, lo


def _dot(a, b):
    return jnp.dot(a, b, preferred_element_type=F32)


def _dot_nt(a, b):
    return lax.dot_general(a, b, (((1,), (1,)), ((), ())), preferred_element_type=F32)


def _dot_tn(a, b):
    return lax.dot_general(a, b, (((0,), (0,)), ((), ())), preferred_element_type=F32)


def _mod_kernel(c_ref, w_ref, b_ref, o_ref):
    c = c_ref[...]
    a = (c * _sigmoid(c)).astype(BF16)
    o_ref[...] = _dot(a, w_ref[...].astype(BF16)) + b_ref[...]


def _modulation(cin, w_mod, b_mod):
    n = w_mod.shape[1]
    tn = 1536
    return pl.pallas_call(
        _mod_kernel,
        grid=(n // tn,),
        in_specs=[pl.BlockSpec((8, D_MODEL), lambda j: (0, 0)),
                  pl.BlockSpec((D_MODEL, tn), lambda j: (0, j)),
                  pl.BlockSpec((1, tn), lambda j: (0, j))],
        out_specs=pl.BlockSpec((8, tn), lambda j: (0, j)),
        out_shape=jax.ShapeDtypeStruct((8, n), F32),
        compiler_params=_cparams(("arbitrary",), 40),
        name="modulation",
    )(cin, w_mod, b_mod.reshape(1, n))


def _inproj_kernel(x_ref, sh_ref, sc_ref, nw_ref, w_ref, wdt_ref, o_ref, dt_ref, h_scr, *, tm):
    j = pl.program_id(1)

    @pl.when(j == 0)
    def _():
        scale = 1.0 + sc_ref[0]
        shift = sh_ref[0]
        nw = nw_ref[...]

        def body(r, carry):
            rows = pl.ds(pl.multiple_of(r * 128, 128), 128)
            x = x_ref[rows, :]
            ms = jnp.mean(x * x, axis=-1, keepdims=True)
            h = x * lax.rsqrt(ms + EPS) * nw * scale + shift
            h_scr[rows, :] = h.astype(BF16)
            return carry

        lax.fori_loop(0, tm // 128, body, 0)
        dt_ref[...] = _dot(h_scr[...], wdt_ref[...])

    o_ref[...] = _dot(h_scr[...], w_ref[...]).astype(BF16)


def _inproj(x2, mod3, mod_row_fn, norm_w, w, wdt, tm, tn, n, wcol_fn):
    t = x2.shape[0]
    kern = functools.partial(_inproj_kernel, tm=tm)
    return pl.pallas_call(
        kern,
        grid=(t // tm, n // tn),
        in_specs=[pl.BlockSpec((tm, D_MODEL), lambda i, j: (i, 0)),
                  pl.BlockSpec((1, 1, D_MODEL), lambda i, j: (mod_row_fn(i) * N_MOD + 0, 0, 0)),
                  pl.BlockSpec((1, 1, D_MODEL), lambda i, j: (mod_row_fn(i) * N_MOD + 1, 0, 0)),
                  pl.BlockSpec((1, D_MODEL), lambda i, j: (0, 0)),
                  pl.BlockSpec((D_MODEL, tn), lambda i, j: (0, wcol_fn(j))),
                  pl.BlockSpec((D_MODEL, LANE), lambda i, j: (0, 0))],
        out_specs=[pl.BlockSpec((tm, tn), lambda i, j: (i, j)),
                   pl.BlockSpec((tm, LANE), lambda i, j: (i, 0))],
        out_shape=[jax.ShapeDtypeStruct((t, n), BF16),
                   jax.ShapeDtypeStruct((t, LANE), F32)],
        scratch_shapes=[pltpu.VMEM((tm, D_MODEL), BF16)],
        compiler_params=_cparams(("arbitrary", "arbitrary"), 48),
        name="inproj",
    )(x2, mod3, mod3, norm_w, w, wdt)


CONV_OFFSETS = (-2, -1, 1, 2)


def _ssd_prep_kernel(prev_ref, cur_ref, next_ref, dt_ref, cw_ref, cb_ref, dtbias_ref, cos_ref, sin_ref,
                     shift_ref, etop_ref, ebot_ref, xs_ref, bc_ref, dto_ref, *, tl, tiles_per_seq):
    i = pl.program_id(0)
    first = (i % tiles_per_seq) == 0
    last = (i % tiles_per_seq) == tiles_per_seq - 1
    lane = lax.broadcasted_iota(I32, (tl, LANE), 1)
    lo_half = ((lane // 32) % 2) == 0
    shift = shift_ref[...]
    etop = etop_ref[...]
    ebot = ebot_ref[...]
    cosv = cos_ref[...]
    sinv = sin_ref[...]
    cwid = 256
    for c in range(D_XBC // cwid):
        cs = slice(c * cwid, (c + 1) * cwid)
        w = cw_ref[:, cs]
        xc = cur_ref[:, cs]
        sh = _dot(shift, xc).reshape(tl // 8, len(CONV_OFFSETS), 8, cwid)
        tap = lambda k: sh[:, k].reshape(tl, cwid)
        acc = cb_ref[:, cs] + xc.astype(F32) * w[2:3, :]
        acc = acc + tap(0) * w[0:1, :] + tap(1) * w[1:2, :] + tap(2) * w[3:4, :] + tap(3) * w[4:5, :]
        top = _dot(etop, prev_ref[:, cs])
        bot = _dot(ebot, next_ref[:, cs])
        top_c = jnp.where(first, 0.0, top[0:8] * w[0:1, :] + top[8:16] * w[1:2, :])
        bot_c = jnp.where(last, 0.0, bot[0:8] * w[3:4, :] + bot[8:16] * w[4:5, :])
        acc = jnp.concatenate([acc[0:8] + top_c, acc[8:tl - 8], acc[tl - 8:tl] + bot_c], axis=0)
        y = acc * _sigmoid(acc)
        if c * cwid < D_SSD:
            xs_ref[:, cs] = y.astype(BF16)
        else:
            for g in range(cwid // LANE):
                yg = y[:, g * LANE:(g + 1) * LANE]
                partner = jnp.where(lo_half, pltpu.roll(yg, 96, 1), pltpu.roll(yg, 32, 1))
                o = yg * cosv + partner * sinv
                c0 = c * cwid - D_SSD + g * LANE
                bc_ref[:, c0:c0 + LANE] = o.astype(BF16)

    v = dt_ref[...] + dtbias_ref[...]
    dto_ref[...] = jnp.maximum(v, 0.0) + jnp.log(1.0 + jnp.exp(-jnp.abs(v)))


def _conv_shift_tables(tl):
    t = jnp.arange(tl)[:, None]
    u = jnp.arange(tl)[None, :]
    shift = jnp.stack([(u == t + off) for off in CONV_OFFSETS], axis=0).reshape(len(CONV_OFFSETS), tl // 8, 8, tl)
    shift = shift.transpose(1, 0, 2, 3).reshape(len(CONV_OFFSETS) * tl, tl).astype(BF16)
    r = jnp.arange(16)[:, None]
    q = jnp.arange(16)[None, :]
    etop = (((r < 8) & (q == r + 14)) | ((r >= 8) & (q == r - 8 + 15))).astype(BF16)
    ebot = (((r < 8) & (q == r - 7)) | ((r >= 8) & (q == r - 8 - 6))).astype(BF16)
    return shift, etop, ebot


def _ssd_prep(big, dt_raw, conv_w, conv_b, dtbias, cos_t, sin_t, seq_len, tl):
    t = big.shape[0]
    tps = seq_len // tl
    hb = tl // 16
    nhalo = t // 16
    kern = functools.partial(_ssd_prep_kernel, tl=tl, tiles_per_seq=tps)
    shift, etop, ebot = _conv_shift_tables(tl)
    const = lambda i: (0, 0)
    return pl.pallas_call(
        kern,
        grid=(t // tl,),
        in_specs=[pl.BlockSpec((16, D_XBC), lambda i: (jnp.maximum(i * hb - 1, 0), 0)),
                  pl.BlockSpec((tl, D_XBC), lambda i: (i, 0)),
                  pl.BlockSpec((16, D_XBC), lambda i: (jnp.minimum((i + 1) * hb, nhalo - 1), 0)),
                  pl.BlockSpec((tl, LANE), lambda i: (i, 0)),
                  pl.BlockSpec((CONV_K, D_XBC), const),
                  pl.BlockSpec((1, D_XBC), const),
                  pl.BlockSpec((1, LANE), const),
                  pl.BlockSpec((tl, LANE), lambda i: (i % tps, 0)),
                  pl.BlockSpec((tl, LANE), lambda i: (i % tps, 0)),
                  pl.BlockSpec((len(CONV_OFFSETS) * tl, tl), const),
                  pl.BlockSpec((16, 16), const),
                  pl.BlockSpec((16, 16), const)],
        out_specs=[pl.BlockSpec((tl, D_SSD), lambda i: (i, 0)),
                   pl.BlockSpec((tl, 2 * D_BC), lambda i: (i, 0)),
                   pl.BlockSpec((tl, LANE), lambda i: (i, 0))],
        out_shape=[jax.ShapeDtypeStruct((t, D_SSD), BF16),
                   jax.ShapeDtypeStruct((t, 2 * D_BC), BF16),
                   jax.ShapeDtypeStruct((t, LANE), F32)],
        compiler_params=_cparams(("arbitrary",), 40),
        name="ssd_prep",
    )(big, big, big, dt_raw, conv_w, conv_b, dtbias, cos_t, sin_t, shift, etop, ebot)


GROUP_W = HEADS_PER_GROUP * SSD_HEADDIM


def _scan_decays(dt_ref, alog_ref, tri_ref, ee_ref, rev):
    edge = 0 if rev else SSD_CHUNK - 1
    a_row = -jnp.exp(alog_ref[...])
    dt = dt_ref[...]
    trib = tri_ref[...]
    d_hi, d_mid, d_lo = _split3(dt * a_row)
    cum = _dot(trib, d_hi) + _dot(trib, d_mid) + _dot(trib, d_lo)
    tot = cum[edge:edge + 1, :]
    ee = ee_ref[...]
    wscale = _dot((jnp.exp(tot - cum) * dt).astype(BF16), ee)
    t_hi, t_lo = _split2(jnp.broadcast_to(jnp.exp(tot), (8, LANE)))
    etot = (_dot(t_hi, ee) + _dot(t_lo, ee))[0:1, :]
    return dt, trib, cum, ee, wscale, etot


def _ssd_state_kernel(xs_ref, bc_ref, dt_ref, alog_ref, tri_ref, ee_ref, hfin_ref, h_scr, *, rev, nc):
    k = pl.program_id(1)

    @pl.when(k == 0)
    def _():
        h_scr[...] = jnp.zeros_like(h_scr)

    _, _, _, _, wscale, etot = _scan_decays(dt_ref, alog_ref, tri_ref, ee_ref, rev)
    for g in range(SSD_GROUPS):
        gs = slice(g * GROUP_W, (g + 1) * GROUP_W)
        xw = (xs_ref[:, gs].astype(F32) * wscale[:, gs]).astype(BF16)
        h_scr[:, gs] = etot[:, gs] * h_scr[:, gs] + _dot_tn(bc_ref[:, g * SSD_STATE:(g + 1) * SSD_STATE], xw)

    @pl.when(k == nc - 1)
    def _():
        hfin_ref[0] = h_scr[...]


def _ssd_scan_kernel(xs_ref, bc_ref, dt_ref, alog_ref, dskip_ref, tri_ref, ee_ref, h0_ref,
                     y_ref, h_scr, *, rev):
    k = pl.program_id(1)
    L = SSD_CHUNK

    @pl.when(k == 0)
    def _():
        h_scr[...] = h0_ref[0]

    col = SSD_HEADS if rev else 0
    dt, trib, cum, ee, wscale, etot = _scan_decays(dt_ref, alog_ref, tri_ref, ee_ref, rev)
    tri = trib > 0.5
    cum_t = cum.T
    dt_t = dt.T
    oscale = _dot(jnp.exp(cum).astype(BF16), ee)

    gw = GROUP_W
    lane_head = lax.broadcasted_iota(I32, (L, gw), 1) // SSD_HEADDIM

    def b_of(g):
        return bc_ref[:, g * SSD_STATE:(g + 1) * SSD_STATE]

    def c_of(g):
        return bc_ref[:, D_BC + g * SSD_STATE:D_BC + (g + 1) * SSD_STATE]

    def operands(g, cb):
        gs = slice(g * gw, (g + 1) * gw)
        xs_g = xs_ref[:, gs]
        ms = []
        for r in range(HEADS_PER_GROUP):
            hh = col + g * HEADS_PER_GROUP + r
            seg = cum[:, hh:hh + 1] - cum_t[hh:hh + 1, :]
            lm = jnp.exp(jnp.where(tri, seg, NEG_BIG)) * dt_t[hh:hh + 1, :]
            ms.append((cb * lm).astype(BF16))
        mcat = jnp.concatenate(ms, axis=1)
        zero = jnp.zeros_like(xs_g)
        bd = jnp.concatenate([jnp.where(lane_head == r, xs_g, zero) for r in range(HEADS_PER_GROUP)], axis=0)
        xs_f = xs_g.astype(F32)
        xw = (xs_f * wscale[:, gs]).astype(BF16)
        return mcat, bd, xs_f, xw

    cbs = [_dot_nt(c_of(g), b_of(g)) for g in range(SSD_GROUPS)]
    nxt = operands(0, cbs[0])
    for g in range(SSD_GROUPS):
        gs = slice(g * gw, (g + 1) * gw)
        mcat, bd, xs_f, xw = nxt
        h_g = h_scr[:, gs]
        y_diag = _dot(mcat, bd)
        y_off = _dot(c_of(g), h_g.astype(BF16))
        loc = _dot_tn(b_of(g), xw)
        if g + 1 < SSD_GROUPS:
            nxt = operands(g + 1, cbs[g + 1])
        y = y_diag + oscale[:, gs] * y_off
        if not rev:
            y = y + xs_f * dskip_ref[:, gs]
        y_ref[:, gs] = y.astype(BF16)
        h_scr[:, gs] = etot[:, gs] * h_g + loc


def _scan_rowmap(nc, rev):
    def rowmap(b, k):
        c = (nc - 1 - k) if rev else k
        return (b * nc + c, 0)
    return rowmap


def _ssd_state(xs, bc, dt, alog_row, tri, ee, nb, rev):
    t = xs.shape[0]
    nc = t // nb // SSD_CHUNK
    rowmap = _scan_rowmap(nc, rev)
    const2 = lambda b, k: (0, 0)
    return pl.pallas_call(
        functools.partial(_ssd_state_kernel, rev=rev, nc=nc),
        grid=(nb, nc),
        in_specs=[pl.BlockSpec((SSD_CHUNK, D_SSD), rowmap),
                  pl.BlockSpec((SSD_CHUNK, D_BC), rowmap),
                  pl.BlockSpec((SSD_CHUNK, LANE), rowmap),
                  pl.BlockSpec((1, LANE), const2),
                  pl.BlockSpec((SSD_CHUNK, SSD_CHUNK), const2),
                  pl.BlockSpec((LANE, D_SSD), const2)],
        out_specs=pl.BlockSpec((1, SSD_STATE, D_SSD), lambda b, k: (b, 0, 0)),
        out_shape=jax.ShapeDtypeStruct((nb, SSD_STATE, D_SSD), F32),
        scratch_shapes=[pltpu.VMEM((SSD_STATE, D_SSD), F32)],
        compiler_params=_cparams(("arbitrary", "arbitrary"), 40),
        name="ssd_state_bwd" if rev else "ssd_state_fwd",
    )(xs, bc, dt, alog_row, tri, ee)


def _ssd_scan(xs, bc, dt, alog_row, dskip_row, tri, ee, h0, nb, rev):
    t = xs.shape[0]
    nc = t // nb // SSD_CHUNK
    rowmap = _scan_rowmap(nc, rev)
    const2 = lambda b, k: (0, 0)
    return pl.pallas_call(
        functools.partial(_ssd_scan_kernel, rev=rev),
        grid=(nb, nc),
        in_specs=[pl.BlockSpec((SSD_CHUNK, D_SSD), rowmap),
                  pl.BlockSpec((SSD_CHUNK, 2 * D_BC), rowmap),
                  pl.BlockSpec((SSD_CHUNK, LANE), rowmap),
                  pl.BlockSpec((1, LANE), const2),
                  pl.BlockSpec((1, D_SSD), const2),
                  pl.BlockSpec((SSD_CHUNK, SSD_CHUNK), const2),
                  pl.BlockSpec((LANE, D_SSD), const2),
                  pl.BlockSpec((1, SSD_STATE, D_SSD), lambda b, k: (b, 0, 0))],
        out_specs=pl.BlockSpec((SSD_CHUNK, D_SSD), rowmap),
        out_shape=jax.ShapeDtypeStruct((t, D_SSD), BF16),
        scratch_shapes=[pltpu.VMEM((SSD_STATE, D_SSD), F32)],
        compiler_params=_cparams(("arbitrary", "arbitrary"), 40),
        name="ssd_scan_bwd" if rev else "ssd_scan_fwd",
    )(xs, bc, dt, alog_row, dskip_row, tri, ee, h0)


def _headnorm_cols(src_ref, w_ref, g, dst_ref, scale):
    for c in range(D_NA // LANE):
        cs = slice(c * LANE, (c + 1) * LANE)
        x = src_ref[:, cs].astype(F32)
        hi, lo = _split2(x * x)
        ms = _dot(hi, g) + _dot(lo, g)
        y = x * lax.rsqrt(ms + EPS) * w_ref[:, cs]
        if scale is not None:
            y = y * scale
        dst_ref[:, cs] = y.astype(BF16)


def _na_prep_qk_kernel(q_ref, k_ref, qw_ref, kw_ref, g_ref, qo_ref, ko_ref):
    g = g_ref[...]
    _headnorm_cols(q_ref, qw_ref, g, qo_ref, NA_HEADDIM ** -0.5)
    _headnorm_cols(k_ref, kw_ref, g, ko_ref, None)


def _na_prep_k_kernel(k_ref, kw_ref, g_ref, ko_ref):
    _headnorm_cols(k_ref, kw_ref, g_ref[...], ko_ref, None)


def _na_prep(big, qcol, kcol, qw, kw, gmat, tm):
    t = big.shape[0]
    blk = lambda cidx: pl.BlockSpec((tm, D_NA), lambda i: (i, cidx))
    row = pl.BlockSpec((1, D_NA), lambda i: (0, 0))
    gspec = pl.BlockSpec((LANE, LANE), lambda i: (0, 0))
    out = pl.BlockSpec((tm, D_NA), lambda i: (i, 0))
    if qcol is None:
        return pl.pallas_call(
            _na_prep_k_kernel, grid=(t // tm,),
            in_specs=[blk(kcol), row, gspec], out_specs=out,
            out_shape=jax.ShapeDtypeStruct((t, D_NA), BF16),
            compiler_params=_cparams(("arbitrary",), 32), name="na_prep_ctx",
        )(big, kw, gmat)
    return pl.pallas_call(
        _na_prep_qk_kernel, grid=(t // tm,),
        in_specs=[blk(qcol), blk(kcol), row, row, gspec], out_specs=[out, out],
        out_shape=[jax.ShapeDtypeStruct((t, D_NA), BF16)] * 2,
        compiler_params=_cparams(("arbitrary",), 32), name="na_prep",
    )(big, big, qw, kw, gmat)


NA_LOOKAHEAD = 3


def _na_kernel(q_ref, k_ref, v_ref, kc_ref, vc_ref, bias_ref, o_ref, *, rows):
    r = pl.program_id(1)
    rs = jnp.clip(r - NA_KH // 2, 0, rows - NA_KH)
    start = pl.multiple_of(rs * GRID_W, GRID_W)
    nk = NA_KH * GRID_W
    lane = lax.broadcasted_iota(I32, (GRID_W, LANE), 1)
    first_head = lane < NA_HEADDIM
    npair = NA_HEADS // 2

    def scores(j):
        cs = slice(j * LANE, (j + 1) * LANE)
        qp = q_ref[:, cs]
        zero = jnp.zeros_like(qp)
        qs = jnp.concatenate([jnp.where(first_head, qp, zero), jnp.where(first_head, zero, qp)], axis=0)
        kb = k_ref[pl.ds(start, nk), cs]
        s_loc = _dot_nt(qs, kb) + bias_ref[0, j * LANE:(j + 1) * LANE, :]
        s_ctx = _dot_nt(qs, kc_ref[:, cs])
        return s_loc, s_ctx

    pending = [scores(j) for j in range(NA_LOOKAHEAD)]
    for j in range(npair):
        cs = slice(j * LANE, (j + 1) * LANE)
        s_loc, s_ctx = pending.pop(0)
        if j + NA_LOOKAHEAD < npair:
            pending.append(scores(j + NA_LOOKAHEAD))
        vb = v_ref[pl.ds(start, nk), cs]
        m = jnp.maximum(jnp.max(s_loc, axis=-1, keepdims=True), jnp.max(s_ctx, axis=-1, keepdims=True))
        p_loc = jnp.exp(s_loc - m)
        p_ctx = jnp.exp(s_ctx - m)
        den = jnp.sum(p_loc, axis=-1, keepdims=True) + jnp.sum(p_ctx, axis=-1, keepdims=True)
        o = _dot(p_loc.astype(BF16), vb) + _dot(p_ctx.astype(BF16), vc_ref[:, cs])
        o = o / den
        o_ref[:, cs] = jnp.where(first_head, o[:GRID_W], o[GRID_W:]).astype(BF16)


def _neigh_attention(qn, kn, big, kcn, bigc, bias_tab, nb, seq_len, ctx_len, vcol, vccol):
    t = qn.shape[0]
    rows = seq_len // GRID_W
    kern = functools.partial(_na_kernel, rows=rows)

    def pat(b, r):
        return (r - jnp.clip(r - NA_KH // 2, 0, rows - NA_KH), 0, 0)

    return pl.pallas_call(
        kern,
        grid=(nb, rows),
        in_specs=[pl.BlockSpec((GRID_W, D_NA), lambda b, r: (b * rows + r, 0)),
                  pl.BlockSpec((seq_len, D_NA), lambda b, r: (b, 0)),
                  pl.BlockSpec((seq_len, D_NA), lambda b, r: (b, vcol)),
                  pl.BlockSpec((ctx_len, D_NA), lambda b, r: (b, 0)),
                  pl.BlockSpec((ctx_len, D_NA), lambda b, r: (b, vccol)),
                  pl.BlockSpec((1, NA_HEADS * GRID_W, NA_KH * GRID_W), pat)],
        out_specs=pl.BlockSpec((GRID_W, D_NA), lambda b, r: (b * rows + r, 0)),
        out_shape=jax.ShapeDtypeStruct((t, D_NA), BF16),
        compiler_params=_cparams(("arbitrary", "arbitrary"), 56),
        name="neigh_attention",
    )(qn, kn, big, kcn, bigc, bias_tab)


def _na_bias_kernel(rpb_ref, o_ref):
    rp = rpb_ref[0]
    r64 = pltpu.roll(rp, GRID_W, 1)
    c = lax.broadcasted_iota(I32, (GRID_W, LANE), 0)
    kc = lax.broadcasted_iota(I32, (GRID_W, LANE), 1) % GRID_W
    cs = jnp.clip(c - NA_KW // 2, 0, GRID_W - NA_KW)
    valid = (kc >= cs) & (kc < cs + NA_KW)
    pair = []
    for d in range(2 * NA_KH - 2):
        vec = rp[d:d + 1, :] + r64[d + 1:d + 2, :]
        w = pltpu.roll(jnp.broadcast_to(vec, (GRID_W, LANE)), LANE - (NA_KW - 1), 1, stride=1, stride_axis=0)
        pair.append(jnp.where(valid, w, NEG_BIG))
    for p in range(NA_KH):
        for ii in range(NA_KH // 2):
            o_ref[p, :, ii * LANE:(ii + 1) * LANE] = pair[2 * ii - p + NA_KH - 1]


def _na_bias_table(rpb):
    rp = jnp.pad(rpb, ((0, 0), (0, 1), (0, LANE - (2 * NA_KW - 1))))
    return pl.pallas_call(
        _na_bias_kernel,
        grid=(NA_HEADS,),
        in_specs=[pl.BlockSpec((1, 2 * NA_KH, LANE), lambda h: (h, 0, 0))],
        out_specs=pl.BlockSpec((NA_KH, GRID_W, NA_KH * GRID_W), lambda h: (0, h, 0)),
        out_shape=jax.ShapeDtypeStruct((NA_KH, NA_HEADS * GRID_W, NA_KH * GRID_W), F32),
        compiler_params=_cparams(("arbitrary",), 32),
        name="na_bias",
    )(rp)


MERGE_SUB = 128


def _merge_kernel(yf_ref, yb_ref, z_ref, gt_ref, yna_ref, x_ref, g1_ref, sh2_ref, sc2_ref, snw_ref, n2w_ref,
                  wbs_ref, wbn_ref, wo_ref, wrh_ref, wrl_ref, br_ref, x1_ref, h2p_ref, lg_ref):
    tm = x_ref.shape[0]
    subs = [slice(r * MERGE_SUB, (r + 1) * MERGE_SUB) for r in range(tm // MERGE_SUB)]
    half = D_MODEL // 2
    yn = []
    for rs in subs:
        z = z_ref[rs, :].astype(F32)
        y = (yf_ref[rs, :] + yb_ref[rs, :]).astype(F32) * (z * _sigmoid(z))
        ms = jnp.mean(y * y, axis=-1, keepdims=True)
        yn.append((y * lax.rsqrt(ms + EPS) * snw_ref[...]).astype(BF16))
    ab = [(_dot(yn[r], wbs_ref[...]), _dot(yna_ref[rs, :], wbn_ref[...])) for r, rs in enumerate(subs)]
    merged = []
    for r, rs in enumerate(subs):
        g_ssd = gt_ref[rs, :D_MODEL].astype(F32)
        g_na = gt_ref[rs, D_MODEL:].astype(F32)
        merged.append((_sigmoid(g_ssd) * ab[r][0] + _sigmoid(g_na) * ab[r][1]).astype(BF16))
    mo = [_dot(m, wo_ref[...]) for m in merged]
    hs = []
    for r, rs in enumerate(subs):
        x1 = x_ref[rs, :] + g1_ref[0] * mo[r]
        x1_ref[rs, :] = x1
        ms2 = jnp.mean(x1 * x1, axis=-1, keepdims=True)
        h2 = x1 * lax.rsqrt(ms2 + EPS) * n2w_ref[...] * (1.0 + sc2_ref[0]) + sh2_ref[0]
        h_hi, h_lo = _split2(h2)
        hs.append((h_hi, h_lo))
        bits = pltpu.bitcast(h_hi.astype(F32), U32)
        h2p_ref[rs, :] = (bits[:, :half] >> 16) | bits[:, half:]
    for r, rs in enumerate(subs):
        h_hi, h_lo = hs[r]
        lg_ref[rs, :] = (_dot(h_hi, wrh_ref[...]) + _dot(h_lo, wrh_ref[...]) + _dot(h_hi, wrl_ref[...])) + br_ref[...]


def _merge(yf, yb, big, yna, x2, mod3, snw, n2w, wbs, wbn, wo, wrh, wrl, br, seq_len, tm, zcol, gcol):
    t = x2.shape[0]
    tiles_per_seq = seq_len // tm
    modspec = lambda kidx: pl.BlockSpec((1, 1, D_MODEL), lambda i: ((i // tiles_per_seq) * N_MOD + kidx, 0, 0))
    full = lambda shp: pl.BlockSpec(shp, lambda i: (0,) * len(shp), pipeline_mode=pl.Buffered(1))
    return pl.pallas_call(
        _merge_kernel,
        grid=(t // tm,),
        in_specs=[pl.BlockSpec((tm, D_SSD), lambda i: (i, 0)),
                  pl.BlockSpec((tm, D_SSD), lambda i: (i, 0)),
                  pl.BlockSpec((tm, D_SSD), lambda i: (i, zcol)),
                  pl.BlockSpec((tm, 2 * D_MODEL), lambda i: (i, gcol)),
                  pl.BlockSpec((tm, D_NA), lambda i: (i, 0)),
                  pl.BlockSpec((tm, D_MODEL), lambda i: (i, 0)),
                  modspec(2), modspec(3), modspec(4),
                  full((1, D_SSD)), full((1, D_MODEL)),
                  full((D_SSD, D_MODEL)), full((D_NA, D_MODEL)), full((D_MODEL, D_MODEL)),
                  full((D_MODEL, LANE)), full((D_MODEL, LANE)), full((1, LANE))],
        out_specs=[pl.BlockSpec((tm, D_MODEL), lambda i: (i, 0)),
                   pl.BlockSpec((tm, D_MODEL // 2), lambda i: (i, 0)),
                   pl.BlockSpec((tm, LANE), lambda i: (i, 0))],
        out_shape=[jax.ShapeDtypeStruct((t, D_MODEL), F32),
                   jax.ShapeDtypeStruct((t, D_MODEL // 2), U32),
                   jax.ShapeDtypeStruct((t, LANE), F32)],
        compiler_params=_cparams(("arbitrary",), 56),
        name="merge",
    )(yf, yb, big, big, yna, x2, mod3, mod3, mod3, snw, n2w, wbs, wbn, wo, wrh, wrl, br)


ROUTE_TM = 512
GRP_LANE0 = N_EXPERTS


def _route_topk(lg):
    tm = lg.shape[0]
    lane = lax.broadcasted_iota(I32, (tm, LANE), 1)
    neg_inf = jnp.float32(-jnp.inf)
    big_lane = jnp.int32(4 * LANE)
    is_grp = (lane >= GRP_LANE0) & (lane < GRP_LANE0 + N_GROUPS)
    gl = jnp.where(is_grp, lg, neg_inf)
    gmax = jnp.max(gl, axis=-1, keepdims=True)
    grp = jnp.min(jnp.where(gl == gmax, lane, big_lane), axis=-1, keepdims=True) - GRP_LANE0
    psum = jnp.sum(jnp.where(is_grp, jnp.exp(lg - gmax), 0.0), axis=-1, keepdims=True)
    p_grp = 1.0 / psum
    in_g = (lane < N_EXPERTS) & ((lane // EXPERTS_PER_GROUP) == grp)
    el = jnp.where(in_g, lg, neg_inf)
    v1 = jnp.max(el, axis=-1, keepdims=True)
    i1 = jnp.min(jnp.where(el == v1, lane, big_lane), axis=-1, keepdims=True)
    el2 = jnp.where(lane == i1, neg_inf, el)
    v2 = jnp.max(el2, axis=-1, keepdims=True)
    i2 = jnp.min(jnp.where(el2 == v2, lane, big_lane), axis=-1, keepdims=True)
    tt = jnp.exp(v2 - v1)
    den = 1.0 + tt
    ga = p_grp / den
    gb = p_grp * tt / den

    sel1 = lane == i1
    sel2 = lane == i2
    return lane, sel1, sel2, ga, gb


def _route_count_kernel(lg_ref, cnt_ref, run_scr):
    @pl.when(pl.program_id(0) == 0)
    def _():
        run_scr[...] = jnp.zeros_like(run_scr)

    _, sel1, sel2, _, _ = _route_topk(lg_ref[...])
    onehot = jnp.where(sel1 | sel2, 1.0, 0.0)
    run_scr[...] = run_scr[...] + jnp.sum(onehot, axis=0, keepdims=True)
    cnt_ref[...] = run_scr[...].astype(I32)


def _route_dest_kernel(lg_ref, stri_ref, pstart_ref, gs_ref, rt_ref, run_scr):
    @pl.when(pl.program_id(0) == 0)
    def _():
        run_scr[...] = pstart_ref[...]

    lane, sel1, sel2, ga, gb = _route_topk(lg_ref[...])
    onehot = jnp.where(sel1 | sel2, 1.0, 0.0)
    pos = _dot(stri_ref[...], onehot.astype(BF16)) + run_scr[...]
    d1 = jnp.sum(jnp.where(sel1, pos, 0.0), axis=-1, keepdims=True)
    d2 = jnp.sum(jnp.where(sel2, pos, 0.0), axis=-1, keepdims=True)
    run_scr[...] = run_scr[...] + jnp.sum(onehot, axis=0, keepdims=True)

    slab = jnp.where(lane == 0, ga, 0.0)
    slab = jnp.where(lane == 1, gb, slab)
    slab = jnp.where(lane == 2, d1, slab)
    slab = jnp.where(lane == 3, d2, slab)
    gs_ref[...] = slab
    for q in range(ROUTE_TM // LANE):
        blk_t = slab[q * LANE:(q + 1) * LANE, :].T
        rt_ref[:, q * LANE:(q + 1) * LANE] = blk_t[0:8, :].astype(I32)


def _route_count(logits):
    t = logits.shape[0]
    tm = ROUTE_TM
    return pl.pallas_call(
        _route_count_kernel,
        grid=(t // tm,),
        in_specs=[pl.BlockSpec((tm, LANE), lambda i: (i, 0))],
        out_specs=pl.BlockSpec((1, LANE), lambda i: (0, 0)),
        out_shape=jax.ShapeDtypeStruct((1, LANE), I32),
        scratch_shapes=[pltpu.VMEM((1, LANE), F32)],
        compiler_params=_cparams(("arbitrary",), 32),
        name="route_count",
    )(logits)


def _route_dest(logits, stri, pstart_row):
    t = logits.shape[0]
    tm = ROUTE_TM
    return pl.pallas_call(
        _route_dest_kernel,
        grid=(t // tm,),
        in_specs=[pl.BlockSpec((tm, LANE), lambda i: (i, 0)),
                  pl.BlockSpec((tm, tm), lambda i: (0, 0)),
                  pl.BlockSpec((1, LANE), lambda i: (0, 0))],
        out_specs=[pl.BlockSpec((tm, LANE), lambda i: (i, 0)),
                   pl.BlockSpec((8, tm), lambda i: (0, i))],
        out_shape=[jax.ShapeDtypeStruct((t, LANE), F32),
                   jax.ShapeDtypeStruct((8, t), I32)],
        scratch_shapes=[pltpu.VMEM((1, LANE), F32)],
        compiler_params=_cparams(("arbitrary",), 32),
        name="route_dest",
    )(logits, stri, pstart_row)


MOE_BLOCK = 512
DISPATCH_TM = 512
COMBINE_TM = 256


ROW_UNROLL = 8


def _dispatch_kernel(d1_ref, d2_ref, h_ref, xb_in_hbm, xb_hbm, sem):
    del xb_in_hbm
    tm = DISPATCH_TM

    def body(g, carry):
        for u in range(ROW_UNROLL):
            tt = g * ROW_UNROLL + u
            src = h_ref.at[pl.ds(tt, 1)]
            pltpu.make_async_copy(src, xb_hbm.at[pl.ds(d1_ref[0, 0, tt], 1)], sem).start()
            pltpu.make_async_copy(src, xb_hbm.at[pl.ds(d2_ref[0, 0, tt], 1)], sem).start(priority=1)
        return carry

    lax.fori_loop(0, tm // ROW_UNROLL, body, 0)
    for _ in range(2):
        pltpu.make_async_copy(h_ref, xb_hbm.at[pl.ds(0, tm)], sem).wait()


def _dispatch(d1, d2, h2p, n_rows):
    t = h2p.shape[0]
    tm = DISPATCH_TM
    w = h2p.shape[1]
    smem = pl.BlockSpec((1, 1, tm), lambda i: (i, 0, 0), memory_space=pltpu.SMEM)
    anyspec = pl.BlockSpec(memory_space=pl.ANY)
    xb0 = jnp.zeros((n_rows, w), U32)
    return pl.pallas_call(
        _dispatch_kernel,
        grid=(t // tm,),
        in_specs=[smem, smem, pl.BlockSpec((tm, w), lambda i: (i, 0)), anyspec],
        out_specs=anyspec,
        out_shape=jax.ShapeDtypeStruct((n_rows, w), U32),
        scratch_shapes=[pltpu.SemaphoreType.DMA(())],
        input_output_aliases={3: 0},
        compiler_params=pltpu.CompilerParams(dimension_semantics=("arbitrary",), has_side_effects=True),
        name="moe_dispatch",
    )(d1, d2, h2p, xb0)


def _ffn_kernel(be_ref, nu_ref, xb_ref, w1_ref, w3_ref, w2_ref, yb_ref, w13_scr, w2_scr):
    i = pl.program_id(0)
    prev = be_ref[jnp.maximum(i - 1, 0)]
    changed = (i == 0) | (be_ref[i] != prev)
    used = i < nu_ref[0]

    @pl.when(changed & used)
    def _():
        w13_scr[:, :D_EXPERT] = w1_ref[0].astype(BF16)
        w13_scr[:, D_EXPERT:] = w3_ref[0].astype(BF16)
        w2_scr[...] = w2_ref[0].astype(BF16)

    @pl.when(used)
    def _():
        xw = xb_ref[...]
        lo = pltpu.bitcast(xw << 16, F32)
        hi = pltpu.bitcast(xw & jnp.uint32(0xFFFF0000), F32)
        x = jnp.concatenate([lo, hi], axis=1).astype(BF16)
        h = _dot(x, w13_scr[...])
        h1 = h[:, :D_EXPERT]
        h3 = h[:, D_EXPERT:]
        a = (h1 * _sigmoid(h1) * h3).astype(BF16)
        yb_ref[...] = _dot(a, w2_scr[...])

    @pl.when(jnp.logical_not(used))
    def _():
        yb_ref[...] = jnp.zeros_like(yb_ref)


def _ffn(blk_exp, n_used, xb, w1, w3, w2):
    n_rows = xb.shape[0]
    nblk = n_rows // MOE_BLOCK
    wmap = lambda i, be, nu: (be[i], 0, 0)
    grid_spec = pltpu.PrefetchScalarGridSpec(
        num_scalar_prefetch=2, grid=(nblk,),
        in_specs=[pl.BlockSpec((MOE_BLOCK, D_MODEL // 2), lambda i, be, nu: (i, 0)),
                  pl.BlockSpec((1, D_MODEL, D_EXPERT), wmap),
                  pl.BlockSpec((1, D_MODEL, D_EXPERT), wmap),
                  pl.BlockSpec((1, D_EXPERT, D_MODEL), wmap)],
        out_specs=pl.BlockSpec((MOE_BLOCK, D_MODEL), lambda i, be, nu: (i, 0)),
        scratch_shapes=[pltpu.VMEM((D_MODEL, 2 * D_EXPERT), BF16),
                        pltpu.VMEM((D_EXPERT, D_MODEL), BF16)])
    return pl.pallas_call(
        _ffn_kernel, grid_spec=grid_spec,
        out_shape=jax.ShapeDtypeStruct((n_rows, D_MODEL), F32),
        compiler_params=_cparams(("arbitrary",), 40),
        name="moe_ffn",
    )(blk_exp, n_used, xb, w1, w3, w2)


def _combine_kernel(d1_ref, d2_ref, d1n_ref, d2n_ref, yb_hbm, gs_ref, x1_ref, g2_ref, o_ref,
                    ya_scr, yb_scr, sem):
    tm = COMBINE_TM
    i = pl.program_id(0)
    slot = i % 2

    def issue(da_ref, db_ref, s):
        def body(g, carry):
            for u in range(ROW_UNROLL):
                tt = g * ROW_UNROLL + u
                pltpu.make_async_copy(yb_hbm.at[pl.ds(da_ref[0, 0, tt], 1)],
                                      ya_scr.at[s, pl.ds(tt, 1)], sem.at[s]).start()
                pltpu.make_async_copy(yb_hbm.at[pl.ds(db_ref[0, 0, tt], 1)],
                                      yb_scr.at[s, pl.ds(tt, 1)], sem.at[s]).start(priority=1)
            return carry

        lax.fori_loop(0, tm // ROW_UNROLL, body, 0)

    @pl.when(i == 0)
    def _():
        issue(d1_ref, d2_ref, 0)

    @pl.when(i + 1 < pl.num_programs(0))
    def _():
        issue(d1n_ref, d2n_ref, 1 - slot)

    pltpu.make_async_copy(yb_hbm.at[pl.ds(0, tm)], ya_scr.at[slot], sem.at[slot]).wait()
    pltpu.make_async_copy(yb_hbm.at[pl.ds(0, tm)], yb_scr.at[slot], sem.at[slot]).wait()
    ga = gs_ref[:, 0:1]
    gb = gs_ref[:, 1:2]
    o_ref[...] = x1_ref[...] + g2_ref[0] * (ga * ya_scr[slot] + gb * yb_scr[slot])


def _combine(d1, d2, ybuf, gs, x1, mod3, seq_len):
    t = x1.shape[0]
    tm = COMBINE_TM
    nt = t // tm
    tiles_per_seq = seq_len // tm
    smem = pl.BlockSpec((1, 1, tm), lambda i: (i, 0, 0), memory_space=pltpu.SMEM)
    smem_next = pl.BlockSpec((1, 1, tm), lambda i: (jnp.minimum(i + 1, nt - 1), 0, 0), memory_space=pltpu.SMEM)
    return pl.pallas_call(
        _combine_kernel,
        grid=(nt,),
        in_specs=[smem, smem, smem_next, smem_next,
                  pl.BlockSpec(memory_space=pl.ANY),
                  pl.BlockSpec((tm, LANE), lambda i: (i, 0)),
                  pl.BlockSpec((tm, D_MODEL), lambda i: (i, 0)),
                  pl.BlockSpec((1, 1, D_MODEL), lambda i: ((i // tiles_per_seq) * N_MOD + 5, 0, 0))],
        out_specs=pl.BlockSpec((tm, D_MODEL), lambda i: (i, 0)),
        out_shape=jax.ShapeDtypeStruct((t, D_MODEL), F32),
        scratch_shapes=[pltpu.VMEM((2, tm, D_MODEL), F32), pltpu.VMEM((2, tm, D_MODEL), F32),
                        pltpu.SemaphoreType.DMA((2,))],
        compiler_params=_cparams(("arbitrary",), 32),
        name="moe_combine",
    )(d1, d2, d1, d2, ybuf, gs, x1, mod3)


def _rope_tables(seq_len):
    t = jnp.arange(seq_len, dtype=I32)
    row = (t // GRID_W).astype(F32)
    colp = (t % GRID_W).astype(F32)
    half = SSD_STATE // 2
    inv = ROPE_THETA ** (-jnp.arange(0, half, 2, dtype=F32) / half)
    ar = row[:, None] * inv
    ac = colp[:, None] * inv
    cos_t = jnp.concatenate([jnp.cos(ar), jnp.cos(ar), jnp.cos(ac), jnp.cos(ac)], axis=-1)
    sin_t = jnp.concatenate([-jnp.sin(ar), jnp.sin(ar), -jnp.sin(ac), jnp.sin(ac)], axis=-1)
    return cos_t, sin_t


def _scan_tables(rev):
    li = jnp.arange(SSD_CHUNK)[:, None]
    ui = jnp.arange(SSD_CHUNK)[None, :]
    tri = ((ui >= li) if rev else (ui <= li)).astype(BF16)
    col = SSD_HEADS if rev else 0
    j = jnp.arange(LANE)[:, None]
    c = jnp.arange(D_SSD)[None, :]
    ee = (j == col + c // SSD_HEADDIM).astype(BF16)
    return tri, ee


def _pad_lanes(v, width=LANE):
    v = v.reshape(1, -1)
    return jnp.pad(v, ((0, 0), (0, width - v.shape[1])))


def kernel(x, c, ctx, c_ctx, w_mod, b_mod, norm1_w, w_in, conv_w, conv_b, a_log_f, a_log_b, dt_bias_f, dt_bias_b, d_skip, ssd_norm_w, q_norm_w, k_norm_w, rpb, w_br_ssd, w_br_na, w_out, norm2_w, w_grp, b_grp, w_rt, b_rt, w1, w3, w2):
    nb, seq_len, d = x.shape
    ctx_len = ctx.shape[1]
    t = nb * seq_len
    tc = nb * ctx_len
    assert w_mod.shape[0] == 1 and d == D_MODEL and nb <= 7
    assert seq_len % 256 == 0 and ctx_len % SSD_CHUNK == 0 and seq_len // GRID_W >= NA_KH

    cin = jnp.concatenate([c, c_ctx[None, :], jnp.zeros((8 - nb - 1, d), F32)], axis=0)
    mod = _modulation(cin, w_mod[0], b_mod[0])
    mod3 = mod.reshape(8 * N_MOD, 1, D_MODEL)

    wi = w_in[0]
    o_z, o_xbc, o_dt = 0, D_SSD, D_SSD + D_XBC
    o_qkv = o_dt + 2 * SSD_HEADS
    o_g = o_qkv + 3 * D_NA
    w_z = wi[:, o_z:o_xbc]
    w_xbc = wi[:, o_xbc:o_dt]
    w_dt = wi[:, o_dt:o_qkv]
    w_qkv = wi[:, o_qkv:o_g]
    w_g = wi[:, o_g:]
    w_lat = jnp.concatenate([w_xbc.astype(BF16), w_z.astype(BF16), w_g.astype(BF16), w_qkv.astype(BF16)], axis=1)
    w_dtp = jnp.pad(w_dt.astype(BF16), ((0, 0), (0, LANE - 2 * SSD_HEADS)))
    n1w = norm1_w[0].reshape(1, D_MODEL)

    x2 = x.reshape(t, D_MODEL)
    ctx2 = ctx.reshape(tc, D_MODEL)
    tm_in = 2048 if seq_len % 2048 == 0 else 256
    tiles = seq_len // tm_in
    tn_in = 1024
    big, dt_raw = _inproj(x2, mod3, lambda i: i // tiles, n1w, w_lat, w_dtp, tm_in, tn_in, COLS_LAT, lambda j: j)
    tm_c = 256 if tc % 256 == 0 else SSD_CHUNK
    xbc_tiles = D_XBC // tn_in
    ctx_skip = (COLS_LAT - 2 * D_NA) // tn_in - xbc_tiles
    bigc, dtc_raw = _inproj(ctx2, mod3, lambda i: nb, n1w, w_lat, w_dtp, tm_c, tn_in, COLS_CTX,
                            lambda j: j + ctx_skip * (j >= xbc_tiles))
    ZCOL, GCOL, QCOL, KCOL, VCOL = 2, 3, 8, 9, 10
    KC_COL, VC_COL = 4, 5

    dtbias = _pad_lanes(jnp.concatenate([dt_bias_f[0], dt_bias_b[0]]))
    cw = conv_w[0]
    cbias = conv_b[0].reshape(1, D_XBC)
    cos_t, sin_t = _rope_tables(seq_len)
    xs, bc, dts = _ssd_prep(big, dt_raw, cw, cbias, dtbias, cos_t, sin_t, seq_len, 256)
    ctl = 256 if ctx_len % 256 == 0 else SSD_CHUNK
    ones_t = jnp.ones((ctx_len, LANE), F32)
    zeros_t = jnp.zeros((ctx_len, LANE), F32)
    xsc, bcc, dtsc = _ssd_prep(bigc, dtc_raw, cw, cbias, dtbias, ones_t, zeros_t, ctx_len, ctl)

    alog = _pad_lanes(jnp.concatenate([a_log_f[0], a_log_b[0]]))
    dskip = jnp.repeat(d_skip[0], SSD_HEADDIM).reshape(1, D_SSD)
    tri_f, ee_f = _scan_tables(False)
    tri_b, ee_b = _scan_tables(True)
    hcf = _ssd_state(xsc, bcc, dtsc, alog, tri_f, ee_f, nb, False)
    hcb = _ssd_state(xsc, bcc, dtsc, alog, tri_b, ee_b, nb, True)
    yf = _ssd_scan(xs, bc, dts, alog, dskip, tri_f, ee_f, hcf, nb, False)
    yb = _ssd_scan(xs, bc, dts, alog, dskip, tri_b, ee_b, hcb, nb, True)

    qw = jnp.tile(q_norm_w[0], NA_HEADS).reshape(1, D_NA)
    kw = jnp.tile(k_norm_w[0], NA_HEADS).reshape(1, D_NA)
    gi = jnp.arange(LANE)
    gmat = ((gi[:, None] // NA_HEADDIM) == (gi[None, :] // NA_HEADDIM)).astype(BF16) * (1.0 / NA_HEADDIM)
    gmat = gmat.astype(BF16)
    qn, kn = _na_prep(big, QCOL, KCOL, qw, kw, gmat, 512 if t % 512 == 0 else 256)
    kcn = _na_prep(bigc, None, KC_COL, qw, kw, gmat, 256 if tc % 256 == 0 else SSD_CHUNK)
    bias_tab = _na_bias_table(rpb[0])
    y_na = _neigh_attention(qn, kn, big, kcn, bigc, bias_tab, nb, seq_len, ctx_len, VCOL, VC_COL)

    w_r = jnp.pad(jnp.concatenate([w_rt[0], w_grp[0]], axis=1), ((0, 0), (0, LANE - N_EXPERTS - N_GROUPS)))
    wrh = w_r.astype(BF16)
    wrl = (w_r - wrh.astype(F32)).astype(BF16)
    br = _pad_lanes(jnp.concatenate([b_rt[0], b_grp[0]]))
    x1, h2p, logits = _merge(yf, yb, big, y_na, x2, mod3,
                             ssd_norm_w[0].reshape(1, D_SSD), norm2_w[0].reshape(1, D_MODEL),
                             w_br_ssd[0].astype(BF16), w_br_na[0].astype(BF16), w_out[0].astype(BF16),
                             wrh, wrl, br, seq_len, 512 if seq_len % 512 == 0 else 256, ZCOL, GCOL)

    si = jnp.arange(ROUTE_TM)
    stri = (si[None, :] < si[:, None]).astype(BF16)
    cnt = _route_count(logits)

    counts = cnt[0, :N_EXPERTS]
    padded = (counts + MOE_BLOCK - 1) // MOE_BLOCK * MOE_BLOCK
    pend = jnp.cumsum(padded)
    pstart = pend - padded
    nblk = (2 * t + N_EXPERTS * (MOE_BLOCK - 1) + MOE_BLOCK - 1) // MOE_BLOCK
    n_used = (pend[-1] // MOE_BLOCK).astype(I32).reshape(1)
    blk_row0 = jnp.arange(nblk, dtype=I32) * MOE_BLOCK
    blk_exp = jnp.minimum(jnp.sum((pend[None, :] <= blk_row0[:, None]).astype(I32), axis=1), N_EXPERTS - 1)
    gs, rt = _route_dest(logits, stri, _pad_lanes(pstart.astype(F32)))

    def tok_tiles(row, tm):
        return rt[row].reshape(t // tm, 1, tm)

    xb = _dispatch(tok_tiles(2, DISPATCH_TM), tok_tiles(3, DISPATCH_TM), h2p, nblk * MOE_BLOCK)
    ybuf = _ffn(blk_exp, n_used, xb, w1[0], w3[0], w2[0])
    out = _combine(tok_tiles(2, COMBINE_TM), tok_tiles(3, COMBINE_TM), ybuf, gs, x1, mod3, seq_len)
    return out.reshape(nb, seq_len, D_MODEL)
```

```python
import functools
import math

import jax
import jax.numpy as jnp
from jax import lax
from jax.experimental import pallas as pl
from jax.experimental.pallas import tpu as pltpu

F32 = jnp.float32
BF16 = jnp.bfloat16
I32 = jnp.int32
U32 = jnp.uint32

D_MODEL = 1024
GRID_W = 64
D_SSD = 2048
SSD_HEADDIM = 64
SSD_HEADS = 32
SSD_GROUPS = 8
HEADS_PER_GROUP = 4
SSD_STATE = 128
SSD_CHUNK = 128
CONV_K = 5
D_BC = SSD_GROUPS * SSD_STATE
D_XBC = D_SSD + 2 * D_BC
NA_HEADDIM = 64
NA_HEADS = 16
D_NA = 1024
NA_KH = 8
NA_KW = 16
ROPE_THETA = 10000.0
N_GROUPS = 4
EXPERTS_PER_GROUP = 8
N_EXPERTS = 32
D_EXPERT = 512
N_MOD = 6
EPS = 1e-6
NEG_BIG = -1e30

COLS_LAT = D_XBC + D_SSD + 2 * D_MODEL + 3 * D_NA
COLS_CTX = D_XBC + 2 * D_NA
LANE = 128

V7X_VMEM_BYTES = 64 * 1024 * 1024


def _cparams(sem, vmem_mb):
    return pltpu.CompilerParams(dimension_semantics=sem, vmem_limit_bytes=vmem_mb * 1024 * 1024)


def _sigmoid(x):
    return 1.0 / (1.0 + jnp.exp(-x))


def _split2(x):
    hi = x.astype(BF16)
    lo = (x - hi.astype(F32)).astype(BF16)
    return hi, lo


def _split3(x):
    hi = x.astype(BF16)
    r = x - hi.astype(F32)
    mid = r.astype(BF16)
    lo = (r - mid.astype(F32)).astype(BF16)
    return hi, mid, lo


def _pack_bf16_pairs(x):
    bits = pltpu.bitcast(x.astype(BF16).astype(F32), U32)
    n = x.shape[1] // 2
    return (bits[:, :n] >> 16) | bits[:, n:]


def _unpack_bf16_pairs(w):
    return pltpu.bitcast(w << 16, F32), pltpu.bitcast(w & jnp.uint32(0xFFFF0000), F32)


def _dot(a, b):
    return jnp.dot(a, b, preferred_element_type=F32)


def _dot_nt(a, b):
    return lax.dot_general(a, b, (((1,), (1,)), ((), ())), preferred_element_type=F32)


def _dot_tn(a, b):
    return lax.dot_general(a, b, (((0,), (0,)), ((), ())), preferred_element_type=F32)


def _mod_kernel(c_ref, w_ref, b_ref, o_ref):
    c = c_ref[...]
    a = (c * _sigmoid(c)).astype(BF16)
    o_ref[...] = _dot(a, w_ref[...].astype(BF16)) + b_ref[...]


def _modulation(cin, w_mod, b_mod):
    n = w_mod.shape[1]
    tn = 1536
    return pl.pallas_call(
        _mod_kernel,
        grid=(n // tn,),
        in_specs=[pl.BlockSpec((8, D_MODEL), lambda j: (0, 0)),
                  pl.BlockSpec((D_MODEL, tn), lambda j: (0, j)),
                  pl.BlockSpec((1, tn), lambda j: (0, j))],
        out_specs=pl.BlockSpec((8, tn), lambda j: (0, j)),
        out_shape=jax.ShapeDtypeStruct((8, n), F32),
        compiler_params=_cparams(("arbitrary",), 40),
        name="modulation",
    )(cin, w_mod, b_mod.reshape(1, n))


def _inproj_kernel(x_ref, sh_ref, sc_ref, nw_ref, wa_ref, wb_ref, wdt_ref, o_ref, dt_ref, h_scr, *, tm, n_a):
    j = pl.program_id(1)

    @pl.when(j == 0)
    def _():
        scale = 1.0 + sc_ref[0]
        shift = sh_ref[0]
        nw = nw_ref[...]

        def body(r, carry):
            rows = pl.ds(pl.multiple_of(r * 128, 128), 128)
            x = x_ref[rows, :]
            ms = jnp.mean(x * x, axis=-1, keepdims=True)
            h = x * lax.rsqrt(ms + EPS) * nw * scale + shift
            h_scr[rows, :] = h.astype(BF16)
            return carry

        lax.fori_loop(0, tm // 128, body, 0)
        dt_ref[...] = _dot(h_scr[...], wdt_ref[...])

    @pl.when(j < n_a)
    def _():
        o_ref[...] = _dot(h_scr[...], wa_ref[...]).astype(BF16)

    @pl.when(j >= n_a)
    def _():
        o_ref[...] = _dot(h_scr[...], wb_ref[...]).astype(BF16)


def _inproj(x2, mod3, mod_row_fn, norm_w, wa, wb, wdt, tm, tn, a_tiles, b_tiles):
    t = x2.shape[0]
    n_a, n_b = len(a_tiles), len(b_tiles)
    a_tab = list(a_tiles) + [a_tiles[-1]] * n_b
    b_tab = [b_tiles[0]] * n_a + list(b_tiles)

    def pick(tab, j):
        idx = jnp.int32(tab[-1])
        for pos in range(len(tab) - 2, -1, -1):
            idx = jnp.where(j <= pos, tab[pos], idx)
        return idx

    kern = functools.partial(_inproj_kernel, tm=tm, n_a=n_a)
    return pl.pallas_call(
        kern,
        grid=(t // tm, n_a + n_b),
        in_specs=[pl.BlockSpec((tm, D_MODEL), lambda i, j: (i, 0)),
                  pl.BlockSpec((1, 1, D_MODEL), lambda i, j: (mod_row_fn(i) * N_MOD + 0, 0, 0)),
                  pl.BlockSpec((1, 1, D_MODEL), lambda i, j: (mod_row_fn(i) * N_MOD + 1, 0, 0)),
                  pl.BlockSpec((1, D_MODEL), lambda i, j: (0, 0)),
                  pl.BlockSpec((D_MODEL, tn), lambda i, j: (0, pick(a_tab, j))),
                  pl.BlockSpec((D_MODEL, tn), lambda i, j: (0, pick(b_tab, j))),
                  pl.BlockSpec((D_MODEL, LANE), lambda i, j: (0, 0))],
        out_specs=[pl.BlockSpec((tm, tn), lambda i, j: (i, j)),
                   pl.BlockSpec((tm, LANE), lambda i, j: (i, 0))],
        out_shape=[jax.ShapeDtypeStruct((t, (n_a + n_b) * tn), BF16),
                   jax.ShapeDtypeStruct((t, LANE), F32)],
        scratch_shapes=[pltpu.VMEM((tm, D_MODEL), BF16)],
        compiler_params=_cparams(("arbitrary", "arbitrary"), 52),
        name="inproj",
    )(x2, mod3, mod3, norm_w, wa, wb, wdt)


CONV_OFFSETS = (-2, -1, 1, 2)


def _ssd_prep_kernel(prev_ref, cur_ref, next_ref, dt_ref, cw_ref, cb_ref, dtbias_ref, cos_ref, sin_ref,
                     shift_ref, etop_ref, ebot_ref, xs_ref, bc_ref, dto_ref, *, tl, tiles_per_seq):
    i = pl.program_id(0)
    first = (i % tiles_per_seq) == 0
    last = (i % tiles_per_seq) == tiles_per_seq - 1
    lane = lax.broadcasted_iota(I32, (tl, LANE), 1)
    lo_half = ((lane // 32) % 2) == 0
    shift = shift_ref[...]
    etop = etop_ref[...]
    ebot = ebot_ref[...]
    cosv = cos_ref[...]
    sinv = sin_ref[...]
    cwid = 256
    for c in range(D_XBC // cwid):
        cs = slice(c * cwid, (c + 1) * cwid)
        w = cw_ref[:, cs]
        xc = cur_ref[:, cs]
        sh = _dot(shift, xc).reshape(tl // 8, len(CONV_OFFSETS), 8, cwid)
        tap = lambda k: sh[:, k].reshape(tl, cwid)
        acc = cb_ref[:, cs] + xc.astype(F32) * w[2:3, :]
        acc = acc + tap(0) * w[0:1, :] + tap(1) * w[1:2, :] + tap(2) * w[3:4, :] + tap(3) * w[4:5, :]
        top = _dot(etop, prev_ref[:, cs])
        bot = _dot(ebot, next_ref[:, cs])
        top_c = jnp.where(first, 0.0, top[0:8] * w[0:1, :] + top[8:16] * w[1:2, :])
        bot_c = jnp.where(last, 0.0, bot[0:8] * w[3:4, :] + bot[8:16] * w[4:5, :])
        acc = jnp.concatenate([acc[0:8] + top_c, acc[8:tl - 8], acc[tl - 8:tl] + bot_c], axis=0)
        y = acc * _sigmoid(acc)
        if c * cwid < D_SSD:
            xs_ref[:, cs] = y.astype(BF16)
        else:
            for g in range(cwid // LANE):
                yg = y[:, g * LANE:(g + 1) * LANE]
                partner = jnp.where(lo_half, pltpu.roll(yg, 96, 1), pltpu.roll(yg, 32, 1))
                o = yg * cosv + partner * sinv
                c0 = c * cwid - D_SSD + g * LANE
                bc_ref[:, c0:c0 + LANE] = o.astype(BF16)

    v = dt_ref[...] + dtbias_ref[...]
    dto_ref[...] = jnp.maximum(v, 0.0) + jnp.log(1.0 + jnp.exp(-jnp.abs(v)))


def _conv_shift_tables(tl):
    t = jnp.arange(tl)[:, None]
    u = jnp.arange(tl)[None, :]
    shift = jnp.stack([(u == t + off) for off in CONV_OFFSETS], axis=0).reshape(len(CONV_OFFSETS), tl // 8, 8, tl)
    shift = shift.transpose(1, 0, 2, 3).reshape(len(CONV_OFFSETS) * tl, tl).astype(BF16)
    r = jnp.arange(16)[:, None]
    q = jnp.arange(16)[None, :]
    etop = (((r < 8) & (q == r + 14)) | ((r >= 8) & (q == r - 8 + 15))).astype(BF16)
    ebot = (((r < 8) & (q == r - 7)) | ((r >= 8) & (q == r - 8 - 6))).astype(BF16)
    return shift, etop, ebot


def _ssd_prep(big, dt_raw, conv_w, conv_b, dtbias, cos_t, sin_t, seq_len, tl):
    t = big.shape[0]
    tps = seq_len // tl
    hb = tl // 16
    nhalo = t // 16
    kern = functools.partial(_ssd_prep_kernel, tl=tl, tiles_per_seq=tps)
    shift, etop, ebot = _conv_shift_tables(tl)
    const = lambda i: (0, 0)
    return pl.pallas_call(
        kern,
        grid=(t // tl,),
        in_specs=[pl.BlockSpec((16, D_XBC), lambda i: (jnp.maximum(i * hb - 1, 0), 0)),
                  pl.BlockSpec((tl, D_XBC), lambda i: (i, 0)),
                  pl.BlockSpec((16, D_XBC), lambda i: (jnp.minimum((i + 1) * hb, nhalo - 1), 0)),
                  pl.BlockSpec((tl, LANE), lambda i: (i, 0)),
                  pl.BlockSpec((CONV_K, D_XBC), const),
                  pl.BlockSpec((1, D_XBC), const),
                  pl.BlockSpec((1, LANE), const),
                  pl.BlockSpec((tl, LANE), lambda i: (i % tps, 0)),
                  pl.BlockSpec((tl, LANE), lambda i: (i % tps, 0)),
                  pl.BlockSpec((len(CONV_OFFSETS) * tl, tl), const),
                  pl.BlockSpec((16, 16), const),
                  pl.BlockSpec((16, 16), const)],
        out_specs=[pl.BlockSpec((tl, D_SSD), lambda i: (i, 0)),
                   pl.BlockSpec((tl, 2 * D_BC), lambda i: (i, 0)),
                   pl.BlockSpec((tl, LANE), lambda i: (i, 0))],
        out_shape=[jax.ShapeDtypeStruct((t, D_SSD), BF16),
                   jax.ShapeDtypeStruct((t, 2 * D_BC), BF16),
                   jax.ShapeDtypeStruct((t, LANE), F32)],
        compiler_params=_cparams(("arbitrary",), 40),
        name="ssd_prep",
    )(big, big, big, dt_raw, conv_w, conv_b, dtbias, cos_t, sin_t, shift, etop, ebot)


GROUP_W = HEADS_PER_GROUP * SSD_HEADDIM


def _scan_decays(dt_ref, alog_ref, tri_ref, ee_ref, rev):
    edge = 0 if rev else SSD_CHUNK - 1
    a_row = -jnp.exp(alog_ref[...])
    dt = dt_ref[...]
    trib = tri_ref[...]
    d_hi, d_mid, d_lo = _split3(dt * a_row)
    cum = _dot(trib, d_hi) + _dot(trib, d_mid) + _dot(trib, d_lo)
    tot = cum[edge:edge + 1, :]
    ee = ee_ref[...]
    w_heads = (jnp.exp(tot - cum) * dt).astype(BF16)
    t_hi, t_lo = _split2(jnp.broadcast_to(jnp.exp(tot), (8, LANE)))
    etot = (_dot(t_hi, ee) + _dot(t_lo, ee))[0:1, :]
    return dt, trib, cum, w_heads, etot


def _spread(per_head, ee_ref, g):
    return _dot(per_head, ee_ref[:, g * GROUP_W:(g + 1) * GROUP_W])


def _ssd_state_kernel(xs_ref, bc_ref, dt_ref, alog_ref, tri_ref, ee_ref, hfin_ref, h_scr, *, rev, nc):
    k = pl.program_id(1)

    @pl.when(k == 0)
    def _():
        h_scr[...] = jnp.zeros_like(h_scr)

    _, _, _, w_heads, etot = _scan_decays(dt_ref, alog_ref, tri_ref, ee_ref, rev)
    for g in range(SSD_GROUPS):
        gs = slice(g * GROUP_W, (g + 1) * GROUP_W)
        xw = (xs_ref[:, gs].astype(F32) * _spread(w_heads, ee_ref, g)).astype(BF16)
        h_scr[:, gs] = etot[:, gs] * h_scr[:, gs] + _dot_tn(bc_ref[:, g * SSD_STATE:(g + 1) * SSD_STATE], xw)

    @pl.when(k == nc - 1)
    def _():
        hfin_ref[0] = h_scr[...]


def _scan_chunk_stages(xs_ref, bc_ref, dt_ref, alog_ref, dskip_ref, tri_ref, ee_ref, y_ref, h_scr, rev):
    L = SSD_CHUNK
    col = SSD_HEADS if rev else 0
    dt, trib, cum, w_heads, etot = _scan_decays(dt_ref, alog_ref, tri_ref, ee_ref, rev)
    tri = trib > 0.5
    cum_t = cum.T
    dt_t = dt.T
    o_heads = jnp.exp(cum).astype(BF16)

    gw = GROUP_W
    lane_head = lax.broadcasted_iota(I32, (L, gw), 1) // SSD_HEADDIM

    def b_of(g):
        return bc_ref[:, g * SSD_STATE:(g + 1) * SSD_STATE]

    def c_of(g):
        return bc_ref[:, D_BC + g * SSD_STATE:D_BC + (g + 1) * SSD_STATE]

    def operands(g, cb):
        gs = slice(g * gw, (g + 1) * gw)
        xs_g = xs_ref[:, gs]
        ms = []
        for r in range(HEADS_PER_GROUP):
            hh = col + g * HEADS_PER_GROUP + r
            seg = cum[:, hh:hh + 1] - cum_t[hh:hh + 1, :]
            lm = jnp.exp(jnp.where(tri, seg, NEG_BIG)) * dt_t[hh:hh + 1, :]
            ms.append((cb * lm).astype(BF16))
        mcat = jnp.concatenate(ms, axis=1)
        zero = jnp.zeros_like(xs_g)
        bd = jnp.concatenate([jnp.where(lane_head == r, xs_g, zero) for r in range(HEADS_PER_GROUP)], axis=0)
        xs_f = xs_g.astype(F32)
        xw = (xs_f * _spread(w_heads, ee_ref, g)).astype(BF16)
        return mcat, bd, xs_f, xw, _spread(o_heads, ee_ref, g)

    def cb_of(g):
        return _dot_nt(c_of(g), b_of(g))

    yield
    cb_next = cb_of(0)
    yield
    nxt = operands(0, cb_next)
    cb_next = cb_of(1)
    for g in range(SSD_GROUPS):
        yield
        gs = slice(g * gw, (g + 1) * gw)
        mcat, bd, xs_f, xw, oscale_g = nxt
        h_g = h_scr[:, gs]
        y_diag = _dot(mcat, bd)
        y_off = _dot(c_of(g), h_g.astype(BF16))
        loc = _dot_tn(b_of(g), xw)
        if g + 1 < SSD_GROUPS:
            nxt = operands(g + 1, cb_next)
            if g + 2 < SSD_GROUPS:
                cb_next = cb_of(g + 2)
        y = y_diag + oscale_g * y_off
        if not rev:
            y = y + xs_f * dskip_ref[:, gs]
        y_ref[:, gs] = y.astype(BF16)
        h_scr[:, gs] = etot[:, gs] * h_g + loc


def _ssd_scan_kernel(xsf_ref, bcf_ref, dtf_ref, xsb_ref, bcb_ref, dtb_ref, alog_ref, dskip_ref,
                     trif_ref, eef_ref, trib_ref, eeb_ref, h0f_ref, h0b_ref, yf_ref, yb_ref, hf_scr, hb_scr):
    @pl.when(pl.program_id(1) == 0)
    def _():
        hf_scr[...] = h0f_ref[0]
        hb_scr[...] = h0b_ref[0]

    chains = [_scan_chunk_stages(xsf_ref, bcf_ref, dtf_ref, alog_ref, dskip_ref, trif_ref, eef_ref,
                                 yf_ref, hf_scr, False),
              _scan_chunk_stages(xsb_ref, bcb_ref, dtb_ref, alog_ref, dskip_ref, trib_ref, eeb_ref,
                                 yb_ref, hb_scr, True)]
    while chains:
        for ch in list(chains):
            try:
                next(ch)
            except StopIteration:
                chains.remove(ch)


def _scan_rowmap(nc, rev):
    def rowmap(b, k):
        c = (nc - 1 - k) if rev else k
        return (b * nc + c, 0)
    return rowmap


def _ssd_state(xs, bc, dt, alog_row, tri, ee, nb, rev):
    t = xs.shape[0]
    nc = t // nb // SSD_CHUNK
    rowmap = _scan_rowmap(nc, rev)
    const2 = lambda b, k: (0, 0)
    return pl.pallas_call(
        functools.partial(_ssd_state_kernel, rev=rev, nc=nc),
        grid=(nb, nc),
        in_specs=[pl.BlockSpec((SSD_CHUNK, D_SSD), rowmap),
                  pl.BlockSpec((SSD_CHUNK, D_BC), rowmap),
                  pl.BlockSpec((SSD_CHUNK, LANE), rowmap),
                  pl.BlockSpec((1, LANE), const2),
                  pl.BlockSpec((SSD_CHUNK, SSD_CHUNK), const2),
                  pl.BlockSpec((LANE, D_SSD), const2)],
        out_specs=pl.BlockSpec((1, SSD_STATE, D_SSD), lambda b, k: (b, 0, 0)),
        out_shape=jax.ShapeDtypeStruct((nb, SSD_STATE, D_SSD), F32),
        scratch_shapes=[pltpu.VMEM((SSD_STATE, D_SSD), F32)],
        compiler_params=_cparams(("arbitrary", "arbitrary"), 40),
        name="ssd_state_bwd" if rev else "ssd_state_fwd",
    )(xs, bc, dt, alog_row, tri, ee)


def _ssd_scan(xs, bc, dt, alog_row, dskip_row, tables_f, tables_b, h0f, h0b, nb):
    t = xs.shape[0]
    nc = t // nb // SSD_CHUNK
    fmap = _scan_rowmap(nc, False)
    bmap = _scan_rowmap(nc, True)
    const2 = lambda b, k: (0, 0)
    chunk = lambda width, m: pl.BlockSpec((SSD_CHUNK, width), m)
    tri_spec = pl.BlockSpec((SSD_CHUNK, SSD_CHUNK), const2)
    ee_spec = pl.BlockSpec((LANE, D_SSD), const2)
    state = pl.BlockSpec((1, SSD_STATE, D_SSD), lambda b, k: (b, 0, 0))
    return pl.pallas_call(
        _ssd_scan_kernel,
        grid=(nb, nc),
        in_specs=[chunk(D_SSD, fmap), chunk(2 * D_BC, fmap), chunk(LANE, fmap),
                  chunk(D_SSD, bmap), chunk(2 * D_BC, bmap), chunk(LANE, bmap),
                  pl.BlockSpec((1, LANE), const2),
                  pl.BlockSpec((1, D_SSD), const2),
                  tri_spec, ee_spec, tri_spec, ee_spec, state, state],
        out_specs=[chunk(D_SSD, fmap), chunk(D_SSD, bmap)],
        out_shape=[jax.ShapeDtypeStruct((t, D_SSD), BF16)] * 2,
        scratch_shapes=[pltpu.VMEM((SSD_STATE, D_SSD), F32)] * 2,
        compiler_params=_cparams(("arbitrary", "arbitrary"), 48),
        name="ssd_scan",
    )(xs, bc, dt, xs, bc, dt, alog_row, dskip_row, *tables_f, *tables_b, h0f, h0b)


def _headnorm_cols(src_ref, w_ref, g, dst_ref, scale):
    for c in range(D_NA // LANE):
        cs = slice(c * LANE, (c + 1) * LANE)
        x = src_ref[:, cs].astype(F32)
        hi, lo = _split2(x * x)
        ms = _dot(hi, g) + _dot(lo, g)
        y = x * lax.rsqrt(ms + EPS) * w_ref[:, cs]
        if scale is not None:
            y = y * scale
        dst_ref[:, cs] = y.astype(BF16)


def _na_prep_qk_kernel(q_ref, k_ref, qw_ref, kw_ref, g_ref, qo_ref, ko_ref):
    g = g_ref[...]
    _headnorm_cols(q_ref, qw_ref, g, qo_ref, NA_HEADDIM ** -0.5)
    _headnorm_cols(k_ref, kw_ref, g, ko_ref, None)


def _na_prep_k_kernel(k_ref, kw_ref, g_ref, ko_ref):
    _headnorm_cols(k_ref, kw_ref, g_ref[...], ko_ref, None)


def _na_prep(big, qcol, kcol, qw, kw, gmat, tm):
    t = big.shape[0]
    blk = lambda cidx: pl.BlockSpec((tm, D_NA), lambda i: (i, cidx))
    row = pl.BlockSpec((1, D_NA), lambda i: (0, 0))
    gspec = pl.BlockSpec((LANE, LANE), lambda i: (0, 0))
    out = pl.BlockSpec((tm, D_NA), lambda i: (i, 0))
    if qcol is None:
        return pl.pallas_call(
            _na_prep_k_kernel, grid=(t // tm,),
            in_specs=[blk(kcol), row, gspec], out_specs=out,
            out_shape=jax.ShapeDtypeStruct((t, D_NA), BF16),
            compiler_params=_cparams(("arbitrary",), 32), name="na_prep_ctx",
        )(big, kw, gmat)
    return pl.pallas_call(
        _na_prep_qk_kernel, grid=(t // tm,),
        in_specs=[blk(qcol), blk(kcol), row, row, gspec], out_specs=[out, out],
        out_shape=[jax.ShapeDtypeStruct((t, D_NA), BF16)] * 2,
        compiler_params=_cparams(("arbitrary",), 32), name="na_prep",
    )(big, big, qw, kw, gmat)


NA_LOOKAHEAD = 2


NA_ROWS_PER_STEP = 2


def _na_row_stages(r, qrow, q_ref, k_ref, v_ref, kc_ref, vc_ref, bias_ref, o_ref, rows):
    rs = jnp.clip(r - NA_KH // 2, 0, rows - NA_KH)
    start = pl.multiple_of(rs * GRID_W, GRID_W)
    nk = NA_KH * GRID_W
    lane = lax.broadcasted_iota(I32, (GRID_W, LANE), 1)
    first_head = lane < NA_HEADDIM
    npair = NA_HEADS // 2
    qrows = slice(qrow, qrow + GRID_W)

    def scores(j):
        cs = slice(j * LANE, (j + 1) * LANE)
        qp = q_ref[qrows, cs]
        zero = jnp.zeros_like(qp)
        qs = jnp.concatenate([jnp.where(first_head, qp, zero), jnp.where(first_head, zero, qp)], axis=0)
        kb = k_ref[pl.ds(start, nk), cs]
        s_loc = _dot_nt(qs, kb) + bias_ref[0, j * LANE:(j + 1) * LANE, :]
        s_ctx = _dot_nt(qs, kc_ref[:, cs])
        return s_loc, s_ctx

    pending = [scores(j) for j in range(NA_LOOKAHEAD)]
    for j in range(npair):
        yield
        cs = slice(j * LANE, (j + 1) * LANE)
        s_loc, s_ctx = pending.pop(0)
        if j + NA_LOOKAHEAD < npair:
            pending.append(scores(j + NA_LOOKAHEAD))
        vb = v_ref[pl.ds(start, nk), cs]
        m = jnp.maximum(jnp.max(s_loc, axis=-1, keepdims=True), jnp.max(s_ctx, axis=-1, keepdims=True))
        p_loc = jnp.exp(s_loc - m)
        p_ctx = jnp.exp(s_ctx - m)
        den = jnp.sum(p_loc, axis=-1, keepdims=True) + jnp.sum(p_ctx, axis=-1, keepdims=True)
        o = _dot(p_loc.astype(BF16), vb) + _dot(p_ctx.astype(BF16), vc_ref[:, cs])
        o = o / den
        o_ref[qrows, cs] = jnp.where(first_head, o[:GRID_W], o[GRID_W:]).astype(BF16)


def _na_kernel(q_ref, k_ref, v_ref, kc_ref, vc_ref, *rest, rows):
    bias_refs, o_ref = rest[:NA_ROWS_PER_STEP], rest[NA_ROWS_PER_STEP]
    r0 = pl.program_id(1) * NA_ROWS_PER_STEP
    chains = [_na_row_stages(r0 + u, u * GRID_W, q_ref, k_ref, v_ref, kc_ref, vc_ref, bias_refs[u], o_ref, rows)
              for u in range(NA_ROWS_PER_STEP)]
    while chains:
        for ch in list(chains):
            try:
                next(ch)
            except StopIteration:
                chains.remove(ch)


def _neigh_attention(qn, kn, big, kcn, bigc, bias_tab, nb, seq_len, ctx_len, vcol, vccol):
    t = qn.shape[0]
    rows = seq_len // GRID_W
    rps = NA_ROWS_PER_STEP
    steps = rows // rps
    kern = functools.partial(_na_kernel, rows=rows)

    def pat(u):
        def index_map(b, s):
            r = s * rps + u
            return (r - jnp.clip(r - NA_KH // 2, 0, rows - NA_KH), 0, 0)
        return index_map

    bias_specs = [pl.BlockSpec((1, NA_HEADS * GRID_W, NA_KH * GRID_W), pat(u)) for u in range(rps)]
    return pl.pallas_call(
        kern,
        grid=(nb, steps),
        in_specs=[pl.BlockSpec((rps * GRID_W, D_NA), lambda b, s: (b * steps + s, 0)),
                  pl.BlockSpec((seq_len, D_NA), lambda b, s: (b, 0), pipeline_mode=pl.Buffered(1)),
                  pl.BlockSpec((seq_len, D_NA), lambda b, s: (b, vcol), pipeline_mode=pl.Buffered(1)),
                  pl.BlockSpec((ctx_len, D_NA), lambda b, s: (b, 0), pipeline_mode=pl.Buffered(1)),
                  pl.BlockSpec((ctx_len, D_NA), lambda b, s: (b, vccol), pipeline_mode=pl.Buffered(1))] + bias_specs,
        out_specs=pl.BlockSpec((rps * GRID_W, D_NA), lambda b, s: (b * steps + s, 0)),
        out_shape=jax.ShapeDtypeStruct((t, D_NA), BF16),
        compiler_params=_cparams(("arbitrary", "arbitrary"), 48),
        name="neigh_attention",
    )(qn, kn, big, kcn, bigc, *([bias_tab] * rps))


def _na_bias_kernel(rpb_ref, o_ref):
    rp = rpb_ref[0]
    r64 = pltpu.roll(rp, GRID_W, 1)
    c = lax.broadcasted_iota(I32, (GRID_W, LANE), 0)
    kc = lax.broadcasted_iota(I32, (GRID_W, LANE), 1) % GRID_W
    cs = jnp.clip(c - NA_KW // 2, 0, GRID_W - NA_KW)
    valid = (kc >= cs) & (kc < cs + NA_KW)
    pair = []
    for d in range(2 * NA_KH - 2):
        vec = rp[d:d + 1, :] + r64[d + 1:d + 2, :]
        w = pltpu.roll(jnp.broadcast_to(vec, (GRID_W, LANE)), LANE - (NA_KW - 1), 1, stride=1, stride_axis=0)
        pair.append(jnp.where(valid, w, NEG_BIG))
    for p in range(NA_KH):
        for ii in range(NA_KH // 2):
            o_ref[p, :, ii * LANE:(ii + 1) * LANE] = pair[2 * ii - p + NA_KH - 1]


def _na_bias_table(rpb):
    rp = jnp.pad(rpb, ((0, 0), (0, 1), (0, LANE - (2 * NA_KW - 1))))
    return pl.pallas_call(
        _na_bias_kernel,
        grid=(NA_HEADS,),
        in_specs=[pl.BlockSpec((1, 2 * NA_KH, LANE), lambda h: (h, 0, 0))],
        out_specs=pl.BlockSpec((NA_KH, GRID_W, NA_KH * GRID_W), lambda h: (0, h, 0)),
        out_shape=jax.ShapeDtypeStruct((NA_KH, NA_HEADS * GRID_W, NA_KH * GRID_W), F32),
        compiler_params=_cparams(("arbitrary",), 32),
        name="na_bias",
    )(rp)


MERGE_SUB = 128


def _merge_kernel(yf_ref, yb_ref, z_ref, gt_ref, yna_ref, x_ref, g1_ref, sh2_ref, sc2_ref, snw_ref, n2w_ref,
                  wbs_ref, wbn_ref, wo_ref, wrh_ref, wrl_ref, br_ref, x1_ref, h2p_ref, lg_ref):
    tm = x_ref.shape[0]
    subs = [slice(r * MERGE_SUB, (r + 1) * MERGE_SUB) for r in range(tm // MERGE_SUB)]
    half = D_MODEL // 2
    yn = []
    for rs in subs:
        z = z_ref[rs, :].astype(F32)
        y = (yf_ref[rs, :] + yb_ref[rs, :]).astype(F32) * (z * _sigmoid(z))
        ms = jnp.mean(y * y, axis=-1, keepdims=True)
        yn.append((y * lax.rsqrt(ms + EPS) * snw_ref[...]).astype(BF16))
    ab = [(_dot(yn[r], wbs_ref[...]), _dot(yna_ref[rs, :], wbn_ref[...])) for r, rs in enumerate(subs)]
    merged = []
    for r, rs in enumerate(subs):
        g_ssd = gt_ref[rs, :D_MODEL].astype(F32)
        g_na = gt_ref[rs, D_MODEL:].astype(F32)
        merged.append((_sigmoid(g_ssd) * ab[r][0] + _sigmoid(g_na) * ab[r][1]).astype(BF16))
    mo = [_dot(m, wo_ref[...]) for m in merged]
    hs = []
    for r, rs in enumerate(subs):
        x1 = x_ref[rs, :] + g1_ref[0] * mo[r]
        x1_ref[rs, :] = x1
        ms2 = jnp.mean(x1 * x1, axis=-1, keepdims=True)
        h2 = x1 * lax.rsqrt(ms2 + EPS) * n2w_ref[...] * (1.0 + sc2_ref[0]) + sh2_ref[0]
        h_hi, h_lo = _split2(h2)
        hs.append((h_hi, h_lo))
        h2p_ref[rs, :] = _pack_bf16_pairs(h2)
    for r, rs in enumerate(subs):
        h_hi, h_lo = hs[r]
        lg_ref[rs, :] = (_dot(h_hi, wrh_ref[...]) + _dot(h_lo, wrh_ref[...]) + _dot(h_hi, wrl_ref[...])) + br_ref[...]


def _merge(yf, yb, big, yna, x2, mod3, snw, n2w, wbs, wbn, wo, wrh, wrl, br, seq_len, tm, zcol, gcol):
    t = x2.shape[0]
    tiles_per_seq = seq_len // tm
    modspec = lambda kidx: pl.BlockSpec((1, 1, D_MODEL), lambda i: ((i // tiles_per_seq) * N_MOD + kidx, 0, 0))
    full = lambda shp: pl.BlockSpec(shp, lambda i: (0,) * len(shp), pipeline_mode=pl.Buffered(1))
    return pl.pallas_call(
        _merge_kernel,
        grid=(t // tm,),
        in_specs=[pl.BlockSpec((tm, D_SSD), lambda i: (i, 0)),
                  pl.BlockSpec((tm, D_SSD), lambda i: (i, 0)),
                  pl.BlockSpec((tm, D_SSD), lambda i: (i, zcol)),
                  pl.BlockSpec((tm, 2 * D_MODEL), lambda i: (i, gcol)),
                  pl.BlockSpec((tm, D_NA), lambda i: (i, 0)),
                  pl.BlockSpec((tm, D_MODEL), lambda i: (i, 0)),
                  modspec(2), modspec(3), modspec(4),
                  full((1, D_SSD)), full((1, D_MODEL)),
                  full((D_SSD, D_MODEL)), full((D_NA, D_MODEL)), full((D_MODEL, D_MODEL)),
                  full((D_MODEL, LANE)), full((D_MODEL, LANE)), full((1, LANE))],
        out_specs=[pl.BlockSpec((tm, D_MODEL), lambda i: (i, 0)),
                   pl.BlockSpec((tm, D_MODEL // 2), lambda i: (i, 0)),
                   pl.BlockSpec((tm, LANE), lambda i: (i, 0))],
        out_shape=[jax.ShapeDtypeStruct((t, D_MODEL), F32),
                   jax.ShapeDtypeStruct((t, D_MODEL // 2), U32),
                   jax.ShapeDtypeStruct((t, LANE), F32)],
        compiler_params=_cparams(("arbitrary",), 56),
        name="merge",
    )(yf, yb, big, big, yna, x2, mod3, mod3, mod3, snw, n2w, wbs, wbn, wo, wrh, wrl, br)


ROUTE_TM = 512
GRP_LANE0 = N_EXPERTS


def _route_topk(lg):
    tm = lg.shape[0]
    lane = lax.broadcasted_iota(I32, (tm, LANE), 1)
    neg_inf = jnp.float32(-jnp.inf)
    big_lane = jnp.int32(4 * LANE)
    is_grp = (lane >= GRP_LANE0) & (lane < GRP_LANE0 + N_GROUPS)
    gl = jnp.where(is_grp, lg, neg_inf)
    gmax = jnp.max(gl, axis=-1, keepdims=True)
    grp = jnp.min(jnp.where(gl == gmax, lane, big_lane), axis=-1, keepdims=True) - GRP_LANE0
    psum = jnp.sum(jnp.where(is_grp, jnp.exp(lg - gmax), 0.0), axis=-1, keepdims=True)
    p_grp = 1.0 / psum
    in_g = (lane < N_EXPERTS) & ((lane // EXPERTS_PER_GROUP) == grp)
    el = jnp.where(in_g, lg, neg_inf)
    v1 = jnp.max(el, axis=-1, keepdims=True)
    i1 = jnp.min(jnp.where(el == v1, lane, big_lane), axis=-1, keepdims=True)
    el2 = jnp.where(lane == i1, neg_inf, el)
    v2 = jnp.max(el2, axis=-1, keepdims=True)
    i2 = jnp.min(jnp.where(el2 == v2, lane, big_lane), axis=-1, keepdims=True)
    tt = jnp.exp(v2 - v1)
    den = 1.0 + tt
    ga = p_grp / den
    gb = p_grp * tt / den

    sel1 = lane == i1
    sel2 = lane == i2
    return lane, sel1, sel2, ga, gb


def _route_count_kernel(lg_ref, cnt_ref, run_scr):
    @pl.when(pl.program_id(0) == 0)
    def _():
        run_scr[...] = jnp.zeros_like(run_scr)

    _, sel1, sel2, _, _ = _route_topk(lg_ref[...])
    onehot = jnp.where(sel1 | sel2, 1.0, 0.0)
    run_scr[...] = run_scr[...] + jnp.sum(onehot, axis=0, keepdims=True)
    cnt_ref[...] = run_scr[...].astype(I32)


def _route_dest_kernel(lg_ref, stri_ref, pstart_ref, gs_ref, rt_ref, run_scr):
    @pl.when(pl.program_id(0) == 0)
    def _():
        run_scr[...] = pstart_ref[...]

    lane, sel1, sel2, ga, gb = _route_topk(lg_ref[...])
    onehot = jnp.where(sel1 | sel2, 1.0, 0.0)
    pos = _dot(stri_ref[...], onehot.astype(BF16)) + run_scr[...]
    d1 = jnp.sum(jnp.where(sel1, pos, 0.0), axis=-1, keepdims=True)
    d2 = jnp.sum(jnp.where(sel2, pos, 0.0), axis=-1, keepdims=True)
    run_scr[...] = run_scr[...] + jnp.sum(onehot, axis=0, keepdims=True)

    slab = jnp.where(lane == 0, ga, 0.0)
    slab = jnp.where(lane == 1, gb, slab)
    slab = jnp.where(lane == 2, d1, slab)
    slab = jnp.where(lane == 3, d2, slab)
    gs_ref[...] = slab
    for q in range(ROUTE_TM // LANE):
        blk_t = slab[q * LANE:(q + 1) * LANE, :].T
        rt_ref[:, q * LANE:(q + 1) * LANE] = blk_t[0:8, :].astype(I32)


def _route_count(logits):
    t = logits.shape[0]
    tm = ROUTE_TM
    return pl.pallas_call(
        _route_count_kernel,
        grid=(t // tm,),
        in_specs=[pl.BlockSpec((tm, LANE), lambda i: (i, 0))],
        out_specs=pl.BlockSpec((1, LANE), lambda i: (0, 0)),
        out_shape=jax.ShapeDtypeStruct((1, LANE), I32),
        scratch_shapes=[pltpu.VMEM((1, LANE), F32)],
        compiler_params=_cparams(("arbitrary",), 32),
        name="route_count",
    )(logits)


def _route_dest(logits, stri, pstart_row):
    t = logits.shape[0]
    tm = ROUTE_TM
    return pl.pallas_call(
        _route_dest_kernel,
        grid=(t // tm,),
        in_specs=[pl.BlockSpec((tm, LANE), lambda i: (i, 0)),
                  pl.BlockSpec((tm, tm), lambda i: (0, 0)),
                  pl.BlockSpec((1, LANE), lambda i: (0, 0))],
        out_specs=[pl.BlockSpec((tm, LANE), lambda i: (i, 0)),
                   pl.BlockSpec((8, tm), lambda i: (0, i))],
        out_shape=[jax.ShapeDtypeStruct((t, LANE), F32),
                   jax.ShapeDtypeStruct((8, t), I32)],
        scratch_shapes=[pltpu.VMEM((1, LANE), F32)],
        compiler_params=_cparams(("arbitrary",), 32),
        name="route_dest",
    )(logits, stri, pstart_row)


MOE_BLOCK = 512
DISPATCH_TM = 512
COMBINE_TM = 256


ROW_UNROLL = 8


def _dispatch_kernel(d1_ref, d2_ref, h_ref, xb_in_hbm, xb_hbm, sem):
    del xb_in_hbm
    tm = DISPATCH_TM

    def body(g, carry):
        for u in range(ROW_UNROLL):
            tt = g * ROW_UNROLL + u
            src = h_ref.at[pl.ds(tt, 1)]
            pltpu.make_async_copy(src, xb_hbm.at[pl.ds(d1_ref[0, 0, tt], 1)], sem).start()
            pltpu.make_async_copy(src, xb_hbm.at[pl.ds(d2_ref[0, 0, tt], 1)], sem).start(priority=1)
        return carry

    lax.fori_loop(0, tm // ROW_UNROLL, body, 0)
    for _ in range(2):
        pltpu.make_async_copy(h_ref, xb_hbm.at[pl.ds(0, tm)], sem).wait()


def _dispatch(d1, d2, h2p, n_rows):
    t = h2p.shape[0]
    tm = DISPATCH_TM
    w = h2p.shape[1]
    smem = pl.BlockSpec((1, 1, tm), lambda i: (i, 0, 0), memory_space=pltpu.SMEM)
    anyspec = pl.BlockSpec(memory_space=pl.ANY)
    xb0 = jnp.zeros((n_rows, w), U32)
    return pl.pallas_call(
        _dispatch_kernel,
        grid=(t // tm,),
        in_specs=[smem, smem, pl.BlockSpec((tm, w), lambda i: (i, 0)), anyspec],
        out_specs=anyspec,
        out_shape=jax.ShapeDtypeStruct((n_rows, w), U32),
        scratch_shapes=[pltpu.SemaphoreType.DMA(())],
        input_output_aliases={3: 0},
        compiler_params=pltpu.CompilerParams(dimension_semantics=("arbitrary",), has_side_effects=True),
        name="moe_dispatch",
    )(d1, d2, h2p, xb0)


def _ffn_kernel(be_ref, nu_ref, xb_ref, w1_ref, w3_ref, w2_ref, yb_ref, w13_scr, w2_scr):
    i = pl.program_id(0)
    prev = be_ref[jnp.maximum(i - 1, 0)]
    changed = (i == 0) | (be_ref[i] != prev)
    used = i < nu_ref[0]

    @pl.when(changed & used)
    def _():
        w13_scr[:, :D_EXPERT] = w1_ref[0].astype(BF16)
        w13_scr[:, D_EXPERT:] = w3_ref[0].astype(BF16)
        w2_scr[...] = w2_ref[0].astype(BF16)

    @pl.when(used)
    def _():
        x = jnp.concatenate(_unpack_bf16_pairs(xb_ref[...]), axis=1).astype(BF16)
        h = _dot(x, w13_scr[...])
        h1 = h[:, :D_EXPERT]
        h3 = h[:, D_EXPERT:]
        a = (h1 * _sigmoid(h1) * h3).astype(BF16)
        yb_ref[...] = _pack_bf16_pairs(_dot(a, w2_scr[...]))

    @pl.when(jnp.logical_not(used))
    def _():
        yb_ref[...] = jnp.zeros_like(yb_ref)


def _ffn(blk_exp, n_used, xb, w1, w3, w2):
    n_rows = xb.shape[0]
    nblk = n_rows // MOE_BLOCK
    wmap = lambda i, be, nu: (be[i], 0, 0)
    grid_spec = pltpu.PrefetchScalarGridSpec(
        num_scalar_prefetch=2, grid=(nblk,),
        in_specs=[pl.BlockSpec((MOE_BLOCK, D_MODEL // 2), lambda i, be, nu: (i, 0)),
                  pl.BlockSpec((1, D_MODEL, D_EXPERT), wmap),
                  pl.BlockSpec((1, D_MODEL, D_EXPERT), wmap),
                  pl.BlockSpec((1, D_EXPERT, D_MODEL), wmap)],
        out_specs=pl.BlockSpec((MOE_BLOCK, D_MODEL // 2), lambda i, be, nu: (i, 0)),
        scratch_shapes=[pltpu.VMEM((D_MODEL, 2 * D_EXPERT), BF16),
                        pltpu.VMEM((D_EXPERT, D_MODEL), BF16)])
    return pl.pallas_call(
        _ffn_kernel, grid_spec=grid_spec,
        out_shape=jax.ShapeDtypeStruct((n_rows, D_MODEL // 2), U32),
        compiler_params=_cparams(("arbitrary",), 40),
        name="moe_ffn",
    )(blk_exp, n_used, xb, w1, w3, w2)


def _combine_kernel(d1_ref, d2_ref, d1n_ref, d2n_ref, yb_hbm, gs_ref, x1_ref, g2_ref, o_ref,
                    ya_scr, yb_scr, sem):
    tm = COMBINE_TM
    i = pl.program_id(0)
    slot = i % 2

    def issue(da_ref, db_ref, s):
        def body(g, carry):
            for u in range(ROW_UNROLL):
                tt = g * ROW_UNROLL + u
                pltpu.make_async_copy(yb_hbm.at[pl.ds(da_ref[0, 0, tt], 1)],
                                      ya_scr.at[s, pl.ds(tt, 1)], sem.at[s]).start()
                pltpu.make_async_copy(yb_hbm.at[pl.ds(db_ref[0, 0, tt], 1)],
                                      yb_scr.at[s, pl.ds(tt, 1)], sem.at[s]).start(priority=1)
            return carry

        lax.fori_loop(0, tm // ROW_UNROLL, body, 0)

    @pl.when(i == 0)
    def _():
        issue(d1_ref, d2_ref, 0)

    @pl.when(i + 1 < pl.num_programs(0))
    def _():
        issue(d1n_ref, d2n_ref, 1 - slot)

    pltpu.make_async_copy(yb_hbm.at[pl.ds(0, tm)], ya_scr.at[slot], sem.at[slot]).wait()
    pltpu.make_async_copy(yb_hbm.at[pl.ds(0, tm)], yb_scr.at[slot], sem.at[slot]).wait()
    ga = gs_ref[:, 0:1]
    gb = gs_ref[:, 1:2]
    a_lo, a_hi = _unpack_bf16_pairs(ya_scr[slot])
    b_lo, b_hi = _unpack_bf16_pairs(yb_scr[slot])
    half = D_MODEL // 2
    o_ref[:, :half] = x1_ref[:, :half] + g2_ref[0][:, :half] * (ga * a_lo + gb * b_lo)
    o_ref[:, half:] = x1_ref[:, half:] + g2_ref[0][:, half:] * (ga * a_hi + gb * b_hi)


def _combine(d1, d2, ybuf, gs, x1, mod3, seq_len):
    t = x1.shape[0]
    tm = COMBINE_TM
    nt = t // tm
    tiles_per_seq = seq_len // tm
    smem = pl.BlockSpec((1, 1, tm), lambda i: (i, 0, 0), memory_space=pltpu.SMEM)
    smem_next = pl.BlockSpec((1, 1, tm), lambda i: (jnp.minimum(i + 1, nt - 1), 0, 0), memory_space=pltpu.SMEM)
    return pl.pallas_call(
        _combine_kernel,
        grid=(nt,),
        in_specs=[smem, smem, smem_next, smem_next,
                  pl.BlockSpec(memory_space=pl.ANY),
                  pl.BlockSpec((tm, LANE), lambda i: (i, 0)),
                  pl.BlockSpec((tm, D_MODEL), lambda i: (i, 0)),
                  pl.BlockSpec((1, 1, D_MODEL), lambda i: ((i // tiles_per_seq) * N_MOD + 5, 0, 0))],
        out_specs=pl.BlockSpec((tm, D_MODEL), lambda i: (i, 0)),
        out_shape=jax.ShapeDtypeStruct((t, D_MODEL), F32),
        scratch_shapes=[pltpu.VMEM((2, tm, D_MODEL // 2), U32), pltpu.VMEM((2, tm, D_MODEL // 2), U32),
                        pltpu.SemaphoreType.DMA((2,))],
        compiler_params=_cparams(("arbitrary",), 32),
        name="moe_combine",
    )(d1, d2, d1, d2, ybuf, gs, x1, mod3)


def _rope_tables(seq_len):
    t = jnp.arange(seq_len, dtype=I32)
    row = (t // GRID_W).astype(F32)
    colp = (t % GRID_W).astype(F32)
    half = SSD_STATE // 2
    inv = ROPE_THETA ** (-jnp.arange(0, half, 2, dtype=F32) / half)
    ar = row[:, None] * inv
    ac = colp[:, None] * inv
    cos_t = jnp.concatenate([jnp.cos(ar), jnp.cos(ar), jnp.cos(ac), jnp.cos(ac)], axis=-1)
    sin_t = jnp.concatenate([-jnp.sin(ar), jnp.sin(ar), -jnp.sin(ac), jnp.sin(ac)], axis=-1)
    return cos_t, sin_t


def _scan_tables(rev):
    li = jnp.arange(SSD_CHUNK)[:, None]
    ui = jnp.arange(SSD_CHUNK)[None, :]
    tri = ((ui >= li) if rev else (ui <= li)).astype(BF16)
    col = SSD_HEADS if rev else 0
    j = jnp.arange(LANE)[:, None]
    c = jnp.arange(D_SSD)[None, :]
    ee = (j == col + c // SSD_HEADDIM).astype(BF16)
    return tri, ee


def _pad_lanes(v, width=LANE):
    v = v.reshape(1, -1)
    return jnp.pad(v, ((0, 0), (0, width - v.shape[1])))


def kernel(x, c, ctx, c_ctx, w_mod, b_mod, norm1_w, w_in, conv_w, conv_b, a_log_f, a_log_b, dt_bias_f, dt_bias_b, d_skip, ssd_norm_w, q_norm_w, k_norm_w, rpb, w_br_ssd, w_br_na, w_out, norm2_w, w_grp, b_grp, w_rt, b_rt, w1, w3, w2):
    nb, seq_len, d = x.shape
    ctx_len = ctx.shape[1]
    t = nb * seq_len
    tc = nb * ctx_len
    assert w_mod.shape[0] == 1 and d == D_MODEL and nb <= 7
    assert seq_len % 256 == 0 and ctx_len % SSD_CHUNK == 0 and seq_len // GRID_W >= NA_KH

    cin = jnp.concatenate([c, c_ctx[None, :], jnp.zeros((8 - nb - 1, d), F32)], axis=0)
    mod = _modulation(cin, w_mod[0], b_mod[0])
    mod3 = mod.reshape(8 * N_MOD, 1, D_MODEL)

    wi = w_in[0]
    o_dt = D_SSD + D_XBC
    o_qkv = o_dt + 2 * SSD_HEADS
    w_a = wi[:, :o_dt].astype(BF16)
    w_b = wi[:, o_qkv:].astype(BF16)
    w_dtp = jnp.pad(wi[:, o_dt:o_qkv].astype(BF16), ((0, 0), (0, LANE - 2 * SSD_HEADS)))
    n1w = norm1_w[0].reshape(1, D_MODEL)

    x2 = x.reshape(t, D_MODEL)
    ctx2 = ctx.reshape(tc, D_MODEL)
    tm_in = 2048 if seq_len % 2048 == 0 else 256
    tiles = seq_len // tm_in
    tn_in = 1024
    z_t = list(range(0, D_SSD // tn_in))
    xbc_t = list(range(D_SSD // tn_in, o_dt // tn_in))
    q_t, k_t, v_t = [0], [1], [2]
    g_t = [3, 4]
    big, dt_raw = _inproj(x2, mod3, lambda i: i // tiles, n1w, w_a, w_b, w_dtp, tm_in, tn_in,
                          xbc_t + z_t, g_t + q_t + k_t + v_t)
    tm_c = 256 if tc % 256 == 0 else SSD_CHUNK
    bigc, dtc_raw = _inproj(ctx2, mod3, lambda i: nb, n1w, w_a, w_b, w_dtp, tm_c, tn_in, xbc_t, k_t + v_t)
    ZCOL, GCOL, QCOL, KCOL, VCOL = 2, 3, 8, 9, 10
    KC_COL, VC_COL = 4, 5

    dtbias = _pad_lanes(jnp.concatenate([dt_bias_f[0], dt_bias_b[0]]))
    cw = conv_w[0]
    cbias = conv_b[0].reshape(1, D_XBC)
    cos_t, sin_t = _rope_tables(seq_len)
    xs, bc, dts = _ssd_prep(big, dt_raw, cw, cbias, dtbias, cos_t, sin_t, seq_len, 256)
    ctl = 256 if ctx_len % 256 == 0 else SSD_CHUNK
    ones_t = jnp.ones((ctx_len, LANE), F32)
    zeros_t = jnp.zeros((ctx_len, LANE), F32)
    xsc, bcc, dtsc = _ssd_prep(bigc, dtc_raw, cw, cbias, dtbias, ones_t, zeros_t, ctx_len, ctl)

    alog = _pad_lanes(jnp.concatenate([a_log_f[0], a_log_b[0]]))
    dskip = jnp.repeat(d_skip[0], SSD_HEADDIM).reshape(1, D_SSD)
    tri_f, ee_f = _scan_tables(False)
    tri_b, ee_b = _scan_tables(True)
    hcf = _ssd_state(xsc, bcc, dtsc, alog, tri_f, ee_f, nb, False)
    hcb = _ssd_state(xsc, bcc, dtsc, alog, tri_b, ee_b, nb, True)
    yf, yb = _ssd_scan(xs, bc, dts, alog, dskip, (tri_f, ee_f), (tri_b, ee_b), hcf, hcb, nb)

    qw = jnp.tile(q_norm_w[0], NA_HEADS).reshape(1, D_NA)
    kw = jnp.tile(k_norm_w[0], NA_HEADS).reshape(1, D_NA)
    gi = jnp.arange(LANE)
    gmat = ((gi[:, None] // NA_HEADDIM) == (gi[None, :] // NA_HEADDIM)).astype(BF16) * (1.0 / NA_HEADDIM)
    gmat = gmat.astype(BF16)
    qn, kn = _na_prep(big, QCOL, KCOL, qw, kw, gmat, 512 if t % 512 == 0 else 256)
    kcn = _na_prep(bigc, None, KC_COL, qw, kw, gmat, 256 if tc % 256 == 0 else SSD_CHUNK)
    bias_tab = _na_bias_table(rpb[0])
    y_na = _neigh_attention(qn, kn, big, kcn, bigc, bias_tab, nb, seq_len, ctx_len, VCOL, VC_COL)

    w_r = jnp.pad(jnp.concatenate([w_rt[0], w_grp[0]], axis=1), ((0, 0), (0, LANE - N_EXPERTS - N_GROUPS)))
    wrh = w_r.astype(BF16)
    wrl = (w_r - wrh.astype(F32)).astype(BF16)
    br = _pad_lanes(jnp.concatenate([b_rt[0], b_grp[0]]))
    x1, h2p, logits = _merge(yf, yb, big, y_na, x2, mod3,
                             ssd_norm_w[0].reshape(1, D_SSD), norm2_w[0].reshape(1, D_MODEL),
                             w_br_ssd[0].astype(BF16), w_br_na[0].astype(BF16), w_out[0].astype(BF16),
                             wrh, wrl, br, seq_len, 512 if seq_len % 512 == 0 else 256, ZCOL, GCOL)

    si = jnp.arange(ROUTE_TM)
    stri = (si[None, :] < si[:, None]).astype(BF16)
    cnt = _route_count(logits)

    counts = cnt[0, :N_EXPERTS]
    padded = (counts + MOE_BLOCK - 1) // MOE_BLOCK * MOE_BLOCK
    pend = jnp.cumsum(padded)
    pstart = pend - padded
    nblk = (2 * t + N_EXPERTS * (MOE_BLOCK - 1) + MOE_BLOCK - 1) // MOE_BLOCK
    n_used = (pend[-1] // MOE_BLOCK).astype(I32).reshape(1)
    blk_row0 = jnp.arange(nblk, dtype=I32) * MOE_BLOCK
    blk_exp = jnp.minimum(jnp.sum((pend[None, :] <= blk_row0[:, None]).astype(I32), axis=1), N_EXPERTS - 1)
    gs, rt = _route_dest(logits, stri, _pad_lanes(pstart.astype(F32)))

    def tok_tiles(row, tm):
        return rt[row].reshape(t // tm, 1, tm)

    xb = _dispatch(tok_tiles(2, DISPATCH_TM), tok_tiles(3, DISPATCH_TM), h2p, nblk * MOE_BLOCK)
    ybuf = _ffn(blk_exp, n_used, xb, w1[0], w3[0], w2[0])
    out = _combine(tok_tiles(2, COMBINE_TM), tok_tiles(3, COMBINE_TM), ybuf, gs, x1, mod3, seq_len)
    return out.reshape(nb, seq_len, D_MODEL)
```

```python
import functools
import math

import numpy as np

import jax
import jax.numpy as jnp
from jax import lax
from jax.experimental import pallas as pl
from jax.experimental.pallas import tpu as pltpu

F32 = jnp.float32
BF16 = jnp.bfloat16
I32 = jnp.int32
U32 = jnp.uint32

D_MODEL = 1024
GRID_W = 64
D_SSD = 2048
SSD_HEADDIM = 64
SSD_HEADS = 32
SSD_GROUPS = 8
HEADS_PER_GROUP = 4
SSD_STATE = 128
SSD_CHUNK = 128
CONV_K = 5
D_BC = SSD_GROUPS * SSD_STATE
D_XBC = D_SSD + 2 * D_BC
NA_HEADDIM = 64
NA_HEADS = 16
D_NA = 1024
NA_KH = 8
NA_KW = 16
ROPE_THETA = 10000.0
N_GROUPS = 4
EXPERTS_PER_GROUP = 8
N_EXPERTS = 32
D_EXPERT = 512
N_MOD = 6
EPS = 1e-6
NEG_BIG = -1e30

COLS_LAT = D_XBC + D_SSD + 2 * D_MODEL + 3 * D_NA
COLS_CTX = D_XBC + 2 * D_NA
LANE = 128

V7X_VMEM_BYTES = 64 * 1024 * 1024


def _cparams(sem, vmem_mb):
    return pltpu.CompilerParams(dimension_semantics=sem, vmem_limit_bytes=vmem_mb * 1024 * 1024)


def _sigmoid(x):
    return 1.0 / (1.0 + jnp.exp(-x))


def _silu(x):
    h = 0.5 * x
    return h * jnp.tanh(h) + h


def _split2(x):
    hi = x.astype(BF16)
    lo = (x - hi.astype(F32)).astype(BF16)
    return hi, lo


def _split3(x):
    hi = x.astype(BF16)
    r = x - hi.astype(F32)
    mid = r.astype(BF16)
    lo = (r - mid.astype(F32)).astype(BF16)
    return hi, mid, lo


def _pack_bf16_pairs(x):
    bits = pltpu.bitcast(x.astype(BF16).astype(F32), U32)
    n = x.shape[1] // 2
    return (bits[:, :n] >> 16) | bits[:, n:]


def _unpack_bf16_pairs(w):
    return pltpu.bitcast(w << 16, F32), pltpu.bitcast(w & jnp.uint32(0xFFFF0000), F32)


def _dot(a, b):
    return jnp.dot(a, b, preferred_element_type=F32)


def _dot_nt(a, b):
    return lax.dot_general(a, b, (((1,), (1,)), ((), ())), preferred_element_type=F32)


def _dot_tn(a, b):
    return lax.dot_general(a, b, (((0,), (0,)), ((), ())), preferred_element_type=F32)


def _mod_kernel(c_ref, w_ref, b_ref, o_ref):
    c = c_ref[...]
    a = _silu(c).astype(BF16)
    o_ref[...] = _dot(a, w_ref[...].astype(BF16)) + b_ref[...]


def _modulation(cin, w_mod, b_mod):
    n = w_mod.shape[1]
    tn = 1536
    return pl.pallas_call(
        _mod_kernel,
        grid=(n // tn,),
        in_specs=[pl.BlockSpec((8, D_MODEL), lambda j: (0, 0)),
                  pl.BlockSpec((D_MODEL, tn), lambda j: (0, j)),
                  pl.BlockSpec((1, tn), lambda j: (0, j))],
        out_specs=pl.BlockSpec((8, tn), lambda j: (0, j)),
        out_shape=jax.ShapeDtypeStruct((8, n), F32),
        compiler_params=_cparams(("arbitrary",), 40),
        name="modulation",
    )(cin, w_mod, b_mod.reshape(1, n))


def _inproj_kernel(x_ref, sh_ref, sc_ref, nw_ref, wa_ref, wb_ref, wdt_ref, o_ref, dt_ref, h_scr, *, tm, n_a):
    j = pl.program_id(1)

    @pl.when(j == 0)
    def _():
        scale = 1.0 + sc_ref[0]
        shift = sh_ref[0]
        nw = nw_ref[...]

        def body(r, carry):
            rows = pl.ds(pl.multiple_of(r * 128, 128), 128)
            x = x_ref[rows, :]
            ms = jnp.mean(x * x, axis=-1, keepdims=True)
            h = x * lax.rsqrt(ms + EPS) * nw * scale + shift
            h_scr[rows, :] = h.astype(BF16)
            return carry

        lax.fori_loop(0, tm // 128, body, 0)
        dt_ref[...] = _dot(h_scr[...], wdt_ref[...])

    @pl.when(j < n_a)
    def _():
        o_ref[...] = _dot(h_scr[...], wa_ref[...].astype(BF16)).astype(BF16)

    @pl.when(j >= n_a)
    def _():
        o_ref[...] = _dot(h_scr[...], wb_ref[...]).astype(BF16)


def _inproj(x2, mod3, mod_row_fn, norm_w, wa, wb, wdt, tm, tn, a_tiles, b_tiles):
    t = x2.shape[0]
    n_a, n_b = len(a_tiles), len(b_tiles)
    a_tab = list(a_tiles) + [a_tiles[-1]] * n_b
    b_tab = [b_tiles[0]] * n_a + list(b_tiles)

    def pick(tab, j):
        idx = jnp.int32(tab[-1])
        for pos in range(len(tab) - 2, -1, -1):
            idx = jnp.where(j <= pos, tab[pos], idx)
        return idx

    kern = functools.partial(_inproj_kernel, tm=tm, n_a=n_a)
    return pl.pallas_call(
        kern,
        grid=(t // tm, n_a + n_b),
        in_specs=[pl.BlockSpec((tm, D_MODEL), lambda i, j: (i, 0)),
                  pl.BlockSpec((1, 1, D_MODEL), lambda i, j: (mod_row_fn(i) * N_MOD + 0, 0, 0)),
                  pl.BlockSpec((1, 1, D_MODEL), lambda i, j: (mod_row_fn(i) * N_MOD + 1, 0, 0)),
                  pl.BlockSpec((1, D_MODEL), lambda i, j: (0, 0)),
                  pl.BlockSpec((D_MODEL, tn), lambda i, j: (0, pick(a_tab, j))),
                  pl.BlockSpec((D_MODEL, tn), lambda i, j: (0, pick(b_tab, j))),
                  pl.BlockSpec((D_MODEL, LANE), lambda i, j: (0, 0))],
        out_specs=[pl.BlockSpec((tm, tn), lambda i, j: (i, j)),
                   pl.BlockSpec((tm, LANE), lambda i, j: (i, 0))],
        out_shape=[jax.ShapeDtypeStruct((t, (n_a + n_b) * tn), BF16),
                   jax.ShapeDtypeStruct((t, LANE), F32)],
        scratch_shapes=[pltpu.VMEM((tm, D_MODEL), BF16)],
        compiler_params=_cparams(("arbitrary", "arbitrary"), 52),
        name="inproj",
    )(x2, mod3, mod3, norm_w, wa, wb, wdt)


CONV_OFFSETS = (-2, -1, 1, 2)


def _ssd_prep_kernel(prev_ref, cur_ref, next_ref, dt_ref, cw_ref, cb_ref, dtbias_ref, cos_ref, sin_ref,
                     shift_ref, etop_ref, ebot_ref, xs_ref, bc_ref, dto_ref, *, tl, tiles_per_seq):
    i = pl.program_id(0)
    first = (i % tiles_per_seq) == 0
    last = (i % tiles_per_seq) == tiles_per_seq - 1
    lane = lax.broadcasted_iota(I32, (tl, LANE), 1)
    lo_half = ((lane // 32) % 2) == 0
    shift = shift_ref[...]
    etop = etop_ref[...]
    ebot = ebot_ref[...]
    cosv = cos_ref[...]
    sinv = sin_ref[...]
    cwid = 256
    for c in range(D_XBC // cwid):
        cs = slice(c * cwid, (c + 1) * cwid)
        w = cw_ref[:, cs]
        xc = cur_ref[:, cs]
        sh = _dot(shift, xc).reshape(tl // 8, len(CONV_OFFSETS), 8, cwid)
        tap = lambda k: sh[:, k].reshape(tl, cwid)
        acc = cb_ref[:, cs] + xc.astype(F32) * w[2:3, :]
        acc = acc + tap(0) * w[0:1, :] + tap(1) * w[1:2, :] + tap(2) * w[3:4, :] + tap(3) * w[4:5, :]
        top = _dot(etop, prev_ref[:, cs])
        bot = _dot(ebot, next_ref[:, cs])
        top_c = jnp.where(first, 0.0, top[0:8] * w[0:1, :] + top[8:16] * w[1:2, :])
        bot_c = jnp.where(last, 0.0, bot[0:8] * w[3:4, :] + bot[8:16] * w[4:5, :])
        acc = jnp.concatenate([acc[0:8] + top_c, acc[8:tl - 8], acc[tl - 8:tl] + bot_c], axis=0)
        y = _silu(acc)
        if c * cwid < D_SSD:
            xs_ref[:, cs] = y.astype(BF16)
        else:
            for g in range(cwid // LANE):
                yg = y[:, g * LANE:(g + 1) * LANE]
                partner = jnp.where(lo_half, pltpu.roll(yg, 96, 1), pltpu.roll(yg, 32, 1))
                o = yg * cosv + partner * sinv
                c0 = c * cwid - D_SSD + g * LANE
                bc_ref[:, c0:c0 + LANE] = o.astype(BF16)

    v = dt_ref[...] + dtbias_ref[...]
    dto_ref[...] = jnp.maximum(v, 0.0) + jnp.log(1.0 + jnp.exp(-jnp.abs(v)))


def _conv_shift_tables(tl):
    t = np.arange(tl)[:, None]
    u = np.arange(tl)[None, :]
    shift = np.stack([(u == t + off) for off in CONV_OFFSETS], axis=0).reshape(len(CONV_OFFSETS), tl // 8, 8, tl)
    shift = shift.transpose(1, 0, 2, 3).reshape(len(CONV_OFFSETS) * tl, tl)
    r = np.arange(16)[:, None]
    q = np.arange(16)[None, :]
    etop = ((r < 8) & (q == r + 14)) | ((r >= 8) & (q == r - 8 + 15))
    ebot = ((r < 8) & (q == r - 7)) | ((r >= 8) & (q == r - 8 - 6))
    return jnp.asarray(shift, BF16), jnp.asarray(etop, BF16), jnp.asarray(ebot, BF16)


def _ssd_prep(big, dt_raw, conv_w, conv_b, dtbias, cos_t, sin_t, seq_len, tl):
    t = big.shape[0]
    tps = seq_len // tl
    hb = tl // 16
    nhalo = t // 16
    kern = functools.partial(_ssd_prep_kernel, tl=tl, tiles_per_seq=tps)
    shift, etop, ebot = _conv_shift_tables(tl)
    const = lambda i: (0, 0)
    return pl.pallas_call(
        kern,
        grid=(t // tl,),
        in_specs=[pl.BlockSpec((16, D_XBC), lambda i: (jnp.maximum(i * hb - 1, 0), 0)),
                  pl.BlockSpec((tl, D_XBC), lambda i: (i, 0)),
                  pl.BlockSpec((16, D_XBC), lambda i: (jnp.minimum((i + 1) * hb, nhalo - 1), 0)),
                  pl.BlockSpec((tl, LANE), lambda i: (i, 0)),
                  pl.BlockSpec((CONV_K, D_XBC), const),
                  pl.BlockSpec((1, D_XBC), const),
                  pl.BlockSpec((1, LANE), const),
                  pl.BlockSpec((tl, LANE), lambda i: (i % tps, 0)),
                  pl.BlockSpec((tl, LANE), lambda i: (i % tps, 0)),
                  pl.BlockSpec((len(CONV_OFFSETS) * tl, tl), const),
                  pl.BlockSpec((16, 16), const),
                  pl.BlockSpec((16, 16), const)],
        out_specs=[pl.BlockSpec((tl, D_SSD), lambda i: (i, 0)),
                   pl.BlockSpec((tl, 2 * D_BC), lambda i: (i, 0)),
                   pl.BlockSpec((tl, LANE), lambda i: (i, 0))],
        out_shape=[jax.ShapeDtypeStruct((t, D_SSD), BF16),
                   jax.ShapeDtypeStruct((t, 2 * D_BC), BF16),
                   jax.ShapeDtypeStruct((t, LANE), F32)],
        compiler_params=_cparams(("arbitrary",), 40),
        name="ssd_prep",
    )(big, big, big, dt_raw, conv_w, conv_b, dtbias, cos_t, sin_t, shift, etop, ebot)


GROUP_W = HEADS_PER_GROUP * SSD_HEADDIM


def _scan_decays(dt_ref, alog_ref, tri_ref, ee_ref, rev):
    edge = 0 if rev else SSD_CHUNK - 1
    a_row = -jnp.exp(alog_ref[...])
    dt = dt_ref[...]
    trib = tri_ref[...]
    d_hi, d_mid, d_lo = _split3(dt * a_row)
    cum = _dot(trib, d_hi) + _dot(trib, d_mid) + _dot(trib, d_lo)
    tot = cum[edge:edge + 1, :]
    ee = ee_ref[...]
    w_heads = (jnp.exp(tot - cum) * dt).astype(BF16)
    t_hi, t_lo = _split2(jnp.broadcast_to(jnp.exp(tot), (8, LANE)))
    etot = (_dot(t_hi, ee) + _dot(t_lo, ee))[0:1, :]
    return dt, trib, cum, w_heads, etot


def _spread(per_head, ee_ref, g):
    return _dot(per_head, ee_ref[:, g * GROUP_W:(g + 1) * GROUP_W])


def _ssd_state_kernel(xs_ref, bc_ref, dt_ref, alog_ref, tri_ref, ee_ref, hfin_ref, h_scr, *, rev, nc):
    k = pl.program_id(1)

    @pl.when(k == 0)
    def _():
        h_scr[...] = jnp.zeros_like(h_scr)

    _, _, _, w_heads, etot = _scan_decays(dt_ref, alog_ref, tri_ref, ee_ref, rev)
    for g in range(SSD_GROUPS):
        gs = slice(g * GROUP_W, (g + 1) * GROUP_W)
        xw = (xs_ref[:, gs].astype(F32) * _spread(w_heads, ee_ref, g)).astype(BF16)
        h_scr[:, gs] = etot[:, gs] * h_scr[:, gs] + _dot_tn(bc_ref[:, g * SSD_STATE:(g + 1) * SSD_STATE], xw)

    @pl.when(k == nc - 1)
    def _():
        hfin_ref[0] = h_scr[...]


def _scan_chunk_stages(xs_ref, bc_ref, dt_ref, alog_ref, dskip_ref, tri_ref, ee_ref, y_ref, h_scr, rev):
    L = SSD_CHUNK
    col = SSD_HEADS if rev else 0
    dt, trib, cum, w_heads, etot = _scan_decays(dt_ref, alog_ref, tri_ref, ee_ref, rev)
    tri = trib > 0.5
    cum_t = cum.T
    dt_t = dt.T
    o_heads = jnp.exp(cum).astype(BF16)

    gw = GROUP_W
    lane_head = lax.broadcasted_iota(I32, (L, gw), 1) // SSD_HEADDIM

    def b_of(g):
        return bc_ref[:, g * SSD_STATE:(g + 1) * SSD_STATE]

    def c_of(g):
        return bc_ref[:, D_BC + g * SSD_STATE:D_BC + (g + 1) * SSD_STATE]

    def operands(g, cb):
        gs = slice(g * gw, (g + 1) * gw)
        xs_g = xs_ref[:, gs]
        ms = []
        for r in range(HEADS_PER_GROUP):
            hh = col + g * HEADS_PER_GROUP + r
            seg = cum[:, hh:hh + 1] - cum_t[hh:hh + 1, :]
            lm = jnp.exp(jnp.where(tri, seg, NEG_BIG)) * dt_t[hh:hh + 1, :]
            ms.append((cb * lm).astype(BF16))
        mcat = jnp.concatenate(ms, axis=1)
        zero = jnp.zeros_like(xs_g)
        bd = jnp.concatenate([jnp.where(lane_head == r, xs_g, zero) for r in range(HEADS_PER_GROUP)], axis=0)
        xs_f = xs_g.astype(F32)
        xw = (xs_f * _spread(w_heads, ee_ref, g)).astype(BF16)
        return mcat, bd, xs_f, xw, _spread(o_heads, ee_ref, g)

    def cb_of(g):
        return _dot_nt(c_of(g), b_of(g))

    yield
    cb_next = cb_of(0)
    yield
    nxt = operands(0, cb_next)
    cb_next = cb_of(1)
    for g in range(SSD_GROUPS):
        yield
        gs = slice(g * gw, (g + 1) * gw)
        mcat, bd, xs_f, xw, oscale_g = nxt
        h_g = h_scr[:, gs]
        y_diag = _dot(mcat, bd)
        y_off = _dot(c_of(g), h_g.astype(BF16))
        loc = _dot_tn(b_of(g), xw)
        if g + 1 < SSD_GROUPS:
            nxt = operands(g + 1, cb_next)
            if g + 2 < SSD_GROUPS:
                cb_next = cb_of(g + 2)
        y = y_diag + oscale_g * y_off
        if not rev:
            y = y + xs_f * dskip_ref[:, gs]
        y_ref[:, gs] = y.astype(BF16)
        h_scr[:, gs] = etot[:, gs] * h_g + loc


def _ssd_scan_kernel(xsf_ref, bcf_ref, dtf_ref, xsb_ref, bcb_ref, dtb_ref, alog_ref, dskip_ref,
                     trif_ref, eef_ref, trib_ref, eeb_ref, h0f_ref, h0b_ref, yf_ref, yb_ref, hf_scr, hb_scr):
    @pl.when(pl.program_id(1) == 0)
    def _():
        hf_scr[...] = h0f_ref[0]
        hb_scr[...] = h0b_ref[0]

    chains = [_scan_chunk_stages(xsf_ref, bcf_ref, dtf_ref, alog_ref, dskip_ref, trif_ref, eef_ref,
                                 yf_ref, hf_scr, False),
              _scan_chunk_stages(xsb_ref, bcb_ref, dtb_ref, alog_ref, dskip_ref, trib_ref, eeb_ref,
                                 yb_ref, hb_scr, True)]
    while chains:
        for ch in list(chains):
            try:
                next(ch)
            except StopIteration:
                chains.remove(ch)


def _scan_rowmap(nc, rev):
    def rowmap(b, k):
        c = (nc - 1 - k) if rev else k
        return (b * nc + c, 0)
    return rowmap


def _ssd_state(xs, bc, dt, alog_row, tri, ee, nb, rev):
    t = xs.shape[0]
    nc = t // nb // SSD_CHUNK
    rowmap = _scan_rowmap(nc, rev)
    const2 = lambda b, k: (0, 0)
    return pl.pallas_call(
        functools.partial(_ssd_state_kernel, rev=rev, nc=nc),
        grid=(nb, nc),
        in_specs=[pl.BlockSpec((SSD_CHUNK, D_SSD), rowmap),
                  pl.BlockSpec((SSD_CHUNK, D_BC), rowmap),
                  pl.BlockSpec((SSD_CHUNK, LANE), rowmap),
                  pl.BlockSpec((1, LANE), const2),
                  pl.BlockSpec((SSD_CHUNK, SSD_CHUNK), const2),
                  pl.BlockSpec((LANE, D_SSD), const2)],
        out_specs=pl.BlockSpec((1, SSD_STATE, D_SSD), lambda b, k: (b, 0, 0)),
        out_shape=jax.ShapeDtypeStruct((nb, SSD_STATE, D_SSD), F32),
        scratch_shapes=[pltpu.VMEM((SSD_STATE, D_SSD), F32)],
        compiler_params=_cparams(("arbitrary", "arbitrary"), 40),
        name="ssd_state_bwd" if rev else "ssd_state_fwd",
    )(xs, bc, dt, alog_row, tri, ee)


def _ssd_scan(xs, bc, dt, alog_row, dskip_row, tables_f, tables_b, h0f, h0b, nb):
    t = xs.shape[0]
    nc = t // nb // SSD_CHUNK
    fmap = _scan_rowmap(nc, False)
    bmap = _scan_rowmap(nc, True)
    const2 = lambda b, k: (0, 0)
    chunk = lambda width, m: pl.BlockSpec((SSD_CHUNK, width), m)
    tri_spec = pl.BlockSpec((SSD_CHUNK, SSD_CHUNK), const2)
    ee_spec = pl.BlockSpec((LANE, D_SSD), const2)
    state = pl.BlockSpec((1, SSD_STATE, D_SSD), lambda b, k: (b, 0, 0))
    return pl.pallas_call(
        _ssd_scan_kernel,
        grid=(nb, nc),
        in_specs=[chunk(D_SSD, fmap), chunk(2 * D_BC, fmap), chunk(LANE, fmap),
                  chunk(D_SSD, bmap), chunk(2 * D_BC, bmap), chunk(LANE, bmap),
                  pl.BlockSpec((1, LANE), const2),
                  pl.BlockSpec((1, D_SSD), const2),
                  tri_spec, ee_spec, tri_spec, ee_spec, state, state],
        out_specs=[chunk(D_SSD, fmap), chunk(D_SSD, bmap)],
        out_shape=[jax.ShapeDtypeStruct((t, D_SSD), BF16)] * 2,
        scratch_shapes=[pltpu.VMEM((SSD_STATE, D_SSD), F32)] * 2,
        compiler_params=_cparams(("arbitrary", "arbitrary"), 48),
        name="ssd_scan",
    )(xs, bc, dt, xs, bc, dt, alog_row, dskip_row, *tables_f, *tables_b, h0f, h0b)


def _headnorm_cols(src_ref, w_ref, g, dst_ref, scale):
    for c in range(D_NA // LANE):
        cs = slice(c * LANE, (c + 1) * LANE)
        x = src_ref[:, cs].astype(F32)
        ms = _dot((x * x).astype(BF16), g)
        y = x * lax.rsqrt(ms + EPS) * w_ref[:, cs]
        if scale is not None:
            y = y * scale
        dst_ref[:, cs] = y.astype(BF16)


def _na_prep_qk_kernel(q_ref, k_ref, qw_ref, kw_ref, g_ref, qo_ref, ko_ref):
    g = g_ref[...]
    _headnorm_cols(q_ref, qw_ref, g, qo_ref, NA_HEADDIM ** -0.5)
    _headnorm_cols(k_ref, kw_ref, g, ko_ref, None)


def _na_prep_k_kernel(k_ref, kw_ref, g_ref, ko_ref):
    _headnorm_cols(k_ref, kw_ref, g_ref[...], ko_ref, None)


def _na_prep(big, qcol, kcol, qw, kw, gmat, tm):
    t = big.shape[0]
    blk = lambda cidx: pl.BlockSpec((tm, D_NA), lambda i: (i, cidx))
    row = pl.BlockSpec((1, D_NA), lambda i: (0, 0))
    gspec = pl.BlockSpec((LANE, LANE), lambda i: (0, 0))
    out = pl.BlockSpec((tm, D_NA), lambda i: (i, 0))
    if qcol is None:
        return pl.pallas_call(
            _na_prep_k_kernel, grid=(t // tm,),
            in_specs=[blk(kcol), row, gspec], out_specs=out,
            out_shape=jax.ShapeDtypeStruct((t, D_NA), BF16),
            compiler_params=_cparams(("arbitrary",), 32), name="na_prep_ctx",
        )(big, kw, gmat)
    return pl.pallas_call(
        _na_prep_qk_kernel, grid=(t // tm,),
        in_specs=[blk(qcol), blk(kcol), row, row, gspec], out_specs=[out, out],
        out_shape=[jax.ShapeDtypeStruct((t, D_NA), BF16)] * 2,
        compiler_params=_cparams(("arbitrary",), 32), name="na_prep",
    )(big, big, qw, kw, gmat)


NA_LOOKAHEAD = 2


NA_ROWS_PER_STEP = 2


def _na_row_stages(r, qrow, q_ref, k_ref, v_ref, kc_ref, vc_ref, bias_ref, o_ref, rows):
    rs = jnp.clip(r - NA_KH // 2, 0, rows - NA_KH)
    start = pl.multiple_of(rs * GRID_W, GRID_W)
    nk = NA_KH * GRID_W
    lane = lax.broadcasted_iota(I32, (GRID_W, LANE), 1)
    first_head = lane < NA_HEADDIM
    npair = NA_HEADS // 2
    qrows = slice(qrow, qrow + GRID_W)

    def scores(j):
        cs = slice(j * LANE, (j + 1) * LANE)
        qp = q_ref[qrows, cs]
        zero = jnp.zeros_like(qp)
        qs = jnp.concatenate([jnp.where(first_head, qp, zero), jnp.where(first_head, zero, qp)], axis=0)
        kb = k_ref[pl.ds(start, nk), cs]
        s_loc = _dot_nt(qs, kb) + bias_ref[0, j * LANE:(j + 1) * LANE, :]
        s_ctx = _dot_nt(qs, kc_ref[:, cs])
        return s_loc, s_ctx

    pending = [scores(j) for j in range(NA_LOOKAHEAD)]
    for j in range(npair):
        yield
        cs = slice(j * LANE, (j + 1) * LANE)
        s_loc, s_ctx = pending.pop(0)
        if j + NA_LOOKAHEAD < npair:
            pending.append(scores(j + NA_LOOKAHEAD))
        vb = v_ref[pl.ds(start, nk), cs]
        m = jnp.maximum(jnp.max(s_loc, axis=-1, keepdims=True), jnp.max(s_ctx, axis=-1, keepdims=True))
        p_loc = jnp.exp(s_loc - m)
        p_ctx = jnp.exp(s_ctx - m)
        den = jnp.sum(p_loc, axis=-1, keepdims=True) + jnp.sum(p_ctx, axis=-1, keepdims=True)
        o = _dot(p_loc.astype(BF16), vb) + _dot(p_ctx.astype(BF16), vc_ref[:, cs])
        o = o / den
        o_ref[qrows, cs] = jnp.where(first_head, o[:GRID_W], o[GRID_W:]).astype(BF16)


def _na_kernel(q_ref, k_ref, v_ref, kc_ref, vc_ref, *rest, rows):
    bias_refs, o_ref = rest[:NA_ROWS_PER_STEP], rest[NA_ROWS_PER_STEP]
    r0 = pl.program_id(1) * NA_ROWS_PER_STEP
    chains = [_na_row_stages(r0 + u, u * GRID_W, q_ref, k_ref, v_ref, kc_ref, vc_ref, bias_refs[u], o_ref, rows)
              for u in range(NA_ROWS_PER_STEP)]
    while chains:
        for ch in list(chains):
            try:
                next(ch)
            except StopIteration:
                chains.remove(ch)


def _neigh_attention(qn, kn, big, kcn, bigc, bias_tab, nb, seq_len, ctx_len, vcol, vccol):
    t = qn.shape[0]
    rows = seq_len // GRID_W
    rps = NA_ROWS_PER_STEP
    steps = rows // rps
    kern = functools.partial(_na_kernel, rows=rows)

    def pat(u):
        def index_map(b, s):
            r = s * rps + u
            return (r - jnp.clip(r - NA_KH // 2, 0, rows - NA_KH), 0, 0)
        return index_map

    bias_specs = [pl.BlockSpec((1, NA_HEADS * GRID_W, NA_KH * GRID_W), pat(u)) for u in range(rps)]
    return pl.pallas_call(
        kern,
        grid=(nb, steps),
        in_specs=[pl.BlockSpec((rps * GRID_W, D_NA), lambda b, s: (b * steps + s, 0)),
                  pl.BlockSpec((seq_len, D_NA), lambda b, s: (b, 0), pipeline_mode=pl.Buffered(1)),
                  pl.BlockSpec((seq_len, D_NA), lambda b, s: (b, vcol), pipeline_mode=pl.Buffered(1)),
                  pl.BlockSpec((ctx_len, D_NA), lambda b, s: (b, 0), pipeline_mode=pl.Buffered(1)),
                  pl.BlockSpec((ctx_len, D_NA), lambda b, s: (b, vccol), pipeline_mode=pl.Buffered(1))] + bias_specs,
        out_specs=pl.BlockSpec((rps * GRID_W, D_NA), lambda b, s: (b * steps + s, 0)),
        out_shape=jax.ShapeDtypeStruct((t, D_NA), BF16),
        compiler_params=_cparams(("arbitrary", "arbitrary"), 48),
        name="neigh_attention",
    )(qn, kn, big, kcn, bigc, *([bias_tab] * rps))


def _na_bias_kernel(rpb_ref, o_ref):
    rp = rpb_ref[0]
    r64 = pltpu.roll(rp, GRID_W, 1)
    c = lax.broadcasted_iota(I32, (GRID_W, LANE), 0)
    kc = lax.broadcasted_iota(I32, (GRID_W, LANE), 1) % GRID_W
    cs = jnp.clip(c - NA_KW // 2, 0, GRID_W - NA_KW)
    valid = (kc >= cs) & (kc < cs + NA_KW)
    pair = []
    for d in range(2 * NA_KH - 2):
        vec = rp[d:d + 1, :] + r64[d + 1:d + 2, :]
        w = pltpu.roll(jnp.broadcast_to(vec, (GRID_W, LANE)), LANE - (NA_KW - 1), 1, stride=1, stride_axis=0)
        pair.append(jnp.where(valid, w, NEG_BIG))
    for p in range(NA_KH):
        for ii in range(NA_KH // 2):
            o_ref[p, :, ii * LANE:(ii + 1) * LANE] = pair[2 * ii - p + NA_KH - 1]


def _na_bias_table(rpb):
    rp = jnp.pad(rpb, ((0, 0), (0, 1), (0, LANE - (2 * NA_KW - 1))))
    return pl.pallas_call(
        _na_bias_kernel,
        grid=(NA_HEADS,),
        in_specs=[pl.BlockSpec((1, 2 * NA_KH, LANE), lambda h: (h, 0, 0))],
        out_specs=pl.BlockSpec((NA_KH, GRID_W, NA_KH * GRID_W), lambda h: (0, h, 0)),
        out_shape=jax.ShapeDtypeStruct((NA_KH, NA_HEADS * GRID_W, NA_KH * GRID_W), F32),
        compiler_params=_cparams(("arbitrary",), 32),
        name="na_bias",
    )(rp)


MERGE_SUB = 128


def _merge_kernel(yf_ref, yb_ref, z_ref, gt_ref, yna_ref, x_ref, g1_ref, sh2_ref, sc2_ref, snw_ref, n2w_ref,
                  wbs_ref, wbn_ref, wo_ref, wrh_ref, wrl_ref, br_ref, x1_ref, h2p_ref, lg_ref):
    tm = x_ref.shape[0]
    subs = [slice(r * MERGE_SUB, (r + 1) * MERGE_SUB) for r in range(tm // MERGE_SUB)]
    half = D_MODEL // 2
    yn = []
    for rs in subs:
        z = z_ref[rs, :].astype(F32)
        y = (yf_ref[rs, :] + yb_ref[rs, :]).astype(F32) * _silu(z)
        ms = jnp.mean(y * y, axis=-1, keepdims=True)
        yn.append((y * lax.rsqrt(ms + EPS) * snw_ref[...]).astype(BF16))
    ab = [(_dot(yn[r], wbs_ref[...]), _dot(yna_ref[rs, :], wbn_ref[...])) for r, rs in enumerate(subs)]
    merged = []
    for r, rs in enumerate(subs):
        g_ssd = gt_ref[rs, :D_MODEL].astype(F32)
        g_na = gt_ref[rs, D_MODEL:].astype(F32)
        merged.append((_sigmoid(g_ssd) * ab[r][0] + _sigmoid(g_na) * ab[r][1]).astype(BF16))
    mo = [_dot(m, wo_ref[...]) for m in merged]
    hs = []
    for r, rs in enumerate(subs):
        x1 = x_ref[rs, :] + g1_ref[0] * mo[r]
        x1_ref[rs, :] = x1
        ms2 = jnp.mean(x1 * x1, axis=-1, keepdims=True)
        h2 = x1 * lax.rsqrt(ms2 + EPS) * n2w_ref[...] * (1.0 + sc2_ref[0]) + sh2_ref[0]
        h_hi, h_lo = _split2(h2)
        hs.append((h_hi, h_lo))
        h2p_ref[rs, :] = _pack_bf16_pairs(h2)
    for r, rs in enumerate(subs):
        h_hi, h_lo = hs[r]
        lg_ref[rs, :] = (_dot(h_hi, wrh_ref[...]) + _dot(h_lo, wrh_ref[...]) + _dot(h_hi, wrl_ref[...])) + br_ref[...]


def _merge(yf, yb, big, yna, x2, mod3, snw, n2w, wbs, wbn, wo, wrh, wrl, br, seq_len, tm, zcol, gcol):
    t = x2.shape[0]
    tiles_per_seq = seq_len // tm
    modspec = lambda kidx: pl.BlockSpec((1, 1, D_MODEL), lambda i: ((i // tiles_per_seq) * N_MOD + kidx, 0, 0))
    full = lambda shp: pl.BlockSpec(shp, lambda i: (0,) * len(shp), pipeline_mode=pl.Buffered(1))
    return pl.pallas_call(
        _merge_kernel,
        grid=(t // tm,),
        in_specs=[pl.BlockSpec((tm, D_SSD), lambda i: (i, 0)),
                  pl.BlockSpec((tm, D_SSD), lambda i: (i, 0)),
                  pl.BlockSpec((tm, D_SSD), lambda i: (i, zcol)),
                  pl.BlockSpec((tm, 2 * D_MODEL), lambda i: (i, gcol)),
                  pl.BlockSpec((tm, D_NA), lambda i: (i, 0)),
                  pl.BlockSpec((tm, D_MODEL), lambda i: (i, 0)),
                  modspec(2), modspec(3), modspec(4),
                  full((1, D_SSD)), full((1, D_MODEL)),
                  full((D_SSD, D_MODEL)), full((D_NA, D_MODEL)), full((D_MODEL, D_MODEL)),
                  full((D_MODEL, LANE)), full((D_MODEL, LANE)), full((1, LANE))],
        out_specs=[pl.BlockSpec((tm, D_MODEL), lambda i: (i, 0)),
                   pl.BlockSpec((tm, D_MODEL // 2), lambda i: (i, 0)),
                   pl.BlockSpec((tm, LANE), lambda i: (i, 0))],
        out_shape=[jax.ShapeDtypeStruct((t, D_MODEL), F32),
                   jax.ShapeDtypeStruct((t, D_MODEL // 2), U32),
                   jax.ShapeDtypeStruct((t, LANE), F32)],
        compiler_params=_cparams(("arbitrary",), 56),
        name="merge",
    )(yf, yb, big, big, yna, x2, mod3, mod3, mod3, snw, n2w, wbs, wbn, wo, wrh, wrl, br)


ROUTE_TM = 512
GRP_LANE0 = N_EXPERTS


def _route_topk(lg):
    tm = lg.shape[0]
    lane = lax.broadcasted_iota(I32, (tm, LANE), 1)
    neg_inf = jnp.float32(-jnp.inf)
    big_lane = jnp.int32(4 * LANE)
    is_grp = (lane >= GRP_LANE0) & (lane < GRP_LANE0 + N_GROUPS)
    gl = jnp.where(is_grp, lg, neg_inf)
    gmax = jnp.max(gl, axis=-1, keepdims=True)
    grp = jnp.min(jnp.where(gl == gmax, lane, big_lane), axis=-1, keepdims=True) - GRP_LANE0
    psum = jnp.sum(jnp.where(is_grp, jnp.exp(lg - gmax), 0.0), axis=-1, keepdims=True)
    p_grp = 1.0 / psum
    in_g = (lane < N_EXPERTS) & ((lane // EXPERTS_PER_GROUP) == grp)
    el = jnp.where(in_g, lg, neg_inf)
    v1 = jnp.max(el, axis=-1, keepdims=True)
    i1 = jnp.min(jnp.where(el == v1, lane, big_lane), axis=-1, keepdims=True)
    el2 = jnp.where(lane == i1, neg_inf, el)
    v2 = jnp.max(el2, axis=-1, keepdims=True)
    i2 = jnp.min(jnp.where(el2 == v2, lane, big_lane), axis=-1, keepdims=True)
    tt = jnp.exp(v2 - v1)
    den = 1.0 + tt
    ga = p_grp / den
    gb = p_grp * tt / den

    sel1 = lane == i1
    sel2 = lane == i2
    return lane, sel1, sel2, ga, gb


def _route_count_kernel(lg_ref, cnt_ref, run_scr):
    @pl.when(pl.program_id(0) == 0)
    def _():
        run_scr[...] = jnp.zeros_like(run_scr)

    _, sel1, sel2, _, _ = _route_topk(lg_ref[...])
    onehot = jnp.where(sel1 | sel2, 1.0, 0.0)
    run_scr[...] = run_scr[...] + jnp.sum(onehot, axis=0, keepdims=True)
    cnt_ref[...] = run_scr[...].astype(I32)


def _route_dest_kernel(lg_ref, stri_ref, pstart_ref, gs_ref, rt_ref, run_scr):
    @pl.when(pl.program_id(0) == 0)
    def _():
        run_scr[...] = pstart_ref[...]

    lane, sel1, sel2, ga, gb = _route_topk(lg_ref[...])
    onehot = jnp.where(sel1 | sel2, 1.0, 0.0)
    pos = _dot(stri_ref[...], onehot.astype(BF16)) + run_scr[...]
    d1 = jnp.sum(jnp.where(sel1, pos, 0.0), axis=-1, keepdims=True)
    d2 = jnp.sum(jnp.where(sel2, pos, 0.0), axis=-1, keepdims=True)
    run_scr[...] = run_scr[...] + jnp.sum(onehot, axis=0, keepdims=True)

    slab = jnp.where(lane == 0, ga, 0.0)
    slab = jnp.where(lane == 1, gb, slab)
    slab = jnp.where(lane == 2, d1, slab)
    slab = jnp.where(lane == 3, d2, slab)
    gs_ref[...] = slab
    for q in range(ROUTE_TM // LANE):
        blk_t = slab[q * LANE:(q + 1) * LANE, :].T
        rt_ref[:, q * LANE:(q + 1) * LANE] = blk_t[0:8, :].astype(I32)


def _route_count(logits):
    t = logits.shape[0]
    tm = ROUTE_TM
    return pl.pallas_call(
        _route_count_kernel,
        grid=(t // tm,),
        in_specs=[pl.BlockSpec((tm, LANE), lambda i: (i, 0))],
        out_specs=pl.BlockSpec((1, LANE), lambda i: (0, 0)),
        out_shape=jax.ShapeDtypeStruct((1, LANE), I32),
        scratch_shapes=[pltpu.VMEM((1, LANE), F32)],
        compiler_params=_cparams(("arbitrary",), 32),
        name="route_count",
    )(logits)


def _route_dest(logits, stri, pstart_row):
    t = logits.shape[0]
    tm = ROUTE_TM
    return pl.pallas_call(
        _route_dest_kernel,
        grid=(t // tm,),
        in_specs=[pl.BlockSpec((tm, LANE), lambda i: (i, 0)),
                  pl.BlockSpec((tm, tm), lambda i: (0, 0)),
                  pl.BlockSpec((1, LANE), lambda i: (0, 0))],
        out_specs=[pl.BlockSpec((tm, LANE), lambda i: (i, 0)),
                   pl.BlockSpec((8, tm), lambda i: (0, i))],
        out_shape=[jax.ShapeDtypeStruct((t, LANE), F32),
                   jax.ShapeDtypeStruct((8, t), I32)],
        scratch_shapes=[pltpu.VMEM((1, LANE), F32)],
        compiler_params=_cparams(("arbitrary",), 32),
        name="route_dest",
    )(logits, stri, pstart_row)


MOE_BLOCK = 512
DISPATCH_TM = 512
COMBINE_TM = 256


ROW_UNROLL = 8


def _dispatch_kernel(d1_ref, d2_ref, h_ref, xb_in_hbm, xb_hbm, sem):
    del xb_in_hbm
    tm = DISPATCH_TM

    def body(g, carry):
        for u in range(ROW_UNROLL):
            tt = g * ROW_UNROLL + u
            src = h_ref.at[pl.ds(tt, 1)]
            pltpu.make_async_copy(src, xb_hbm.at[pl.ds(d1_ref[0, 0, tt], 1)], sem).start()
            pltpu.make_async_copy(src, xb_hbm.at[pl.ds(d2_ref[0, 0, tt], 1)], sem).start(priority=1)
        return carry

    lax.fori_loop(0, tm // ROW_UNROLL, body, 0)
    for _ in range(2):
        pltpu.make_async_copy(h_ref, xb_hbm.at[pl.ds(0, tm)], sem).wait()


def _dispatch(d1, d2, h2p, n_rows):
    t = h2p.shape[0]
    tm = DISPATCH_TM
    w = h2p.shape[1]
    smem = pl.BlockSpec((1, 1, tm), lambda i: (i, 0, 0), memory_space=pltpu.SMEM)
    anyspec = pl.BlockSpec(memory_space=pl.ANY)
    xb0 = jnp.zeros((n_rows, w), U32)
    return pl.pallas_call(
        _dispatch_kernel,
        grid=(t // tm,),
        in_specs=[smem, smem, pl.BlockSpec((tm, w), lambda i: (i, 0)), anyspec],
        out_specs=anyspec,
        out_shape=jax.ShapeDtypeStruct((n_rows, w), U32),
        scratch_shapes=[pltpu.SemaphoreType.DMA(())],
        input_output_aliases={3: 0},
        compiler_params=pltpu.CompilerParams(dimension_semantics=("arbitrary",), has_side_effects=True),
        name="moe_dispatch",
    )(d1, d2, h2p, xb0)


def _ffn_kernel(be_ref, nu_ref, xb_ref, w1_ref, w3_ref, w2_ref, yb_ref, w13_scr, w2_scr):
    i = pl.program_id(0)
    prev = be_ref[jnp.maximum(i - 1, 0)]
    changed = (i == 0) | (be_ref[i] != prev)
    used = i < nu_ref[0]

    @pl.when(changed & used)
    def _():
        w13_scr[:, :D_EXPERT] = w1_ref[0].astype(BF16)
        w13_scr[:, D_EXPERT:] = w3_ref[0].astype(BF16)
        w2_scr[...] = w2_ref[0].astype(BF16)

    @pl.when(used)
    def _():
        x = jnp.concatenate(_unpack_bf16_pairs(xb_ref[...]), axis=1).astype(BF16)
        h = _dot(x, w13_scr[...])
        h1 = h[:, :D_EXPERT]
        h3 = h[:, D_EXPERT:]
        a = (_silu(h1) * h3).astype(BF16)
        yb_ref[...] = _pack_bf16_pairs(_dot(a, w2_scr[...]))

    @pl.when(jnp.logical_not(used))
    def _():
        yb_ref[...] = jnp.zeros_like(yb_ref)


def _ffn(blk_exp, n_used, xb, w1, w3, w2):
    n_rows = xb.shape[0]
    nblk = n_rows // MOE_BLOCK
    wmap = lambda i, be, nu: (be[i], 0, 0)
    grid_spec = pltpu.PrefetchScalarGridSpec(
        num_scalar_prefetch=2, grid=(nblk,),
        in_specs=[pl.BlockSpec((MOE_BLOCK, D_MODEL // 2), lambda i, be, nu: (i, 0)),
                  pl.BlockSpec((1, D_MODEL, D_EXPERT), wmap),
                  pl.BlockSpec((1, D_MODEL, D_EXPERT), wmap),
                  pl.BlockSpec((1, D_EXPERT, D_MODEL), wmap)],
        out_specs=pl.BlockSpec((MOE_BLOCK, D_MODEL // 2), lambda i, be, nu: (i, 0)),
        scratch_shapes=[pltpu.VMEM((D_MODEL, 2 * D_EXPERT), BF16),
                        pltpu.VMEM((D_EXPERT, D_MODEL), BF16)])
    return pl.pallas_call(
        _ffn_kernel, grid_spec=grid_spec,
        out_shape=jax.ShapeDtypeStruct((n_rows, D_MODEL // 2), U32),
        compiler_params=_cparams(("arbitrary",), 40),
        name="moe_ffn",
    )(blk_exp, n_used, xb, w1, w3, w2)


def _combine_kernel(d1_ref, d2_ref, d1n_ref, d2n_ref, yb_hbm, gs_ref, x1_ref, g2_ref, o_ref,
                    ya_scr, yb_scr, sem):
    tm = COMBINE_TM
    i = pl.program_id(0)
    slot = i % 2

    def issue(da_ref, db_ref, s):
        def body(g, carry):
            for u in range(ROW_UNROLL):
                tt = g * ROW_UNROLL + u
                pltpu.make_async_copy(yb_hbm.at[pl.ds(da_ref[0, 0, tt], 1)],
                                      ya_scr.at[s, pl.ds(tt, 1)], sem.at[s]).start()
                pltpu.make_async_copy(yb_hbm.at[pl.ds(db_ref[0, 0, tt], 1)],
                                      yb_scr.at[s, pl.ds(tt, 1)], sem.at[s]).start(priority=1)
            return carry

        lax.fori_loop(0, tm // ROW_UNROLL, body, 0)

    @pl.when(i == 0)
    def _():
        issue(d1_ref, d2_ref, 0)

    @pl.when(i + 1 < pl.num_programs(0))
    def _():
        issue(d1n_ref, d2n_ref, 1 - slot)

    pltpu.make_async_copy(yb_hbm.at[pl.ds(0, tm)], ya_scr.at[slot], sem.at[slot]).wait()
    pltpu.make_async_copy(yb_hbm.at[pl.ds(0, tm)], yb_scr.at[slot], sem.at[slot]).wait()
    ga = gs_ref[:, 0:1]
    gb = gs_ref[:, 1:2]
    a_lo, a_hi = _unpack_bf16_pairs(ya_scr[slot])
    b_lo, b_hi = _unpack_bf16_pairs(yb_scr[slot])
    half = D_MODEL // 2
    o_ref[:, :half] = x1_ref[:, :half] + g2_ref[0][:, :half] * (ga * a_lo + gb * b_lo)
    o_ref[:, half:] = x1_ref[:, half:] + g2_ref[0][:, half:] * (ga * a_hi + gb * b_hi)


def _combine(d1, d2, ybuf, gs, x1, mod3, seq_len):
    t = x1.shape[0]
    tm = COMBINE_TM
    nt = t // tm
    tiles_per_seq = seq_len // tm
    smem = pl.BlockSpec((1, 1, tm), lambda i: (i, 0, 0), memory_space=pltpu.SMEM)
    smem_next = pl.BlockSpec((1, 1, tm), lambda i: (jnp.minimum(i + 1, nt - 1), 0, 0), memory_space=pltpu.SMEM)
    return pl.pallas_call(
        _combine_kernel,
        grid=(nt,),
        in_specs=[smem, smem, smem_next, smem_next,
                  pl.BlockSpec(memory_space=pl.ANY),
                  pl.BlockSpec((tm, LANE), lambda i: (i, 0)),
                  pl.BlockSpec((tm, D_MODEL), lambda i: (i, 0)),
                  pl.BlockSpec((1, 1, D_MODEL), lambda i: ((i // tiles_per_seq) * N_MOD + 5, 0, 0))],
        out_specs=pl.BlockSpec((tm, D_MODEL), lambda i: (i, 0)),
        out_shape=jax.ShapeDtypeStruct((t, D_MODEL), F32),
        scratch_shapes=[pltpu.VMEM((2, tm, D_MODEL // 2), U32), pltpu.VMEM((2, tm, D_MODEL // 2), U32),
                        pltpu.SemaphoreType.DMA((2,))],
        compiler_params=_cparams(("arbitrary",), 32),
        name="moe_combine",
    )(d1, d2, d1, d2, ybuf, gs, x1, mod3)


def _rope_tables(seq_len):
    t = np.arange(seq_len)
    row = (t // GRID_W).astype(np.float32)
    colp = (t % GRID_W).astype(np.float32)
    half = SSD_STATE // 2
    inv = (ROPE_THETA ** (-np.arange(0, half, 2, dtype=np.float32) / half)).astype(np.float32)
    ar = (row[:, None] * inv).astype(np.float64)
    ac = (colp[:, None] * inv).astype(np.float64)
    cos_t = np.concatenate([np.cos(ar), np.cos(ar), np.cos(ac), np.cos(ac)], axis=-1)
    sin_t = np.concatenate([-np.sin(ar), np.sin(ar), -np.sin(ac), np.sin(ac)], axis=-1)
    return jnp.asarray(cos_t, F32), jnp.asarray(sin_t, F32)


def _scan_tables(rev):
    li = np.arange(SSD_CHUNK)[:, None]
    ui = np.arange(SSD_CHUNK)[None, :]
    tri = (ui >= li) if rev else (ui <= li)
    col = SSD_HEADS if rev else 0
    j = np.arange(LANE)[:, None]
    c = np.arange(D_SSD)[None, :]
    ee = j == col + c // SSD_HEADDIM
    return jnp.asarray(tri, BF16), jnp.asarray(ee, BF16)


def _pad_lanes(v, width=LANE):
    v = v.reshape(1, -1)
    return jnp.pad(v, ((0, 0), (0, width - v.shape[1])))


def kernel(x, c, ctx, c_ctx, w_mod, b_mod, norm1_w, w_in, conv_w, conv_b, a_log_f, a_log_b, dt_bias_f, dt_bias_b, d_skip, ssd_norm_w, q_norm_w, k_norm_w, rpb, w_br_ssd, w_br_na, w_out, norm2_w, w_grp, b_grp, w_rt, b_rt, w1, w3, w2):
    nb, seq_len, d = x.shape
    ctx_len = ctx.shape[1]
    t = nb * seq_len
    tc = nb * ctx_len
    assert w_mod.shape[0] == 1 and d == D_MODEL and nb <= 7
    assert seq_len % 256 == 0 and ctx_len % SSD_CHUNK == 0 and seq_len // GRID_W >= NA_KH

    cin = jnp.concatenate([c, c_ctx[None, :], jnp.zeros((8 - nb - 1, d), F32)], axis=0)
    mod = _modulation(cin, w_mod[0], b_mod[0])
    mod3 = mod.reshape(8 * N_MOD, 1, D_MODEL)

    wi = w_in[0]
    o_dt = D_SSD + D_XBC
    o_qkv = o_dt + 2 * SSD_HEADS
    w_a = wi
    w_b = wi[:, o_qkv:].astype(BF16)
    w_dtp = jnp.pad(wi[:, o_dt:o_qkv].astype(BF16), ((0, 0), (0, LANE - 2 * SSD_HEADS)))
    n1w = norm1_w[0].reshape(1, D_MODEL)

    x2 = x.reshape(t, D_MODEL)
    ctx2 = ctx.reshape(tc, D_MODEL)
    tm_in = 2048 if seq_len % 2048 == 0 else 256
    tiles = seq_len // tm_in
    tn_in = 1024
    z_t = list(range(0, D_SSD // tn_in))
    xbc_t = list(range(D_SSD // tn_in, o_dt // tn_in))
    q_t, k_t, v_t = [0], [1], [2]
    g_t = [3, 4]
    big, dt_raw = _inproj(x2, mod3, lambda i: i // tiles, n1w, w_a, w_b, w_dtp, tm_in, tn_in,
                          xbc_t + z_t, g_t + q_t + k_t + v_t)
    tm_c = 1024 if tc % 1024 == 0 else SSD_CHUNK
    bigc, dtc_raw = _inproj(ctx2, mod3, lambda i: nb, n1w, w_a, w_b, w_dtp, tm_c, tn_in, xbc_t, k_t + v_t)
    ZCOL, GCOL, QCOL, KCOL, VCOL = 2, 3, 8, 9, 10
    KC_COL, VC_COL = 4, 5

    dtbias = _pad_lanes(jnp.concatenate([dt_bias_f[0], dt_bias_b[0]]))
    cw = conv_w[0]
    cbias = conv_b[0].reshape(1, D_XBC)
    cos_t, sin_t = _rope_tables(seq_len)
    xs, bc, dts = _ssd_prep(big, dt_raw, cw, cbias, dtbias, cos_t, sin_t, seq_len, 256)
    ctl = 256 if ctx_len % 256 == 0 else SSD_CHUNK
    ones_t = jnp.ones((ctx_len, LANE), F32)
    zeros_t = jnp.zeros((ctx_len, LANE), F32)
    xsc, bcc, dtsc = _ssd_prep(bigc, dtc_raw, cw, cbias, dtbias, ones_t, zeros_t, ctx_len, ctl)

    alog = _pad_lanes(jnp.concatenate([a_log_f[0], a_log_b[0]]))
    dskip = jnp.repeat(d_skip[0], SSD_HEADDIM).reshape(1, D_SSD)
    tri_f, ee_f = _scan_tables(False)
    tri_b, ee_b = _scan_tables(True)
    hcf = _ssd_state(xsc, bcc, dtsc, alog, tri_f, ee_f, nb, False)
    hcb = _ssd_state(xsc, bcc, dtsc, alog, tri_b, ee_b, nb, True)
    yf, yb = _ssd_scan(xs, bc, dts, alog, dskip, (tri_f, ee_f), (tri_b, ee_b), hcf, hcb, nb)

    qw = jnp.tile(q_norm_w[0], NA_HEADS).reshape(1, D_NA)
    kw = jnp.tile(k_norm_w[0], NA_HEADS).reshape(1, D_NA)
    gi = np.arange(LANE)
    gmat = jnp.asarray(((gi[:, None] // NA_HEADDIM) == (gi[None, :] // NA_HEADDIM)) * (1.0 / NA_HEADDIM), BF16)
    qn, kn = _na_prep(big, QCOL, KCOL, qw, kw, gmat, 512 if t % 512 == 0 else 256)
    kcn = _na_prep(bigc, None, KC_COL, qw, kw, gmat, 256 if tc % 256 == 0 else SSD_CHUNK)
    bias_tab = _na_bias_table(rpb[0])
    y_na = _neigh_attention(qn, kn, big, kcn, bigc, bias_tab, nb, seq_len, ctx_len, VCOL, VC_COL)

    w_r = jnp.pad(jnp.concatenate([w_rt[0], w_grp[0]], axis=1), ((0, 0), (0, LANE - N_EXPERTS - N_GROUPS)))
    wrh = w_r.astype(BF16)
    wrl = (w_r - wrh.astype(F32)).astype(BF16)
    br = _pad_lanes(jnp.concatenate([b_rt[0], b_grp[0]]))
    x1, h2p, logits = _merge(yf, yb, big, y_na, x2, mod3,
                             ssd_norm_w[0].reshape(1, D_SSD), norm2_w[0].reshape(1, D_MODEL),
                             w_br_ssd[0].astype(BF16), w_br_na[0].astype(BF16), w_out[0].astype(BF16),
                             wrh, wrl, br, seq_len, 512 if seq_len % 512 == 0 else 256, ZCOL, GCOL)

    si = np.arange(ROUTE_TM)
    stri = jnp.asarray(si[None, :] < si[:, None], BF16)
    cnt = _route_count(logits)

    counts = cnt[0, :N_EXPERTS]
    padded = (counts + MOE_BLOCK - 1) // MOE_BLOCK * MOE_BLOCK
    pend = jnp.cumsum(padded)
    pstart = pend - padded
    nblk = (2 * t + N_EXPERTS * (MOE_BLOCK - 1) + MOE_BLOCK - 1) // MOE_BLOCK
    n_used = (pend[-1] // MOE_BLOCK).astype(I32).reshape(1)
    blk_row0 = jnp.arange(nblk, dtype=I32) * MOE_BLOCK
    blk_exp = jnp.minimum(jnp.sum((pend[None, :] <= blk_row0[:, None]).astype(I32), axis=1), N_EXPERTS - 1)
    gs, rt = _route_dest(logits, stri, _pad_lanes(pstart.astype(F32)))

    def tok_tiles(row, tm):
        return rt[row].reshape(t // tm, 1, tm)

    xb = _dispatch(tok_tiles(2, DISPATCH_TM), tok_tiles(3, DISPATCH_TM), h2p, nblk * MOE_BLOCK)
    ybuf = _ffn(blk_exp, n_used, xb, w1[0], w3[0], w2[0])
    out = _combine(tok_tiles(2, COMBINE_TM), tok_tiles(3, COMBINE_TM), ybuf, gs, x1, mod3, seq_len)
    return out.reshape(nb, seq_len, D_MODEL)
```

```python
import functools
import math

import numpy as np

import jax
import jax.numpy as jnp
from jax import lax
from jax.experimental import pallas as pl
from jax.experimental.pallas import tpu as pltpu

F32 = jnp.float32
BF16 = jnp.bfloat16
I32 = jnp.int32
U32 = jnp.uint32

D_MODEL = 1024
GRID_W = 64
D_SSD = 2048
SSD_HEADDIM = 64
SSD_HEADS = 32
SSD_GROUPS = 8
HEADS_PER_GROUP = 4
SSD_STATE = 128
SSD_CHUNK = 128
CONV_K = 5
D_BC = SSD_GROUPS * SSD_STATE
D_XBC = D_SSD + 2 * D_BC
NA_HEADDIM = 64
NA_HEADS = 16
D_NA = 1024
NA_KH = 8
NA_KW = 16
ROPE_THETA = 10000.0
N_GROUPS = 4
EXPERTS_PER_GROUP = 8
N_EXPERTS = 32
D_EXPERT = 512
N_MOD = 6
EPS = 1e-6
NEG_BIG = -1e30

COLS_LAT = D_XBC + D_SSD + 2 * D_MODEL + 3 * D_NA
COLS_CTX = D_XBC + 2 * D_NA
LANE = 128

V7X_VMEM_BYTES = 64 * 1024 * 1024


def _cparams(sem, vmem_mb):
    return pltpu.CompilerParams(dimension_semantics=sem, vmem_limit_bytes=vmem_mb * 1024 * 1024)


def _sigmoid(x):
    return 1.0 / (1.0 + jnp.exp(-x))


def _silu(x):
    h = 0.5 * x
    return h * jnp.tanh(h) + h


def _split2(x):
    hi = x.astype(BF16)
    lo = (x - hi.astype(F32)).astype(BF16)
    return hi, lo


def _split3(x):
    hi = x.astype(BF16)
    r = x - hi.astype(F32)
    mid = r.astype(BF16)
    lo = (r - mid.astype(F32)).astype(BF16)
    return hi, mid, lo


def _pack_bf16_pairs(x):
    bits = pltpu.bitcast(x.astype(BF16).astype(F32), U32)
    n = x.shape[1] // 2
    return (bits[:, :n] >> 16) | bits[:, n:]


def _unpack_bf16_pairs(w):
    return pltpu.bitcast(w << 16, F32), pltpu.bitcast(w & jnp.uint32(0xFFFF0000), F32)


def _dot(a, b):
    return jnp.dot(a, b, preferred_element_type=F32)


def _dot_nt(a, b):
    return lax.dot_general(a, b, (((1,), (1,)), ((), ())), preferred_element_type=F32)


def _dot_tn(a, b):
    return lax.dot_general(a, b, (((0,), (0,)), ((), ())), preferred_element_type=F32)


def _mod_kernel(c_ref, w_ref, b_ref, o_ref):
    c = c_ref[...]
    a = _silu(c).astype(BF16)
    o_ref[...] = _dot(a, w_ref[...].astype(BF16)) + b_ref[...]


def _modulation(cin, w_mod, b_mod):
    n = w_mod.shape[1]
    tn = 1536
    return pl.pallas_call(
        _mod_kernel,
        grid=(n // tn,),
        in_specs=[pl.BlockSpec((8, D_MODEL), lambda j: (0, 0)),
                  pl.BlockSpec((D_MODEL, tn), lambda j: (0, j)),
                  pl.BlockSpec((1, tn), lambda j: (0, j))],
        out_specs=pl.BlockSpec((8, tn), lambda j: (0, j)),
        out_shape=jax.ShapeDtypeStruct((8, n), F32),
        compiler_params=_cparams(("arbitrary",), 40),
        name="modulation",
    )(cin, w_mod, b_mod.reshape(1, n))


def _inproj_kernel(x_ref, sh_ref, sc_ref, nw_ref, wa_ref, wb_ref, wdt_ref, o_ref, dt_ref, h_scr, *, tm, n_a):
    j = pl.program_id(1)

    @pl.when(j == 0)
    def _():
        scale = 1.0 + sc_ref[0]
        shift = sh_ref[0]
        nw = nw_ref[...]

        def body(r, carry):
            rows = pl.ds(pl.multiple_of(r * 128, 128), 128)
            x = x_ref[rows, :]
            ms = jnp.mean(x * x, axis=-1, keepdims=True)
            h = x * lax.rsqrt(ms + EPS) * nw * scale + shift
            h_scr[rows, :] = h.astype(BF16)
            return carry

        lax.fori_loop(0, tm // 128, body, 0)
        dt_ref[...] = _dot(h_scr[...], wdt_ref[...])

    @pl.when(j < n_a)
    def _():
        o_ref[...] = _dot(h_scr[...], wa_ref[...]).astype(BF16)

    @pl.when(j >= n_a)
    def _():
        o_ref[...] = _dot(h_scr[...], wb_ref[...]).astype(BF16)


def _inproj(x2, mod3, mod_row_fn, norm_w, wa, wb, wdt, tm, tn, a_tiles, b_tiles):
    t = x2.shape[0]
    n_a, n_b = len(a_tiles), len(b_tiles)
    a_tab = list(a_tiles) + [a_tiles[-1]] * n_b
    b_tab = [b_tiles[0]] * n_a + list(b_tiles)

    def pick(tab, j):
        idx = jnp.int32(tab[-1])
        for pos in range(len(tab) - 2, -1, -1):
            idx = jnp.where(j <= pos, tab[pos], idx)
        return idx

    kern = functools.partial(_inproj_kernel, tm=tm, n_a=n_a)
    return pl.pallas_call(
        kern,
        grid=(t // tm, n_a + n_b),
        in_specs=[pl.BlockSpec((tm, D_MODEL), lambda i, j: (i, 0)),
                  pl.BlockSpec((1, 1, D_MODEL), lambda i, j: (mod_row_fn(i) * N_MOD + 0, 0, 0)),
                  pl.BlockSpec((1, 1, D_MODEL), lambda i, j: (mod_row_fn(i) * N_MOD + 1, 0, 0)),
                  pl.BlockSpec((1, D_MODEL), lambda i, j: (0, 0)),
                  pl.BlockSpec((D_MODEL, tn), lambda i, j: (0, pick(a_tab, j))),
                  pl.BlockSpec((D_MODEL, tn), lambda i, j: (0, pick(b_tab, j))),
                  pl.BlockSpec((D_MODEL, LANE), lambda i, j: (0, 0))],
        out_specs=[pl.BlockSpec((tm, tn), lambda i, j: (i, j)),
                   pl.BlockSpec((tm, LANE), lambda i, j: (i, 0))],
        out_shape=[jax.ShapeDtypeStruct((t, (n_a + n_b) * tn), BF16),
                   jax.ShapeDtypeStruct((t, LANE), F32)],
        scratch_shapes=[pltpu.VMEM((tm, D_MODEL), BF16)],
        compiler_params=_cparams(("arbitrary", "arbitrary"), 52),
        name="inproj",
    )(x2, mod3, mod3, norm_w, wa, wb, wdt)


CONV_OFFSETS = (-2, -1, 1, 2)


def _ssd_prep_kernel(prev_ref, cur_ref, next_ref, dt_ref, cw_ref, cb_ref, dtbias_ref, cos_ref, sin_ref,
                     shift_ref, etop_ref, ebot_ref, xs_ref, bc_ref, dto_ref, *, tl, tiles_per_seq):
    i = pl.program_id(0)
    first = (i % tiles_per_seq) == 0
    last = (i % tiles_per_seq) == tiles_per_seq - 1
    lane = lax.broadcasted_iota(I32, (tl, LANE), 1)
    lo_half = ((lane // 32) % 2) == 0
    shift = shift_ref[...]
    etop = etop_ref[...]
    ebot = ebot_ref[...]
    cosv = cos_ref[...]
    sinv = sin_ref[...]
    cwid = 256
    for c in range(D_XBC // cwid):
        cs = slice(c * cwid, (c + 1) * cwid)
        w = cw_ref[:, cs]
        xc = cur_ref[:, cs]
        sh = _dot(shift, xc).reshape(tl // 8, len(CONV_OFFSETS), 8, cwid)
        tap = lambda k: sh[:, k].reshape(tl, cwid)
        acc = cb_ref[:, cs] + xc.astype(F32) * w[2:3, :]
        acc = acc + tap(0) * w[0:1, :] + tap(1) * w[1:2, :] + tap(2) * w[3:4, :] + tap(3) * w[4:5, :]
        top = _dot(etop, prev_ref[:, cs])
        bot = _dot(ebot, next_ref[:, cs])
        top_c = jnp.where(first, 0.0, top[0:8] * w[0:1, :] + top[8:16] * w[1:2, :])
        bot_c = jnp.where(last, 0.0, bot[0:8] * w[3:4, :] + bot[8:16] * w[4:5, :])
        acc = jnp.concatenate([acc[0:8] + top_c, acc[8:tl - 8], acc[tl - 8:tl] + bot_c], axis=0)
        y = _silu(acc)
        if c * cwid < D_SSD:
            xs_ref[:, cs] = y.astype(BF16)
        else:
            for g in range(cwid // LANE):
                yg = y[:, g * LANE:(g + 1) * LANE]
                partner = jnp.where(lo_half, pltpu.roll(yg, 96, 1), pltpu.roll(yg, 32, 1))
                o = yg * cosv + partner * sinv
                c0 = c * cwid - D_SSD + g * LANE
                bc_ref[:, c0:c0 + LANE] = o.astype(BF16)

    v = dt_ref[...] + dtbias_ref[...]
    dto_ref[...] = jnp.maximum(v, 0.0) + jnp.log(1.0 + jnp.exp(-jnp.abs(v)))


def _conv_shift_tables(tl):
    t = np.arange(tl)[:, None]
    u = np.arange(tl)[None, :]
    shift = np.stack([(u == t + off) for off in CONV_OFFSETS], axis=0).reshape(len(CONV_OFFSETS), tl // 8, 8, tl)
    shift = shift.transpose(1, 0, 2, 3).reshape(len(CONV_OFFSETS) * tl, tl)
    r = np.arange(16)[:, None]
    q = np.arange(16)[None, :]
    etop = ((r < 8) & (q == r + 14)) | ((r >= 8) & (q == r - 8 + 15))
    ebot = ((r < 8) & (q == r - 7)) | ((r >= 8) & (q == r - 8 - 6))
    return jnp.asarray(shift, BF16), jnp.asarray(etop, BF16), jnp.asarray(ebot, BF16)


def _ssd_prep(big, dt_raw, conv_w, conv_b, dtbias, cos_t, sin_t, seq_len, tl):
    t = big.shape[0]
    tps = seq_len // tl
    hb = tl // 16
    nhalo = t // 16
    kern = functools.partial(_ssd_prep_kernel, tl=tl, tiles_per_seq=tps)
    shift, etop, ebot = _conv_shift_tables(tl)
    const = lambda i: (0, 0)
    return pl.pallas_call(
        kern,
        grid=(t // tl,),
        in_specs=[pl.BlockSpec((16, D_XBC), lambda i: (jnp.maximum(i * hb - 1, 0), 0)),
                  pl.BlockSpec((tl, D_XBC), lambda i: (i, 0)),
                  pl.BlockSpec((16, D_XBC), lambda i: (jnp.minimum((i + 1) * hb, nhalo - 1), 0)),
                  pl.BlockSpec((tl, LANE), lambda i: (i, 0)),
                  pl.BlockSpec((CONV_K, D_XBC), const),
                  pl.BlockSpec((1, D_XBC), const),
                  pl.BlockSpec((1, LANE), const),
                  pl.BlockSpec((tl, LANE), lambda i: (i % tps, 0)),
                  pl.BlockSpec((tl, LANE), lambda i: (i % tps, 0)),
                  pl.BlockSpec((len(CONV_OFFSETS) * tl, tl), const),
                  pl.BlockSpec((16, 16), const),
                  pl.BlockSpec((16, 16), const)],
        out_specs=[pl.BlockSpec((tl, D_SSD), lambda i: (i, 0)),
                   pl.BlockSpec((tl, 2 * D_BC), lambda i: (i, 0)),
                   pl.BlockSpec((tl, LANE), lambda i: (i, 0))],
        out_shape=[jax.ShapeDtypeStruct((t, D_SSD), BF16),
                   jax.ShapeDtypeStruct((t, 2 * D_BC), BF16),
                   jax.ShapeDtypeStruct((t, LANE), F32)],
        compiler_params=_cparams(("arbitrary",), 40),
        name="ssd_prep",
    )(big, big, big, dt_raw, conv_w, conv_b, dtbias, cos_t, sin_t, shift, etop, ebot)


GROUP_W = HEADS_PER_GROUP * SSD_HEADDIM


def _scan_decays(dt_ref, alog_ref, tri_ref, ee_ref, rev):
    edge = 0 if rev else SSD_CHUNK - 1
    a_row = -jnp.exp(alog_ref[...])
    dt = dt_ref[...]
    trib = tri_ref[...]
    d_hi, d_mid, d_lo = _split3(dt * a_row)
    cum = _dot(trib, d_hi) + _dot(trib, d_mid) + _dot(trib, d_lo)
    tot = cum[edge:edge + 1, :]
    ee = ee_ref[...]
    w_heads = (jnp.exp(tot - cum) * dt).astype(BF16)
    t_hi, t_lo = _split2(jnp.broadcast_to(jnp.exp(tot), (8, LANE)))
    etot = (_dot(t_hi, ee) + _dot(t_lo, ee))[0:1, :]
    return dt, trib, cum, w_heads, etot


def _spread(per_head, ee_ref, g):
    return _dot(per_head, ee_ref[:, g * GROUP_W:(g + 1) * GROUP_W])


def _ssd_state_kernel(xs_ref, bc_ref, dt_ref, alog_ref, tri_ref, ee_ref, hfin_ref, h_scr, *, rev, nc):
    k = pl.program_id(1)

    @pl.when(k == 0)
    def _():
        h_scr[...] = jnp.zeros_like(h_scr)

    _, _, _, w_heads, etot = _scan_decays(dt_ref, alog_ref, tri_ref, ee_ref, rev)
    for g in range(SSD_GROUPS):
        gs = slice(g * GROUP_W, (g + 1) * GROUP_W)
        xw = (xs_ref[:, gs].astype(F32) * _spread(w_heads, ee_ref, g)).astype(BF16)
        h_scr[:, gs] = etot[:, gs] * h_scr[:, gs] + _dot_tn(bc_ref[:, g * SSD_STATE:(g + 1) * SSD_STATE], xw)

    @pl.when(k == nc - 1)
    def _():
        hfin_ref[0] = h_scr[...]


def _scan_chunk_stages(xs_ref, bc_ref, dt_ref, alog_ref, dskip_ref, tri_ref, ee_ref, y_ref, h_scr, rev):
    L = SSD_CHUNK
    col = SSD_HEADS if rev else 0
    dt, trib, cum, w_heads, etot = _scan_decays(dt_ref, alog_ref, tri_ref, ee_ref, rev)
    tri = trib > 0.5
    cum_t = cum.T
    dt_t = dt.T
    o_heads = jnp.exp(cum).astype(BF16)

    gw = GROUP_W
    lane_head = lax.broadcasted_iota(I32, (L, gw), 1) // SSD_HEADDIM

    def b_of(g):
        return bc_ref[:, g * SSD_STATE:(g + 1) * SSD_STATE]

    def c_of(g):
        return bc_ref[:, D_BC + g * SSD_STATE:D_BC + (g + 1) * SSD_STATE]

    def operands(g, cb):
        gs = slice(g * gw, (g + 1) * gw)
        xs_g = xs_ref[:, gs]
        ms = []
        for r in range(HEADS_PER_GROUP):
            hh = col + g * HEADS_PER_GROUP + r
            seg = cum[:, hh:hh + 1] - cum_t[hh:hh + 1, :]
            lm = jnp.exp(jnp.where(tri, seg, NEG_BIG)) * dt_t[hh:hh + 1, :]
            ms.append((cb * lm).astype(BF16))
        mcat = jnp.concatenate(ms, axis=1)
        zero = jnp.zeros_like(xs_g)
        bd = jnp.concatenate([jnp.where(lane_head == r, xs_g, zero) for r in range(HEADS_PER_GROUP)], axis=0)
        xs_f = xs_g.astype(F32)
        xw = (xs_f * _spread(w_heads, ee_ref, g)).astype(BF16)
        return mcat, bd, xs_f, xw, _spread(o_heads, ee_ref, g)

    def cb_of(g):
        return _dot_nt(c_of(g), b_of(g))

    yield
    cb_next = cb_of(0)
    yield
    nxt = operands(0, cb_next)
    cb_next = cb_of(1)
    for g in range(SSD_GROUPS):
        yield
        gs = slice(g * gw, (g + 1) * gw)
        mcat, bd, xs_f, xw, oscale_g = nxt
        h_g = h_scr[:, gs]
        y_diag = _dot(mcat, bd)
        y_off = _dot(c_of(g), h_g.astype(BF16))
        loc = _dot_tn(b_of(g), xw)
        if g + 1 < SSD_GROUPS:
            nxt = operands(g + 1, cb_next)
            if g + 2 < SSD_GROUPS:
                cb_next = cb_of(g + 2)
        y = y_diag + oscale_g * y_off
        if not rev:
            y = y + xs_f * dskip_ref[:, gs]
        y_ref[:, gs] = y.astype(BF16)
        h_scr[:, gs] = etot[:, gs] * h_g + loc


def _ssd_scan_kernel(xsf_ref, bcf_ref, dtf_ref, xsb_ref, bcb_ref, dtb_ref, alog_ref, dskip_ref,
                     trif_ref, eef_ref, trib_ref, eeb_ref, h0f_ref, h0b_ref, yf_ref, yb_ref, hf_scr, hb_scr):
    @pl.when(pl.program_id(1) == 0)
    def _():
        hf_scr[...] = h0f_ref[0]
        hb_scr[...] = h0b_ref[0]

    chains = [_scan_chunk_stages(xsf_ref, bcf_ref, dtf_ref, alog_ref, dskip_ref, trif_ref, eef_ref,
                                 yf_ref, hf_scr, False),
              _scan_chunk_stages(xsb_ref, bcb_ref, dtb_ref, alog_ref, dskip_ref, trib_ref, eeb_ref,
                                 yb_ref, hb_scr, True)]
    while chains:
        for ch in list(chains):
            try:
                next(ch)
            except StopIteration:
                chains.remove(ch)


def _scan_rowmap(nc, rev):
    def rowmap(b, k):
        c = (nc - 1 - k) if rev else k
        return (b * nc + c, 0)
    return rowmap


def _ssd_state(xs, bc, dt, alog_row, tri, ee, nb, rev):
    t = xs.shape[0]
    nc = t // nb // SSD_CHUNK
    rowmap = _scan_rowmap(nc, rev)
    const2 = lambda b, k: (0, 0)
    return pl.pallas_call(
        functools.partial(_ssd_state_kernel, rev=rev, nc=nc),
        grid=(nb, nc),
        in_specs=[pl.BlockSpec((SSD_CHUNK, D_SSD), rowmap),
                  pl.BlockSpec((SSD_CHUNK, D_BC), rowmap),
                  pl.BlockSpec((SSD_CHUNK, LANE), rowmap),
                  pl.BlockSpec((1, LANE), const2),
                  pl.BlockSpec((SSD_CHUNK, SSD_CHUNK), const2),
                  pl.BlockSpec((LANE, D_SSD), const2)],
        out_specs=pl.BlockSpec((1, SSD_STATE, D_SSD), lambda b, k: (b, 0, 0)),
        out_shape=jax.ShapeDtypeStruct((nb, SSD_STATE, D_SSD), F32),
        scratch_shapes=[pltpu.VMEM((SSD_STATE, D_SSD), F32)],
        compiler_params=_cparams(("arbitrary", "arbitrary"), 40),
        name="ssd_state_bwd" if rev else "ssd_state_fwd",
    )(xs, bc, dt, alog_row, tri, ee)


def _ssd_scan(xs, bc, dt, alog_row, dskip_row, tables_f, tables_b, h0f, h0b, nb):
    t = xs.shape[0]
    nc = t // nb // SSD_CHUNK
    fmap = _scan_rowmap(nc, False)
    bmap = _scan_rowmap(nc, True)
    const2 = lambda b, k: (0, 0)
    chunk = lambda width, m: pl.BlockSpec((SSD_CHUNK, width), m)
    tri_spec = pl.BlockSpec((SSD_CHUNK, SSD_CHUNK), const2)
    ee_spec = pl.BlockSpec((LANE, D_SSD), const2)
    state = pl.BlockSpec((1, SSD_STATE, D_SSD), lambda b, k: (b, 0, 0))
    return pl.pallas_call(
        _ssd_scan_kernel,
        grid=(nb, nc),
        in_specs=[chunk(D_SSD, fmap), chunk(2 * D_BC, fmap), chunk(LANE, fmap),
                  chunk(D_SSD, bmap), chunk(2 * D_BC, bmap), chunk(LANE, bmap),
                  pl.BlockSpec((1, LANE), const2),
                  pl.BlockSpec((1, D_SSD), const2),
                  tri_spec, ee_spec, tri_spec, ee_spec, state, state],
        out_specs=[chunk(D_SSD, fmap), chunk(D_SSD, bmap)],
        out_shape=[jax.ShapeDtypeStruct((t, D_SSD), BF16)] * 2,
        scratch_shapes=[pltpu.VMEM((SSD_STATE, D_SSD), F32)] * 2,
        compiler_params=_cparams(("arbitrary", "arbitrary"), 48),
        name="ssd_scan",
    )(xs, bc, dt, xs, bc, dt, alog_row, dskip_row, *tables_f, *tables_b, h0f, h0b)


def _headnorm_cols(src_ref, w_ref, g, dst_ref, scale, transposed=False):
    for c in range(D_NA // LANE):
        cs = slice(c * LANE, (c + 1) * LANE)
        x = src_ref[:, cs].astype(F32)
        ms = _dot((x * x).astype(BF16), g)
        y = x * lax.rsqrt(ms + EPS) * w_ref[:, cs]
        if scale is not None:
            y = y * scale
        if transposed:
            dst_ref[cs, :] = y.T.astype(BF16)
        else:
            dst_ref[:, cs] = y.astype(BF16)


def _na_prep_qk_kernel(q_ref, k_ref, qw_ref, kw_ref, g_ref, qo_ref, ko_ref):
    g = g_ref[...]
    _headnorm_cols(q_ref, qw_ref, g, qo_ref, NA_HEADDIM ** -0.5)
    _headnorm_cols(k_ref, kw_ref, g, ko_ref, None)


def _na_prep_k_kernel(k_ref, kw_ref, g_ref, ko_ref):
    _headnorm_cols(k_ref, kw_ref, g_ref[...], ko_ref, None, transposed=True)


def _na_prep(big, qcol, kcol, qw, kw, gmat, tm):
    t = big.shape[0]
    blk = lambda cidx: pl.BlockSpec((tm, D_NA), lambda i: (i, cidx))
    row = pl.BlockSpec((1, D_NA), lambda i: (0, 0))
    gspec = pl.BlockSpec((LANE, LANE), lambda i: (0, 0))
    out = pl.BlockSpec((tm, D_NA), lambda i: (i, 0))
    if qcol is None:
        return pl.pallas_call(
            _na_prep_k_kernel, grid=(t // tm,),
            in_specs=[blk(kcol), row, gspec], out_specs=pl.BlockSpec((D_NA, tm), lambda i: (i, 0)),
            out_shape=jax.ShapeDtypeStruct((t // tm * D_NA, tm), BF16),
            compiler_params=_cparams(("arbitrary",), 32), name="na_prep_ctx",
        )(big, kw, gmat)
    return pl.pallas_call(
        _na_prep_qk_kernel, grid=(t // tm,),
        in_specs=[blk(qcol), blk(kcol), row, row, gspec], out_specs=[out, out],
        out_shape=[jax.ShapeDtypeStruct((t, D_NA), BF16)] * 2,
        compiler_params=_cparams(("arbitrary",), 32), name="na_prep",
    )(big, big, qw, kw, gmat)


NA_LOOKAHEAD = 2


NA_ROWS_PER_STEP = 2


def _na_row_stages(r, qrow, q_ref, k_ref, v_ref, kc_ref, vc_ref, bias_ref, o_ref, rows):
    rs = jnp.clip(r - NA_KH // 2, 0, rows - NA_KH)
    start = pl.multiple_of(rs * GRID_W, GRID_W)
    nk = NA_KH * GRID_W
    lane = lax.broadcasted_iota(I32, (GRID_W, LANE), 1)
    first_head = lane < NA_HEADDIM
    npair = NA_HEADS // 2
    qrows = slice(qrow, qrow + GRID_W)

    def scores(j):
        cs = slice(j * LANE, (j + 1) * LANE)
        qp = q_ref[qrows, cs]
        zero = jnp.zeros_like(qp)
        qs = jnp.concatenate([jnp.where(first_head, qp, zero), jnp.where(first_head, zero, qp)], axis=0)
        kb = k_ref[pl.ds(start, nk), cs]
        s_loc = _dot_nt(qs, kb) + bias_ref[0, j * LANE:(j + 1) * LANE, :]
        s_ctx = _dot(qs, kc_ref[cs, :])
        return s_loc, s_ctx

    pending = [scores(j) for j in range(NA_LOOKAHEAD)]
    for j in range(npair):
        yield
        cs = slice(j * LANE, (j + 1) * LANE)
        s_loc, s_ctx = pending.pop(0)
        if j + NA_LOOKAHEAD < npair:
            pending.append(scores(j + NA_LOOKAHEAD))
        vb = v_ref[pl.ds(start, nk), cs]
        m = jnp.maximum(jnp.max(s_loc, axis=-1, keepdims=True), jnp.max(s_ctx, axis=-1, keepdims=True))
        p_loc = jnp.exp(s_loc - m)
        p_ctx = jnp.exp(s_ctx - m)
        den = jnp.sum(p_loc, axis=-1, keepdims=True) + jnp.sum(p_ctx, axis=-1, keepdims=True)
        o = _dot(p_loc.astype(BF16), vb) + _dot(p_ctx.astype(BF16), vc_ref[:, cs])
        o = o / den
        o_ref[qrows, cs] = jnp.where(first_head, o[:GRID_W], o[GRID_W:]).astype(BF16)


def _na_kernel(q_ref, k_ref, v_ref, kc_ref, vc_ref, *rest, rows):
    bias_refs, o_ref = rest[:NA_ROWS_PER_STEP], rest[NA_ROWS_PER_STEP]
    r0 = pl.program_id(1) * NA_ROWS_PER_STEP
    chains = [_na_row_stages(r0 + u, u * GRID_W, q_ref, k_ref, v_ref, kc_ref, vc_ref, bias_refs[u], o_ref, rows)
              for u in range(NA_ROWS_PER_STEP)]
    while chains:
        for ch in list(chains):
            try:
                next(ch)
            except StopIteration:
                chains.remove(ch)


def _neigh_attention(qn, kn, big, kcn, bigc, bias_tab, nb, seq_len, ctx_len, vcol, vccol):
    t = qn.shape[0]
    rows = seq_len // GRID_W
    rps = NA_ROWS_PER_STEP
    steps = rows // rps
    kern = functools.partial(_na_kernel, rows=rows)

    def pat(u):
        def index_map(b, s):
            r = s * rps + u
            return (r - jnp.clip(r - NA_KH // 2, 0, rows - NA_KH), 0, 0)
        return index_map

    bias_specs = [pl.BlockSpec((1, NA_HEADS * GRID_W, NA_KH * GRID_W), pat(u)) for u in range(rps)]
    return pl.pallas_call(
        kern,
        grid=(nb, steps),
        in_specs=[pl.BlockSpec((rps * GRID_W, D_NA), lambda b, s: (b * steps + s, 0)),
                  pl.BlockSpec((seq_len, D_NA), lambda b, s: (b, 0), pipeline_mode=pl.Buffered(1)),
                  pl.BlockSpec((seq_len, D_NA), lambda b, s: (b, vcol), pipeline_mode=pl.Buffered(1)),
                  pl.BlockSpec((D_NA, ctx_len), lambda b, s: (b, 0), pipeline_mode=pl.Buffered(1)),
                  pl.BlockSpec((ctx_len, D_NA), lambda b, s: (b, vccol), pipeline_mode=pl.Buffered(1))] + bias_specs,
        out_specs=pl.BlockSpec((rps * GRID_W, D_NA), lambda b, s: (b * steps + s, 0)),
        out_shape=jax.ShapeDtypeStruct((t, D_NA), BF16),
        compiler_params=_cparams(("arbitrary", "arbitrary"), 48),
        name="neigh_attention",
    )(qn, kn, big, kcn, bigc, *([bias_tab] * rps))


def _na_bias_kernel(rpb_ref, o_ref):
    rp = rpb_ref[0]
    r64 = pltpu.roll(rp, GRID_W, 1)
    c = lax.broadcasted_iota(I32, (GRID_W, LANE), 0)
    kc = lax.broadcasted_iota(I32, (GRID_W, LANE), 1) % GRID_W
    cs = jnp.clip(c - NA_KW // 2, 0, GRID_W - NA_KW)
    valid = (kc >= cs) & (kc < cs + NA_KW)
    pair = []
    for d in range(2 * NA_KH - 2):
        vec = rp[d:d + 1, :] + r64[d + 1:d + 2, :]
        w = pltpu.roll(jnp.broadcast_to(vec, (GRID_W, LANE)), LANE - (NA_KW - 1), 1, stride=1, stride_axis=0)
        pair.append(jnp.where(valid, w, NEG_BIG))
    for p in range(NA_KH):
        for ii in range(NA_KH // 2):
            o_ref[p, :, ii * LANE:(ii + 1) * LANE] = pair[2 * ii - p + NA_KH - 1]


def _na_bias_table(rpb):
    rp = jnp.pad(rpb, ((0, 0), (0, 1), (0, LANE - (2 * NA_KW - 1))))
    return pl.pallas_call(
        _na_bias_kernel,
        grid=(NA_HEADS,),
        in_specs=[pl.BlockSpec((1, 2 * NA_KH, LANE), lambda h: (h, 0, 0))],
        out_specs=pl.BlockSpec((NA_KH, GRID_W, NA_KH * GRID_W), lambda h: (0, h, 0)),
        out_shape=jax.ShapeDtypeStruct((NA_KH, NA_HEADS * GRID_W, NA_KH * GRID_W), F32),
        compiler_params=_cparams(("arbitrary",), 32),
        name="na_bias",
    )(rp)


MERGE_SUB = 128


def _merge_kernel(yf_ref, yb_ref, z_ref, gt_ref, yna_ref, x_ref, g1_ref, sh2_ref, sc2_ref, snw_ref, n2w_ref,
                  wbs_ref, wbn_ref, wo_ref, wrh_ref, wrl_ref, br_ref, x1_ref, h2p_ref, lg_ref):
    tm = x_ref.shape[0]
    subs = [slice(r * MERGE_SUB, (r + 1) * MERGE_SUB) for r in range(tm // MERGE_SUB)]
    half = D_MODEL // 2
    yn = []
    for rs in subs:
        z = z_ref[rs, :].astype(F32)
        y = (yf_ref[rs, :] + yb_ref[rs, :]).astype(F32) * _silu(z)
        ms = jnp.mean(y * y, axis=-1, keepdims=True)
        yn.append((y * lax.rsqrt(ms + EPS) * snw_ref[...]).astype(BF16))
    ab = [(_dot(yn[r], wbs_ref[...]), _dot(yna_ref[rs, :], wbn_ref[...])) for r, rs in enumerate(subs)]
    merged = []
    for r, rs in enumerate(subs):
        g_ssd = gt_ref[rs, :D_MODEL].astype(F32)
        g_na = gt_ref[rs, D_MODEL:].astype(F32)
        merged.append((_sigmoid(g_ssd) * ab[r][0] + _sigmoid(g_na) * ab[r][1]).astype(BF16))
    mo = [_dot(m, wo_ref[...]) for m in merged]
    hs = []
    for r, rs in enumerate(subs):
        x1 = x_ref[rs, :] + g1_ref[0] * mo[r]
        x1_ref[rs, :] = x1
        ms2 = jnp.mean(x1 * x1, axis=-1, keepdims=True)
        h2 = x1 * lax.rsqrt(ms2 + EPS) * n2w_ref[...] * (1.0 + sc2_ref[0]) + sh2_ref[0]
        h_hi, h_lo = _split2(h2)
        hs.append((h_hi, h_lo))
        h2p_ref[rs, :] = _pack_bf16_pairs(h2)
    for r, rs in enumerate(subs):
        h_hi, h_lo = hs[r]
        lg_ref[rs, :] = (_dot(h_hi, wrh_ref[...]) + _dot(h_lo, wrh_ref[...]) + _dot(h_hi, wrl_ref[...])) + br_ref[...]


def _merge(yf, yb, big, yna, x2, mod3, snw, n2w, wbs, wbn, wo, wrh, wrl, br, seq_len, tm, zcol, gcol):
    t = x2.shape[0]
    tiles_per_seq = seq_len // tm
    modspec = lambda kidx: pl.BlockSpec((1, 1, D_MODEL), lambda i: ((i // tiles_per_seq) * N_MOD + kidx, 0, 0))
    full = lambda shp: pl.BlockSpec(shp, lambda i: (0,) * len(shp), pipeline_mode=pl.Buffered(1))
    return pl.pallas_call(
        _merge_kernel,
        grid=(t // tm,),
        in_specs=[pl.BlockSpec((tm, D_SSD), lambda i: (i, 0)),
                  pl.BlockSpec((tm, D_SSD), lambda i: (i, 0)),
                  pl.BlockSpec((tm, D_SSD), lambda i: (i, zcol)),
                  pl.BlockSpec((tm, 2 * D_MODEL), lambda i: (i, gcol)),
                  pl.BlockSpec((tm, D_NA), lambda i: (i, 0)),
                  pl.BlockSpec((tm, D_MODEL), lambda i: (i, 0)),
                  modspec(2), modspec(3), modspec(4),
                  full((1, D_SSD)), full((1, D_MODEL)),
                  full((D_SSD, D_MODEL)), full((D_NA, D_MODEL)), full((D_MODEL, D_MODEL)),
                  full((D_MODEL, LANE)), full((D_MODEL, LANE)), full((1, LANE))],
        out_specs=[pl.BlockSpec((tm, D_MODEL), lambda i: (i, 0)),
                   pl.BlockSpec((tm, D_MODEL // 2), lambda i: (i, 0)),
                   pl.BlockSpec((tm, LANE), lambda i: (i, 0))],
        out_shape=[jax.ShapeDtypeStruct((t, D_MODEL), F32),
                   jax.ShapeDtypeStruct((t, D_MODEL // 2), U32),
                   jax.ShapeDtypeStruct((t, LANE), F32)],
        compiler_params=_cparams(("arbitrary",), 56),
        name="merge",
    )(yf, yb, big, big, yna, x2, mod3, mod3, mod3, snw, n2w, wbs, wbn, wo, wrh, wrl, br)


ROUTE_TM = 512
GRP_LANE0 = N_EXPERTS


def _route_topk(lg):
    tm = lg.shape[0]
    lane = lax.broadcasted_iota(I32, (tm, LANE), 1)
    neg_inf = jnp.float32(-jnp.inf)
    big_lane = jnp.int32(4 * LANE)
    is_grp = (lane >= GRP_LANE0) & (lane < GRP_LANE0 + N_GROUPS)
    gl = jnp.where(is_grp, lg, neg_inf)
    gmax = jnp.max(gl, axis=-1, keepdims=True)
    grp = jnp.min(jnp.where(gl == gmax, lane, big_lane), axis=-1, keepdims=True) - GRP_LANE0
    psum = jnp.sum(jnp.where(is_grp, jnp.exp(lg - gmax), 0.0), axis=-1, keepdims=True)
    p_grp = 1.0 / psum
    in_g = (lane < N_EXPERTS) & ((lane // EXPERTS_PER_GROUP) == grp)
    el = jnp.where(in_g, lg, neg_inf)
    v1 = jnp.max(el, axis=-1, keepdims=True)
    i1 = jnp.min(jnp.where(el == v1, lane, big_lane), axis=-1, keepdims=True)
    el2 = jnp.where(lane == i1, neg_inf, el)
    v2 = jnp.max(el2, axis=-1, keepdims=True)
    i2 = jnp.min(jnp.where(el2 == v2, lane, big_lane), axis=-1, keepdims=True)
    tt = jnp.exp(v2 - v1)
    den = 1.0 + tt
    ga = p_grp / den
    gb = p_grp * tt / den

    sel1 = lane == i1
    sel2 = lane == i2
    return lane, sel1, sel2, ga, gb


def _route_count_kernel(lg_ref, cnt_ref, run_scr):
    @pl.when(pl.program_id(0) == 0)
    def _():
        run_scr[...] = jnp.zeros_like(run_scr)

    _, sel1, sel2, _, _ = _route_topk(lg_ref[...])
    onehot = jnp.where(sel1 | sel2, 1.0, 0.0)
    run_scr[...] = run_scr[...] + jnp.sum(onehot, axis=0, keepdims=True)
    cnt_ref[...] = run_scr[...].astype(I32)


def _route_dest_kernel(lg_ref, stri_ref, pstart_ref, gs_ref, rt_ref, run_scr):
    @pl.when(pl.program_id(0) == 0)
    def _():
        run_scr[...] = pstart_ref[...]

    lane, sel1, sel2, ga, gb = _route_topk(lg_ref[...])
    onehot = jnp.where(sel1 | sel2, 1.0, 0.0)
    pos = _dot(stri_ref[...], onehot.astype(BF16)) + run_scr[...]
    d1 = jnp.sum(jnp.where(sel1, pos, 0.0), axis=-1, keepdims=True)
    d2 = jnp.sum(jnp.where(sel2, pos, 0.0), axis=-1, keepdims=True)
    run_scr[...] = run_scr[...] + jnp.sum(onehot, axis=0, keepdims=True)

    slab = jnp.where(lane == 0, ga, 0.0)
    slab = jnp.where(lane == 1, gb, slab)
    slab = jnp.where(lane == 2, d1, slab)
    slab = jnp.where(lane == 3, d2, slab)
    gs_ref[...] = slab
    for q in range(ROUTE_TM // LANE):
        blk_t = slab[q * LANE:(q + 1) * LANE, :].T
        rt_ref[:, q * LANE:(q + 1) * LANE] = blk_t[0:8, :].astype(I32)


def _route_count(logits):
    t = logits.shape[0]
    tm = ROUTE_TM
    return pl.pallas_call(
        _route_count_kernel,
        grid=(t // tm,),
        in_specs=[pl.BlockSpec((tm, LANE), lambda i: (i, 0))],
        out_specs=pl.BlockSpec((1, LANE), lambda i: (0, 0)),
        out_shape=jax.ShapeDtypeStruct((1, LANE), I32),
        scratch_shapes=[pltpu.VMEM((1, LANE), F32)],
        compiler_params=_cparams(("arbitrary",), 32),
        name="route_count",
    )(logits)


def _route_dest(logits, stri, pstart_row):
    t = logits.shape[0]
    tm = ROUTE_TM
    return pl.pallas_call(
        _route_dest_kernel,
        grid=(t // tm,),
        in_specs=[pl.BlockSpec((tm, LANE), lambda i: (i, 0)),
                  pl.BlockSpec((tm, tm), lambda i: (0, 0)),
                  pl.BlockSpec((1, LANE), lambda i: (0, 0))],
        out_specs=[pl.BlockSpec((tm, LANE), lambda i: (i, 0)),
                   pl.BlockSpec((8, tm), lambda i: (0, i))],
        out_shape=[jax.ShapeDtypeStruct((t, LANE), F32),
                   jax.ShapeDtypeStruct((8, t), I32)],
        scratch_shapes=[pltpu.VMEM((1, LANE), F32)],
        compiler_params=_cparams(("arbitrary",), 32),
        name="route_dest",
    )(logits, stri, pstart_row)


MOE_BLOCK = 512
DISPATCH_TM = 512
COMBINE_TM = 256


ROW_UNROLL = 8


DISPATCH_BUFS = 3


def _dispatch_kernel(d1_ref, d2_ref, h_hbm, xb_in_hbm, xb_hbm, hbuf, load_sem, row_sem):
    del xb_in_hbm
    tm = DISPATCH_TM
    i = pl.program_id(0)
    last = pl.num_programs(0) - 1
    slot = i % DISPATCH_BUFS
    par = i % 2

    def load(tile, s):
        return pltpu.make_async_copy(h_hbm.at[pl.ds(tile * tm, tm)], hbuf.at[s], load_sem.at[s])

    def wait_rows(p):
        for _ in range(2):
            pltpu.make_async_copy(hbuf.at[0], xb_hbm.at[pl.ds(0, tm)], row_sem.at[p]).wait()

    @pl.when(i == 0)
    def _():
        load(0, 0).start()

    @pl.when(i < last)
    def _():
        load(i + 1, (i + 1) % DISPATCH_BUFS).start()

    load(i, slot).wait()

    def body(g, carry):
        for u in range(ROW_UNROLL):
            tt = g * ROW_UNROLL + u
            src = hbuf.at[slot, pl.ds(tt, 1)]
            pltpu.make_async_copy(src, xb_hbm.at[pl.ds(d1_ref[0, 0, tt], 1)], row_sem.at[par]).start()
            pltpu.make_async_copy(src, xb_hbm.at[pl.ds(d2_ref[0, 0, tt], 1)], row_sem.at[par]).start(priority=1)
        return carry

    lax.fori_loop(0, tm // ROW_UNROLL, body, 0)

    @pl.when(i > 0)
    def _():
        wait_rows(1 - par)

    @pl.when(i == last)
    def _():
        wait_rows(par)


def _dispatch(d1, d2, h2p, n_rows):
    t = h2p.shape[0]
    tm = DISPATCH_TM
    w = h2p.shape[1]
    smem = pl.BlockSpec((1, 1, tm), lambda i: (i, 0, 0), memory_space=pltpu.SMEM)
    anyspec = pl.BlockSpec(memory_space=pl.ANY)
    xb0 = jnp.zeros((n_rows, w), U32)
    return pl.pallas_call(
        _dispatch_kernel,
        grid=(t // tm,),
        in_specs=[smem, smem, anyspec, anyspec],
        out_specs=anyspec,
        out_shape=jax.ShapeDtypeStruct((n_rows, w), U32),
        scratch_shapes=[pltpu.VMEM((DISPATCH_BUFS, tm, w), U32),
                        pltpu.SemaphoreType.DMA((DISPATCH_BUFS,)), pltpu.SemaphoreType.DMA((2,))],
        input_output_aliases={3: 0},
        compiler_params=pltpu.CompilerParams(dimension_semantics=("arbitrary",), has_side_effects=True),
        name="moe_dispatch",
    )(d1, d2, h2p, xb0)


def _ffn_kernel(be_ref, nu_ref, xb_ref, w1_ref, w3_ref, w2_ref, yb_ref, w13_scr, w2_scr):
    i = pl.program_id(0)
    prev = be_ref[jnp.maximum(i - 1, 0)]
    changed = (i == 0) | (be_ref[i] != prev)
    used = i < nu_ref[0]

    @pl.when(changed & used)
    def _():
        w13_scr[:, :D_EXPERT] = w1_ref[0].astype(BF16)
        w13_scr[:, D_EXPERT:] = w3_ref[0].astype(BF16)
        w2_scr[...] = w2_ref[0].astype(BF16)

    @pl.when(used)
    def _():
        x = jnp.concatenate(_unpack_bf16_pairs(xb_ref[...]), axis=1).astype(BF16)
        h = _dot(x, w13_scr[...])
        h1 = h[:, :D_EXPERT]
        h3 = h[:, D_EXPERT:]
        a = (_silu(h1) * h3).astype(BF16)
        yb_ref[...] = _pack_bf16_pairs(_dot(a, w2_scr[...]))

    @pl.when(jnp.logical_not(used))
    def _():
        yb_ref[...] = jnp.zeros_like(yb_ref)


def _ffn(blk_exp, n_used, xb, w1, w3, w2):
    n_rows = xb.shape[0]
    nblk = n_rows // MOE_BLOCK
    wmap = lambda i, be, nu: (be[i], 0, 0)
    grid_spec = pltpu.PrefetchScalarGridSpec(
        num_scalar_prefetch=2, grid=(nblk,),
        in_specs=[pl.BlockSpec((MOE_BLOCK, D_MODEL // 2), lambda i, be, nu: (i, 0)),
                  pl.BlockSpec((1, D_MODEL, D_EXPERT), wmap),
                  pl.BlockSpec((1, D_MODEL, D_EXPERT), wmap),
                  pl.BlockSpec((1, D_EXPERT, D_MODEL), wmap)],
        out_specs=pl.BlockSpec((MOE_BLOCK, D_MODEL // 2), lambda i, be, nu: (i, 0)),
        scratch_shapes=[pltpu.VMEM((D_MODEL, 2 * D_EXPERT), BF16),
                        pltpu.VMEM((D_EXPERT, D_MODEL), BF16)])
    return pl.pallas_call(
        _ffn_kernel, grid_spec=grid_spec,
        out_shape=jax.ShapeDtypeStruct((n_rows, D_MODEL // 2), U32),
        compiler_params=_cparams(("arbitrary",), 40),
        name="moe_ffn",
    )(blk_exp, n_used, xb, w1, w3, w2)


def _combine_kernel(d1_ref, d2_ref, d1n_ref, d2n_ref, yb_hbm, gs_ref, x1_ref, g2_ref, o_ref,
                    ya_scr, yb_scr, sem):
    tm = COMBINE_TM
    i = pl.program_id(0)
    slot = i % 2

    def issue(da_ref, db_ref, s):
        def body(g, carry):
            for u in range(ROW_UNROLL):
                tt = g * ROW_UNROLL + u
                pltpu.make_async_copy(yb_hbm.at[pl.ds(da_ref[0, 0, tt], 1)],
                                      ya_scr.at[s, pl.ds(tt, 1)], sem.at[s]).start()
                pltpu.make_async_copy(yb_hbm.at[pl.ds(db_ref[0, 0, tt], 1)],
                                      yb_scr.at[s, pl.ds(tt, 1)], sem.at[s]).start(priority=1)
            return carry

        lax.fori_loop(0, tm // ROW_UNROLL, body, 0)

    @pl.when(i == 0)
    def _():
        issue(d1_ref, d2_ref, 0)

    @pl.when(i + 1 < pl.num_programs(0))
    def _():
        issue(d1n_ref, d2n_ref, 1 - slot)

    pltpu.make_async_copy(yb_hbm.at[pl.ds(0, tm)], ya_scr.at[slot], sem.at[slot]).wait()
    pltpu.make_async_copy(yb_hbm.at[pl.ds(0, tm)], yb_scr.at[slot], sem.at[slot]).wait()
    ga = gs_ref[:, 0:1]
    gb = gs_ref[:, 1:2]
    a_lo, a_hi = _unpack_bf16_pairs(ya_scr[slot])
    b_lo, b_hi = _unpack_bf16_pairs(yb_scr[slot])
    half = D_MODEL // 2
    o_ref[:, :half] = x1_ref[:, :half] + g2_ref[0][:, :half] * (ga * a_lo + gb * b_lo)
    o_ref[:, half:] = x1_ref[:, half:] + g2_ref[0][:, half:] * (ga * a_hi + gb * b_hi)


def _combine(d1, d2, ybuf, gs, x1, mod3, seq_len):
    t = x1.shape[0]
    tm = COMBINE_TM
    nt = t // tm
    tiles_per_seq = seq_len // tm
    smem = pl.BlockSpec((1, 1, tm), lambda i: (i, 0, 0), memory_space=pltpu.SMEM)
    smem_next = pl.BlockSpec((1, 1, tm), lambda i: (jnp.minimum(i + 1, nt - 1), 0, 0), memory_space=pltpu.SMEM)
    return pl.pallas_call(
        _combine_kernel,
        grid=(nt,),
        in_specs=[smem, smem, smem_next, smem_next,
                  pl.BlockSpec(memory_space=pl.ANY),
                  pl.BlockSpec((tm, LANE), lambda i: (i, 0)),
                  pl.BlockSpec((tm, D_MODEL), lambda i: (i, 0)),
                  pl.BlockSpec((1, 1, D_MODEL), lambda i: ((i // tiles_per_seq) * N_MOD + 5, 0, 0))],
        out_specs=pl.BlockSpec((tm, D_MODEL), lambda i: (i, 0)),
        out_shape=jax.ShapeDtypeStruct((t, D_MODEL), F32),
        scratch_shapes=[pltpu.VMEM((2, tm, D_MODEL // 2), U32), pltpu.VMEM((2, tm, D_MODEL // 2), U32),
                        pltpu.SemaphoreType.DMA((2,))],
        compiler_params=_cparams(("arbitrary",), 32),
        name="moe_combine",
    )(d1, d2, d1, d2, ybuf, gs, x1, mod3)


def _rope_tables(seq_len):
    t = np.arange(seq_len)
    row = (t // GRID_W).astype(np.float32)
    colp = (t % GRID_W).astype(np.float32)
    half = SSD_STATE // 2
    inv = (ROPE_THETA ** (-np.arange(0, half, 2, dtype=np.float32) / half)).astype(np.float32)
    ar = (row[:, None] * inv).astype(np.float64)
    ac = (colp[:, None] * inv).astype(np.float64)
    cos_t = np.concatenate([np.cos(ar), np.cos(ar), np.cos(ac), np.cos(ac)], axis=-1)
    sin_t = np.concatenate([-np.sin(ar), np.sin(ar), -np.sin(ac), np.sin(ac)], axis=-1)
    return jnp.asarray(cos_t, F32), jnp.asarray(sin_t, F32)


def _scan_tables(rev):
    li = np.arange(SSD_CHUNK)[:, None]
    ui = np.arange(SSD_CHUNK)[None, :]
    tri = (ui >= li) if rev else (ui <= li)
    col = SSD_HEADS if rev else 0
    j = np.arange(LANE)[:, None]
    c = np.arange(D_SSD)[None, :]
    ee = j == col + c // SSD_HEADDIM
    return jnp.asarray(tri, BF16), jnp.asarray(ee, BF16)


def _pad_lanes(v, width=LANE):
    v = v.reshape(1, -1)
    return jnp.pad(v, ((0, 0), (0, width - v.shape[1])))


def kernel(x, c, ctx, c_ctx, w_mod, b_mod, norm1_w, w_in, conv_w, conv_b, a_log_f, a_log_b, dt_bias_f, dt_bias_b, d_skip, ssd_norm_w, q_norm_w, k_norm_w, rpb, w_br_ssd, w_br_na, w_out, norm2_w, w_grp, b_grp, w_rt, b_rt, w1, w3, w2):
    nb, seq_len, d = x.shape
    ctx_len = ctx.shape[1]
    t = nb * seq_len
    tc = nb * ctx_len
    assert w_mod.shape[0] == 1 and d == D_MODEL and nb <= 7
    assert seq_len % 256 == 0 and ctx_len % SSD_CHUNK == 0 and seq_len // GRID_W >= NA_KH

    cin = jnp.concatenate([c, c_ctx[None, :], jnp.zeros((8 - nb - 1, d), F32)], axis=0)
    mod = _modulation(cin, w_mod[0], b_mod[0])
    mod3 = mod.reshape(8 * N_MOD, 1, D_MODEL)

    wi = w_in[0].astype(BF16)
    o_dt = D_SSD + D_XBC
    o_qkv = o_dt + 2 * SSD_HEADS
    w_a = wi[:, :o_dt]
    w_b = wi[:, o_qkv:]
    w_dtp = jnp.pad(wi[:, o_dt:o_qkv], ((0, 0), (0, LANE - 2 * SSD_HEADS)))
    n1w = norm1_w[0].reshape(1, D_MODEL)

    x2 = x.reshape(t, D_MODEL)
    ctx2 = ctx.reshape(tc, D_MODEL)
    tm_in = 2048 if seq_len % 2048 == 0 else 256
    tiles = seq_len // tm_in
    tn_in = 1024
    z_t = list(range(0, D_SSD // tn_in))
    xbc_t = list(range(D_SSD // tn_in, o_dt // tn_in))
    q_t, k_t, v_t = [0], [1], [2]
    g_t = [3, 4]
    big, dt_raw = _inproj(x2, mod3, lambda i: i // tiles, n1w, w_a, w_b, w_dtp, tm_in, tn_in,
                          xbc_t + z_t, g_t + q_t + k_t + v_t)
    tm_c = 1024 if tc % 1024 == 0 else SSD_CHUNK
    bigc, dtc_raw = _inproj(ctx2, mod3, lambda i: nb, n1w, w_a, w_b, w_dtp, tm_c, tn_in, xbc_t, k_t + v_t)
    ZCOL, GCOL, QCOL, KCOL, VCOL = 2, 3, 8, 9, 10
    KC_COL, VC_COL = 4, 5

    dtbias = _pad_lanes(jnp.concatenate([dt_bias_f[0], dt_bias_b[0]]))
    cw = conv_w[0]
    cbias = conv_b[0].reshape(1, D_XBC)
    cos_t, sin_t = _rope_tables(seq_len)
    xs, bc, dts = _ssd_prep(big, dt_raw, cw, cbias, dtbias, cos_t, sin_t, seq_len, 256)
    ctl = 256 if ctx_len % 256 == 0 else SSD_CHUNK
    ones_t = jnp.ones((ctx_len, LANE), F32)
    zeros_t = jnp.zeros((ctx_len, LANE), F32)
    xsc, bcc, dtsc = _ssd_prep(bigc, dtc_raw, cw, cbias, dtbias, ones_t, zeros_t, ctx_len, ctl)

    alog = _pad_lanes(jnp.concatenate([a_log_f[0], a_log_b[0]]))
    dskip = jnp.repeat(d_skip[0], SSD_HEADDIM).reshape(1, D_SSD)
    tri_f, ee_f = _scan_tables(False)
    tri_b, ee_b = _scan_tables(True)
    hcf = _ssd_state(xsc, bcc, dtsc, alog, tri_f, ee_f, nb, False)
    hcb = _ssd_state(xsc, bcc, dtsc, alog, tri_b, ee_b, nb, True)
    yf, yb = _ssd_scan(xs, bc, dts, alog, dskip, (tri_f, ee_f), (tri_b, ee_b), hcf, hcb, nb)

    qw = jnp.tile(q_norm_w[0], NA_HEADS).reshape(1, D_NA)
    kw = jnp.tile(k_norm_w[0], NA_HEADS).reshape(1, D_NA)
    gi = np.arange(LANE)
    gmat = jnp.asarray(((gi[:, None] // NA_HEADDIM) == (gi[None, :] // NA_HEADDIM)) * (1.0 / NA_HEADDIM), BF16)
    qn, kn = _na_prep(big, QCOL, KCOL, qw, kw, gmat, 512 if t % 512 == 0 else 256)
    kcn = _na_prep(bigc, None, KC_COL, qw, kw, gmat, ctx_len)
    bias_tab = _na_bias_table(rpb[0])
    y_na = _neigh_attention(qn, kn, big, kcn, bigc, bias_tab, nb, seq_len, ctx_len, VCOL, VC_COL)

    w_r = jnp.pad(jnp.concatenate([w_rt[0], w_grp[0]], axis=1), ((0, 0), (0, LANE - N_EXPERTS - N_GROUPS)))
    wrh = w_r.astype(BF16)
    wrl = (w_r - wrh.astype(F32)).astype(BF16)
    br = _pad_lanes(jnp.concatenate([b_rt[0], b_grp[0]]))
    x1, h2p, logits = _merge(yf, yb, big, y_na, x2, mod3,
                             ssd_norm_w[0].reshape(1, D_SSD), norm2_w[0].reshape(1, D_MODEL),
                             w_br_ssd[0].astype(BF16), w_br_na[0].astype(BF16), w_out[0].astype(BF16),
                             wrh, wrl, br, seq_len, 512 if seq_len % 512 == 0 else 256, ZCOL, GCOL)

    si = np.arange(ROUTE_TM)
    stri = jnp.asarray(si[None, :] < si[:, None], BF16)
    cnt = _route_count(logits)

    counts = cnt[0, :N_EXPERTS]
    padded = (counts + MOE_BLOCK - 1) // MOE_BLOCK * MOE_BLOCK
    pend = jnp.cumsum(padded)
    pstart = pend - padded
    nblk = (2 * t + N_EXPERTS * (MOE_BLOCK - 1) + MOE_BLOCK - 1) // MOE_BLOCK
    n_used = (pend[-1] // MOE_BLOCK).astype(I32).reshape(1)
    blk_row0 = jnp.arange(nblk, dtype=I32) * MOE_BLOCK
    blk_exp = jnp.minimum(jnp.sum((pend[None, :] <= blk_row0[:, None]).astype(I32), axis=1), N_EXPERTS - 1)
    gs, rt = _route_dest(logits, stri, _pad_lanes(pstart.astype(F32)))

    def tok_tiles(row, tm):
        return rt[row].reshape(t // tm, 1, tm)

    xb = _dispatch(tok_tiles(2, DISPATCH_TM), tok_tiles(3, DISPATCH_TM), h2p, nblk * MOE_BLOCK)
    ybuf = _ffn(blk_exp, n_used, xb, w1[0], w3[0], w2[0])
    out = _combine(tok_tiles(2, COMBINE_TM), tok_tiles(3, COMBINE_TM), ybuf, gs, x1, mod3, seq_len)
    return out.reshape(nb, seq_len, D_MODEL)
```

```python
import functools
import math

import numpy as np

import jax
import jax.numpy as jnp
from jax import lax
from jax.experimental import pallas as pl
from jax.experimental.pallas import tpu as pltpu

F32 = jnp.float32
BF16 = jnp.bfloat16
I32 = jnp.int32
U32 = jnp.uint32

D_MODEL = 1024
GRID_W = 64
D_SSD = 2048
SSD_HEADDIM = 64
SSD_HEADS = 32
SSD_GROUPS = 8
HEADS_PER_GROUP = 4
SSD_STATE = 128
SSD_CHUNK = 128
CONV_K = 5
D_BC = SSD_GROUPS * SSD_STATE
D_XBC = D_SSD + 2 * D_BC
NA_HEADDIM = 64
NA_HEADS = 16
D_NA = 1024
NA_KH = 8
NA_KW = 16
ROPE_THETA = 10000.0
N_GROUPS = 4
EXPERTS_PER_GROUP = 8
N_EXPERTS = 32
D_EXPERT = 512
N_MOD = 6
EPS = 1e-6
NEG_BIG = -1e30

COLS_LAT = D_XBC + D_SSD + 2 * D_MODEL + 3 * D_NA
COLS_CTX = D_XBC + 2 * D_NA
LANE = 128

V7X_VMEM_BYTES = 64 * 1024 * 1024


def _cparams(sem, vmem_mb):
    return pltpu.CompilerParams(dimension_semantics=sem, vmem_limit_bytes=vmem_mb * 1024 * 1024)


def _sigmoid(x):
    return 1.0 / (1.0 + jnp.exp(-x))


def _silu(x):
    h = 0.5 * x
    return h * jnp.tanh(h) + h


def _split2(x):
    hi = x.astype(BF16)
    lo = (x - hi.astype(F32)).astype(BF16)
    return hi, lo


def _split3(x):
    hi = x.astype(BF16)
    r = x - hi.astype(F32)
    mid = r.astype(BF16)
    lo = (r - mid.astype(F32)).astype(BF16)
    return hi, mid, lo


def _pack_bf16_pairs(x):
    bits = pltpu.bitcast(x.astype(BF16).astype(F32), U32)
    n = x.shape[1] // 2
    return (bits[:, :n] >> 16) | bits[:, n:]


def _unpack_bf16_pairs(w):
    return pltpu.bitcast(w << 16, F32), pltpu.bitcast(w & jnp.uint32(0xFFFF0000), F32)


def _dot(a, b):
    return jnp.dot(a, b, preferred_element_type=F32)


def _dot_nt(a, b):
    return lax.dot_general(a, b, (((1,), (1,)), ((), ())), preferred_element_type=F32)


def _dot_tn(a, b):
    return lax.dot_general(a, b, (((0,), (0,)), ((), ())), preferred_element_type=F32)


def _mod_kernel(c_ref, w_ref, b_ref, o_ref):
    c = c_ref[...]
    a = _silu(c).astype(BF16)
    o_ref[...] = _dot(a, w_ref[...].astype(BF16)) + b_ref[...]


def _modulation(cin, w_mod, b_mod):
    n = w_mod.shape[1]
    tn = 1536
    return pl.pallas_call(
        _mod_kernel,
        grid=(n // tn,),
        in_specs=[pl.BlockSpec((8, D_MODEL), lambda j: (0, 0)),
                  pl.BlockSpec((D_MODEL, tn), lambda j: (0, j)),
                  pl.BlockSpec((1, tn), lambda j: (0, j))],
        out_specs=pl.BlockSpec((8, tn), lambda j: (0, j)),
        out_shape=jax.ShapeDtypeStruct((8, n), F32),
        compiler_params=_cparams(("arbitrary",), 40),
        name="modulation",
    )(cin, w_mod, b_mod.reshape(1, n))


def _inproj_kernel(x_ref, sh_ref, sc_ref, nw_ref, wa_ref, wb_ref, wdt_ref, o_ref, dt_ref, h_scr, *, tm, n_a):
    j = pl.program_id(1)

    @pl.when(j == 0)
    def _():
        scale = 1.0 + sc_ref[0]
        shift = sh_ref[0]
        nw = nw_ref[...]

        def body(r, carry):
            rows = pl.ds(pl.multiple_of(r * 128, 128), 128)
            x = x_ref[rows, :]
            ms = jnp.mean(x * x, axis=-1, keepdims=True)
            h = x * lax.rsqrt(ms + EPS) * nw * scale + shift
            h_scr[rows, :] = h.astype(BF16)
            return carry

        lax.fori_loop(0, tm // 128, body, 0)
        dt_ref[...] = _dot(h_scr[...], wdt_ref[...])

    @pl.when(j < n_a)
    def _():
        o_ref[...] = _dot(h_scr[...], wa_ref[...]).astype(BF16)

    @pl.when(j >= n_a)
    def _():
        o_ref[...] = _dot(h_scr[...], wb_ref[...]).astype(BF16)


def _inproj(x2, mod3, mod_row_fn, norm_w, wa, wb, wdt, tm, tn, a_tiles, b_tiles):
    t = x2.shape[0]
    n_a, n_b = len(a_tiles), len(b_tiles)
    a_tab = list(a_tiles) + [a_tiles[-1]] * n_b
    b_tab = [b_tiles[0]] * n_a + list(b_tiles)

    def pick(tab, j):
        idx = jnp.int32(tab[-1])
        for pos in range(len(tab) - 2, -1, -1):
            idx = jnp.where(j <= pos, tab[pos], idx)
        return idx

    kern = functools.partial(_inproj_kernel, tm=tm, n_a=n_a)
    return pl.pallas_call(
        kern,
        grid=(t // tm, n_a + n_b),
        in_specs=[pl.BlockSpec((tm, D_MODEL), lambda i, j: (i, 0)),
                  pl.BlockSpec((1, 1, D_MODEL), lambda i, j: (mod_row_fn(i) * N_MOD + 0, 0, 0)),
                  pl.BlockSpec((1, 1, D_MODEL), lambda i, j: (mod_row_fn(i) * N_MOD + 1, 0, 0)),
                  pl.BlockSpec((1, D_MODEL), lambda i, j: (0, 0)),
                  pl.BlockSpec((D_MODEL, tn), lambda i, j: (0, pick(a_tab, j))),
                  pl.BlockSpec((D_MODEL, tn), lambda i, j: (0, pick(b_tab, j))),
                  pl.BlockSpec((D_MODEL, LANE), lambda i, j: (0, 0))],
        out_specs=[pl.BlockSpec((tm, tn), lambda i, j: (i, j)),
                   pl.BlockSpec((tm, LANE), lambda i, j: (i, 0))],
        out_shape=[jax.ShapeDtypeStruct((t, (n_a + n_b) * tn), BF16),
                   jax.ShapeDtypeStruct((t, LANE), F32)],
        scratch_shapes=[pltpu.VMEM((tm, D_MODEL), BF16)],
        compiler_params=_cparams(("arbitrary", "arbitrary"), 52),
        name="inproj",
    )(x2, mod3, mod3, norm_w, wa, wb, wdt)


CONV_OFFSETS = (-2, -1, 1, 2)


def _ssd_prep_kernel(prev_ref, cur_ref, next_ref, dt_ref, cw_ref, cb_ref, dtbias_ref, cos_ref, sin_ref,
                     shift_ref, etop_ref, ebot_ref, xs_ref, bc_ref, dto_ref, *, tl, tiles_per_seq):
    i = pl.program_id(0)
    first = (i % tiles_per_seq) == 0
    last = (i % tiles_per_seq) == tiles_per_seq - 1
    lane = lax.broadcasted_iota(I32, (tl, LANE), 1)
    lo_half = ((lane // 32) % 2) == 0
    shift = shift_ref[...]
    etop = etop_ref[...]
    ebot = ebot_ref[...]
    cosv = cos_ref[...]
    sinv = sin_ref[...]
    cwid = 256
    for c in range(D_XBC // cwid):
        cs = slice(c * cwid, (c + 1) * cwid)
        w = cw_ref[:, cs]
        xc = cur_ref[:, cs]
        sh = _dot(shift, xc).reshape(tl // 8, len(CONV_OFFSETS), 8, cwid)
        tap = lambda k: sh[:, k].reshape(tl, cwid)
        acc = cb_ref[:, cs] + xc.astype(F32) * w[2:3, :]
        acc = acc + tap(0) * w[0:1, :] + tap(1) * w[1:2, :] + tap(2) * w[3:4, :] + tap(3) * w[4:5, :]
        top = _dot(etop, prev_ref[:, cs])
        bot = _dot(ebot, next_ref[:, cs])
        top_c = jnp.where(first, 0.0, top[0:8] * w[0:1, :] + top[8:16] * w[1:2, :])
        bot_c = jnp.where(last, 0.0, bot[0:8] * w[3:4, :] + bot[8:16] * w[4:5, :])
        acc = jnp.concatenate([acc[0:8] + top_c, acc[8:tl - 8], acc[tl - 8:tl] + bot_c], axis=0)
        y = _silu(acc)
        if c * cwid < D_SSD:
            xs_ref[:, cs] = y.astype(BF16)
        else:
            for g in range(cwid // LANE):
                yg = y[:, g * LANE:(g + 1) * LANE]
                partner = jnp.where(lo_half, pltpu.roll(yg, 96, 1), pltpu.roll(yg, 32, 1))
                o = yg * cosv + partner * sinv
                c0 = c * cwid - D_SSD + g * LANE
                bc_ref[:, c0:c0 + LANE] = o.astype(BF16)

    v = dt_ref[...] + dtbias_ref[...]
    dto_ref[...] = jnp.maximum(v, 0.0) + jnp.log(1.0 + jnp.exp(-jnp.abs(v)))


def _conv_shift_tables(tl):
    t = np.arange(tl)[:, None]
    u = np.arange(tl)[None, :]
    shift = np.stack([(u == t + off) for off in CONV_OFFSETS], axis=0).reshape(len(CONV_OFFSETS), tl // 8, 8, tl)
    shift = shift.transpose(1, 0, 2, 3).reshape(len(CONV_OFFSETS) * tl, tl)
    r = np.arange(16)[:, None]
    q = np.arange(16)[None, :]
    etop = ((r < 8) & (q == r + 14)) | ((r >= 8) & (q == r - 8 + 15))
    ebot = ((r < 8) & (q == r - 7)) | ((r >= 8) & (q == r - 8 - 6))
    return jnp.asarray(shift, BF16), jnp.asarray(etop, BF16), jnp.asarray(ebot, BF16)


def _ssd_prep(big, dt_raw, conv_w, conv_b, dtbias, cos_t, sin_t, seq_len, tl):
    t = big.shape[0]
    tps = seq_len // tl
    hb = tl // 16
    nhalo = t // 16
    kern = functools.partial(_ssd_prep_kernel, tl=tl, tiles_per_seq=tps)
    shift, etop, ebot = _conv_shift_tables(tl)
    const = lambda i: (0, 0)
    return pl.pallas_call(
        kern,
        grid=(t // tl,),
        in_specs=[pl.BlockSpec((16, D_XBC), lambda i: (jnp.maximum(i * hb - 1, 0), 0)),
                  pl.BlockSpec((tl, D_XBC), lambda i: (i, 0)),
                  pl.BlockSpec((16, D_XBC), lambda i: (jnp.minimum((i + 1) * hb, nhalo - 1), 0)),
                  pl.BlockSpec((tl, LANE), lambda i: (i, 0)),
                  pl.BlockSpec((CONV_K, D_XBC), const),
                  pl.BlockSpec((1, D_XBC), const),
                  pl.BlockSpec((1, LANE), const),
                  pl.BlockSpec((tl, LANE), lambda i: (i % tps, 0)),
                  pl.BlockSpec((tl, LANE), lambda i: (i % tps, 0)),
                  pl.BlockSpec((len(CONV_OFFSETS) * tl, tl), const),
                  pl.BlockSpec((16, 16), const),
                  pl.BlockSpec((16, 16), const)],
        out_specs=[pl.BlockSpec((tl, D_SSD), lambda i: (i, 0)),
                   pl.BlockSpec((tl, 2 * D_BC), lambda i: (i, 0)),
                   pl.BlockSpec((tl, LANE), lambda i: (i, 0))],
        out_shape=[jax.ShapeDtypeStruct((t, D_SSD), BF16),
                   jax.ShapeDtypeStruct((t, 2 * D_BC), BF16),
                   jax.ShapeDtypeStruct((t, LANE), F32)],
        compiler_params=_cparams(("arbitrary",), 40),
        name="ssd_prep",
    )(big, big, big, dt_raw, conv_w, conv_b, dtbias, cos_t, sin_t, shift, etop, ebot)


GROUP_W = HEADS_PER_GROUP * SSD_HEADDIM


def _scan_decays(dt_ref, alog_ref, tri_ref, ee_ref, rev):
    edge = 0 if rev else SSD_CHUNK - 1
    a_row = -jnp.exp(alog_ref[...])
    dt = dt_ref[...]
    trib = tri_ref[...]
    d_hi, d_mid, d_lo = _split3(dt * a_row)
    cum = _dot(trib, d_hi) + _dot(trib, d_mid) + _dot(trib, d_lo)
    tot = cum[edge:edge + 1, :]
    ee = ee_ref[...]
    w_heads = (jnp.exp(tot - cum) * dt).astype(BF16)
    t_hi, t_lo = _split2(jnp.broadcast_to(jnp.exp(tot), (8, LANE)))
    etot = (_dot(t_hi, ee) + _dot(t_lo, ee))[0:1, :]
    return dt, trib, cum, w_heads, etot


def _spread(per_head, ee_ref, g):
    return _dot(per_head, ee_ref[:, g * GROUP_W:(g + 1) * GROUP_W])


def _ssd_state_kernel(xs_ref, bc_ref, dt_ref, alog_ref, tri_ref, ee_ref, hfin_ref, h_scr, *, rev, nc):
    k = pl.program_id(1)

    @pl.when(k == 0)
    def _():
        h_scr[...] = jnp.zeros_like(h_scr)

    _, _, _, w_heads, etot = _scan_decays(dt_ref, alog_ref, tri_ref, ee_ref, rev)
    for g in range(SSD_GROUPS):
        gs = slice(g * GROUP_W, (g + 1) * GROUP_W)
        xw = (xs_ref[:, gs].astype(F32) * _spread(w_heads, ee_ref, g)).astype(BF16)
        h_scr[:, gs] = etot[:, gs] * h_scr[:, gs] + _dot_tn(bc_ref[:, g * SSD_STATE:(g + 1) * SSD_STATE], xw)

    @pl.when(k == nc - 1)
    def _():
        hfin_ref[0] = h_scr[...]


def _scan_chunk_stages(xs_ref, bc_ref, dt_ref, alog_ref, dskip_ref, tri_ref, ee_ref, y_ref, h_scr, rev):
    L = SSD_CHUNK
    col = SSD_HEADS if rev else 0
    dt, trib, cum, w_heads, etot = _scan_decays(dt_ref, alog_ref, tri_ref, ee_ref, rev)
    tri = trib > 0.5
    cum_t = cum.T
    dt_t = dt.T
    o_heads = jnp.exp(cum).astype(BF16)

    gw = GROUP_W
    lane_head = lax.broadcasted_iota(I32, (L, gw), 1) // SSD_HEADDIM

    def b_of(g):
        return bc_ref[:, g * SSD_STATE:(g + 1) * SSD_STATE]

    def c_of(g):
        return bc_ref[:, D_BC + g * SSD_STATE:D_BC + (g + 1) * SSD_STATE]

    def operands(g, cb):
        gs = slice(g * gw, (g + 1) * gw)
        xs_g = xs_ref[:, gs]
        ms = []
        for r in range(HEADS_PER_GROUP):
            hh = col + g * HEADS_PER_GROUP + r
            seg = cum[:, hh:hh + 1] - cum_t[hh:hh + 1, :]
            lm = jnp.exp(jnp.where(tri, seg, NEG_BIG)) * dt_t[hh:hh + 1, :]
            ms.append((cb * lm).astype(BF16))
        mcat = jnp.concatenate(ms, axis=1)
        zero = jnp.zeros_like(xs_g)
        bd = jnp.concatenate([jnp.where(lane_head == r, xs_g, zero) for r in range(HEADS_PER_GROUP)], axis=0)
        xs_f = xs_g.astype(F32)
        xw = (xs_f * _spread(w_heads, ee_ref, g)).astype(BF16)
        return mcat, bd, xs_f, xw, _spread(o_heads, ee_ref, g)

    def cb_of(g):
        return _dot_nt(c_of(g), b_of(g))

    yield
    cb_next = cb_of(0)
    yield
    nxt = operands(0, cb_next)
    cb_next = cb_of(1)
    for g in range(SSD_GROUPS):
        yield
        gs = slice(g * gw, (g + 1) * gw)
        mcat, bd, xs_f, xw, oscale_g = nxt
        h_g = h_scr[:, gs]
        y_diag = _dot(mcat, bd)
        y_off = _dot(c_of(g), h_g.astype(BF16))
        loc = _dot_tn(b_of(g), xw)
        if g + 1 < SSD_GROUPS:
            nxt = operands(g + 1, cb_next)
            if g + 2 < SSD_GROUPS:
                cb_next = cb_of(g + 2)
        y = y_diag + oscale_g * y_off
        if not rev:
            y = y + xs_f * dskip_ref[:, gs]
        y_ref[:, gs] = y.astype(BF16)
        h_scr[:, gs] = etot[:, gs] * h_g + loc


def _ssd_scan_kernel(xsf_ref, bcf_ref, dtf_ref, xsb_ref, bcb_ref, dtb_ref, alog_ref, dskip_ref,
                     trif_ref, eef_ref, trib_ref, eeb_ref, h0f_ref, h0b_ref, yf_ref, yb_ref, hf_scr, hb_scr):
    @pl.when(pl.program_id(1) == 0)
    def _():
        hf_scr[...] = h0f_ref[0]
        hb_scr[...] = h0b_ref[0]

    chains = [_scan_chunk_stages(xsf_ref, bcf_ref, dtf_ref, alog_ref, dskip_ref, trif_ref, eef_ref,
                                 yf_ref, hf_scr, False),
              _scan_chunk_stages(xsb_ref, bcb_ref, dtb_ref, alog_ref, dskip_ref, trib_ref, eeb_ref,
                                 yb_ref, hb_scr, True)]
    while chains:
        for ch in list(chains):
            try:
                next(ch)
            except StopIteration:
                chains.remove(ch)


def _scan_rowmap(nc, rev):
    def rowmap(b, k):
        c = (nc - 1 - k) if rev else k
        return (b * nc + c, 0)
    return rowmap


def _ssd_state(xs, bc, dt, alog_row, tri, ee, nb, rev):
    t = xs.shape[0]
    nc = t // nb // SSD_CHUNK
    rowmap = _scan_rowmap(nc, rev)
    const2 = lambda b, k: (0, 0)
    return pl.pallas_call(
        functools.partial(_ssd_state_kernel, rev=rev, nc=nc),
        grid=(nb, nc),
        in_specs=[pl.BlockSpec((SSD_CHUNK, D_SSD), rowmap),
                  pl.BlockSpec((SSD_CHUNK, D_BC), rowmap),
                  pl.BlockSpec((SSD_CHUNK, LANE), rowmap),
                  pl.BlockSpec((1, LANE), const2),
                  pl.BlockSpec((SSD_CHUNK, SSD_CHUNK), const2),
                  pl.BlockSpec((LANE, D_SSD), const2)],
        out_specs=pl.BlockSpec((1, SSD_STATE, D_SSD), lambda b, k: (b, 0, 0)),
        out_shape=jax.ShapeDtypeStruct((nb, SSD_STATE, D_SSD), F32),
        scratch_shapes=[pltpu.VMEM((SSD_STATE, D_SSD), F32)],
        compiler_params=_cparams(("arbitrary", "arbitrary"), 40),
        name="ssd_state_bwd" if rev else "ssd_state_fwd",
    )(xs, bc, dt, alog_row, tri, ee)


def _ssd_scan(xs, bc, dt, alog_row, dskip_row, tables_f, tables_b, h0f, h0b, nb):
    t = xs.shape[0]
    nc = t // nb // SSD_CHUNK
    fmap = _scan_rowmap(nc, False)
    bmap = _scan_rowmap(nc, True)
    const2 = lambda b, k: (0, 0)
    chunk = lambda width, m: pl.BlockSpec((SSD_CHUNK, width), m)
    tri_spec = pl.BlockSpec((SSD_CHUNK, SSD_CHUNK), const2)
    ee_spec = pl.BlockSpec((LANE, D_SSD), const2)
    state = pl.BlockSpec((1, SSD_STATE, D_SSD), lambda b, k: (b, 0, 0))
    return pl.pallas_call(
        _ssd_scan_kernel,
        grid=(nb, nc),
        in_specs=[chunk(D_SSD, fmap), chunk(2 * D_BC, fmap), chunk(LANE, fmap),
                  chunk(D_SSD, bmap), chunk(2 * D_BC, bmap), chunk(LANE, bmap),
                  pl.BlockSpec((1, LANE), const2),
                  pl.BlockSpec((1, D_SSD), const2),
                  tri_spec, ee_spec, tri_spec, ee_spec, state, state],
        out_specs=[chunk(D_SSD, fmap), chunk(D_SSD, bmap)],
        out_shape=[jax.ShapeDtypeStruct((t, D_SSD), BF16)] * 2,
        scratch_shapes=[pltpu.VMEM((SSD_STATE, D_SSD), F32)] * 2,
        compiler_params=_cparams(("arbitrary", "arbitrary"), 48),
        name="ssd_scan",
    )(xs, bc, dt, xs, bc, dt, alog_row, dskip_row, *tables_f, *tables_b, h0f, h0b)


def _headnorm_cols(src_ref, w_ref, g, dst_ref, scale, transposed=False):
    for c in range(D_NA // LANE):
        cs = slice(c * LANE, (c + 1) * LANE)
        x = src_ref[:, cs].astype(F32)
        ms = _dot((x * x).astype(BF16), g)
        y = x * lax.rsqrt(ms + EPS) * w_ref[:, cs]
        if scale is not None:
            y = y * scale
        if transposed:
            dst_ref[cs, :] = y.T.astype(BF16)
        else:
            dst_ref[:, cs] = y.astype(BF16)


def _na_prep_qk_kernel(q_ref, k_ref, qw_ref, kw_ref, g_ref, qo_ref, ko_ref):
    g = g_ref[...]
    _headnorm_cols(q_ref, qw_ref, g, qo_ref, NA_HEADDIM ** -0.5)
    _headnorm_cols(k_ref, kw_ref, g, ko_ref, None)


def _na_prep_k_kernel(k_ref, kw_ref, g_ref, ko_ref):
    _headnorm_cols(k_ref, kw_ref, g_ref[...], ko_ref, None, transposed=True)


def _na_prep(big, qcol, kcol, qw, kw, gmat, tm):
    t = big.shape[0]
    blk = lambda cidx: pl.BlockSpec((tm, D_NA), lambda i: (i, cidx))
    row = pl.BlockSpec((1, D_NA), lambda i: (0, 0))
    gspec = pl.BlockSpec((LANE, LANE), lambda i: (0, 0))
    out = pl.BlockSpec((tm, D_NA), lambda i: (i, 0))
    if qcol is None:
        return pl.pallas_call(
            _na_prep_k_kernel, grid=(t // tm,),
            in_specs=[blk(kcol), row, gspec], out_specs=pl.BlockSpec((D_NA, tm), lambda i: (i, 0)),
            out_shape=jax.ShapeDtypeStruct((t // tm * D_NA, tm), BF16),
            compiler_params=_cparams(("arbitrary",), 32), name="na_prep_ctx",
        )(big, kw, gmat)
    return pl.pallas_call(
        _na_prep_qk_kernel, grid=(t // tm,),
        in_specs=[blk(qcol), blk(kcol), row, row, gspec], out_specs=[out, out],
        out_shape=[jax.ShapeDtypeStruct((t, D_NA), BF16)] * 2,
        compiler_params=_cparams(("arbitrary",), 32), name="na_prep",
    )(big, big, qw, kw, gmat)


NA_LOOKAHEAD = 2


NA_ROWS_PER_STEP = 2


def _na_row_stages(r, qrow, q_ref, k_ref, v_ref, kc_ref, vc_ref, bias_ref, o_ref, rows):
    rs = jnp.clip(r - NA_KH // 2, 0, rows - NA_KH)
    start = pl.multiple_of(rs * GRID_W, GRID_W)
    nk = NA_KH * GRID_W
    lane = lax.broadcasted_iota(I32, (GRID_W, LANE), 1)
    first_head = lane < NA_HEADDIM
    npair = NA_HEADS // 2
    qrows = slice(qrow, qrow + GRID_W)

    def scores(j):
        cs = slice(j * LANE, (j + 1) * LANE)
        qp = q_ref[qrows, cs]
        zero = jnp.zeros_like(qp)
        qs = jnp.concatenate([jnp.where(first_head, qp, zero), jnp.where(first_head, zero, qp)], axis=0)
        kb = k_ref[pl.ds(start, nk), cs]
        s_loc = _dot_nt(qs, kb) + bias_ref[0, j * LANE:(j + 1) * LANE, :]
        s_ctx = _dot(qs, kc_ref[cs, :])
        return s_loc, s_ctx

    pending = [scores(j) for j in range(NA_LOOKAHEAD)]
    for j in range(npair):
        yield
        cs = slice(j * LANE, (j + 1) * LANE)
        s_loc, s_ctx = pending.pop(0)
        if j + NA_LOOKAHEAD < npair:
            pending.append(scores(j + NA_LOOKAHEAD))
        vb = v_ref[pl.ds(start, nk), cs]
        m = jnp.maximum(jnp.max(s_loc, axis=-1, keepdims=True), jnp.max(s_ctx, axis=-1, keepdims=True))
        p_loc = jnp.exp(s_loc - m)
        p_ctx = jnp.exp(s_ctx - m)
        den = jnp.sum(p_loc, axis=-1, keepdims=True) + jnp.sum(p_ctx, axis=-1, keepdims=True)
        o = _dot(p_loc.astype(BF16), vb) + _dot(p_ctx.astype(BF16), vc_ref[:, cs])
        o = o / den
        o_ref[qrows, cs] = jnp.where(first_head, o[:GRID_W], o[GRID_W:]).astype(BF16)


def _na_kernel(q_ref, k_ref, v_ref, kc_ref, vc_ref, *rest, rows):
    bias_refs, o_ref = rest[:NA_ROWS_PER_STEP], rest[NA_ROWS_PER_STEP]
    r0 = pl.program_id(1) * NA_ROWS_PER_STEP
    chains = [_na_row_stages(r0 + u, u * GRID_W, q_ref, k_ref, v_ref, kc_ref, vc_ref, bias_refs[u], o_ref, rows)
              for u in range(NA_ROWS_PER_STEP)]
    while chains:
        for ch in list(chains):
            try:
                next(ch)
            except StopIteration:
                chains.remove(ch)


def _neigh_attention(qn, kn, big, kcn, bigc, bias_tab, nb, seq_len, ctx_len, vcol, vccol):
    t = qn.shape[0]
    rows = seq_len // GRID_W
    rps = NA_ROWS_PER_STEP
    steps = rows // rps
    kern = functools.partial(_na_kernel, rows=rows)

    def pat(u):
        def index_map(b, s):
            r = s * rps + u
            return (r - jnp.clip(r - NA_KH // 2, 0, rows - NA_KH), 0, 0)
        return index_map

    bias_specs = [pl.BlockSpec((1, NA_HEADS * GRID_W, NA_KH * GRID_W), pat(u)) for u in range(rps)]
    return pl.pallas_call(
        kern,
        grid=(nb, steps),
        in_specs=[pl.BlockSpec((rps * GRID_W, D_NA), lambda b, s: (b * steps + s, 0)),
                  pl.BlockSpec((seq_len, D_NA), lambda b, s: (b, 0), pipeline_mode=pl.Buffered(1)),
                  pl.BlockSpec((seq_len, D_NA), lambda b, s: (b, vcol), pipeline_mode=pl.Buffered(1)),
                  pl.BlockSpec((D_NA, ctx_len), lambda b, s: (b, 0), pipeline_mode=pl.Buffered(1)),
                  pl.BlockSpec((ctx_len, D_NA), lambda b, s: (b, vccol), pipeline_mode=pl.Buffered(1))] + bias_specs,
        out_specs=pl.BlockSpec((rps * GRID_W, D_NA), lambda b, s: (b * steps + s, 0)),
        out_shape=jax.ShapeDtypeStruct((t, D_NA), BF16),
        compiler_params=_cparams(("arbitrary", "arbitrary"), 48),
        name="neigh_attention",
    )(qn, kn, big, kcn, bigc, *([bias_tab] * rps))


def _na_bias_kernel(rpb_ref, o_ref):
    rp = rpb_ref[0]
    r64 = pltpu.roll(rp, GRID_W, 1)
    c = lax.broadcasted_iota(I32, (GRID_W, LANE), 0)
    kc = lax.broadcasted_iota(I32, (GRID_W, LANE), 1) % GRID_W
    cs = jnp.clip(c - NA_KW // 2, 0, GRID_W - NA_KW)
    valid = (kc >= cs) & (kc < cs + NA_KW)
    pair = []
    for d in range(2 * NA_KH - 2):
        vec = rp[d:d + 1, :] + r64[d + 1:d + 2, :]
        w = pltpu.roll(jnp.broadcast_to(vec, (GRID_W, LANE)), LANE - (NA_KW - 1), 1, stride=1, stride_axis=0)
        pair.append(jnp.where(valid, w, NEG_BIG))
    for p in range(NA_KH):
        for ii in range(NA_KH // 2):
            o_ref[p, :, ii * LANE:(ii + 1) * LANE] = pair[2 * ii - p + NA_KH - 1]


def _na_bias_table(rpb):
    rp = jnp.pad(rpb, ((0, 0), (0, 1), (0, LANE - (2 * NA_KW - 1))))
    return pl.pallas_call(
        _na_bias_kernel,
        grid=(NA_HEADS,),
        in_specs=[pl.BlockSpec((1, 2 * NA_KH, LANE), lambda h: (h, 0, 0))],
        out_specs=pl.BlockSpec((NA_KH, GRID_W, NA_KH * GRID_W), lambda h: (0, h, 0)),
        out_shape=jax.ShapeDtypeStruct((NA_KH, NA_HEADS * GRID_W, NA_KH * GRID_W), F32),
        compiler_params=_cparams(("arbitrary",), 32),
        name="na_bias",
    )(rp)


MERGE_SUB = 128


def _merge_kernel(yf_ref, yb_ref, z_ref, gt_ref, yna_ref, x_ref, g1_ref, sh2_ref, sc2_ref, snw_ref, n2w_ref,
                  wbs_ref, wbn_ref, wo_ref, wrh_ref, wrl_ref, br_ref, x1_ref, h2p_ref, lg_ref, xbz_ref):
    xbz_ref[...] = jnp.zeros_like(xbz_ref)
    tm = x_ref.shape[0]
    subs = [slice(r * MERGE_SUB, (r + 1) * MERGE_SUB) for r in range(tm // MERGE_SUB)]
    half = D_MODEL // 2
    yn = []
    for rs in subs:
        z = z_ref[rs, :].astype(F32)
        y = (yf_ref[rs, :] + yb_ref[rs, :]).astype(F32) * _silu(z)
        ms = jnp.mean(y * y, axis=-1, keepdims=True)
        yn.append((y * lax.rsqrt(ms + EPS) * snw_ref[...]).astype(BF16))
    ab = [(_dot(yn[r], wbs_ref[...]), _dot(yna_ref[rs, :], wbn_ref[...])) for r, rs in enumerate(subs)]
    merged = []
    for r, rs in enumerate(subs):
        g_ssd = gt_ref[rs, :D_MODEL].astype(F32)
        g_na = gt_ref[rs, D_MODEL:].astype(F32)
        merged.append((_sigmoid(g_ssd) * ab[r][0] + _sigmoid(g_na) * ab[r][1]).astype(BF16))
    mo = [_dot(m, wo_ref[...]) for m in merged]
    hs = []
    for r, rs in enumerate(subs):
        x1 = x_ref[rs, :] + g1_ref[0] * mo[r]
        x1_ref[rs, :] = x1
        ms2 = jnp.mean(x1 * x1, axis=-1, keepdims=True)
        h2 = x1 * lax.rsqrt(ms2 + EPS) * n2w_ref[...] * (1.0 + sc2_ref[0]) + sh2_ref[0]
        h_hi, h_lo = _split2(h2)
        hs.append((h_hi, h_lo))
        h2p_ref[rs, :] = _pack_bf16_pairs(h2)
    for r, rs in enumerate(subs):
        h_hi, h_lo = hs[r]
        lg_ref[rs, :] = (_dot(h_hi, wrh_ref[...]) + _dot(h_lo, wrh_ref[...]) + _dot(h_hi, wrl_ref[...])) + br_ref[...]


def _merge(yf, yb, big, yna, x2, mod3, snw, n2w, wbs, wbn, wo, wrh, wrl, br, seq_len, tm, zcol, gcol, n_rows):
    t = x2.shape[0]
    tiles_per_seq = seq_len // tm
    zrows = n_rows // (t // tm)
    assert zrows * (t // tm) == n_rows and zrows % 8 == 0
    modspec = lambda kidx: pl.BlockSpec((1, 1, D_MODEL), lambda i: ((i // tiles_per_seq) * N_MOD + kidx, 0, 0))
    full = lambda shp: pl.BlockSpec(shp, lambda i: (0,) * len(shp), pipeline_mode=pl.Buffered(1))
    return pl.pallas_call(
        _merge_kernel,
        grid=(t // tm,),
        in_specs=[pl.BlockSpec((tm, D_SSD), lambda i: (i, 0)),
                  pl.BlockSpec((tm, D_SSD), lambda i: (i, 0)),
                  pl.BlockSpec((tm, D_SSD), lambda i: (i, zcol)),
                  pl.BlockSpec((tm, 2 * D_MODEL), lambda i: (i, gcol)),
                  pl.BlockSpec((tm, D_NA), lambda i: (i, 0)),
                  pl.BlockSpec((tm, D_MODEL), lambda i: (i, 0)),
                  modspec(2), modspec(3), modspec(4),
                  full((1, D_SSD)), full((1, D_MODEL)),
                  full((D_SSD, D_MODEL)), full((D_NA, D_MODEL)), full((D_MODEL, D_MODEL)),
                  full((D_MODEL, LANE)), full((D_MODEL, LANE)), full((1, LANE))],
        out_specs=[pl.BlockSpec((tm, D_MODEL), lambda i: (i, 0)),
                   pl.BlockSpec((tm, D_MODEL // 2), lambda i: (i, 0)),
                   pl.BlockSpec((tm, LANE), lambda i: (i, 0)),
                   pl.BlockSpec((zrows, D_MODEL // 2), lambda i: (i, 0))],
        out_shape=[jax.ShapeDtypeStruct((t, D_MODEL), F32),
                   jax.ShapeDtypeStruct((t, D_MODEL // 2), U32),
                   jax.ShapeDtypeStruct((t, LANE), F32),
                   jax.ShapeDtypeStruct((n_rows, D_MODEL // 2), U32)],
        compiler_params=_cparams(("arbitrary",), 60),
        name="merge",
    )(yf, yb, big, big, yna, x2, mod3, mod3, mod3, snw, n2w, wbs, wbn, wo, wrh, wrl, br)


ROUTE_TM = 512
GRP_LANE0 = N_EXPERTS


def _route_topk(lg):
    tm = lg.shape[0]
    lane = lax.broadcasted_iota(I32, (tm, LANE), 1)
    neg_inf = jnp.float32(-jnp.inf)
    big_lane = jnp.int32(4 * LANE)
    is_grp = (lane >= GRP_LANE0) & (lane < GRP_LANE0 + N_GROUPS)
    gl = jnp.where(is_grp, lg, neg_inf)
    gmax = jnp.max(gl, axis=-1, keepdims=True)
    grp = jnp.min(jnp.where(gl == gmax, lane, big_lane), axis=-1, keepdims=True) - GRP_LANE0
    psum = jnp.sum(jnp.where(is_grp, jnp.exp(lg - gmax), 0.0), axis=-1, keepdims=True)
    p_grp = 1.0 / psum
    in_g = (lane < N_EXPERTS) & ((lane // EXPERTS_PER_GROUP) == grp)
    el = jnp.where(in_g, lg, neg_inf)
    v1 = jnp.max(el, axis=-1, keepdims=True)
    i1 = jnp.min(jnp.where(el == v1, lane, big_lane), axis=-1, keepdims=True)
    el2 = jnp.where(lane == i1, neg_inf, el)
    v2 = jnp.max(el2, axis=-1, keepdims=True)
    i2 = jnp.min(jnp.where(el2 == v2, lane, big_lane), axis=-1, keepdims=True)
    tt = jnp.exp(v2 - v1)
    den = 1.0 + tt
    ga = p_grp / den
    gb = p_grp * tt / den

    sel1 = lane == i1
    sel2 = lane == i2
    return lane, sel1, sel2, ga, gb


def _route_count_kernel(lg_ref, cnt_ref, run_scr):
    @pl.when(pl.program_id(0) == 0)
    def _():
        run_scr[...] = jnp.zeros_like(run_scr)

    _, sel1, sel2, _, _ = _route_topk(lg_ref[...])
    onehot = jnp.where(sel1 | sel2, 1.0, 0.0)
    run_scr[...] = run_scr[...] + jnp.sum(onehot, axis=0, keepdims=True)
    cnt_ref[...] = run_scr[...].astype(I32)


def _route_dest_kernel(lg_ref, stri_ref, pstart_ref, gs_ref, rt_ref, run_scr):
    @pl.when(pl.program_id(0) == 0)
    def _():
        run_scr[...] = pstart_ref[...]

    lane, sel1, sel2, ga, gb = _route_topk(lg_ref[...])
    onehot = jnp.where(sel1 | sel2, 1.0, 0.0)
    pos = _dot(stri_ref[...], onehot.astype(BF16)) + run_scr[...]
    d1 = jnp.sum(jnp.where(sel1, pos, 0.0), axis=-1, keepdims=True)
    d2 = jnp.sum(jnp.where(sel2, pos, 0.0), axis=-1, keepdims=True)
    run_scr[...] = run_scr[...] + jnp.sum(onehot, axis=0, keepdims=True)

    slab = jnp.where(lane == 0, ga, 0.0)
    slab = jnp.where(lane == 1, gb, slab)
    slab = jnp.where(lane == 2, d1, slab)
    slab = jnp.where(lane == 3, d2, slab)
    gs_ref[...] = slab
    for q in range(ROUTE_TM // LANE):
        blk_t = slab[q * LANE:(q + 1) * LANE, :].T
        rt_ref[:, q * LANE:(q + 1) * LANE] = blk_t[0:8, :].astype(I32)


def _route_count(logits):
    t = logits.shape[0]
    tm = ROUTE_TM
    return pl.pallas_call(
        _route_count_kernel,
        grid=(t // tm,),
        in_specs=[pl.BlockSpec((tm, LANE), lambda i: (i, 0))],
        out_specs=pl.BlockSpec((1, LANE), lambda i: (0, 0)),
        out_shape=jax.ShapeDtypeStruct((1, LANE), I32),
        scratch_shapes=[pltpu.VMEM((1, LANE), F32)],
        compiler_params=_cparams(("arbitrary",), 32),
        name="route_count",
    )(logits)


def _route_dest(logits, stri, pstart_row):
    t = logits.shape[0]
    tm = ROUTE_TM
    return pl.pallas_call(
        _route_dest_kernel,
        grid=(t // tm,),
        in_specs=[pl.BlockSpec((tm, LANE), lambda i: (i, 0)),
                  pl.BlockSpec((tm, tm), lambda i: (0, 0)),
                  pl.BlockSpec((1, LANE), lambda i: (0, 0))],
        out_specs=[pl.BlockSpec((tm, LANE), lambda i: (i, 0)),
                   pl.BlockSpec((8, tm), lambda i: (0, i))],
        out_shape=[jax.ShapeDtypeStruct((t, LANE), F32),
                   jax.ShapeDtypeStruct((8, t), I32)],
        scratch_shapes=[pltpu.VMEM((1, LANE), F32)],
        compiler_params=_cparams(("arbitrary",), 32),
        name="route_dest",
    )(logits, stri, pstart_row)


MOE_BLOCK = 512
DISPATCH_TM = 512
COMBINE_TM = 256


ROW_UNROLL = 8


DISPATCH_BUFS = 3


def _dispatch_kernel(d1_ref, d2_ref, h_hbm, xb_in_hbm, xb_hbm, hbuf, load_sem, row_sem):
    del xb_in_hbm
    tm = DISPATCH_TM
    i = pl.program_id(0)
    last = pl.num_programs(0) - 1
    slot = i % DISPATCH_BUFS
    par = i % 2

    def load(tile, s):
        return pltpu.make_async_copy(h_hbm.at[pl.ds(tile * tm, tm)], hbuf.at[s], load_sem.at[s])

    def wait_rows(p):
        for _ in range(2):
            pltpu.make_async_copy(hbuf.at[0], xb_hbm.at[pl.ds(0, tm)], row_sem.at[p]).wait()

    @pl.when(i == 0)
    def _():
        load(0, 0).start()

    @pl.when(i < last)
    def _():
        load(i + 1, (i + 1) % DISPATCH_BUFS).start()

    load(i, slot).wait()

    def body(g, carry):
        for u in range(ROW_UNROLL):
            tt = g * ROW_UNROLL + u
            src = hbuf.at[slot, pl.ds(tt, 1)]
            pltpu.make_async_copy(src, xb_hbm.at[pl.ds(d1_ref[0, 0, tt], 1)], row_sem.at[par]).start()
            pltpu.make_async_copy(src, xb_hbm.at[pl.ds(d2_ref[0, 0, tt], 1)], row_sem.at[par]).start(priority=1)
        return carry

    lax.fori_loop(0, tm // ROW_UNROLL, body, 0)

    @pl.when(i > 0)
    def _():
        wait_rows(1 - par)

    @pl.when(i == last)
    def _():
        wait_rows(par)


def _dispatch(d1, d2, h2p, xb0):
    t = h2p.shape[0]
    tm = DISPATCH_TM
    n_rows, w = xb0.shape
    smem = pl.BlockSpec((1, 1, tm), lambda i: (i, 0, 0), memory_space=pltpu.SMEM)
    anyspec = pl.BlockSpec(memory_space=pl.ANY)
    return pl.pallas_call(
        _dispatch_kernel,
        grid=(t // tm,),
        in_specs=[smem, smem, anyspec, anyspec],
        out_specs=anyspec,
        out_shape=jax.ShapeDtypeStruct((n_rows, w), U32),
        scratch_shapes=[pltpu.VMEM((DISPATCH_BUFS, tm, w), U32),
                        pltpu.SemaphoreType.DMA((DISPATCH_BUFS,)), pltpu.SemaphoreType.DMA((2,))],
        input_output_aliases={3: 0},
        compiler_params=pltpu.CompilerParams(dimension_semantics=("arbitrary",), has_side_effects=True),
        name="moe_dispatch",
    )(d1, d2, h2p, xb0)


def _ffn_kernel(be_ref, nu_ref, xb_ref, w1_ref, w3_ref, w2_ref, yb_ref, w13_scr, w2_scr):
    i = pl.program_id(0)
    prev = be_ref[jnp.maximum(i - 1, 0)]
    changed = (i == 0) | (be_ref[i] != prev)
    used = i < nu_ref[0]

    @pl.when(changed & used)
    def _():
        w13_scr[:, :D_EXPERT] = w1_ref[0].astype(BF16)
        w13_scr[:, D_EXPERT:] = w3_ref[0].astype(BF16)
        w2_scr[...] = w2_ref[0].astype(BF16)

    @pl.when(used)
    def _():
        x = jnp.concatenate(_unpack_bf16_pairs(xb_ref[...]), axis=1).astype(BF16)
        h = _dot(x, w13_scr[...])
        h1 = h[:, :D_EXPERT]
        h3 = h[:, D_EXPERT:]
        a = (_silu(h1) * h3).astype(BF16)
        yb_ref[...] = _pack_bf16_pairs(_dot(a, w2_scr[...]))

    @pl.when(jnp.logical_not(used))
    def _():
        yb_ref[...] = jnp.zeros_like(yb_ref)


def _ffn(blk_exp, n_used, xb, w1, w3, w2):
    n_rows = xb.shape[0]
    nblk = n_rows // MOE_BLOCK
    wmap = lambda i, be, nu: (be[i], 0, 0)
    grid_spec = pltpu.PrefetchScalarGridSpec(
        num_scalar_prefetch=2, grid=(nblk,),
        in_specs=[pl.BlockSpec((MOE_BLOCK, D_MODEL // 2), lambda i, be, nu: (i, 0)),
                  pl.BlockSpec((1, D_MODEL, D_EXPERT), wmap),
                  pl.BlockSpec((1, D_MODEL, D_EXPERT), wmap),
                  pl.BlockSpec((1, D_EXPERT, D_MODEL), wmap)],
        out_specs=pl.BlockSpec((MOE_BLOCK, D_MODEL // 2), lambda i, be, nu: (i, 0)),
        scratch_shapes=[pltpu.VMEM((D_MODEL, 2 * D_EXPERT), BF16),
                        pltpu.VMEM((D_EXPERT, D_MODEL), BF16)])
    return pl.pallas_call(
        _ffn_kernel, grid_spec=grid_spec,
        out_shape=jax.ShapeDtypeStruct((n_rows, D_MODEL // 2), U32),
        compiler_params=_cparams(("arbitrary",), 40),
        name="moe_ffn",
    )(blk_exp, n_used, xb, w1, w3, w2)


def _combine_kernel(d1_ref, d2_ref, d1n_ref, d2n_ref, yb_hbm, gs_ref, x1_ref, g2_ref, o_ref,
                    ya_scr, yb_scr, sem):
    tm = COMBINE_TM
    i = pl.program_id(0)
    slot = i % 2

    def issue(da_ref, db_ref, s):
        def body(g, carry):
            for u in range(ROW_UNROLL):
                tt = g * ROW_UNROLL + u
                pltpu.make_async_copy(yb_hbm.at[pl.ds(da_ref[0, 0, tt], 1)],
                                      ya_scr.at[s, pl.ds(tt, 1)], sem.at[s]).start()
                pltpu.make_async_copy(yb_hbm.at[pl.ds(db_ref[0, 0, tt], 1)],
                                      yb_scr.at[s, pl.ds(tt, 1)], sem.at[s]).start(priority=1)
            return carry

        lax.fori_loop(0, tm // ROW_UNROLL, body, 0)

    @pl.when(i == 0)
    def _():
        issue(d1_ref, d2_ref, 0)

    @pl.when(i + 1 < pl.num_programs(0))
    def _():
        issue(d1n_ref, d2n_ref, 1 - slot)

    pltpu.make_async_copy(yb_hbm.at[pl.ds(0, tm)], ya_scr.at[slot], sem.at[slot]).wait()
    pltpu.make_async_copy(yb_hbm.at[pl.ds(0, tm)], yb_scr.at[slot], sem.at[slot]).wait()
    ga = gs_ref[:, 0:1]
    gb = gs_ref[:, 1:2]
    a_lo, a_hi = _unpack_bf16_pairs(ya_scr[slot])
    b_lo, b_hi = _unpack_bf16_pairs(yb_scr[slot])
    half = D_MODEL // 2
    o_ref[:, :half] = x1_ref[:, :half] + g2_ref[0][:, :half] * (ga * a_lo + gb * b_lo)
    o_ref[:, half:] = x1_ref[:, half:] + g2_ref[0][:, half:] * (ga * a_hi + gb * b_hi)


def _combine(d1, d2, ybuf, gs, x1, mod3, seq_len):
    t = x1.shape[0]
    tm = COMBINE_TM
    nt = t // tm
    tiles_per_seq = seq_len // tm
    smem = pl.BlockSpec((1, 1, tm), lambda i: (i, 0, 0), memory_space=pltpu.SMEM)
    smem_next = pl.BlockSpec((1, 1, tm), lambda i: (jnp.minimum(i + 1, nt - 1), 0, 0), memory_space=pltpu.SMEM)
    return pl.pallas_call(
        _combine_kernel,
        grid=(nt,),
        in_specs=[smem, smem, smem_next, smem_next,
                  pl.BlockSpec(memory_space=pl.ANY),
                  pl.BlockSpec((tm, LANE), lambda i: (i, 0)),
                  pl.BlockSpec((tm, D_MODEL), lambda i: (i, 0)),
                  pl.BlockSpec((1, 1, D_MODEL), lambda i: ((i // tiles_per_seq) * N_MOD + 5, 0, 0))],
        out_specs=pl.BlockSpec((tm, D_MODEL), lambda i: (i, 0)),
        out_shape=jax.ShapeDtypeStruct((t, D_MODEL), F32),
        scratch_shapes=[pltpu.VMEM((2, tm, D_MODEL // 2), U32), pltpu.VMEM((2, tm, D_MODEL // 2), U32),
                        pltpu.SemaphoreType.DMA((2,))],
        compiler_params=_cparams(("arbitrary",), 32),
        name="moe_combine",
    )(d1, d2, d1, d2, ybuf, gs, x1, mod3)


def _rope_tables(seq_len):
    t = np.arange(seq_len)
    row = (t // GRID_W).astype(np.float32)
    colp = (t % GRID_W).astype(np.float32)
    half = SSD_STATE // 2
    inv = (ROPE_THETA ** (-np.arange(0, half, 2, dtype=np.float32) / half)).astype(np.float32)
    ar = (row[:, None] * inv).astype(np.float64)
    ac = (colp[:, None] * inv).astype(np.float64)
    cos_t = np.concatenate([np.cos(ar), np.cos(ar), np.cos(ac), np.cos(ac)], axis=-1)
    sin_t = np.concatenate([-np.sin(ar), np.sin(ar), -np.sin(ac), np.sin(ac)], axis=-1)
    return jnp.asarray(cos_t, F32), jnp.asarray(sin_t, F32)


def _scan_tables(rev):
    li = np.arange(SSD_CHUNK)[:, None]
    ui = np.arange(SSD_CHUNK)[None, :]
    tri = (ui >= li) if rev else (ui <= li)
    col = SSD_HEADS if rev else 0
    j = np.arange(LANE)[:, None]
    c = np.arange(D_SSD)[None, :]
    ee = j == col + c // SSD_HEADDIM
    return jnp.asarray(tri, BF16), jnp.asarray(ee, BF16)


def _pad_lanes(v, width=LANE):
    v = v.reshape(1, -1)
    return jnp.pad(v, ((0, 0), (0, width - v.shape[1])))


def kernel(x, c, ctx, c_ctx, w_mod, b_mod, norm1_w, w_in, conv_w, conv_b, a_log_f, a_log_b, dt_bias_f, dt_bias_b, d_skip, ssd_norm_w, q_norm_w, k_norm_w, rpb, w_br_ssd, w_br_na, w_out, norm2_w, w_grp, b_grp, w_rt, b_rt, w1, w3, w2):
    nb, seq_len, d = x.shape
    ctx_len = ctx.shape[1]
    t = nb * seq_len
    tc = nb * ctx_len
    assert w_mod.shape[0] == 1 and d == D_MODEL and nb <= 7
    assert seq_len % 256 == 0 and ctx_len % SSD_CHUNK == 0 and seq_len // GRID_W >= NA_KH

    cin = jnp.concatenate([c, c_ctx[None, :], jnp.zeros((8 - nb - 1, d), F32)], axis=0)
    mod = _modulation(cin, w_mod[0], b_mod[0])
    mod3 = mod.reshape(8 * N_MOD, 1, D_MODEL)

    o_dt = D_SSD + D_XBC
    o_qkv = o_dt + 2 * SSD_HEADS
    wi = w_in[0].astype(BF16)
    w_a = wi
    w_b = wi[:, o_qkv:]
    w_dtp = jnp.pad(wi[:, o_dt:o_qkv], ((0, 0), (0, LANE - 2 * SSD_HEADS)))
    n1w = norm1_w[0].reshape(1, D_MODEL)

    x2 = x.reshape(t, D_MODEL)
    ctx2 = ctx.reshape(tc, D_MODEL)
    tm_in = 2048 if seq_len % 2048 == 0 else 256
    tiles = seq_len // tm_in
    tn_in = 1024
    z_t = list(range(0, D_SSD // tn_in))
    xbc_t = list(range(D_SSD // tn_in, o_dt // tn_in))
    q_t, k_t, v_t = [0], [1], [2]
    g_t = [3, 4]
    big, dt_raw = _inproj(x2, mod3, lambda i: i // tiles, n1w, w_a, w_b, w_dtp, tm_in, tn_in,
                          xbc_t + z_t, g_t + q_t + k_t + v_t)
    tm_c = 1024 if tc % 1024 == 0 else SSD_CHUNK
    bigc, dtc_raw = _inproj(ctx2, mod3, lambda i: nb, n1w, w_a, w_b, w_dtp, tm_c, tn_in, xbc_t, k_t + v_t)
    ZCOL, GCOL, QCOL, KCOL, VCOL = 2, 3, 8, 9, 10
    KC_COL, VC_COL = 4, 5

    dtbias = _pad_lanes(jnp.concatenate([dt_bias_f[0], dt_bias_b[0]]))
    cw = conv_w[0]
    cbias = conv_b[0].reshape(1, D_XBC)
    cos_t, sin_t = _rope_tables(seq_len)
    xs, bc, dts = _ssd_prep(big, dt_raw, cw, cbias, dtbias, cos_t, sin_t, seq_len, 256)
    ctl = 256 if ctx_len % 256 == 0 else SSD_CHUNK
    ones_t = jnp.ones((ctx_len, LANE), F32)
    zeros_t = jnp.zeros((ctx_len, LANE), F32)
    xsc, bcc, dtsc = _ssd_prep(bigc, dtc_raw, cw, cbias, dtbias, ones_t, zeros_t, ctx_len, ctl)

    alog = _pad_lanes(jnp.concatenate([a_log_f[0], a_log_b[0]]))
    dskip = jnp.repeat(d_skip[0], SSD_HEADDIM).reshape(1, D_SSD)
    tri_f, ee_f = _scan_tables(False)
    tri_b, ee_b = _scan_tables(True)
    hcf = _ssd_state(xsc, bcc, dtsc, alog, tri_f, ee_f, nb, False)
    hcb = _ssd_state(xsc, bcc, dtsc, alog, tri_b, ee_b, nb, True)
    yf, yb = _ssd_scan(xs, bc, dts, alog, dskip, (tri_f, ee_f), (tri_b, ee_b), hcf, hcb, nb)

    qw = jnp.tile(q_norm_w[0], NA_HEADS).reshape(1, D_NA)
    kw = jnp.tile(k_norm_w[0], NA_HEADS).reshape(1, D_NA)
    gi = np.arange(LANE)
    gmat = jnp.asarray(((gi[:, None] // NA_HEADDIM) == (gi[None, :] // NA_HEADDIM)) * (1.0 / NA_HEADDIM), BF16)
    qn, kn = _na_prep(big, QCOL, KCOL, qw, kw, gmat, 512 if t % 512 == 0 else 256)
    kcn = _na_prep(bigc, None, KC_COL, qw, kw, gmat, ctx_len)
    bias_tab = _na_bias_table(rpb[0])
    y_na = _neigh_attention(qn, kn, big, kcn, bigc, bias_tab, nb, seq_len, ctx_len, VCOL, VC_COL)

    w_r = jnp.pad(jnp.concatenate([w_rt[0], w_grp[0]], axis=1), ((0, 0), (0, LANE - N_EXPERTS - N_GROUPS)))
    wrh = w_r.astype(BF16)
    wrl = (w_r - wrh.astype(F32)).astype(BF16)
    br = _pad_lanes(jnp.concatenate([b_rt[0], b_grp[0]]))
    nblk = (2 * t + N_EXPERTS * (MOE_BLOCK - 1) + MOE_BLOCK - 1) // MOE_BLOCK
    x1, h2p, logits, xb_zero = _merge(yf, yb, big, y_na, x2, mod3,
                             ssd_norm_w[0].reshape(1, D_SSD), norm2_w[0].reshape(1, D_MODEL),
                             w_br_ssd[0].astype(BF16), w_br_na[0].astype(BF16), w_out[0].astype(BF16),
                             wrh, wrl, br, seq_len, 512 if seq_len % 512 == 0 else 256, ZCOL, GCOL,
                                      nblk * MOE_BLOCK)

    si = np.arange(ROUTE_TM)
    stri = jnp.asarray(si[None, :] < si[:, None], BF16)
    cnt = _route_count(logits)

    counts = cnt[0, :N_EXPERTS]
    padded = (counts + MOE_BLOCK - 1) // MOE_BLOCK * MOE_BLOCK
    pend = jnp.cumsum(padded)
    pstart = pend - padded
    n_used =(pend[-1] // MOE_BLOCK).astype(I32).reshape(1)
    blk_row0 = jnp.arange(nblk, dtype=I32) * MOE_BLOCK
    blk_exp = jnp.minimum(jnp.sum((pend[None, :] <= blk_row0[:, None]).astype(I32), axis=1), N_EXPERTS - 1)
    gs, rt = _route_dest(logits, stri, _pad_lanes(pstart.astype(F32)))

    def tok_tiles(row, tm):
        return rt[row].reshape(t // tm, 1, tm)

    xb = _dispatch(tok_tiles(2, DISPATCH_TM), tok_tiles(3, DISPATCH_TM), h2p, xb_zero)
    ybuf = _ffn(blk_exp, n_used, xb, w1[0], w3[0], w2[0])
    out = _combine(tok_tiles(2, COMBINE_TM), tok_tiles(3, COMBINE_TM), ybuf, gs, x1, mod3, seq_len)
    return out.reshape(nb, seq_len, D_MODEL)
```

```python
import functools
import math

import numpy as np

import jax
import jax.numpy as jnp
from jax import lax
from jax.experimental import pallas as pl
from jax.experimental.pallas import tpu as pltpu

F32 = jnp.float32
BF16 = jnp.bfloat16
I32 = jnp.int32
U32 = jnp.uint32

D_MODEL = 1024
GRID_W = 64
D_SSD = 2048
SSD_HEADDIM = 64
SSD_HEADS = 32
SSD_GROUPS = 8
HEADS_PER_GROUP = 4
SSD_STATE = 128
SSD_CHUNK = 128
CONV_K = 5
D_BC = SSD_GROUPS * SSD_STATE
D_XBC = D_SSD + 2 * D_BC
NA_HEADDIM = 64
NA_HEADS = 16
D_NA = 1024
NA_KH = 8
NA_KW = 16
ROPE_THETA = 10000.0
N_GROUPS = 4
EXPERTS_PER_GROUP = 8
N_EXPERTS = 32
D_EXPERT = 512
N_MOD = 6
EPS = 1e-6
NEG_BIG = -1e30

COLS_LAT = D_XBC + D_SSD + 2 * D_MODEL + 3 * D_NA
COLS_CTX = D_XBC + 2 * D_NA
LANE = 128

V7X_VMEM_BYTES = 64 * 1024 * 1024


def _cparams(sem, vmem_mb):
    return pltpu.CompilerParams(dimension_semantics=sem, vmem_limit_bytes=vmem_mb * 1024 * 1024)


def _sigmoid(x):
    return 1.0 / (1.0 + jnp.exp(-x))


def _silu(x):
    h = 0.5 * x
    return h * jnp.tanh(h) + h


def _split2(x):
    hi = x.astype(BF16)
    lo = (x - hi.astype(F32)).astype(BF16)
    return hi, lo


def _split3(x):
    hi = x.astype(BF16)
    r = x - hi.astype(F32)
    mid = r.astype(BF16)
    lo = (r - mid.astype(F32)).astype(BF16)
    return hi, mid, lo


def _pack_bf16_pairs(x):
    bits = pltpu.bitcast(x.astype(BF16).astype(F32), U32)
    n = x.shape[1] // 2
    return (bits[:, :n] >> 16) | bits[:, n:]


def _unpack_bf16_pairs(w):
    return pltpu.bitcast(w << 16, F32), pltpu.bitcast(w & jnp.uint32(0xFFFF0000), F32)


def _dot(a, b):
    return jnp.dot(a, b, preferred_element_type=F32)


def _dot_nt(a, b):
    return lax.dot_general(a, b, (((1,), (1,)), ((), ())), preferred_element_type=F32)


def _dot_tn(a, b):
    return lax.dot_general(a, b, (((0,), (0,)), ((), ())), preferred_element_type=F32)


def _mod_kernel(c_ref, w_ref, b_ref, o_ref):
    c = c_ref[...]
    a = _silu(c).astype(BF16)
    o_ref[...] = _dot(a, w_ref[...].astype(BF16)) + b_ref[...]


def _modulation(cin, w_mod, b_mod):
    n = w_mod.shape[1]
    tn = 1536
    return pl.pallas_call(
        _mod_kernel,
        grid=(n // tn,),
        in_specs=[pl.BlockSpec((8, D_MODEL), lambda j: (0, 0)),
                  pl.BlockSpec((D_MODEL, tn), lambda j: (0, j)),
                  pl.BlockSpec((1, tn), lambda j: (0, j))],
        out_specs=pl.BlockSpec((8, tn), lambda j: (0, j)),
        out_shape=jax.ShapeDtypeStruct((8, n), F32),
        compiler_params=_cparams(("arbitrary",), 40),
        name="modulation",
    )(cin, w_mod, b_mod.reshape(1, n))


def _inproj_kernel(x_ref, sh_ref, sc_ref, nw_ref, wa_ref, wb_ref, wdt_ref, o_ref, dt_ref, h_scr, *, tm, n_a):
    j = pl.program_id(1)

    @pl.when(j == 0)
    def _():
        scale = 1.0 + sc_ref[0]
        shift = sh_ref[0]
        nw = nw_ref[...]

        def body(r, carry):
            rows = pl.ds(pl.multiple_of(r * 128, 128), 128)
            x = x_ref[rows, :]
            ms = jnp.mean(x * x, axis=-1, keepdims=True)
            h = x * lax.rsqrt(ms + EPS) * nw * scale + shift
            h_scr[rows, :] = h.astype(BF16)
            return carry

        lax.fori_loop(0, tm // 128, body, 0)
        dt_ref[...] = _dot(h_scr[...], wdt_ref[...])

    @pl.when(j < n_a)
    def _():
        o_ref[...] = _dot(h_scr[...], wa_ref[...]).astype(BF16)

    @pl.when(j >= n_a)
    def _():
        o_ref[...] = _dot(h_scr[...], wb_ref[...]).astype(BF16)


def _inproj(x2, mod3, mod_row_fn, norm_w, wa, wb, wdt, tm, tn, a_tiles, b_tiles):
    t = x2.shape[0]
    n_a, n_b = len(a_tiles), len(b_tiles)
    a_tab = list(a_tiles) + [a_tiles[-1]] * n_b
    b_tab = [b_tiles[0]] * n_a + list(b_tiles)

    def pick(tab, j):
        idx = jnp.int32(tab[-1])
        for pos in range(len(tab) - 2, -1, -1):
            idx = jnp.where(j <= pos, tab[pos], idx)
        return idx

    kern = functools.partial(_inproj_kernel, tm=tm, n_a=n_a)
    return pl.pallas_call(
        kern,
        grid=(t // tm, n_a + n_b),
        in_specs=[pl.BlockSpec((tm, D_MODEL), lambda i, j: (i, 0)),
                  pl.BlockSpec((1, 1, D_MODEL), lambda i, j: (mod_row_fn(i) * N_MOD + 0, 0, 0)),
                  pl.BlockSpec((1, 1, D_MODEL), lambda i, j: (mod_row_fn(i) * N_MOD + 1, 0, 0)),
                  pl.BlockSpec((1, D_MODEL), lambda i, j: (0, 0)),
                  pl.BlockSpec((D_MODEL, tn), lambda i, j: (0, pick(a_tab, j))),
                  pl.BlockSpec((D_MODEL, tn), lambda i, j: (0, pick(b_tab, j))),
                  pl.BlockSpec((D_MODEL, LANE), lambda i, j: (0, 0))],
        out_specs=[pl.BlockSpec((tm, tn), lambda i, j: (i, j)),
                   pl.BlockSpec((tm, LANE), lambda i, j: (i, 0))],
        out_shape=[jax.ShapeDtypeStruct((t, (n_a + n_b) * tn), BF16),
                   jax.ShapeDtypeStruct((t, LANE), F32)],
        scratch_shapes=[pltpu.VMEM((tm, D_MODEL), BF16)],
        compiler_params=_cparams(("arbitrary", "arbitrary"), 52),
        name="inproj",
    )(x2, mod3, mod3, norm_w, wa, wb, wdt)


CONV_OFFSETS = (-2, -1, 1, 2)


def _ssd_prep_kernel(prev_ref, cur_ref, next_ref, dt_ref, cw_ref, cb_ref, dtbias_ref, cos_ref, sin_ref,
                     shift_ref, etop_ref, ebot_ref, xs_ref, bc_ref, dto_ref, *, tl, tiles_per_seq):
    i = pl.program_id(0)
    first = (i % tiles_per_seq) == 0
    last = (i % tiles_per_seq) == tiles_per_seq - 1
    lane = lax.broadcasted_iota(I32, (tl, LANE), 1)
    lo_half = ((lane // 32) % 2) == 0
    shift = shift_ref[...]
    etop = etop_ref[...]
    ebot = ebot_ref[...]
    cosv = cos_ref[...]
    sinv = sin_ref[...]
    cwid = 256
    for c in range(D_XBC // cwid):
        cs = slice(c * cwid, (c + 1) * cwid)
        w = cw_ref[:, cs]
        xc = cur_ref[:, cs]
        sh = _dot(shift, xc).reshape(tl // 8, len(CONV_OFFSETS), 8, cwid)
        tap = lambda k: sh[:, k].reshape(tl, cwid)
        acc = cb_ref[:, cs] + xc.astype(F32) * w[2:3, :]
        acc = acc + tap(0) * w[0:1, :] + tap(1) * w[1:2, :] + tap(2) * w[3:4, :] + tap(3) * w[4:5, :]
        top = _dot(etop, prev_ref[:, cs])
        bot = _dot(ebot, next_ref[:, cs])
        top_c = jnp.where(first, 0.0, top[0:8] * w[0:1, :] + top[8:16] * w[1:2, :])
        bot_c = jnp.where(last, 0.0, bot[0:8] * w[3:4, :] + bot[8:16] * w[4:5, :])
        acc = jnp.concatenate([acc[0:8] + top_c, acc[8:tl - 8], acc[tl - 8:tl] + bot_c], axis=0)
        y = _silu(acc)
        if c * cwid < D_SSD:
            xs_ref[:, cs] = y.astype(BF16)
        else:
            for g in range(cwid // LANE):
                yg = y[:, g * LANE:(g + 1) * LANE]
                partner = jnp.where(lo_half, pltpu.roll(yg, 96, 1), pltpu.roll(yg, 32, 1))
                o = yg * cosv + partner * sinv
                c0 = c * cwid - D_SSD + g * LANE
                bc_ref[:, c0:c0 + LANE] = o.astype(BF16)

    v = dt_ref[...] + dtbias_ref[...]
    dto_ref[...] = jnp.maximum(v, 0.0) + jnp.log(1.0 + jnp.exp(-jnp.abs(v)))


def _conv_shift_tables(tl):
    t = np.arange(tl)[:, None]
    u = np.arange(tl)[None, :]
    shift = np.stack([(u == t + off) for off in CONV_OFFSETS], axis=0).reshape(len(CONV_OFFSETS), tl // 8, 8, tl)
    shift = shift.transpose(1, 0, 2, 3).reshape(len(CONV_OFFSETS) * tl, tl)
    r = np.arange(16)[:, None]
    q = np.arange(16)[None, :]
    etop = ((r < 8) & (q == r + 14)) | ((r >= 8) & (q == r - 8 + 15))
    ebot = ((r < 8) & (q == r - 7)) | ((r >= 8) & (q == r - 8 - 6))
    return jnp.asarray(shift, BF16), jnp.asarray(etop, BF16), jnp.asarray(ebot, BF16)


def _ssd_prep(big, dt_raw, conv_w, conv_b, dtbias, cos_t, sin_t, seq_len, tl):
    t = big.shape[0]
    tps = seq_len // tl
    hb = tl // 16
    nhalo = t // 16
    kern = functools.partial(_ssd_prep_kernel, tl=tl, tiles_per_seq=tps)
    shift, etop, ebot = _conv_shift_tables(tl)
    const = lambda i: (0, 0)
    return pl.pallas_call(
        kern,
        grid=(t // tl,),
        in_specs=[pl.BlockSpec((16, D_XBC), lambda i: (jnp.maximum(i * hb - 1, 0), 0)),
                  pl.BlockSpec((tl, D_XBC), lambda i: (i, 0)),
                  pl.BlockSpec((16, D_XBC), lambda i: (jnp.minimum((i + 1) * hb, nhalo - 1), 0)),
                  pl.BlockSpec((tl, LANE), lambda i: (i, 0)),
                  pl.BlockSpec((CONV_K, D_XBC), const),
                  pl.BlockSpec((1, D_XBC), const),
                  pl.BlockSpec((1, LANE), const),
                  pl.BlockSpec((tl, LANE), lambda i: (i % tps, 0)),
                  pl.BlockSpec((tl, LANE), lambda i: (i % tps, 0)),
                  pl.BlockSpec((len(CONV_OFFSETS) * tl, tl), const),
                  pl.BlockSpec((16, 16), const),
                  pl.BlockSpec((16, 16), const)],
        out_specs=[pl.BlockSpec((tl, D_SSD), lambda i: (i, 0)),
                   pl.BlockSpec((tl, 2 * D_BC), lambda i: (i, 0)),
                   pl.BlockSpec((tl, LANE), lambda i: (i, 0))],
        out_shape=[jax.ShapeDtypeStruct((t, D_SSD), BF16),
                   jax.ShapeDtypeStruct((t, 2 * D_BC), BF16),
                   jax.ShapeDtypeStruct((t, LANE), F32)],
        compiler_params=_cparams(("arbitrary",), 40),
        name="ssd_prep",
    )(big, big, big, dt_raw, conv_w, conv_b, dtbias, cos_t, sin_t, shift, etop, ebot)


GROUP_W = HEADS_PER_GROUP * SSD_HEADDIM


def _scan_decays(dt_ref, alog_ref, tri_ref, ee_ref, rev):
    edge = 0 if rev else SSD_CHUNK - 1
    a_row = -jnp.exp(alog_ref[...])
    dt = dt_ref[...]
    trib = tri_ref[...]
    d_hi, d_mid, d_lo = _split3(dt * a_row)
    cum = _dot(trib, d_hi) + _dot(trib, d_mid) + _dot(trib, d_lo)
    tot = cum[edge:edge + 1, :]
    ee = ee_ref[...]
    w_heads = (jnp.exp(tot - cum) * dt).astype(BF16)
    t_hi, t_lo = _split2(jnp.broadcast_to(jnp.exp(tot), (8, LANE)))
    etot = (_dot(t_hi, ee) + _dot(t_lo, ee))[0:1, :]
    return dt, trib, cum, w_heads, etot


def _spread(per_head, ee_ref, g):
    return _dot(per_head, ee_ref[:, g * GROUP_W:(g + 1) * GROUP_W])


def _ssd_state_kernel(xs_ref, bc_ref, dt_ref, alog_ref, tri_ref, ee_ref, hfin_ref, h_scr, *, rev, nc):
    k = pl.program_id(1)

    @pl.when(k == 0)
    def _():
        h_scr[...] = jnp.zeros_like(h_scr)

    _, _, _, w_heads, etot = _scan_decays(dt_ref, alog_ref, tri_ref, ee_ref, rev)
    for g in range(SSD_GROUPS):
        gs = slice(g * GROUP_W, (g + 1) * GROUP_W)
        xw = (xs_ref[:, gs].astype(F32) * _spread(w_heads, ee_ref, g)).astype(BF16)
        h_scr[:, gs] = etot[:, gs] * h_scr[:, gs] + _dot_tn(bc_ref[:, g * SSD_STATE:(g + 1) * SSD_STATE], xw)

    @pl.when(k == nc - 1)
    def _():
        hfin_ref[0] = h_scr[...]


def _scan_chunk_stages(xs_ref, bc_ref, dt_ref, alog_ref, dskip_ref, tri_ref, ee_ref, y_ref, h_scr, rev):
    L = SSD_CHUNK
    col = SSD_HEADS if rev else 0
    dt, trib, cum, w_heads, etot = _scan_decays(dt_ref, alog_ref, tri_ref, ee_ref, rev)
    tri = trib > 0.5
    cum_t = cum.T
    dt_t = dt.T
    o_heads = jnp.exp(cum).astype(BF16)

    gw = GROUP_W
    lane_head = lax.broadcasted_iota(I32, (L, gw), 1) // SSD_HEADDIM

    def b_of(g):
        return bc_ref[:, g * SSD_STATE:(g + 1) * SSD_STATE]

    def c_of(g):
        return bc_ref[:, D_BC + g * SSD_STATE:D_BC + (g + 1) * SSD_STATE]

    def operands(g, cb):
        gs = slice(g * gw, (g + 1) * gw)
        xs_g = xs_ref[:, gs]
        ms = []
        for r in range(HEADS_PER_GROUP):
            hh = col + g * HEADS_PER_GROUP + r
            seg = cum[:, hh:hh + 1] - cum_t[hh:hh + 1, :]
            lm = jnp.exp(jnp.where(tri, seg, NEG_BIG)) * dt_t[hh:hh + 1, :]
            ms.append((cb * lm).astype(BF16))
        mcat = jnp.concatenate(ms, axis=1)
        zero = jnp.zeros_like(xs_g)
        bd = jnp.concatenate([jnp.where(lane_head == r, xs_g, zero) for r in range(HEADS_PER_GROUP)], axis=0)
        xs_f = xs_g.astype(F32)
        xw = (xs_f * _spread(w_heads, ee_ref, g)).astype(BF16)
        return mcat, bd, xs_f, xw, _spread(o_heads, ee_ref, g)

    def cb_of(g):
        return _dot_nt(c_of(g), b_of(g))

    yield
    cb_next = cb_of(0)
    yield
    nxt = operands(0, cb_next)
    cb_next = cb_of(1)
    for g in range(SSD_GROUPS):
        yield
        gs = slice(g * gw, (g + 1) * gw)
        mcat, bd, xs_f, xw, oscale_g = nxt
        h_g = h_scr[:, gs]
        y_diag = _dot(mcat, bd)
        y_off = _dot(c_of(g), h_g.astype(BF16))
        loc = _dot_tn(b_of(g), xw)
        if g + 1 < SSD_GROUPS:
            nxt = operands(g + 1, cb_next)
            if g + 2 < SSD_GROUPS:
                cb_next = cb_of(g + 2)
        y = y_diag + oscale_g * y_off
        if not rev:
            y = y + xs_f * dskip_ref[:, gs]
        y_ref[:, gs] = y.astype(BF16)
        h_scr[:, gs] = etot[:, gs] * h_g + loc


def _ssd_scan_kernel(xsf_ref, bcf_ref, dtf_ref, xsb_ref, bcb_ref, dtb_ref, alog_ref, dskip_ref,
                     trif_ref, eef_ref, trib_ref, eeb_ref, h0f_ref, h0b_ref, yf_ref, yb_ref, hf_scr, hb_scr):
    @pl.when(pl.program_id(1) == 0)
    def _():
        hf_scr[...] = h0f_ref[0]
        hb_scr[...] = h0b_ref[0]

    chains = [_scan_chunk_stages(xsf_ref, bcf_ref, dtf_ref, alog_ref, dskip_ref, trif_ref, eef_ref,
                                 yf_ref, hf_scr, False),
              _scan_chunk_stages(xsb_ref, bcb_ref, dtb_ref, alog_ref, dskip_ref, trib_ref, eeb_ref,
                                 yb_ref, hb_scr, True)]
    while chains:
        for ch in list(chains):
            try:
                next(ch)
            except StopIteration:
                chains.remove(ch)


def _scan_rowmap(nc, rev):
    def rowmap(b, k):
        c = (nc - 1 - k) if rev else k
        return (b * nc + c, 0)
    return rowmap


def _ssd_state(xs, bc, dt, alog_row, tri, ee, nb, rev):
    t = xs.shape[0]
    nc = t // nb // SSD_CHUNK
    rowmap = _scan_rowmap(nc, rev)
    const2 = lambda b, k: (0, 0)
    return pl.pallas_call(
        functools.partial(_ssd_state_kernel, rev=rev, nc=nc),
        grid=(nb, nc),
        in_specs=[pl.BlockSpec((SSD_CHUNK, D_SSD), rowmap),
                  pl.BlockSpec((SSD_CHUNK, D_BC), rowmap),
                  pl.BlockSpec((SSD_CHUNK, LANE), rowmap),
                  pl.BlockSpec((1, LANE), const2),
                  pl.BlockSpec((SSD_CHUNK, SSD_CHUNK), const2),
                  pl.BlockSpec((LANE, D_SSD), const2)],
        out_specs=pl.BlockSpec((1, SSD_STATE, D_SSD), lambda b, k: (b, 0, 0)),
        out_shape=jax.ShapeDtypeStruct((nb, SSD_STATE, D_SSD), F32),
        scratch_shapes=[pltpu.VMEM((SSD_STATE, D_SSD), F32)],
        compiler_params=_cparams(("arbitrary", "arbitrary"), 40),
        name="ssd_state_bwd" if rev else "ssd_state_fwd",
    )(xs, bc, dt, alog_row, tri, ee)


def _ssd_scan(xs, bc, dt, alog_row, dskip_row, tables_f, tables_b, h0f, h0b, nb):
    t = xs.shape[0]
    nc = t // nb // SSD_CHUNK
    fmap = _scan_rowmap(nc, False)
    bmap = _scan_rowmap(nc, True)
    const2 = lambda b, k: (0, 0)
    chunk = lambda width, m: pl.BlockSpec((SSD_CHUNK, width), m)
    tri_spec = pl.BlockSpec((SSD_CHUNK, SSD_CHUNK), const2)
    ee_spec = pl.BlockSpec((LANE, D_SSD), const2)
    state = pl.BlockSpec((1, SSD_STATE, D_SSD), lambda b, k: (b, 0, 0))
    return pl.pallas_call(
        _ssd_scan_kernel,
        grid=(nb, nc),
        in_specs=[chunk(D_SSD, fmap), chunk(2 * D_BC, fmap), chunk(LANE, fmap),
                  chunk(D_SSD, bmap), chunk(2 * D_BC, bmap), chunk(LANE, bmap),
                  pl.BlockSpec((1, LANE), const2),
                  pl.BlockSpec((1, D_SSD), const2),
                  tri_spec, ee_spec, tri_spec, ee_spec, state, state],
        out_specs=[chunk(D_SSD, fmap), chunk(D_SSD, bmap)],
        out_shape=[jax.ShapeDtypeStruct((t, D_SSD), BF16)] * 2,
        scratch_shapes=[pltpu.VMEM((SSD_STATE, D_SSD), F32)] * 2,
        compiler_params=_cparams(("arbitrary", "arbitrary"), 48),
        name="ssd_scan",
    )(xs, bc, dt, xs, bc, dt, alog_row, dskip_row, *tables_f, *tables_b, h0f, h0b)


def _headnorm_cols(src_ref, w_ref, g, dst_ref, scale, transposed=False):
    for c in range(D_NA // LANE):
        cs = slice(c * LANE, (c + 1) * LANE)
        x = src_ref[:, cs].astype(F32)
        ms = _dot((x * x).astype(BF16), g)
        y = x * lax.rsqrt(ms + EPS) * w_ref[:, cs]
        if scale is not None:
            y = y * scale
        if transposed:
            dst_ref[cs, :] = y.T.astype(BF16)
        else:
            dst_ref[:, cs] = y.astype(BF16)


def _na_prep_qk_kernel(q_ref, k_ref, qw_ref, kw_ref, g_ref, qo_ref, ko_ref):
    g = g_ref[...]
    _headnorm_cols(q_ref, qw_ref, g, qo_ref, NA_HEADDIM ** -0.5)
    _headnorm_cols(k_ref, kw_ref, g, ko_ref, None)


def _na_prep_k_kernel(k_ref, kw_ref, g_ref, ko_ref):
    _headnorm_cols(k_ref, kw_ref, g_ref[...], ko_ref, None, transposed=True)


def _na_prep(big, qcol, kcol, qw, kw, gmat, tm):
    t = big.shape[0]
    blk = lambda cidx: pl.BlockSpec((tm, D_NA), lambda i: (i, cidx))
    row = pl.BlockSpec((1, D_NA), lambda i: (0, 0))
    gspec = pl.BlockSpec((LANE, LANE), lambda i: (0, 0))
    out = pl.BlockSpec((tm, D_NA), lambda i: (i, 0))
    if qcol is None:
        return pl.pallas_call(
            _na_prep_k_kernel, grid=(t // tm,),
            in_specs=[blk(kcol), row, gspec], out_specs=pl.BlockSpec((D_NA, tm), lambda i: (i, 0)),
            out_shape=jax.ShapeDtypeStruct((t // tm * D_NA, tm), BF16),
            compiler_params=_cparams(("arbitrary",), 32), name="na_prep_ctx",
        )(big, kw, gmat)
    return pl.pallas_call(
        _na_prep_qk_kernel, grid=(t // tm,),
        in_specs=[blk(qcol), blk(kcol), row, row, gspec], out_specs=[out, out],
        out_shape=[jax.ShapeDtypeStruct((t, D_NA), BF16)] * 2,
        compiler_params=_cparams(("arbitrary",), 32), name="na_prep",
    )(big, big, qw, kw, gmat)


NA_LOOKAHEAD = 1


NA_ROWS_PER_STEP = 4


def _na_row_stages(r, qrow, q_ref, k_ref, v_ref, kc_ref, vc_ref, bias_ref, o_ref, rows):
    rs = jnp.clip(r - NA_KH // 2, 0, rows - NA_KH)
    start = pl.multiple_of(rs * GRID_W, GRID_W)
    nk = NA_KH * GRID_W
    lane = lax.broadcasted_iota(I32, (GRID_W, LANE), 1)
    first_head = lane < NA_HEADDIM
    npair = NA_HEADS // 2
    qrows = slice(qrow, qrow + GRID_W)

    def scores(j):
        cs = slice(j * LANE, (j + 1) * LANE)
        qp = q_ref[qrows, cs]
        zero = jnp.zeros_like(qp)
        qs = jnp.concatenate([jnp.where(first_head, qp, zero), jnp.where(first_head, zero, qp)], axis=0)
        kb = k_ref[pl.ds(start, nk), cs]
        s_loc = _dot_nt(qs, kb) + bias_ref[0, j * LANE:(j + 1) * LANE, :]
        s_ctx = _dot(qs, kc_ref[cs, :])
        return s_loc, s_ctx

    pending = [scores(j) for j in range(NA_LOOKAHEAD)]
    for j in range(npair):
        yield
        cs = slice(j * LANE, (j + 1) * LANE)
        s_loc, s_ctx = pending.pop(0)
        if j + NA_LOOKAHEAD < npair:
            pending.append(scores(j + NA_LOOKAHEAD))
        vb = v_ref[pl.ds(start, nk), cs]
        m = jnp.maximum(jnp.max(s_loc, axis=-1, keepdims=True), jnp.max(s_ctx, axis=-1, keepdims=True))
        p_loc = jnp.exp(s_loc - m)
        p_ctx = jnp.exp(s_ctx - m)
        den = jnp.sum(p_loc, axis=-1, keepdims=True) + jnp.sum(p_ctx, axis=-1, keepdims=True)
        o = _dot(p_loc.astype(BF16), vb) + _dot(p_ctx.astype(BF16), vc_ref[:, cs])
        o = o / den
        o_ref[qrows, cs] = jnp.where(first_head, o[:GRID_W], o[GRID_W:]).astype(BF16)


def _na_kernel(q_ref, k_ref, v_ref, kc_ref, vc_ref, *rest, rows):
    bias_refs, o_ref = rest[:NA_ROWS_PER_STEP], rest[NA_ROWS_PER_STEP]
    r0 = pl.program_id(1) * NA_ROWS_PER_STEP
    chains = [_na_row_stages(r0 + u, u * GRID_W, q_ref, k_ref, v_ref, kc_ref, vc_ref, bias_refs[u], o_ref, rows)
              for u in range(NA_ROWS_PER_STEP)]
    while chains:
        for ch in list(chains):
            try:
                next(ch)
            except StopIteration:
                chains.remove(ch)


def _neigh_attention(qn, kn, big, kcn, bigc, bias_tab, nb, seq_len, ctx_len, vcol, vccol):
    t = qn.shape[0]
    rows = seq_len // GRID_W
    rps = NA_ROWS_PER_STEP
    steps = rows // rps
    kern = functools.partial(_na_kernel, rows=rows)

    def pat(u):
        def index_map(b, s):
            r = s * rps + u
            return (r - jnp.clip(r - NA_KH // 2, 0, rows - NA_KH), 0, 0)
        return index_map

    bias_specs = [pl.BlockSpec((1, NA_HEADS * GRID_W, NA_KH * GRID_W), pat(u)) for u in range(rps)]
    return pl.pallas_call(
        kern,
        grid=(nb, steps),
        in_specs=[pl.BlockSpec((rps * GRID_W, D_NA), lambda b, s: (b * steps + s, 0)),
                  pl.BlockSpec((seq_len, D_NA), lambda b, s: (b, 0), pipeline_mode=pl.Buffered(1)),
                  pl.BlockSpec((seq_len, D_NA), lambda b, s: (b, vcol), pipeline_mode=pl.Buffered(1)),
                  pl.BlockSpec((D_NA, ctx_len), lambda b, s: (b, 0), pipeline_mode=pl.Buffered(1)),
                  pl.BlockSpec((ctx_len, D_NA), lambda b, s: (b, vccol), pipeline_mode=pl.Buffered(1))] + bias_specs,
        out_specs=pl.BlockSpec((rps * GRID_W, D_NA), lambda b, s: (b * steps + s, 0)),
        out_shape=jax.ShapeDtypeStruct((t, D_NA), BF16),
        compiler_params=_cparams(("arbitrary", "arbitrary"), 48),
        name="neigh_attention",
    )(qn, kn, big, kcn, bigc, *([bias_tab] * rps))


def _na_bias_kernel(rpb_ref, o_ref):
    rp = rpb_ref[0]
    r64 = pltpu.roll(rp, GRID_W, 1)
    c = lax.broadcasted_iota(I32, (GRID_W, LANE), 0)
    kc = lax.broadcasted_iota(I32, (GRID_W, LANE), 1) % GRID_W
    cs = jnp.clip(c - NA_KW // 2, 0, GRID_W - NA_KW)
    valid = (kc >= cs) & (kc < cs + NA_KW)
    pair = []
    for d in range(2 * NA_KH - 2):
        vec = rp[d:d + 1, :] + r64[d + 1:d + 2, :]
        w = pltpu.roll(jnp.broadcast_to(vec, (GRID_W, LANE)), LANE - (NA_KW - 1), 1, stride=1, stride_axis=0)
        pair.append(jnp.where(valid, w, NEG_BIG))
    for p in range(NA_KH):
        for ii in range(NA_KH // 2):
            o_ref[p, :, ii * LANE:(ii + 1) * LANE] = pair[2 * ii - p + NA_KH - 1]


def _na_bias_table(rpb):
    rp = jnp.pad(rpb, ((0, 0), (0, 1), (0, LANE - (2 * NA_KW - 1))))
    return pl.pallas_call(
        _na_bias_kernel,
        grid=(NA_HEADS,),
        in_specs=[pl.BlockSpec((1, 2 * NA_KH, LANE), lambda h: (h, 0, 0))],
        out_specs=pl.BlockSpec((NA_KH, GRID_W, NA_KH * GRID_W), lambda h: (0, h, 0)),
        out_shape=jax.ShapeDtypeStruct((NA_KH, NA_HEADS * GRID_W, NA_KH * GRID_W), F32),
        compiler_params=_cparams(("arbitrary",), 32),
        name="na_bias",
    )(rp)


MERGE_SUB = 128


def _merge_kernel(yf_ref, yb_ref, z_ref, gt_ref, yna_ref, x_ref, g1_ref, sh2_ref, sc2_ref, snw_ref, n2w_ref,
                  wbs_ref, wbn_ref, wo_ref, wrh_ref, wrl_ref, br_ref, x1_ref, h2p_ref, lg_ref, xbz_ref):
    xbz_ref[...] = jnp.zeros_like(xbz_ref)
    tm = x_ref.shape[0]
    subs = [slice(r * MERGE_SUB, (r + 1) * MERGE_SUB) for r in range(tm // MERGE_SUB)]
    half = D_MODEL // 2
    yn = []
    for rs in subs:
        z = z_ref[rs, :].astype(F32)
        y = (yf_ref[rs, :] + yb_ref[rs, :]).astype(F32) * _silu(z)
        ms = jnp.mean(y * y, axis=-1, keepdims=True)
        yn.append((y * lax.rsqrt(ms + EPS) * snw_ref[...]).astype(BF16))
    ab = [(_dot(yn[r], wbs_ref[...]), _dot(yna_ref[rs, :], wbn_ref[...])) for r, rs in enumerate(subs)]
    merged = []
    for r, rs in enumerate(subs):
        g_ssd = gt_ref[rs, :D_MODEL].astype(F32)
        g_na = gt_ref[rs, D_MODEL:].astype(F32)
        merged.append((_sigmoid(g_ssd) * ab[r][0] + _sigmoid(g_na) * ab[r][1]).astype(BF16))
    mo = [_dot(m, wo_ref[...]) for m in merged]
    hs = []
    for r, rs in enumerate(subs):
        x1 = x_ref[rs, :] + g1_ref[0] * mo[r]
        x1_ref[rs, :] = x1
        ms2 = jnp.mean(x1 * x1, axis=-1, keepdims=True)
        h2 = x1 * lax.rsqrt(ms2 + EPS) * n2w_ref[...] * (1.0 + sc2_ref[0]) + sh2_ref[0]
        h_hi, h_lo = _split2(h2)
        hs.append((h_hi, h_lo))
        h2p_ref[rs, :] = _pack_bf16_pairs(h2)
    for r, rs in enumerate(subs):
        h_hi, h_lo = hs[r]
        lg_ref[rs, :] = (_dot(h_hi, wrh_ref[...]) + _dot(h_lo, wrh_ref[...]) + _dot(h_hi, wrl_ref[...])) + br_ref[...]


def _merge(yf, yb, big, yna, x2, mod3, snw, n2w, wbs, wbn, wo, wrh, wrl, br, seq_len, tm, zcol, gcol, n_rows):
    t = x2.shape[0]
    tiles_per_seq = seq_len // tm
    zrows = n_rows // (t // tm)
    assert zrows * (t // tm) == n_rows and zrows % 8 == 0
    modspec = lambda kidx: pl.BlockSpec((1, 1, D_MODEL), lambda i: ((i // tiles_per_seq) * N_MOD + kidx, 0, 0))
    full = lambda shp: pl.BlockSpec(shp, lambda i: (0,) * len(shp), pipeline_mode=pl.Buffered(1))
    return pl.pallas_call(
        _merge_kernel,
        grid=(t // tm,),
        in_specs=[pl.BlockSpec((tm, D_SSD), lambda i: (i, 0)),
                  pl.BlockSpec((tm, D_SSD), lambda i: (i, 0)),
                  pl.BlockSpec((tm, D_SSD), lambda i: (i, zcol)),
                  pl.BlockSpec((tm, 2 * D_MODEL), lambda i: (i, gcol)),
                  pl.BlockSpec((tm, D_NA), lambda i: (i, 0)),
                  pl.BlockSpec((tm, D_MODEL), lambda i: (i, 0)),
                  modspec(2), modspec(3), modspec(4),
                  full((1, D_SSD)), full((1, D_MODEL)),
                  full((D_SSD, D_MODEL)), full((D_NA, D_MODEL)), full((D_MODEL, D_MODEL)),
                  full((D_MODEL, LANE)), full((D_MODEL, LANE)), full((1, LANE))],
        out_specs=[pl.BlockSpec((tm, D_MODEL), lambda i: (i, 0)),
                   pl.BlockSpec((tm, D_MODEL // 2), lambda i: (i, 0)),
                   pl.BlockSpec((tm, LANE), lambda i: (i, 0)),
                   pl.BlockSpec((zrows, D_MODEL // 2), lambda i: (i, 0))],
        out_shape=[jax.ShapeDtypeStruct((t, D_MODEL), F32),
                   jax.ShapeDtypeStruct((t, D_MODEL // 2), U32),
                   jax.ShapeDtypeStruct((t, LANE), F32),
                   jax.ShapeDtypeStruct((n_rows, D_MODEL // 2), U32)],
        compiler_params=_cparams(("arbitrary",), 60),
        name="merge",
    )(yf, yb, big, big, yna, x2, mod3, mod3, mod3, snw, n2w, wbs, wbn, wo, wrh, wrl, br)


ROUTE_TM = 512
GRP_LANE0 = N_EXPERTS


def _route_topk(lg):
    tm = lg.shape[0]
    lane = lax.broadcasted_iota(I32, (tm, LANE), 1)
    neg_inf = jnp.float32(-jnp.inf)
    big_lane = jnp.int32(4 * LANE)
    is_grp = (lane >= GRP_LANE0) & (lane < GRP_LANE0 + N_GROUPS)
    gl = jnp.where(is_grp, lg, neg_inf)
    gmax = jnp.max(gl, axis=-1, keepdims=True)
    grp = jnp.min(jnp.where(gl == gmax, lane, big_lane), axis=-1, keepdims=True) - GRP_LANE0
    psum = jnp.sum(jnp.where(is_grp, jnp.exp(lg - gmax), 0.0), axis=-1, keepdims=True)
    p_grp = 1.0 / psum
    in_g = (lane < N_EXPERTS) & ((lane // EXPERTS_PER_GROUP) == grp)
    el = jnp.where(in_g, lg, neg_inf)
    v1 = jnp.max(el, axis=-1, keepdims=True)
    i1 = jnp.min(jnp.where(el == v1, lane, big_lane), axis=-1, keepdims=True)
    el2 = jnp.where(lane == i1, neg_inf, el)
    v2 = jnp.max(el2, axis=-1, keepdims=True)
    i2 = jnp.min(jnp.where(el2 == v2, lane, big_lane), axis=-1, keepdims=True)
    tt = jnp.exp(v2 - v1)
    den = 1.0 + tt
    ga = p_grp / den
    gb = p_grp * tt / den

    return lane, i1, i2, ga, gb


def _route_kernel(lg_ref, stri_ref, utri_ref, gs_ref, rt_ref, cnt_ref, run_scr, slab_scr):
    p = pl.program_id(0)
    i = pl.program_id(1)
    tm = ROUTE_TM
    rows = pl.ds(pl.multiple_of(i * tm, tm), tm)
    lane = lax.broadcasted_iota(I32, (tm, LANE), 1)

    @pl.when((p == 0) & (i == 0))
    def _():
        run_scr[...] = jnp.zeros_like(run_scr)

    @pl.when(p == 0)
    def _():
        _, i1, i2, ga, gb = _route_topk(lg_ref[...])
        onehot = jnp.where((lane == i1) | (lane == i2), 1.0, 0.0)
        run_scr[...] = run_scr[...] + jnp.sum(onehot, axis=0, keepdims=True)
        cnt_ref[...] = run_scr[...].astype(I32)
        slab = jnp.where(lane == 0, ga, 0.0)
        slab = jnp.where(lane == 1, gb, slab)
        slab = jnp.where(lane == 2, i1.astype(F32), slab)
        slab = jnp.where(lane == 3, i2.astype(F32), slab)
        slab_scr[rows, :] = slab

    @pl.when((p == 1) & (i == 0))
    def _():
        lane1 = lax.broadcasted_iota(I32, (8, LANE), 1)
        cnt = jnp.broadcast_to(run_scr[...], (8, LANE))
        blocks = jnp.where(lane1 < N_EXPERTS, jnp.floor((cnt + (MOE_BLOCK - 1)) * (1.0 / MOE_BLOCK)), 0.0)
        ends = _dot(blocks.astype(BF16), utri_ref[...])
        run_scr[...] = ((ends - blocks) * MOE_BLOCK)[0:1, :]

    @pl.when(p == 1)
    def _():
        slab = slab_scr[rows, :]
        lane_f = lane.astype(F32)
        sel1 = lane_f == slab[:, 2:3]
        sel2 = lane_f == slab[:, 3:4]
        onehot = jnp.where(sel1 | sel2, 1.0, 0.0)
        pos = _dot(stri_ref[...], onehot.astype(BF16)) + run_scr[...]
        d1 = jnp.sum(jnp.where(sel1, pos, 0.0), axis=-1, keepdims=True)
        d2 = jnp.sum(jnp.where(sel2, pos, 0.0), axis=-1, keepdims=True)
        run_scr[...] = run_scr[...] + jnp.sum(onehot, axis=0, keepdims=True)
        out = jnp.where(lane == 2, d1, jnp.where(lane == 3, d2, slab))
        gs_ref[...] = out
        for q in range(tm // LANE):
            blk_t = out[q * LANE:(q + 1) * LANE, :].T
            rt_ref[:, q * LANE:(q + 1) * LANE] = blk_t[0:8, :].astype(I32)


def _route(logits, stri, utri):
    t = logits.shape[0]
    tm = ROUTE_TM
    nt = t // tm
    return pl.pallas_call(
        _route_kernel,
        grid=(2, nt),
        in_specs=[pl.BlockSpec((tm, LANE), lambda p, i: (i * (1 - p) + (nt - 1) * p, 0)),
                  pl.BlockSpec((tm, tm), lambda p, i: (0, 0)),
                  pl.BlockSpec((LANE, LANE), lambda p, i: (0, 0))],
        out_specs=[pl.BlockSpec((tm, LANE), lambda p, i: (p * i, 0)),
                   pl.BlockSpec((8, tm), lambda p, i: (0, p * i)),
                   pl.BlockSpec((1, LANE), lambda p, i: (0, 0))],
        out_shape=[jax.ShapeDtypeStruct((t, LANE), F32),
                   jax.ShapeDtypeStruct((8, t), I32),
                   jax.ShapeDtypeStruct((1, LANE), I32)],
        scratch_shapes=[pltpu.VMEM((1, LANE), F32), pltpu.VMEM((t, LANE), F32)],
        compiler_params=_cparams(("arbitrary", "arbitrary"), 40),
        name="route",
    )(logits, stri, utri)


MOE_BLOCK = 512
DISPATCH_TM = 512
COMBINE_TM = 256


ROW_UNROLL = 8


DISPATCH_BUFS = 3


def _dispatch_kernel(d1_ref, d2_ref, h_hbm, xb_in_hbm, xb_hbm, hbuf, load_sem, row_sem):
    del xb_in_hbm
    tm = DISPATCH_TM
    i = pl.program_id(0)
    last = pl.num_programs(0) - 1
    slot = i % DISPATCH_BUFS
    par = i % 2

    def load(tile, s):
        return pltpu.make_async_copy(h_hbm.at[pl.ds(tile * tm, tm)], hbuf.at[s], load_sem.at[s])

    def wait_rows(p):
        for _ in range(2):
            pltpu.make_async_copy(hbuf.at[0], xb_hbm.at[pl.ds(0, tm)], row_sem.at[p]).wait()

    @pl.when(i == 0)
    def _():
        load(0, 0).start()

    @pl.when(i < last)
    def _():
        load(i + 1, (i + 1) % DISPATCH_BUFS).start()

    load(i, slot).wait()

    def body(g, carry):
        for u in range(ROW_UNROLL):
            tt = g * ROW_UNROLL + u
            src = hbuf.at[slot, pl.ds(tt, 1)]
            pltpu.make_async_copy(src, xb_hbm.at[pl.ds(d1_ref[0, 0, tt], 1)], row_sem.at[par]).start()
            pltpu.make_async_copy(src, xb_hbm.at[pl.ds(d2_ref[0, 0, tt], 1)], row_sem.at[par]).start(priority=1)
        return carry

    lax.fori_loop(0, tm // ROW_UNROLL, body, 0)

    @pl.when(i > 0)
    def _():
        wait_rows(1 - par)

    @pl.when(i == last)
    def _():
        wait_rows(par)


def _dispatch(d1, d2, h2p, xb0):
    t = h2p.shape[0]
    tm = DISPATCH_TM
    n_rows, w = xb0.shape
    smem = pl.BlockSpec((1, 1, tm), lambda i: (i, 0, 0), memory_space=pltpu.SMEM)
    anyspec = pl.BlockSpec(memory_space=pl.ANY)
    return pl.pallas_call(
        _dispatch_kernel,
        grid=(t // tm,),
        in_specs=[smem, smem, anyspec, anyspec],
        out_specs=anyspec,
        out_shape=jax.ShapeDtypeStruct((n_rows, w), U32),
        scratch_shapes=[pltpu.VMEM((DISPATCH_BUFS, tm, w), U32),
                        pltpu.SemaphoreType.DMA((DISPATCH_BUFS,)), pltpu.SemaphoreType.DMA((2,))],
        input_output_aliases={3: 0},
        compiler_params=pltpu.CompilerParams(dimension_semantics=("arbitrary",), has_side_effects=True),
        name="moe_dispatch",
    )(d1, d2, h2p, xb0)


def _ffn_kernel(be_ref, nu_ref, xb_ref, w1_ref, w3_ref, w2_ref, yb_ref, w13_scr, w2_scr):
    i = pl.program_id(0)
    prev = be_ref[jnp.maximum(i - 1, 0)]
    changed = (i == 0) | (be_ref[i] != prev)
    used = i < nu_ref[0]

    @pl.when(changed & used)
    def _():
        w13_scr[:, :D_EXPERT] = w1_ref[0].astype(BF16)
        w13_scr[:, D_EXPERT:] = w3_ref[0].astype(BF16)
        w2_scr[...] = w2_ref[0].astype(BF16)

    @pl.when(used)
    def _():
        x = jnp.concatenate(_unpack_bf16_pairs(xb_ref[...]), axis=1).astype(BF16)
        h = _dot(x, w13_scr[...])
        h1 = h[:, :D_EXPERT]
        h3 = h[:, D_EXPERT:]
        a = (_silu(h1) * h3).astype(BF16)
        yb_ref[...] = _pack_bf16_pairs(_dot(a, w2_scr[...]))

    @pl.when(jnp.logical_not(used))
    def _():
        yb_ref[...] = jnp.zeros_like(yb_ref)


def _ffn(blk_exp, n_used, xb, w1, w3, w2):
    n_rows = xb.shape[0]
    nblk = n_rows // MOE_BLOCK
    wmap = lambda i, be, nu: (be[i], 0, 0)
    grid_spec = pltpu.PrefetchScalarGridSpec(
        num_scalar_prefetch=2, grid=(nblk,),
        in_specs=[pl.BlockSpec((MOE_BLOCK, D_MODEL // 2), lambda i, be, nu: (i, 0)),
                  pl.BlockSpec((1, D_MODEL, D_EXPERT), wmap),
                  pl.BlockSpec((1, D_MODEL, D_EXPERT), wmap),
                  pl.BlockSpec((1, D_EXPERT, D_MODEL), wmap)],
        out_specs=pl.BlockSpec((MOE_BLOCK, D_MODEL // 2), lambda i, be, nu: (i, 0)),
        scratch_shapes=[pltpu.VMEM((D_MODEL, 2 * D_EXPERT), BF16),
                        pltpu.VMEM((D_EXPERT, D_MODEL), BF16)])
    return pl.pallas_call(
        _ffn_kernel, grid_spec=grid_spec,
        out_shape=jax.ShapeDtypeStruct((n_rows, D_MODEL // 2), U32),
        compiler_params=_cparams(("arbitrary",), 40),
        name="moe_ffn",
    )(blk_exp, n_used, xb, w1, w3, w2)


def _combine_kernel(d1_ref, d2_ref, d1n_ref, d2n_ref, yb_hbm, gs_ref, x1_ref, g2_ref, o_ref,
                    ya_scr, yb_scr, sem):
    tm = COMBINE_TM
    i = pl.program_id(0)
    slot = i % 2

    def issue(da_ref, db_ref, s):
        def body(g, carry):
            for u in range(ROW_UNROLL):
                tt = g * ROW_UNROLL + u
                pltpu.make_async_copy(yb_hbm.at[pl.ds(da_ref[0, 0, tt], 1)],
                                      ya_scr.at[s, pl.ds(tt, 1)], sem.at[s]).start()
                pltpu.make_async_copy(yb_hbm.at[pl.ds(db_ref[0, 0, tt], 1)],
                                      yb_scr.at[s, pl.ds(tt, 1)], sem.at[s]).start(priority=1)
            return carry

        lax.fori_loop(0, tm // ROW_UNROLL, body, 0)

    @pl.when(i == 0)
    def _():
        issue(d1_ref, d2_ref, 0)

    @pl.when(i + 1 < pl.num_programs(0))
    def _():
        issue(d1n_ref, d2n_ref, 1 - slot)

    pltpu.make_async_copy(yb_hbm.at[pl.ds(0, tm)], ya_scr.at[slot], sem.at[slot]).wait()
    pltpu.make_async_copy(yb_hbm.at[pl.ds(0, tm)], yb_scr.at[slot], sem.at[slot]).wait()
    ga = gs_ref[:, 0:1]
    gb = gs_ref[:, 1:2]
    a_lo, a_hi = _unpack_bf16_pairs(ya_scr[slot])
    b_lo, b_hi = _unpack_bf16_pairs(yb_scr[slot])
    half = D_MODEL // 2
    o_ref[:, :half] = x1_ref[:, :half] + g2_ref[0][:, :half] * (ga * a_lo + gb * b_lo)
    o_ref[:, half:] = x1_ref[:, half:] + g2_ref[0][:, half:] * (ga * a_hi + gb * b_hi)


def _combine(d1, d2, ybuf, gs, x1, mod3, seq_len):
    t = x1.shape[0]
    tm = COMBINE_TM
    nt = t // tm
    tiles_per_seq = seq_len // tm
    smem = pl.BlockSpec((1, 1, tm), lambda i: (i, 0, 0), memory_space=pltpu.SMEM)
    smem_next = pl.BlockSpec((1, 1, tm), lambda i: (jnp.minimum(i + 1, nt - 1), 0, 0), memory_space=pltpu.SMEM)
    return pl.pallas_call(
        _combine_kernel,
        grid=(nt,),
        in_specs=[smem, smem, smem_next, smem_next,
                  pl.BlockSpec(memory_space=pl.ANY),
                  pl.BlockSpec((tm, LANE), lambda i: (i, 0)),
                  pl.BlockSpec((tm, D_MODEL), lambda i: (i, 0)),
                  pl.BlockSpec((1, 1, D_MODEL), lambda i: ((i // tiles_per_seq) * N_MOD + 5, 0, 0))],
        out_specs=pl.BlockSpec((tm, D_MODEL), lambda i: (i, 0)),
        out_shape=jax.ShapeDtypeStruct((t, D_MODEL), F32),
        scratch_shapes=[pltpu.VMEM((2, tm, D_MODEL // 2), U32), pltpu.VMEM((2, tm, D_MODEL // 2), U32),
                        pltpu.SemaphoreType.DMA((2,))],
        compiler_params=_cparams(("arbitrary",), 32),
        name="moe_combine",
    )(d1, d2, d1, d2, ybuf, gs, x1, mod3)


def _rope_tables(seq_len):
    t = np.arange(seq_len)
    row = (t // GRID_W).astype(np.float32)
    colp = (t % GRID_W).astype(np.float32)
    half = SSD_STATE // 2
    inv = (ROPE_THETA ** (-np.arange(0, half, 2, dtype=np.float32) / half)).astype(np.float32)
    ar = (row[:, None] * inv).astype(np.float64)
    ac = (colp[:, None] * inv).astype(np.float64)
    cos_t = np.concatenate([np.cos(ar), np.cos(ar), np.cos(ac), np.cos(ac)], axis=-1)
    sin_t = np.concatenate([-np.sin(ar), np.sin(ar), -np.sin(ac), np.sin(ac)], axis=-1)
    return jnp.asarray(cos_t, F32), jnp.asarray(sin_t, F32)


def _scan_tables(rev):
    li = np.arange(SSD_CHUNK)[:, None]
    ui = np.arange(SSD_CHUNK)[None, :]
    tri = (ui >= li) if rev else (ui <= li)
    col = SSD_HEADS if rev else 0
    j = np.arange(LANE)[:, None]
    c = np.arange(D_SSD)[None, :]
    ee = j == col + c // SSD_HEADDIM
    return jnp.asarray(tri, BF16), jnp.asarray(ee, BF16)


def _pad_lanes(v, width=LANE):
    v = v.reshape(1, -1)
    return jnp.pad(v, ((0, 0), (0, width - v.shape[1])))


def kernel(x, c, ctx, c_ctx, w_mod, b_mod, norm1_w, w_in, conv_w, conv_b, a_log_f, a_log_b, dt_bias_f, dt_bias_b, d_skip, ssd_norm_w, q_norm_w, k_norm_w, rpb, w_br_ssd, w_br_na, w_out, norm2_w, w_grp, b_grp, w_rt, b_rt, w1, w3, w2):
    nb, seq_len, d = x.shape
    ctx_len = ctx.shape[1]
    t = nb * seq_len
    tc = nb * ctx_len
    assert w_mod.shape[0] == 1 and d == D_MODEL and nb <= 7
    assert seq_len % 256 == 0 and ctx_len % SSD_CHUNK == 0 and seq_len // GRID_W >= NA_KH

    cin = jnp.concatenate([c, c_ctx[None, :], jnp.zeros((8 - nb - 1, d), F32)], axis=0)
    mod = _modulation(cin, w_mod[0], b_mod[0])
    mod3 = mod.reshape(8 * N_MOD, 1, D_MODEL)

    o_dt = D_SSD + D_XBC
    o_qkv = o_dt + 2 * SSD_HEADS
    wi = w_in[0].astype(BF16)
    w_a = wi
    w_b = wi[:, o_qkv:]
    w_dtp = jnp.pad(wi[:, o_dt:o_qkv], ((0, 0), (0, LANE - 2 * SSD_HEADS)))
    n1w = norm1_w[0].reshape(1, D_MODEL)

    x2 = x.reshape(t, D_MODEL)
    ctx2 = ctx.reshape(tc, D_MODEL)
    tm_in = 2048 if seq_len % 2048 == 0 else 256
    tiles = seq_len // tm_in
    tn_in = 1024
    z_t = list(range(0, D_SSD // tn_in))
    xbc_t = list(range(D_SSD // tn_in, o_dt // tn_in))
    q_t, k_t, v_t = [0], [1], [2]
    g_t = [3, 4]
    big, dt_raw = _inproj(x2, mod3, lambda i: i // tiles, n1w, w_a, w_b, w_dtp, tm_in, tn_in,
                          xbc_t + z_t, g_t + q_t + k_t + v_t)
    tm_c = 1024 if tc % 1024 == 0 else SSD_CHUNK
    bigc, dtc_raw = _inproj(ctx2, mod3, lambda i: nb, n1w, w_a, w_b, w_dtp, tm_c, tn_in, xbc_t, k_t + v_t)
    ZCOL, GCOL, QCOL, KCOL, VCOL = 2, 3, 8, 9, 10
    KC_COL, VC_COL = 4, 5

    dtbias = _pad_lanes(jnp.concatenate([dt_bias_f[0], dt_bias_b[0]]))
    cw = conv_w[0]
    cbias = conv_b[0].reshape(1, D_XBC)
    cos_t, sin_t = _rope_tables(seq_len)
    xs, bc, dts = _ssd_prep(big, dt_raw, cw, cbias, dtbias, cos_t, sin_t, seq_len, 256)
    ctl = 256 if ctx_len % 256 == 0 else SSD_CHUNK
    ones_t = jnp.ones((ctx_len, LANE), F32)
    zeros_t = jnp.zeros((ctx_len, LANE), F32)
    xsc, bcc, dtsc = _ssd_prep(bigc, dtc_raw, cw, cbias, dtbias, ones_t, zeros_t, ctx_len, ctl)

    alog = _pad_lanes(jnp.concatenate([a_log_f[0], a_log_b[0]]))
    dskip = jnp.repeat(d_skip[0], SSD_HEADDIM).reshape(1, D_SSD)
    tri_f, ee_f = _scan_tables(False)
    tri_b, ee_b = _scan_tables(True)
    hcf = _ssd_state(xsc, bcc, dtsc, alog, tri_f, ee_f, nb, False)
    hcb = _ssd_state(xsc, bcc, dtsc, alog, tri_b, ee_b, nb, True)
    yf, yb = _ssd_scan(xs, bc, dts, alog, dskip, (tri_f, ee_f), (tri_b, ee_b), hcf, hcb, nb)

    qw = jnp.tile(q_norm_w[0], NA_HEADS).reshape(1, D_NA)
    kw = jnp.tile(k_norm_w[0], NA_HEADS).reshape(1, D_NA)
    gi = np.arange(LANE)
    gmat = jnp.asarray(((gi[:, None] // NA_HEADDIM) == (gi[None, :] // NA_HEADDIM)) * (1.0 / NA_HEADDIM), BF16)
    qn, kn = _na_prep(big, QCOL, KCOL, qw, kw, gmat, 512 if t % 512 == 0 else 256)
    kcn = _na_prep(bigc, None, KC_COL, qw, kw, gmat, ctx_len)
    bias_tab = _na_bias_table(rpb[0])
    y_na = _neigh_attention(qn, kn, big, kcn, bigc, bias_tab, nb, seq_len, ctx_len, VCOL, VC_COL)

    w_r = jnp.pad(jnp.concatenate([w_rt[0], w_grp[0]], axis=1), ((0, 0), (0, LANE - N_EXPERTS - N_GROUPS)))
    wrh = w_r.astype(BF16)
    wrl = (w_r - wrh.astype(F32)).astype(BF16)
    br = _pad_lanes(jnp.concatenate([b_rt[0], b_grp[0]]))
    nblk = (2 * t + N_EXPERTS * (MOE_BLOCK - 1) + MOE_BLOCK - 1) // MOE_BLOCK
    x1, h2p, logits, xb_zero = _merge(yf, yb, big, y_na, x2, mod3,
                             ssd_norm_w[0].reshape(1, D_SSD), norm2_w[0].reshape(1, D_MODEL),
                             w_br_ssd[0].astype(BF16), w_br_na[0].astype(BF16), w_out[0].astype(BF16),
                             wrh, wrl, br, seq_len, 512 if seq_len % 512 == 0 else 256, ZCOL, GCOL,
                                      nblk * MOE_BLOCK)

    si = np.arange(ROUTE_TM)
    stri = jnp.asarray(si[None, :] < si[:, None], BF16)
    ui = np.arange(LANE)
    utri = jnp.asarray(ui[:, None] <= ui[None, :], BF16)
    gs, rt, cnt = _route(logits, stri, utri)

    counts = cnt[0, :N_EXPERTS]
    pend = jnp.cumsum((counts + MOE_BLOCK - 1) // MOE_BLOCK * MOE_BLOCK)
    n_used = (pend[-1] // MOE_BLOCK).astype(I32).reshape(1)
    blk_row0 = jnp.arange(nblk, dtype=I32) * MOE_BLOCK
    blk_exp = jnp.minimum(jnp.sum((pend[None, :] <= blk_row0[:, None]).astype(I32), axis=1), N_EXPERTS - 1)

    def tok_tiles(row, tm):
        return rt[row].reshape(t // tm, 1, tm)

    xb = _dispatch(tok_tiles(2, DISPATCH_TM), tok_tiles(3, DISPATCH_TM), h2p, xb_zero)
    ybuf = _ffn(blk_exp, n_used, xb, w1[0], w3[0], w2[0])
    out = _combine(tok_tiles(2, COMBINE_TM), tok_tiles(3, COMBINE_TM), ybuf, gs, x1, mod3, seq_len)
    return out.reshape(nb, seq_len, D_MODEL)
```

```python
import functools
import math

import numpy as np

import jax
import jax.numpy as jnp
from jax import lax
from jax.experimental import pallas as pl
from jax.experimental.pallas import tpu as pltpu

F32 = jnp.float32
BF16 = jnp.bfloat16
I32 = jnp.int32
U32 = jnp.uint32

D_MODEL = 1024
GRID_W = 64
D_SSD = 2048
SSD_HEADDIM = 64
SSD_HEADS = 32
SSD_GROUPS = 8
HEADS_PER_GROUP = 4
SSD_STATE = 128
SSD_CHUNK = 128
CONV_K = 5
D_BC = SSD_GROUPS * SSD_STATE
D_XBC = D_SSD + 2 * D_BC
NA_HEADDIM = 64
NA_HEADS = 16
D_NA = 1024
NA_KH = 8
NA_KW = 16
ROPE_THETA = 10000.0
N_GROUPS = 4
EXPERTS_PER_GROUP = 8
N_EXPERTS = 32
D_EXPERT = 512
N_MOD = 6
EPS = 1e-6
NEG_BIG = -1e30

COLS_LAT = D_XBC + D_SSD + 2 * D_MODEL + 3 * D_NA
COLS_CTX = D_XBC + 2 * D_NA
LANE = 128

V7X_VMEM_BYTES = 64 * 1024 * 1024


def _cparams(sem, vmem_mb):
    return pltpu.CompilerParams(dimension_semantics=sem, vmem_limit_bytes=vmem_mb * 1024 * 1024)


def _sigmoid(x):
    return 1.0 / (1.0 + jnp.exp(-x))


def _silu(x):
    h = 0.5 * x
    return h * jnp.tanh(h) + h


def _split2(x):
    hi = x.astype(BF16)
    lo = (x - hi.astype(F32)).astype(BF16)
    return hi, lo


def _split3(x):
    hi = x.astype(BF16)
    r = x - hi.astype(F32)
    mid = r.astype(BF16)
    lo = (r - mid.astype(F32)).astype(BF16)
    return hi, mid, lo


def _pack_bf16_pairs(x):
    bits = pltpu.bitcast(x.astype(BF16).astype(F32), U32)
    n = x.shape[1] // 2
    return (bits[:, :n] >> 16) | bits[:, n:]


def _unpack_bf16_pairs(w):
    return pltpu.bitcast(w << 16, F32), pltpu.bitcast(w & jnp.uint32(0xFFFF0000), F32)


def _dot(a, b):
    return jnp.dot(a, b, preferred_element_type=F32)


def _dot_nt(a, b):
    return lax.dot_general(a, b, (((1,), (1,)), ((), ())), preferred_element_type=F32)


def _dot_tn(a, b):
    return lax.dot_general(a, b, (((0,), (0,)), ((), ())), preferred_element_type=F32)


def _mod_kernel(c_ref, w_ref, b_ref, o_ref):
    c = c_ref[...]
    a = _silu(c).astype(BF16)
    o_ref[...] = _dot(a, w_ref[...].astype(BF16)) + b_ref[...]


def _modulation(cin, w_mod, b_mod):
    n = w_mod.shape[1]
    tn = 1536
    return pl.pallas_call(
        _mod_kernel,
        grid=(n // tn,),
        in_specs=[pl.BlockSpec((8, D_MODEL), lambda j: (0, 0)),
                  pl.BlockSpec((D_MODEL, tn), lambda j: (0, j)),
                  pl.BlockSpec((1, tn), lambda j: (0, j))],
        out_specs=pl.BlockSpec((8, tn), lambda j: (0, j)),
        out_shape=jax.ShapeDtypeStruct((8, n), F32),
        compiler_params=_cparams(("arbitrary",), 40),
        name="modulation",
    )(cin, w_mod, b_mod.reshape(1, n))


def _inproj_kernel(x_ref, sh_ref, sc_ref, nw_ref, wa_ref, wb_ref, wdt_ref, o_ref, dt_ref, h_scr, *, tm, n_a):
    j = pl.program_id(1)

    @pl.when(j == 0)
    def _():
        scale = 1.0 + sc_ref[0]
        shift = sh_ref[0]
        nw = nw_ref[...]

        def body(r, carry):
            rows = pl.ds(pl.multiple_of(r * 128, 128), 128)
            x = x_ref[rows, :]
            ms = jnp.mean(x * x, axis=-1, keepdims=True)
            h = x * lax.rsqrt(ms + EPS) * nw * scale + shift
            h_scr[rows, :] = h.astype(BF16)
            return carry

        lax.fori_loop(0, tm // 128, body, 0)
        dt_ref[...] = _dot(h_scr[...], wdt_ref[...])

    @pl.when(j < n_a)
    def _():
        o_ref[...] = _dot(h_scr[...], wa_ref[...]).astype(BF16)

    @pl.when(j >= n_a)
    def _():
        o_ref[...] = _dot(h_scr[...], wb_ref[...]).astype(BF16)


def _inproj(x2, mod3, mod_row_fn, norm_w, wa, wb, wdt, tm, tn, a_tiles, b_tiles):
    t = x2.shape[0]
    n_a, n_b = len(a_tiles), len(b_tiles)
    a_tab = list(a_tiles) + [a_tiles[-1]] * n_b
    b_tab = [b_tiles[0]] * n_a + list(b_tiles)

    def pick(tab, j):
        idx = jnp.int32(tab[-1])
        for pos in range(len(tab) - 2, -1, -1):
            idx = jnp.where(j <= pos, tab[pos], idx)
        return idx

    kern = functools.partial(_inproj_kernel, tm=tm, n_a=n_a)
    return pl.pallas_call(
        kern,
        grid=(t // tm, n_a + n_b),
        in_specs=[pl.BlockSpec((tm, D_MODEL), lambda i, j: (i, 0)),
                  pl.BlockSpec((1, 1, D_MODEL), lambda i, j: (mod_row_fn(i) * N_MOD + 0, 0, 0)),
                  pl.BlockSpec((1, 1, D_MODEL), lambda i, j: (mod_row_fn(i) * N_MOD + 1, 0, 0)),
                  pl.BlockSpec((1, D_MODEL), lambda i, j: (0, 0)),
                  pl.BlockSpec((D_MODEL, tn), lambda i, j: (0, pick(a_tab, j))),
                  pl.BlockSpec((D_MODEL, tn), lambda i, j: (0, pick(b_tab, j))),
                  pl.BlockSpec((D_MODEL, LANE), lambda i, j: (0, 0))],
        out_specs=[pl.BlockSpec((tm, tn), lambda i, j: (i, j)),
                   pl.BlockSpec((tm, LANE), lambda i, j: (i, 0))],
        out_shape=[jax.ShapeDtypeStruct((t, (n_a + n_b) * tn), BF16),
                   jax.ShapeDtypeStruct((t, LANE), F32)],
        scratch_shapes=[pltpu.VMEM((tm, D_MODEL), BF16)],
        compiler_params=_cparams(("arbitrary", "arbitrary"), 52),
        name="inproj",
    )(x2, mod3, mod3, norm_w, wa, wb, wdt)


CONV_OFFSETS = (-2, -1, 1, 2)


def _ssd_prep_kernel(prev_ref, cur_ref, next_ref, dt_ref, cw_ref, cb_ref, dtbias_ref, cos_ref, sin_ref,
                     shift_ref, etop_ref, ebot_ref, xs_ref, bc_ref, dto_ref, *, tl, tiles_per_seq):
    i = pl.program_id(0)
    first = (i % tiles_per_seq) == 0
    last = (i % tiles_per_seq) == tiles_per_seq - 1
    lane = lax.broadcasted_iota(I32, (tl, LANE), 1)
    lo_half = ((lane // 32) % 2) == 0
    shift = shift_ref[...]
    etop = etop_ref[...]
    ebot = ebot_ref[...]
    cosv = cos_ref[...]
    sinv = sin_ref[...]
    cwid = 256
    for c in range(D_XBC // cwid):
        cs = slice(c * cwid, (c + 1) * cwid)
        w = cw_ref[:, cs]
        xc = cur_ref[:, cs]
        sh = _dot(shift, xc).reshape(tl // 8, len(CONV_OFFSETS), 8, cwid)
        tap = lambda k: sh[:, k].reshape(tl, cwid)
        acc = cb_ref[:, cs] + xc.astype(F32) * w[2:3, :]
        acc = acc + tap(0) * w[0:1, :] + tap(1) * w[1:2, :] + tap(2) * w[3:4, :] + tap(3) * w[4:5, :]
        top = _dot(etop, prev_ref[:, cs])
        bot = _dot(ebot, next_ref[:, cs])
        top_c = jnp.where(first, 0.0, top[0:8] * w[0:1, :] + top[8:16] * w[1:2, :])
        bot_c = jnp.where(last, 0.0, bot[0:8] * w[3:4, :] + bot[8:16] * w[4:5, :])
        acc = jnp.concatenate([acc[0:8] + top_c, acc[8:tl - 8], acc[tl - 8:tl] + bot_c], axis=0)
        y = _silu(acc)
        if c * cwid < D_SSD:
            xs_ref[:, cs] = y.astype(BF16)
        else:
            for g in range(cwid // LANE):
                yg = y[:, g * LANE:(g + 1) * LANE]
                partner = jnp.where(lo_half, pltpu.roll(yg, 96, 1), pltpu.roll(yg, 32, 1))
                o = yg * cosv + partner * sinv
                c0 = c * cwid - D_SSD + g * LANE
                bc_ref[:, c0:c0 + LANE] = o.astype(BF16)

    v = dt_ref[...] + dtbias_ref[...]
    dto_ref[...] = jnp.maximum(v, 0.0) + jnp.log(1.0 + jnp.exp(-jnp.abs(v)))


def _conv_shift_tables(tl):
    t = np.arange(tl)[:, None]
    u = np.arange(tl)[None, :]
    shift = np.stack([(u == t + off) for off in CONV_OFFSETS], axis=0).reshape(len(CONV_OFFSETS), tl // 8, 8, tl)
    shift = shift.transpose(1, 0, 2, 3).reshape(len(CONV_OFFSETS) * tl, tl)
    r = np.arange(16)[:, None]
    q = np.arange(16)[None, :]
    etop = ((r < 8) & (q == r + 14)) | ((r >= 8) & (q == r - 8 + 15))
    ebot = ((r < 8) & (q == r - 7)) | ((r >= 8) & (q == r - 8 - 6))
    return jnp.asarray(shift, BF16), jnp.asarray(etop, BF16), jnp.asarray(ebot, BF16)


def _ssd_prep(big, dt_raw, conv_w, conv_b, dtbias, cos_t, sin_t, seq_len, tl):
    t = big.shape[0]
    tps = seq_len // tl
    hb = tl // 16
    nhalo = t // 16
    kern = functools.partial(_ssd_prep_kernel, tl=tl, tiles_per_seq=tps)
    shift, etop, ebot = _conv_shift_tables(tl)
    const = lambda i: (0, 0)
    return pl.pallas_call(
        kern,
        grid=(t // tl,),
        in_specs=[pl.BlockSpec((16, D_XBC), lambda i: (jnp.maximum(i * hb - 1, 0), 0)),
                  pl.BlockSpec((tl, D_XBC), lambda i: (i, 0)),
                  pl.BlockSpec((16, D_XBC), lambda i: (jnp.minimum((i + 1) * hb, nhalo - 1), 0)),
                  pl.BlockSpec((tl, LANE), lambda i: (i, 0)),
                  pl.BlockSpec((CONV_K, D_XBC), const),
                  pl.BlockSpec((1, D_XBC), const),
                  pl.BlockSpec((1, LANE), const),
                  pl.BlockSpec((tl, LANE), lambda i: (i % tps, 0)),
                  pl.BlockSpec((tl, LANE), lambda i: (i % tps, 0)),
                  pl.BlockSpec((len(CONV_OFFSETS) * tl, tl), const),
                  pl.BlockSpec((16, 16), const),
                  pl.BlockSpec((16, 16), const)],
        out_specs=[pl.BlockSpec((tl, D_SSD), lambda i: (i, 0)),
                   pl.BlockSpec((tl, 2 * D_BC), lambda i: (i, 0)),
                   pl.BlockSpec((tl, LANE), lambda i: (i, 0))],
        out_shape=[jax.ShapeDtypeStruct((t, D_SSD), BF16),
                   jax.ShapeDtypeStruct((t, 2 * D_BC), BF16),
                   jax.ShapeDtypeStruct((t, LANE), F32)],
        compiler_params=_cparams(("arbitrary",), 40),
        name="ssd_prep",
    )(big, big, big, dt_raw, conv_w, conv_b, dtbias, cos_t, sin_t, shift, etop, ebot)


GROUP_W = HEADS_PER_GROUP * SSD_HEADDIM


def _scan_decays(dt_ref, alog_ref, tri_ref, ee_ref, rev):
    edge = 0 if rev else SSD_CHUNK - 1
    a_row = -jnp.exp(alog_ref[...])
    dt = dt_ref[...]
    trib = tri_ref[...]
    d_hi, d_mid, d_lo = _split3(dt * a_row)
    cum = _dot(trib, d_hi) + _dot(trib, d_mid) + _dot(trib, d_lo)
    tot = cum[edge:edge + 1, :]
    ee = ee_ref[...]
    w_heads = (jnp.exp(tot - cum) * dt).astype(BF16)
    t_hi, t_lo = _split2(jnp.broadcast_to(jnp.exp(tot), (8, LANE)))
    etot = (_dot(t_hi, ee) + _dot(t_lo, ee))[0:1, :]
    return dt, trib, cum, w_heads, etot


def _spread(per_head, ee_ref, g):
    return _dot(per_head, ee_ref[:, g * GROUP_W:(g + 1) * GROUP_W])


def _ssd_state_kernel(xs_ref, bc_ref, dt_ref, alog_ref, tri_ref, ee_ref, hfin_ref, h_scr, *, rev, nc):
    k = pl.program_id(1)

    @pl.when(k == 0)
    def _():
        h_scr[...] = jnp.zeros_like(h_scr)

    _, _, _, w_heads, etot = _scan_decays(dt_ref, alog_ref, tri_ref, ee_ref, rev)
    for g in range(SSD_GROUPS):
        gs = slice(g * GROUP_W, (g + 1) * GROUP_W)
        xw = (xs_ref[:, gs].astype(F32) * _spread(w_heads, ee_ref, g)).astype(BF16)
        h_scr[:, gs] = etot[:, gs] * h_scr[:, gs] + _dot_tn(bc_ref[:, g * SSD_STATE:(g + 1) * SSD_STATE], xw)

    @pl.when(k == nc - 1)
    def _():
        hfin_ref[0] = h_scr[...]


def _scan_chunk_stages(xs_ref, bc_ref, dt_ref, alog_ref, dskip_ref, tri_ref, ee_ref, y_ref, h_scr, rev):
    L = SSD_CHUNK
    col = SSD_HEADS if rev else 0
    dt, trib, cum, w_heads, etot = _scan_decays(dt_ref, alog_ref, tri_ref, ee_ref, rev)
    tri = trib > 0.5
    cum_t = cum.T
    dt_t = dt.T
    o_heads = jnp.exp(cum).astype(BF16)

    gw = GROUP_W
    lane_head = lax.broadcasted_iota(I32, (L, gw), 1) // SSD_HEADDIM

    def b_of(g):
        return bc_ref[:, g * SSD_STATE:(g + 1) * SSD_STATE]

    def c_of(g):
        return bc_ref[:, D_BC + g * SSD_STATE:D_BC + (g + 1) * SSD_STATE]

    def operands(g, cb):
        gs = slice(g * gw, (g + 1) * gw)
        xs_g = xs_ref[:, gs]
        ms = []
        for r in range(HEADS_PER_GROUP):
            hh = col + g * HEADS_PER_GROUP + r
            seg = cum[:, hh:hh + 1] - cum_t[hh:hh + 1, :]
            lm = jnp.exp(jnp.where(tri, seg, NEG_BIG)) * dt_t[hh:hh + 1, :]
            ms.append((cb * lm).astype(BF16))
        mcat = jnp.concatenate(ms, axis=1)
        zero = jnp.zeros_like(xs_g)
        bd = jnp.concatenate([jnp.where(lane_head == r, xs_g, zero) for r in range(HEADS_PER_GROUP)], axis=0)
        xs_f = xs_g.astype(F32)
        xw = (xs_f * _spread(w_heads, ee_ref, g)).astype(BF16)
        return mcat, bd, xs_f, xw, _spread(o_heads, ee_ref, g)

    def cb_of(g):
        return _dot_nt(c_of(g), b_of(g))

    yield
    cb_next = cb_of(0)
    yield
    nxt = operands(0, cb_next)
    cb_next = cb_of(1)
    for g in range(SSD_GROUPS):
        yield
        gs = slice(g * gw, (g + 1) * gw)
        mcat, bd, xs_f, xw, oscale_g = nxt
        h_g = h_scr[:, gs]
        y_diag = _dot(mcat, bd)
        y_off = _dot(c_of(g), h_g.astype(BF16))
        loc = _dot_tn(b_of(g), xw)
        if g + 1 < SSD_GROUPS:
            nxt = operands(g + 1, cb_next)
            if g + 2 < SSD_GROUPS:
                cb_next = cb_of(g + 2)
        y = y_diag + oscale_g * y_off
        if not rev:
            y = y + xs_f * dskip_ref[:, gs]
        y_ref[:, gs] = y.astype(BF16)
        h_scr[:, gs] = etot[:, gs] * h_g + loc


def _ssd_scan_kernel(xsf_ref, bcf_ref, dtf_ref, xsb_ref, bcb_ref, dtb_ref, alog_ref, dskip_ref,
                     trif_ref, eef_ref, trib_ref, eeb_ref, h0f_ref, h0b_ref, yf_ref, yb_ref, hf_scr, hb_scr):
    @pl.when(pl.program_id(1) == 0)
    def _():
        hf_scr[...] = h0f_ref[0]
        hb_scr[...] = h0b_ref[0]

    chains = [_scan_chunk_stages(xsf_ref, bcf_ref, dtf_ref, alog_ref, dskip_ref, trif_ref, eef_ref,
                                 yf_ref, hf_scr, False),
              _scan_chunk_stages(xsb_ref, bcb_ref, dtb_ref, alog_ref, dskip_ref, trib_ref, eeb_ref,
                                 yb_ref, hb_scr, True)]
    while chains:
        for ch in list(chains):
            try:
                next(ch)
            except StopIteration:
                chains.remove(ch)


def _scan_rowmap(nc, rev):
    def rowmap(b, k):
        c = (nc - 1 - k) if rev else k
        return (b * nc + c, 0)
    return rowmap


def _ssd_state(xs, bc, dt, alog_row, tri, ee, nb, rev):
    t = xs.shape[0]
    nc = t // nb // SSD_CHUNK
    rowmap = _scan_rowmap(nc, rev)
    const2 = lambda b, k: (0, 0)
    return pl.pallas_call(
        functools.partial(_ssd_state_kernel, rev=rev, nc=nc),
        grid=(nb, nc),
        in_specs=[pl.BlockSpec((SSD_CHUNK, D_SSD), rowmap),
                  pl.BlockSpec((SSD_CHUNK, D_BC), rowmap),
                  pl.BlockSpec((SSD_CHUNK, LANE), rowmap),
                  pl.BlockSpec((1, LANE), const2),
                  pl.BlockSpec((SSD_CHUNK, SSD_CHUNK), const2),
                  pl.BlockSpec((LANE, D_SSD), const2)],
        out_specs=pl.BlockSpec((1, SSD_STATE, D_SSD), lambda b, k: (b, 0, 0)),
        out_shape=jax.ShapeDtypeStruct((nb, SSD_STATE, D_SSD), F32),
        scratch_shapes=[pltpu.VMEM((SSD_STATE, D_SSD), F32)],
        compiler_params=_cparams(("arbitrary", "arbitrary"), 40),
        name="ssd_state_bwd" if rev else "ssd_state_fwd",
    )(xs, bc, dt, alog_row, tri, ee)


def _ssd_scan(xs, bc, dt, alog_row, dskip_row, tables_f, tables_b, h0f, h0b, nb):
    t = xs.shape[0]
    nc = t // nb // SSD_CHUNK
    fmap = _scan_rowmap(nc, False)
    bmap = _scan_rowmap(nc, True)
    const2 = lambda b, k: (0, 0)
    chunk = lambda width, m: pl.BlockSpec((SSD_CHUNK, width), m)
    tri_spec = pl.BlockSpec((SSD_CHUNK, SSD_CHUNK), const2)
    ee_spec = pl.BlockSpec((LANE, D_SSD), const2)
    state = pl.BlockSpec((1, SSD_STATE, D_SSD), lambda b, k: (b, 0, 0))
    return pl.pallas_call(
        _ssd_scan_kernel,
        grid=(nb, nc),
        in_specs=[chunk(D_SSD, fmap), chunk(2 * D_BC, fmap), chunk(LANE, fmap),
                  chunk(D_SSD, bmap), chunk(2 * D_BC, bmap), chunk(LANE, bmap),
                  pl.BlockSpec((1, LANE), const2),
                  pl.BlockSpec((1, D_SSD), const2),
                  tri_spec, ee_spec, tri_spec, ee_spec, state, state],
        out_specs=[chunk(D_SSD, fmap), chunk(D_SSD, bmap)],
        out_shape=[jax.ShapeDtypeStruct((t, D_SSD), BF16)] * 2,
        scratch_shapes=[pltpu.VMEM((SSD_STATE, D_SSD), F32)] * 2,
        compiler_params=_cparams(("arbitrary", "arbitrary"), 48),
        name="ssd_scan",
    )(xs, bc, dt, xs, bc, dt, alog_row, dskip_row, *tables_f, *tables_b, h0f, h0b)


def _headnorm_cols(src_ref, w_ref, g, dst_ref, scale, transposed=False):
    for c in range(D_NA // LANE):
        cs = slice(c * LANE, (c + 1) * LANE)
        x = src_ref[:, cs].astype(F32)
        ms = _dot((x * x).astype(BF16), g)
        y = x * lax.rsqrt(ms + EPS) * w_ref[:, cs]
        if scale is not None:
            y = y * scale
        if transposed:
            dst_ref[cs, :] = y.T.astype(BF16)
        else:
            dst_ref[:, cs] = y.astype(BF16)


def _na_prep_qk_kernel(q_ref, k_ref, qw_ref, kw_ref, g_ref, qo_ref, ko_ref):
    g = g_ref[...]
    _headnorm_cols(q_ref, qw_ref, g, qo_ref, NA_HEADDIM ** -0.5)
    _headnorm_cols(k_ref, kw_ref, g, ko_ref, None)


def _na_prep_k_kernel(k_ref, kw_ref, g_ref, ko_ref):
    _headnorm_cols(k_ref, kw_ref, g_ref[...], ko_ref, None, transposed=True)


def _na_prep(big, qcol, kcol, qw, kw, gmat, tm):
    t = big.shape[0]
    blk = lambda cidx: pl.BlockSpec((tm, D_NA), lambda i: (i, cidx))
    row = pl.BlockSpec((1, D_NA), lambda i: (0, 0))
    gspec = pl.BlockSpec((LANE, LANE), lambda i: (0, 0))
    out = pl.BlockSpec((tm, D_NA), lambda i: (i, 0))
    if qcol is None:
        return pl.pallas_call(
            _na_prep_k_kernel, grid=(t // tm,),
            in_specs=[blk(kcol), row, gspec], out_specs=pl.BlockSpec((D_NA, tm), lambda i: (i, 0)),
            out_shape=jax.ShapeDtypeStruct((t // tm * D_NA, tm), BF16),
            compiler_params=_cparams(("arbitrary",), 32), name="na_prep_ctx",
        )(big, kw, gmat)
    return pl.pallas_call(
        _na_prep_qk_kernel, grid=(t // tm,),
        in_specs=[blk(qcol), blk(kcol), row, row, gspec], out_specs=[out, out],
        out_shape=[jax.ShapeDtypeStruct((t, D_NA), BF16)] * 2,
        compiler_params=_cparams(("arbitrary",), 32), name="na_prep",
    )(big, big, qw, kw, gmat)


NA_LOOKAHEAD = 1


NA_ROWS_PER_STEP = 4


def _na_row_stages(r, qrow, q_ref, k_ref, v_ref, kc_ref, vc_ref, bias_ref, o_ref, rows):
    rs = jnp.clip(r - NA_KH // 2, 0, rows - NA_KH)
    start = pl.multiple_of(rs * GRID_W, GRID_W)
    nk = NA_KH * GRID_W
    lane = lax.broadcasted_iota(I32, (GRID_W, LANE), 1)
    first_head = lane < NA_HEADDIM
    npair = NA_HEADS // 2
    qrows = slice(qrow, qrow + GRID_W)

    def scores(j):
        cs = slice(j * LANE, (j + 1) * LANE)
        qp = q_ref[qrows, cs]
        zero = jnp.zeros_like(qp)
        qs = jnp.concatenate([jnp.where(first_head, qp, zero), jnp.where(first_head, zero, qp)], axis=0)
        kb = k_ref[pl.ds(start, nk), cs]
        s_loc = _dot_nt(qs, kb) + bias_ref[0, j * LANE:(j + 1) * LANE, :]
        s_ctx = _dot(qs, kc_ref[cs, :])
        return s_loc, s_ctx

    pending = [scores(j) for j in range(NA_LOOKAHEAD)]
    for j in range(npair):
        yield
        cs = slice(j * LANE, (j + 1) * LANE)
        s_loc, s_ctx = pending.pop(0)
        if j + NA_LOOKAHEAD < npair:
            pending.append(scores(j + NA_LOOKAHEAD))
        vb = v_ref[pl.ds(start, nk), cs]
        m = jnp.maximum(jnp.max(s_loc, axis=-1, keepdims=True), jnp.max(s_ctx, axis=-1, keepdims=True))
        p_loc = jnp.exp(s_loc - m)
        p_ctx = jnp.exp(s_ctx - m)
        den = jnp.sum(p_loc, axis=-1, keepdims=True) + jnp.sum(p_ctx, axis=-1, keepdims=True)
        o = _dot(p_loc.astype(BF16), vb) + _dot(p_ctx.astype(BF16), vc_ref[:, cs])
        o = o / den
        o_ref[qrows, cs] = jnp.where(first_head, o[:GRID_W], o[GRID_W:]).astype(BF16)


def _na_kernel(q_ref, k_ref, v_ref, kc_ref, vc_ref, *rest, rows):
    bias_refs, o_ref = rest[:NA_ROWS_PER_STEP], rest[NA_ROWS_PER_STEP]
    r0 = pl.program_id(1) * NA_ROWS_PER_STEP
    chains = [_na_row_stages(r0 + u, u * GRID_W, q_ref, k_ref, v_ref, kc_ref, vc_ref, bias_refs[u], o_ref, rows)
              for u in range(NA_ROWS_PER_STEP)]
    while chains:
        for ch in list(chains):
            try:
                next(ch)
            except StopIteration:
                chains.remove(ch)


def _neigh_attention(qn, kn, big, kcn, bigc, bias_tab, nb, seq_len, ctx_len, vcol, vccol):
    t = qn.shape[0]
    rows = seq_len // GRID_W
    rps = NA_ROWS_PER_STEP
    steps = rows // rps
    kern = functools.partial(_na_kernel, rows=rows)

    def pat(u):
        def index_map(b, s):
            r = s * rps + u
            return (r - jnp.clip(r - NA_KH // 2, 0, rows - NA_KH), 0, 0)
        return index_map

    bias_specs = [pl.BlockSpec((1, NA_HEADS * GRID_W, NA_KH * GRID_W), pat(u)) for u in range(rps)]
    return pl.pallas_call(
        kern,
        grid=(nb, steps),
        in_specs=[pl.BlockSpec((rps * GRID_W, D_NA), lambda b, s: (b * steps + s, 0)),
                  pl.BlockSpec((seq_len, D_NA), lambda b, s: (b, 0), pipeline_mode=pl.Buffered(1)),
                  pl.BlockSpec((seq_len, D_NA), lambda b, s: (b, vcol), pipeline_mode=pl.Buffered(1)),
                  pl.BlockSpec((D_NA, ctx_len), lambda b, s: (b, 0), pipeline_mode=pl.Buffered(1)),
                  pl.BlockSpec((ctx_len, D_NA), lambda b, s: (b, vccol), pipeline_mode=pl.Buffered(1))] + bias_specs,
        out_specs=pl.BlockSpec((rps * GRID_W, D_NA), lambda b, s: (b * steps + s, 0)),
        out_shape=jax.ShapeDtypeStruct((t, D_NA), BF16),
        compiler_params=_cparams(("arbitrary", "arbitrary"), 48),
        name="neigh_attention",
    )(qn, kn, big, kcn, bigc, *([bias_tab] * rps))


def _na_bias_kernel(rpb_ref, o_ref):
    rp = rpb_ref[0]
    r64 = pltpu.roll(rp, GRID_W, 1)
    c = lax.broadcasted_iota(I32, (GRID_W, LANE), 0)
    kc = lax.broadcasted_iota(I32, (GRID_W, LANE), 1) % GRID_W
    cs = jnp.clip(c - NA_KW // 2, 0, GRID_W - NA_KW)
    valid = (kc >= cs) & (kc < cs + NA_KW)
    pair = []
    for d in range(2 * NA_KH - 2):
        vec = rp[d:d + 1, :] + r64[d + 1:d + 2, :]
        w = pltpu.roll(jnp.broadcast_to(vec, (GRID_W, LANE)), LANE - (NA_KW - 1), 1, stride=1, stride_axis=0)
        pair.append(jnp.where(valid, w, NEG_BIG))
    for p in range(NA_KH):
        for ii in range(NA_KH // 2):
            o_ref[p, :, ii * LANE:(ii + 1) * LANE] = pair[2 * ii - p + NA_KH - 1]


def _na_bias_table(rpb):
    rp = jnp.pad(rpb, ((0, 0), (0, 1), (0, LANE - (2 * NA_KW - 1))))
    return pl.pallas_call(
        _na_bias_kernel,
        grid=(NA_HEADS,),
        in_specs=[pl.BlockSpec((1, 2 * NA_KH, LANE), lambda h: (h, 0, 0))],
        out_specs=pl.BlockSpec((NA_KH, GRID_W, NA_KH * GRID_W), lambda h: (0, h, 0)),
        out_shape=jax.ShapeDtypeStruct((NA_KH, NA_HEADS * GRID_W, NA_KH * GRID_W), F32),
        compiler_params=_cparams(("arbitrary",), 32),
        name="na_bias",
    )(rp)


MERGE_SUB = 128


def _merge_kernel(yf_ref, yb_ref, z_ref, gt_ref, yna_ref, x_ref, g1_ref, sh2_ref, sc2_ref, snw_ref, n2w_ref,
                  wbs_ref, wbn_ref, wo_ref, wrh_ref, wrl_ref, br_ref, x1_ref, h2p_ref, lg_ref, xbz_ref):
    xbz_ref[...] = jnp.zeros_like(xbz_ref)
    tm = x_ref.shape[0]
    subs = [slice(r * MERGE_SUB, (r + 1) * MERGE_SUB) for r in range(tm // MERGE_SUB)]
    half = D_MODEL // 2
    yn = []
    for rs in subs:
        z = z_ref[rs, :].astype(F32)
        y = (yf_ref[rs, :] + yb_ref[rs, :]).astype(F32) * _silu(z)
        ms = jnp.mean(y * y, axis=-1, keepdims=True)
        yn.append((y * lax.rsqrt(ms + EPS) * snw_ref[...]).astype(BF16))
    ab = [(_dot(yn[r], wbs_ref[...]), _dot(yna_ref[rs, :], wbn_ref[...])) for r, rs in enumerate(subs)]
    merged = []
    for r, rs in enumerate(subs):
        g_ssd = gt_ref[rs, :D_MODEL].astype(F32)
        g_na = gt_ref[rs, D_MODEL:].astype(F32)
        merged.append((_sigmoid(g_ssd) * ab[r][0] + _sigmoid(g_na) * ab[r][1]).astype(BF16))
    mo = [_dot(m, wo_ref[...]) for m in merged]
    hs = []
    for r, rs in enumerate(subs):
        x1 = x_ref[rs, :] + g1_ref[0] * mo[r]
        x1_ref[rs, :] = x1
        ms2 = jnp.mean(x1 * x1, axis=-1, keepdims=True)
        h2 = x1 * lax.rsqrt(ms2 + EPS) * n2w_ref[...] * (1.0 + sc2_ref[0]) + sh2_ref[0]
        h_hi, h_lo = _split2(h2)
        hs.append((h_hi, h_lo))
        h2p_ref[rs, :] = _pack_bf16_pairs(h2)
    for r, rs in enumerate(subs):
        h_hi, h_lo = hs[r]
        lg_ref[rs, :] = (_dot(h_hi, wrh_ref[...]) + _dot(h_lo, wrh_ref[...]) + _dot(h_hi, wrl_ref[...])) + br_ref[...]


def _merge(yf, yb, big, yna, x2, mod3, snw, n2w, wbs, wbn, wo, wrh, wrl, br, seq_len, tm, zcol, gcol, n_rows):
    t = x2.shape[0]
    tiles_per_seq = seq_len // tm
    zrows = n_rows // (t // tm)
    assert zrows * (t // tm) == n_rows and zrows % 8 == 0
    modspec = lambda kidx: pl.BlockSpec((1, 1, D_MODEL), lambda i: ((i // tiles_per_seq) * N_MOD + kidx, 0, 0))
    full = lambda shp: pl.BlockSpec(shp, lambda i: (0,) * len(shp), pipeline_mode=pl.Buffered(1))
    return pl.pallas_call(
        _merge_kernel,
        grid=(t // tm,),
        in_specs=[pl.BlockSpec((tm, D_SSD), lambda i: (i, 0)),
                  pl.BlockSpec((tm, D_SSD), lambda i: (i, 0)),
                  pl.BlockSpec((tm, D_SSD), lambda i: (i, zcol)),
                  pl.BlockSpec((tm, 2 * D_MODEL), lambda i: (i, gcol)),
                  pl.BlockSpec((tm, D_NA), lambda i: (i, 0)),
                  pl.BlockSpec((tm, D_MODEL), lambda i: (i, 0)),
                  modspec(2), modspec(3), modspec(4),
                  full((1, D_SSD)), full((1, D_MODEL)),
                  full((D_SSD, D_MODEL)), full((D_NA, D_MODEL)), full((D_MODEL, D_MODEL)),
                  full((D_MODEL, LANE)), full((D_MODEL, LANE)), full((1, LANE))],
        out_specs=[pl.BlockSpec((tm, D_MODEL), lambda i: (i, 0)),
                   pl.BlockSpec((tm, D_MODEL // 2), lambda i: (i, 0)),
                   pl.BlockSpec((tm, LANE), lambda i: (i, 0)),
                   pl.BlockSpec((zrows, D_MODEL // 2), lambda i: (i, 0))],
        out_shape=[jax.ShapeDtypeStruct((t, D_MODEL), F32),
                   jax.ShapeDtypeStruct((t, D_MODEL // 2), U32),
                   jax.ShapeDtypeStruct((t, LANE), F32),
                   jax.ShapeDtypeStruct((n_rows, D_MODEL // 2), U32)],
        compiler_params=_cparams(("arbitrary",), 60),
        name="merge",
    )(yf, yb, big, big, yna, x2, mod3, mod3, mod3, snw, n2w, wbs, wbn, wo, wrh, wrl, br)


ROUTE_TM = 512
GRP_LANE0 = N_EXPERTS


def _route_topk(lg):
    tm = lg.shape[0]
    lane = lax.broadcasted_iota(I32, (tm, LANE), 1)
    neg_inf = jnp.float32(-jnp.inf)
    big_lane = jnp.int32(4 * LANE)
    is_grp = (lane >= GRP_LANE0) & (lane < GRP_LANE0 + N_GROUPS)
    gl = jnp.where(is_grp, lg, neg_inf)
    gmax = jnp.max(gl, axis=-1, keepdims=True)
    grp = jnp.min(jnp.where(gl == gmax, lane, big_lane), axis=-1, keepdims=True) - GRP_LANE0
    psum = jnp.sum(jnp.where(is_grp, jnp.exp(lg - gmax), 0.0), axis=-1, keepdims=True)
    p_grp = 1.0 / psum
    in_g = (lane < N_EXPERTS) & ((lane // EXPERTS_PER_GROUP) == grp)
    el = jnp.where(in_g, lg, neg_inf)
    v1 = jnp.max(el, axis=-1, keepdims=True)
    i1 = jnp.min(jnp.where(el == v1, lane, big_lane), axis=-1, keepdims=True)
    el2 = jnp.where(lane == i1, neg_inf, el)
    v2 = jnp.max(el2, axis=-1, keepdims=True)
    i2 = jnp.min(jnp.where(el2 == v2, lane, big_lane), axis=-1, keepdims=True)
    tt = jnp.exp(v2 - v1)
    den = 1.0 + tt
    ga = p_grp / den
    gb = p_grp * tt / den

    return lane, i1, i2, ga, gb


def _route_kernel(lg_ref, stri_ref, utri_ref, gs_ref, rt_ref, cnt_ref, run_scr, slab_scr):
    p = pl.program_id(0)
    i = pl.program_id(1)
    tm = ROUTE_TM
    rows = pl.ds(pl.multiple_of(i * tm, tm), tm)
    lane = lax.broadcasted_iota(I32, (tm, LANE), 1)

    @pl.when((p == 0) & (i == 0))
    def _():
        run_scr[...] = jnp.zeros_like(run_scr)

    @pl.when(p == 0)
    def _():
        _, i1, i2, ga, gb = _route_topk(lg_ref[...])
        onehot = jnp.where((lane == i1) | (lane == i2), 1.0, 0.0)
        run_scr[...] = run_scr[...] + jnp.sum(onehot, axis=0, keepdims=True)
        cnt_ref[...] = run_scr[...].astype(I32)
        slab = jnp.where(lane == 0, ga, 0.0)
        slab = jnp.where(lane == 1, gb, slab)
        slab = jnp.where(lane == 2, i1.astype(F32), slab)
        slab = jnp.where(lane == 3, i2.astype(F32), slab)
        slab_scr[rows, :] = slab

    @pl.when((p == 1) & (i == 0))
    def _():
        lane1 = lax.broadcasted_iota(I32, (8, LANE), 1)
        cnt = jnp.broadcast_to(run_scr[...], (8, LANE))
        blocks = jnp.where(lane1 < N_EXPERTS, jnp.floor((cnt + (MOE_BLOCK - 1)) * (1.0 / MOE_BLOCK)), 0.0)
        ends = _dot(blocks.astype(BF16), utri_ref[...])
        run_scr[...] = ((ends - blocks) * MOE_BLOCK)[0:1, :]

    @pl.when(p == 1)
    def _():
        slab = slab_scr[rows, :]
        lane_f = lane.astype(F32)
        sel1 = lane_f == slab[:, 2:3]
        sel2 = lane_f == slab[:, 3:4]
        onehot = jnp.where(sel1 | sel2, 1.0, 0.0)
        pos = _dot(stri_ref[...], onehot.astype(BF16)) + run_scr[...]
        d1 = jnp.sum(jnp.where(sel1, pos, 0.0), axis=-1, keepdims=True)
        d2 = jnp.sum(jnp.where(sel2, pos, 0.0), axis=-1, keepdims=True)
        run_scr[...] = run_scr[...] + jnp.sum(onehot, axis=0, keepdims=True)
        out = jnp.where(lane == 2, d1, jnp.where(lane == 3, d2, slab))
        gs_ref[...] = out
        for q in range(tm // LANE):
            blk_t = out[q * LANE:(q + 1) * LANE, :].T
            rt_ref[:, q * LANE:(q + 1) * LANE] = blk_t[0:8, :].astype(I32)


def _route(logits, stri, utri):
    t = logits.shape[0]
    tm = ROUTE_TM
    nt = t // tm
    return pl.pallas_call(
        _route_kernel,
        grid=(2, nt),
        in_specs=[pl.BlockSpec((tm, LANE), lambda p, i: (i * (1 - p) + (nt - 1) * p, 0)),
                  pl.BlockSpec((tm, tm), lambda p, i: (0, 0)),
                  pl.BlockSpec((LANE, LANE), lambda p, i: (0, 0))],
        out_specs=[pl.BlockSpec((tm, LANE), lambda p, i: (p * i, 0)),
                   pl.BlockSpec((8, tm), lambda p, i: (0, p * i)),
                   pl.BlockSpec((1, LANE), lambda p, i: (0, 0))],
        out_shape=[jax.ShapeDtypeStruct((t, LANE), F32),
                   jax.ShapeDtypeStruct((8, t), I32),
                   jax.ShapeDtypeStruct((1, LANE), I32)],
        scratch_shapes=[pltpu.VMEM((1, LANE), F32), pltpu.VMEM((t, LANE), F32)],
        compiler_params=_cparams(("arbitrary", "arbitrary"), 40),
        name="route",
    )(logits, stri, utri)


MOE_BLOCK = 512
DISPATCH_TM = 1024
COMBINE_TM = 512


ROW_UNROLL = 16


DISPATCH_BUFS = 3


def _dispatch_kernel(d1_ref, d2_ref, h_hbm, xb_in_hbm, xb_hbm, hbuf, load_sem, row_sem):
    del xb_in_hbm
    tm = DISPATCH_TM
    i = pl.program_id(0)
    last = pl.num_programs(0) - 1
    slot = i % DISPATCH_BUFS
    par = i % 2

    def load(tile, s):
        return pltpu.make_async_copy(h_hbm.at[pl.ds(tile * tm, tm)], hbuf.at[s], load_sem.at[s])

    def wait_rows(p):
        for _ in range(2):
            pltpu.make_async_copy(hbuf.at[0], xb_hbm.at[pl.ds(0, tm)], row_sem.at[p]).wait()

    @pl.when(i == 0)
    def _():
        load(0, 0).start()

    @pl.when(i < last)
    def _():
        load(i + 1, (i + 1) % DISPATCH_BUFS).start()

    load(i, slot).wait()

    def body(g, carry):
        for u in range(ROW_UNROLL):
            tt = g * ROW_UNROLL + u
            src = hbuf.at[slot, pl.ds(tt, 1)]
            pltpu.make_async_copy(src, xb_hbm.at[pl.ds(d1_ref[0, 0, tt], 1)], row_sem.at[par]).start()
            pltpu.make_async_copy(src, xb_hbm.at[pl.ds(d2_ref[0, 0, tt], 1)], row_sem.at[par]).start(priority=1)
        return carry

    lax.fori_loop(0, tm // ROW_UNROLL, body, 0)

    @pl.when(i > 0)
    def _():
        wait_rows(1 - par)

    @pl.when(i == last)
    def _():
        wait_rows(par)


def _dispatch(d1, d2, h2p, xb0):
    t = h2p.shape[0]
    tm = DISPATCH_TM
    n_rows, w = xb0.shape
    smem = pl.BlockSpec((1, 1, tm), lambda i: (i, 0, 0), memory_space=pltpu.SMEM)
    anyspec = pl.BlockSpec(memory_space=pl.ANY)
    return pl.pallas_call(
        _dispatch_kernel,
        grid=(t // tm,),
        in_specs=[smem, smem, anyspec, anyspec],
        out_specs=anyspec,
        out_shape=jax.ShapeDtypeStruct((n_rows, w), U32),
        scratch_shapes=[pltpu.VMEM((DISPATCH_BUFS, tm, w), U32),
                        pltpu.SemaphoreType.DMA((DISPATCH_BUFS,)), pltpu.SemaphoreType.DMA((2,))],
        input_output_aliases={3: 0},
        compiler_params=pltpu.CompilerParams(dimension_semantics=("arbitrary",), has_side_effects=True),
        name="moe_dispatch",
    )(d1, d2, h2p, xb0)


def _ffn_kernel(be_ref, nu_ref, xb_ref, w1_ref, w3_ref, w2_ref, yb_ref, w13_scr, w2_scr):
    i = pl.program_id(0)
    prev = be_ref[jnp.maximum(i - 1, 0)]
    changed = (i == 0) | (be_ref[i] != prev)
    used = i < nu_ref[0]

    @pl.when(changed & used)
    def _():
        w13_scr[:, :D_EXPERT] = w1_ref[0].astype(BF16)
        w13_scr[:, D_EXPERT:] = w3_ref[0].astype(BF16)
        w2_scr[...] = w2_ref[0].astype(BF16)

    @pl.when(used)
    def _():
        x = jnp.concatenate(_unpack_bf16_pairs(xb_ref[...]), axis=1).astype(BF16)
        h = _dot(x, w13_scr[...])
        h1 = h[:, :D_EXPERT]
        h3 = h[:, D_EXPERT:]
        a = (_silu(h1) * h3).astype(BF16)
        yb_ref[...] = _pack_bf16_pairs(_dot(a, w2_scr[...]))

    @pl.when(jnp.logical_not(used))
    def _():
        yb_ref[...] = jnp.zeros_like(yb_ref)


def _ffn(blk_exp, n_used, xb, w1, w3, w2):
    n_rows = xb.shape[0]
    nblk = n_rows // MOE_BLOCK
    wmap = lambda i, be, nu: (be[i], 0, 0)
    grid_spec = pltpu.PrefetchScalarGridSpec(
        num_scalar_prefetch=2, grid=(nblk,),
        in_specs=[pl.BlockSpec((MOE_BLOCK, D_MODEL // 2), lambda i, be, nu: (i, 0)),
                  pl.BlockSpec((1, D_MODEL, D_EXPERT), wmap),
                  pl.BlockSpec((1, D_MODEL, D_EXPERT), wmap),
                  pl.BlockSpec((1, D_EXPERT, D_MODEL), wmap)],
        out_specs=pl.BlockSpec((MOE_BLOCK, D_MODEL // 2), lambda i, be, nu: (i, 0)),
        scratch_shapes=[pltpu.VMEM((D_MODEL, 2 * D_EXPERT), BF16),
                        pltpu.VMEM((D_EXPERT, D_MODEL), BF16)])
    return pl.pallas_call(
        _ffn_kernel, grid_spec=grid_spec,
        out_shape=jax.ShapeDtypeStruct((n_rows, D_MODEL // 2), U32),
        compiler_params=_cparams(("arbitrary",), 40),
        name="moe_ffn",
    )(blk_exp, n_used, xb, w1, w3, w2)


def _combine_kernel(d1_ref, d2_ref, d1n_ref, d2n_ref, yb_hbm, gs_ref, x1_ref, g2_ref, o_ref,
                    ya_scr, yb_scr, sem):
    tm = COMBINE_TM
    i = pl.program_id(0)
    slot = i % 2

    def issue(da_ref, db_ref, s):
        def body(g, carry):
            for u in range(ROW_UNROLL):
                tt = g * ROW_UNROLL + u
                pltpu.make_async_copy(yb_hbm.at[pl.ds(da_ref[0, 0, tt], 1)],
                                      ya_scr.at[s, pl.ds(tt, 1)], sem.at[s]).start()
                pltpu.make_async_copy(yb_hbm.at[pl.ds(db_ref[0, 0, tt], 1)],
                                      yb_scr.at[s, pl.ds(tt, 1)], sem.at[s]).start(priority=1)
            return carry

        lax.fori_loop(0, tm // ROW_UNROLL, body, 0)

    @pl.when(i == 0)
    def _():
        issue(d1_ref, d2_ref, 0)

    @pl.when(i + 1 < pl.num_programs(0))
    def _():
        issue(d1n_ref, d2n_ref, 1 - slot)

    pltpu.make_async_copy(yb_hbm.at[pl.ds(0, tm)], ya_scr.at[slot], sem.at[slot]).wait()
    pltpu.make_async_copy(yb_hbm.at[pl.ds(0, tm)], yb_scr.at[slot], sem.at[slot]).wait()
    ga = gs_ref[:, 0:1]
    gb = gs_ref[:, 1:2]
    a_lo, a_hi = _unpack_bf16_pairs(ya_scr[slot])
    b_lo, b_hi = _unpack_bf16_pairs(yb_scr[slot])
    half = D_MODEL // 2
    o_ref[:, :half] = x1_ref[:, :half] + g2_ref[0][:, :half] * (ga * a_lo + gb * b_lo)
    o_ref[:, half:] = x1_ref[:, half:] + g2_ref[0][:, half:] * (ga * a_hi + gb * b_hi)


def _combine(d1, d2, ybuf, gs, x1, mod3, seq_len):
    t = x1.shape[0]
    tm = COMBINE_TM
    nt = t // tm
    tiles_per_seq = seq_len // tm
    smem = pl.BlockSpec((1, 1, tm), lambda i: (i, 0, 0), memory_space=pltpu.SMEM)
    smem_next = pl.BlockSpec((1, 1, tm), lambda i: (jnp.minimum(i + 1, nt - 1), 0, 0), memory_space=pltpu.SMEM)
    return pl.pallas_call(
        _combine_kernel,
        grid=(nt,),
        in_specs=[smem, smem, smem_next, smem_next,
                  pl.BlockSpec(memory_space=pl.ANY),
                  pl.BlockSpec((tm, LANE), lambda i: (i, 0)),
                  pl.BlockSpec((tm, D_MODEL), lambda i: (i, 0)),
                  pl.BlockSpec((1, 1, D_MODEL), lambda i: ((i // tiles_per_seq) * N_MOD + 5, 0, 0))],
        out_specs=pl.BlockSpec((tm, D_MODEL), lambda i: (i, 0)),
        out_shape=jax.ShapeDtypeStruct((t, D_MODEL), F32),
        scratch_shapes=[pltpu.VMEM((2, tm, D_MODEL // 2), U32), pltpu.VMEM((2, tm, D_MODEL // 2), U32),
                        pltpu.SemaphoreType.DMA((2,))],
        compiler_params=_cparams(("arbitrary",), 32),
        name="moe_combine",
    )(d1, d2, d1, d2, ybuf, gs, x1, mod3)


def _rope_tables(seq_len):
    t = np.arange(seq_len)
    row = (t // GRID_W).astype(np.float32)
    colp = (t % GRID_W).astype(np.float32)
    half = SSD_STATE // 2
    inv = (ROPE_THETA ** (-np.arange(0, half, 2, dtype=np.float32) / half)).astype(np.float32)
    ar = (row[:, None] * inv).astype(np.float64)
    ac = (colp[:, None] * inv).astype(np.float64)
    cos_t = np.concatenate([np.cos(ar), np.cos(ar), np.cos(ac), np.cos(ac)], axis=-1)
    sin_t = np.concatenate([-np.sin(ar), np.sin(ar), -np.sin(ac), np.sin(ac)], axis=-1)
    return jnp.asarray(cos_t, F32), jnp.asarray(sin_t, F32)


def _scan_tables(rev):
    li = np.arange(SSD_CHUNK)[:, None]
    ui = np.arange(SSD_CHUNK)[None, :]
    tri = (ui >= li) if rev else (ui <= li)
    col = SSD_HEADS if rev else 0
    j = np.arange(LANE)[:, None]
    c = np.arange(D_SSD)[None, :]
    ee = j == col + c // SSD_HEADDIM
    return jnp.asarray(tri, BF16), jnp.asarray(ee, BF16)


def _pad_lanes(v, width=LANE):
    v = v.reshape(1, -1)
    return jnp.pad(v, ((0, 0), (0, width - v.shape[1])))


def kernel(x, c, ctx, c_ctx, w_mod, b_mod, norm1_w, w_in, conv_w, conv_b, a_log_f, a_log_b, dt_bias_f, dt_bias_b, d_skip, ssd_norm_w, q_norm_w, k_norm_w, rpb, w_br_ssd, w_br_na, w_out, norm2_w, w_grp, b_grp, w_rt, b_rt, w1, w3, w2):
    nb, seq_len, d = x.shape
    ctx_len = ctx.shape[1]
    t = nb * seq_len
    tc = nb * ctx_len
    assert w_mod.shape[0] == 1 and d == D_MODEL and nb <= 7
    assert seq_len % 256 == 0 and ctx_len % SSD_CHUNK == 0 and seq_len // GRID_W >= NA_KH

    cin = jnp.concatenate([c, c_ctx[None, :], jnp.zeros((8 - nb - 1, d), F32)], axis=0)
    mod = _modulation(cin, w_mod[0], b_mod[0])
    mod3 = mod.reshape(8 * N_MOD, 1, D_MODEL)

    o_dt = D_SSD + D_XBC
    o_qkv = o_dt + 2 * SSD_HEADS
    wi = w_in[0].astype(BF16)
    w_a = wi
    w_b = wi[:, o_qkv:]
    w_dtp = jnp.pad(wi[:, o_dt:o_qkv], ((0, 0), (0, LANE - 2 * SSD_HEADS)))
    n1w = norm1_w[0].reshape(1, D_MODEL)

    x2 = x.reshape(t, D_MODEL)
    ctx2 = ctx.reshape(tc, D_MODEL)
    tm_in = 2048 if seq_len % 2048 == 0 else 256
    tiles = seq_len // tm_in
    tn_in = 1024
    z_t = list(range(0, D_SSD // tn_in))
    xbc_t = list(range(D_SSD // tn_in, o_dt // tn_in))
    q_t, k_t, v_t = [0], [1], [2]
    g_t = [3, 4]
    big, dt_raw = _inproj(x2, mod3, lambda i: i // tiles, n1w, w_a, w_b, w_dtp, tm_in, tn_in,
                          xbc_t + z_t, g_t + q_t + k_t + v_t)
    tm_c = 1024 if tc % 1024 == 0 else SSD_CHUNK
    bigc, dtc_raw = _inproj(ctx2, mod3, lambda i: nb, n1w, w_a, w_b, w_dtp, tm_c, tn_in, xbc_t, k_t + v_t)
    ZCOL, GCOL, QCOL, KCOL, VCOL = 2, 3, 8, 9, 10
    KC_COL, VC_COL = 4, 5

    dtbias = _pad_lanes(jnp.concatenate([dt_bias_f[0], dt_bias_b[0]]))
    cw = conv_w[0]
    cbias = conv_b[0].reshape(1, D_XBC)
    cos_t, sin_t = _rope_tables(seq_len)
    xs, bc, dts = _ssd_prep(big, dt_raw, cw, cbias, dtbias, cos_t, sin_t, seq_len, 256)
    ctl = 256 if ctx_len % 256 == 0 else SSD_CHUNK
    ones_t = jnp.ones((ctx_len, LANE), F32)
    zeros_t = jnp.zeros((ctx_len, LANE), F32)
    xsc, bcc, dtsc = _ssd_prep(bigc, dtc_raw, cw, cbias, dtbias, ones_t, zeros_t, ctx_len, ctl)

    alog = _pad_lanes(jnp.concatenate([a_log_f[0], a_log_b[0]]))
    dskip = jnp.repeat(d_skip[0], SSD_HEADDIM).reshape(1, D_SSD)
    tri_f, ee_f = _scan_tables(False)
    tri_b, ee_b = _scan_tables(True)
    hcf = _ssd_state(xsc, bcc, dtsc, alog, tri_f, ee_f, nb, False)
    hcb = _ssd_state(xsc, bcc, dtsc, alog, tri_b, ee_b, nb, True)
    yf, yb = _ssd_scan(xs, bc, dts, alog, dskip, (tri_f, ee_f), (tri_b, ee_b), hcf, hcb, nb)

    qw = jnp.tile(q_norm_w[0], NA_HEADS).reshape(1, D_NA)
    kw = jnp.tile(k_norm_w[0], NA_HEADS).reshape(1, D_NA)
    gi = np.arange(LANE)
    gmat = jnp.asarray(((gi[:, None] // NA_HEADDIM) == (gi[None, :] // NA_HEADDIM)) * (1.0 / NA_HEADDIM), BF16)
    qn, kn = _na_prep(big, QCOL, KCOL, qw, kw, gmat, 512 if t % 512 == 0 else 256)
    kcn = _na_prep(bigc, None, KC_COL, qw, kw, gmat, ctx_len)
    bias_tab = _na_bias_table(rpb[0])
    y_na = _neigh_attention(qn, kn, big, kcn, bigc, bias_tab, nb, seq_len, ctx_len, VCOL, VC_COL)

    w_r = jnp.pad(jnp.concatenate([w_rt[0], w_grp[0]], axis=1), ((0, 0), (0, LANE - N_EXPERTS - N_GROUPS)))
    wrh = w_r.astype(BF16)
    wrl = (w_r - wrh.astype(F32)).astype(BF16)
    br = _pad_lanes(jnp.concatenate([b_rt[0], b_grp[0]]))
    nblk = (2 * t + N_EXPERTS * (MOE_BLOCK - 1) + MOE_BLOCK - 1) // MOE_BLOCK
    x1, h2p, logits, xb_zero = _merge(yf, yb, big, y_na, x2, mod3,
                             ssd_norm_w[0].reshape(1, D_SSD), norm2_w[0].reshape(1, D_MODEL),
                             w_br_ssd[0].astype(BF16), w_br_na[0].astype(BF16), w_out[0].astype(BF16),
                             wrh, wrl, br, seq_len, 512 if seq_len % 512 == 0 else 256, ZCOL, GCOL,
                                      nblk * MOE_BLOCK)

    si = np.arange(ROUTE_TM)
    stri = jnp.asarray(si[None, :] < si[:, None], BF16)
    ui = np.arange(LANE)
    utri = jnp.asarray(ui[:, None] <= ui[None, :], BF16)
    gs, rt, cnt = _route(logits, stri, utri)

    counts = cnt[0, :N_EXPERTS]
    pend = jnp.cumsum((counts + MOE_BLOCK - 1) // MOE_BLOCK * MOE_BLOCK)
    n_used = (pend[-1] // MOE_BLOCK).astype(I32).reshape(1)
    blk_row0 = jnp.arange(nblk, dtype=I32) * MOE_BLOCK
    blk_exp = jnp.minimum(jnp.sum((pend[None, :] <= blk_row0[:, None]).astype(I32), axis=1), N_EXPERTS - 1)

    def tok_tiles(row, tm):
        return rt[row].reshape(t // tm, 1, tm)

    xb = _dispatch(tok_tiles(2, DISPATCH_TM), tok_tiles(3, DISPATCH_TM), h2p, xb_zero)
    ybuf = _ffn(blk_exp, n_used, xb, w1[0], w3[0], w2[0])
    out = _combine(tok_tiles(2, COMBINE_TM), tok_tiles(3, COMBINE_TM), ybuf, gs, x1, mod3, seq_len)
    return out.reshape(nb, seq_len, D_MODEL)
```

```python
import functools
import math

import numpy as np

import jax
import jax.numpy as jnp
from jax import lax
from jax.experimental import pallas as pl
from jax.experimental.pallas import tpu as pltpu

F32 = jnp.float32
BF16 = jnp.bfloat16
I32 = jnp.int32
U32 = jnp.uint32

D_MODEL = 1024
GRID_W = 64
D_SSD = 2048
SSD_HEADDIM = 64
SSD_HEADS = 32
SSD_GROUPS = 8
HEADS_PER_GROUP = 4
SSD_STATE = 128
SSD_CHUNK = 128
CONV_K = 5
D_BC = SSD_GROUPS * SSD_STATE
D_XBC = D_SSD + 2 * D_BC
NA_HEADDIM = 64
NA_HEADS = 16
D_NA = 1024
NA_KH = 8
NA_KW = 16
ROPE_THETA = 10000.0
N_GROUPS = 4
EXPERTS_PER_GROUP = 8
N_EXPERTS = 32
D_EXPERT = 512
N_MOD = 6
EPS = 1e-6
NEG_BIG = -1e30

COLS_LAT = D_XBC + D_SSD + 2 * D_MODEL + 3 * D_NA
COLS_CTX = D_XBC + 2 * D_NA
LANE = 128

V7X_VMEM_BYTES = 64 * 1024 * 1024


def _cparams(sem, vmem_mb):
    return pltpu.CompilerParams(dimension_semantics=sem, vmem_limit_bytes=vmem_mb * 1024 * 1024)


def _sigmoid(x):
    return 1.0 / (1.0 + jnp.exp(-x))


def _silu(x):
    h = 0.5 * x
    return h * jnp.tanh(h) + h


def _split2(x):
    hi = x.astype(BF16)
    lo = (x - hi.astype(F32)).astype(BF16)
    return hi, lo


def _split3(x):
    hi = x.astype(BF16)
    r = x - hi.astype(F32)
    mid = r.astype(BF16)
    lo = (r - mid.astype(F32)).astype(BF16)
    return hi, mid, lo


def _pack_bf16_pairs(x):
    bits = pltpu.bitcast(x.astype(BF16).astype(F32), U32)
    n = x.shape[1] // 2
    return (bits[:, :n] >> 16) | bits[:, n:]


def _unpack_bf16_pairs(w):
    return pltpu.bitcast(w << 16, F32), pltpu.bitcast(w & jnp.uint32(0xFFFF0000), F32)


def _dot(a, b):
    return jnp.dot(a, b, preferred_element_type=F32)


def _dot_nt(a, b):
    return lax.dot_general(a, b, (((1,), (1,)), ((), ())), preferred_element_type=F32)


def _dot_tn(a, b):
    return lax.dot_general(a, b, (((0,), (0,)), ((), ())), preferred_element_type=F32)


def _mod_kernel(c_ref, w_ref, b_ref, o_ref):
    c = c_ref[...]
    a = _silu(c).astype(BF16)
    o_ref[...] = _dot(a, w_ref[...].astype(BF16)) + b_ref[...]


def _modulation(cin, w_mod, b_mod):
    n = w_mod.shape[1]
    tn = 1536
    return pl.pallas_call(
        _mod_kernel,
        grid=(n // tn,),
        in_specs=[pl.BlockSpec((8, D_MODEL), lambda j: (0, 0)),
                  pl.BlockSpec((D_MODEL, tn), lambda j: (0, j)),
                  pl.BlockSpec((1, tn), lambda j: (0, j))],
        out_specs=pl.BlockSpec((8, tn), lambda j: (0, j)),
        out_shape=jax.ShapeDtypeStruct((8, n), F32),
        compiler_params=_cparams(("arbitrary",), 40),
        name="modulation",
    )(cin, w_mod, b_mod.reshape(1, n))


def _inproj_kernel(x_ref, sh_ref, sc_ref, nw_ref, wa_ref, wb_ref, wdt_ref, o_ref, dt_ref, h_scr, *, tm, n_a):
    j = pl.program_id(1)

    @pl.when(j == 0)
    def _():
        scale = 1.0 + sc_ref[0]
        shift = sh_ref[0]
        nw = nw_ref[...]

        def body(r, carry):
            rows = pl.ds(pl.multiple_of(r * 128, 128), 128)
            x = x_ref[rows, :]
            ms = jnp.mean(x * x, axis=-1, keepdims=True)
            h = x * lax.rsqrt(ms + EPS) * nw * scale + shift
            h_scr[rows, :] = h.astype(BF16)
            return carry

        lax.fori_loop(0, tm // 128, body, 0)
        dt_ref[...] = _dot(h_scr[...], wdt_ref[...])

    @pl.when(j < n_a)
    def _():
        o_ref[...] = _dot(h_scr[...], wa_ref[...]).astype(BF16)

    @pl.when(j >= n_a)
    def _():
        o_ref[...] = _dot(h_scr[...], wb_ref[...]).astype(BF16)


def _inproj(x2, mod3, mod_row_fn, norm_w, wa, wb, wdt, tm, tn, a_tiles, b_tiles):
    t = x2.shape[0]
    n_a, n_b = len(a_tiles), len(b_tiles)
    a_tab = list(a_tiles) + [a_tiles[-1]] * n_b
    b_tab = [b_tiles[0]] * n_a + list(b_tiles)

    def pick(tab, j):
        idx = jnp.int32(tab[-1])
        for pos in range(len(tab) - 2, -1, -1):
            idx = jnp.where(j <= pos, tab[pos], idx)
        return idx

    kern = functools.partial(_inproj_kernel, tm=tm, n_a=n_a)
    return pl.pallas_call(
        kern,
        grid=(t // tm, n_a + n_b),
        in_specs=[pl.BlockSpec((tm, D_MODEL), lambda i, j: (i, 0)),
                  pl.BlockSpec((1, 1, D_MODEL), lambda i, j: (mod_row_fn(i) * N_MOD + 0, 0, 0)),
                  pl.BlockSpec((1, 1, D_MODEL), lambda i, j: (mod_row_fn(i) * N_MOD + 1, 0, 0)),
                  pl.BlockSpec((1, D_MODEL), lambda i, j: (0, 0)),
                  pl.BlockSpec((D_MODEL, tn), lambda i, j: (0, pick(a_tab, j))),
                  pl.BlockSpec((D_MODEL, tn), lambda i, j: (0, pick(b_tab, j))),
                  pl.BlockSpec((D_MODEL, LANE), lambda i, j: (0, 0))],
        out_specs=[pl.BlockSpec((tm, tn), lambda i, j: (i, j)),
                   pl.BlockSpec((tm, LANE), lambda i, j: (i, 0))],
        out_shape=[jax.ShapeDtypeStruct((t, (n_a + n_b) * tn), BF16),
                   jax.ShapeDtypeStruct((t, LANE), F32)],
        scratch_shapes=[pltpu.VMEM((tm, D_MODEL), BF16)],
        compiler_params=_cparams(("arbitrary", "arbitrary"), 52),
        name="inproj",
    )(x2, mod3, mod3, norm_w, wa, wb, wdt)


CONV_OFFSETS = (-2, -1, 1, 2)


def _ssd_prep_kernel(prev_ref, cur_ref, next_ref, dt_ref, cw_ref, cb_ref, dtbias_ref, cos_ref, sin_ref,
                     shift_ref, etop_ref, ebot_ref, xs_ref, bc_ref, dto_ref, *, tl, tiles_per_seq):
    i = pl.program_id(0)
    first = (i % tiles_per_seq) == 0
    last = (i % tiles_per_seq) == tiles_per_seq - 1
    lane = lax.broadcasted_iota(I32, (tl, LANE), 1)
    lo_half = ((lane // 32) % 2) == 0
    shift = shift_ref[...]
    etop = etop_ref[...]
    ebot = ebot_ref[...]
    cosv = cos_ref[...]
    sinv = sin_ref[...]
    cwid = 256
    for c in range(D_XBC // cwid):
        cs = slice(c * cwid, (c + 1) * cwid)
        w = cw_ref[:, cs]
        xc = cur_ref[:, cs]
        sh = _dot(shift, xc).reshape(tl // 8, len(CONV_OFFSETS), 8, cwid)
        tap = lambda k: sh[:, k].reshape(tl, cwid)
        acc = cb_ref[:, cs] + xc.astype(F32) * w[2:3, :]
        acc = acc + tap(0) * w[0:1, :] + tap(1) * w[1:2, :] + tap(2) * w[3:4, :] + tap(3) * w[4:5, :]
        top = _dot(etop, prev_ref[:, cs])
        bot = _dot(ebot, next_ref[:, cs])
        top_c = jnp.where(first, 0.0, top[0:8] * w[0:1, :] + top[8:16] * w[1:2, :])
        bot_c = jnp.where(last, 0.0, bot[0:8] * w[3:4, :] + bot[8:16] * w[4:5, :])
        acc = jnp.concatenate([acc[0:8] + top_c, acc[8:tl - 8], acc[tl - 8:tl] + bot_c], axis=0)
        y = _silu(acc)
        if c * cwid < D_SSD:
            xs_ref[:, cs] = y.astype(BF16)
        else:
            for g in range(cwid // LANE):
                yg = y[:, g * LANE:(g + 1) * LANE]
                partner = jnp.where(lo_half, pltpu.roll(yg, 96, 1), pltpu.roll(yg, 32, 1))
                o = yg * cosv + partner * sinv
                c0 = c * cwid - D_SSD + g * LANE
                bc_ref[:, c0:c0 + LANE] = o.astype(BF16)

    v = dt_ref[...] + dtbias_ref[...]
    dto_ref[...] = jnp.maximum(v, 0.0) + jnp.log(1.0 + jnp.exp(-jnp.abs(v)))


def _conv_shift_tables(tl):
    t = np.arange(tl)[:, None]
    u = np.arange(tl)[None, :]
    shift = np.stack([(u == t + off) for off in CONV_OFFSETS], axis=0).reshape(len(CONV_OFFSETS), tl // 8, 8, tl)
    shift = shift.transpose(1, 0, 2, 3).reshape(len(CONV_OFFSETS) * tl, tl)
    r = np.arange(16)[:, None]
    q = np.arange(16)[None, :]
    etop = ((r < 8) & (q == r + 14)) | ((r >= 8) & (q == r - 8 + 15))
    ebot = ((r < 8) & (q == r - 7)) | ((r >= 8) & (q == r - 8 - 6))
    return jnp.asarray(shift, BF16), jnp.asarray(etop, BF16), jnp.asarray(ebot, BF16)


def _ssd_prep(big, dt_raw, conv_w, conv_b, dtbias, cos_t, sin_t, seq_len, tl):
    t = big.shape[0]
    tps = seq_len // tl
    hb = tl // 16
    nhalo = t // 16
    kern = functools.partial(_ssd_prep_kernel, tl=tl, tiles_per_seq=tps)
    shift, etop, ebot = _conv_shift_tables(tl)
    const = lambda i: (0, 0)
    return pl.pallas_call(
        kern,
        grid=(t // tl,),
        in_specs=[pl.BlockSpec((16, D_XBC), lambda i: (jnp.maximum(i * hb - 1, 0), 0)),
                  pl.BlockSpec((tl, D_XBC), lambda i: (i, 0)),
                  pl.BlockSpec((16, D_XBC), lambda i: (jnp.minimum((i + 1) * hb, nhalo - 1), 0)),
                  pl.BlockSpec((tl, LANE), lambda i: (i, 0)),
                  pl.BlockSpec((CONV_K, D_XBC), const),
                  pl.BlockSpec((1, D_XBC), const),
                  pl.BlockSpec((1, LANE), const),
                  pl.BlockSpec((tl, LANE), lambda i: (i % tps, 0)),
                  pl.BlockSpec((tl, LANE), lambda i: (i % tps, 0)),
                  pl.BlockSpec((len(CONV_OFFSETS) * tl, tl), const),
                  pl.BlockSpec((16, 16), const),
                  pl.BlockSpec((16, 16), const)],
        out_specs=[pl.BlockSpec((tl, D_SSD), lambda i: (i, 0)),
                   pl.BlockSpec((tl, 2 * D_BC), lambda i: (i, 0)),
                   pl.BlockSpec((tl, LANE), lambda i: (i, 0))],
        out_shape=[jax.ShapeDtypeStruct((t, D_SSD), BF16),
                   jax.ShapeDtypeStruct((t, 2 * D_BC), BF16),
                   jax.ShapeDtypeStruct((t, LANE), F32)],
        compiler_params=_cparams(("arbitrary",), 40),
        name="ssd_prep",
    )(big, big, big, dt_raw, conv_w, conv_b, dtbias, cos_t, sin_t, shift, etop, ebot)


GROUP_W = HEADS_PER_GROUP * SSD_HEADDIM


def _scan_decays(dt_ref, alog_ref, tri_ref, ee_ref, rev):
    edge = 0 if rev else SSD_CHUNK - 1
    a_row = -jnp.exp(alog_ref[...])
    dt = dt_ref[...]
    trib = tri_ref[...]
    d_hi, d_mid, d_lo = _split3(dt * a_row)
    cum = _dot(trib, d_hi) + _dot(trib, d_mid) + _dot(trib, d_lo)
    tot = cum[edge:edge + 1, :]
    ee = ee_ref[...]
    w_heads = (jnp.exp(tot - cum) * dt).astype(BF16)
    t_hi, t_lo = _split2(jnp.broadcast_to(jnp.exp(tot), (8, LANE)))
    etot = (_dot(t_hi, ee) + _dot(t_lo, ee))[0:1, :]
    return dt, trib, cum, w_heads, etot


def _spread(per_head, ee_ref, g):
    return _dot(per_head, ee_ref[:, g * GROUP_W:(g + 1) * GROUP_W])


def _ssd_state_kernel(xs_ref, bc_ref, dt_ref, alog_ref, tri_ref, ee_ref, hfin_ref, h_scr, *, rev, nc):
    k = pl.program_id(1)

    @pl.when(k == 0)
    def _():
        h_scr[...] = jnp.zeros_like(h_scr)

    _, _, _, w_heads, etot = _scan_decays(dt_ref, alog_ref, tri_ref, ee_ref, rev)
    for g in range(SSD_GROUPS):
        gs = slice(g * GROUP_W, (g + 1) * GROUP_W)
        xw = (xs_ref[:, gs].astype(F32) * _spread(w_heads, ee_ref, g)).astype(BF16)
        h_scr[:, gs] = etot[:, gs] * h_scr[:, gs] + _dot_tn(bc_ref[:, g * SSD_STATE:(g + 1) * SSD_STATE], xw)

    @pl.when(k == nc - 1)
    def _():
        hfin_ref[0] = h_scr[...]


def _scan_chunk_stages(xs_ref, bc_ref, dt_ref, alog_ref, dskip_ref, tri_ref, ee_ref, y_ref, h_scr, rev):
    L = SSD_CHUNK
    col = SSD_HEADS if rev else 0
    dt, trib, cum, w_heads, etot = _scan_decays(dt_ref, alog_ref, tri_ref, ee_ref, rev)
    tri = trib > 0.5
    cum_t = cum.T
    dt_t = dt.T
    o_heads = jnp.exp(cum).astype(BF16)

    gw = GROUP_W
    lane_head = lax.broadcasted_iota(I32, (L, gw), 1) // SSD_HEADDIM

    def b_of(g):
        return bc_ref[:, g * SSD_STATE:(g + 1) * SSD_STATE]

    def c_of(g):
        return bc_ref[:, D_BC + g * SSD_STATE:D_BC + (g + 1) * SSD_STATE]

    def operands(g, cb):
        gs = slice(g * gw, (g + 1) * gw)
        xs_g = xs_ref[:, gs]
        ms = []
        for r in range(HEADS_PER_GROUP):
            hh = col + g * HEADS_PER_GROUP + r
            seg = cum[:, hh:hh + 1] - cum_t[hh:hh + 1, :]
            lm = jnp.exp(jnp.where(tri, seg, NEG_BIG)) * dt_t[hh:hh + 1, :]
            ms.append((cb * lm).astype(BF16))
        mcat = jnp.concatenate(ms, axis=1)
        zero = jnp.zeros_like(xs_g)
        bd = jnp.concatenate([jnp.where(lane_head == r, xs_g, zero) for r in range(HEADS_PER_GROUP)], axis=0)
        xs_f = xs_g.astype(F32)
        xw = (xs_f * _spread(w_heads, ee_ref, g)).astype(BF16)
        return mcat, bd, xs_f, xw, _spread(o_heads, ee_ref, g)

    def cb_of(g):
        return _dot_nt(c_of(g), b_of(g))

    yield
    cb_next = cb_of(0)
    yield
    nxt = operands(0, cb_next)
    cb_next = cb_of(1)
    for g in range(SSD_GROUPS):
        yield
        gs = slice(g * gw, (g + 1) * gw)
        mcat, bd, xs_f, xw, oscale_g = nxt
        h_g = h_scr[:, gs]
        y_diag = _dot(mcat, bd)
        y_off = _dot(c_of(g), h_g.astype(BF16))
        loc = _dot_tn(b_of(g), xw)
        if g + 1 < SSD_GROUPS:
            nxt = operands(g + 1, cb_next)
            if g + 2 < SSD_GROUPS:
                cb_next = cb_of(g + 2)
        y = y_diag + oscale_g * y_off
        if not rev:
            y = y + xs_f * dskip_ref[:, gs]
        y_ref[:, gs] = y.astype(BF16)
        h_scr[:, gs] = etot[:, gs] * h_g + loc


def _ssd_scan_kernel(xsf_ref, bcf_ref, dtf_ref, xsb_ref, bcb_ref, dtb_ref, alog_ref, dskip_ref,
                     trif_ref, eef_ref, trib_ref, eeb_ref, h0f_ref, h0b_ref, yf_ref, yb_ref, hf_scr, hb_scr):
    @pl.when(pl.program_id(1) == 0)
    def _():
        hf_scr[...] = h0f_ref[0]
        hb_scr[...] = h0b_ref[0]

    chains = [_scan_chunk_stages(xsf_ref, bcf_ref, dtf_ref, alog_ref, dskip_ref, trif_ref, eef_ref,
                                 yf_ref, hf_scr, False),
              _scan_chunk_stages(xsb_ref, bcb_ref, dtb_ref, alog_ref, dskip_ref, trib_ref, eeb_ref,
                                 yb_ref, hb_scr, True)]
    while chains:
        for ch in list(chains):
            try:
                next(ch)
            except StopIteration:
                chains.remove(ch)


def _scan_rowmap(nc, rev):
    def rowmap(b, k):
        c = (nc - 1 - k) if rev else k
        return (b * nc + c, 0)
    return rowmap


def _ssd_state(xs, bc, dt, alog_row, tri, ee, nb, rev):
    t = xs.shape[0]
    nc = t // nb // SSD_CHUNK
    rowmap = _scan_rowmap(nc, rev)
    const2 = lambda b, k: (0, 0)
    return pl.pallas_call(
        functools.partial(_ssd_state_kernel, rev=rev, nc=nc),
        grid=(nb, nc),
        in_specs=[pl.BlockSpec((SSD_CHUNK, D_SSD), rowmap),
                  pl.BlockSpec((SSD_CHUNK, D_BC), rowmap),
                  pl.BlockSpec((SSD_CHUNK, LANE), rowmap),
                  pl.BlockSpec((1, LANE), const2),
                  pl.BlockSpec((SSD_CHUNK, SSD_CHUNK), const2),
                  pl.BlockSpec((LANE, D_SSD), const2)],
        out_specs=pl.BlockSpec((1, SSD_STATE, D_SSD), lambda b, k: (b, 0, 0)),
        out_shape=jax.ShapeDtypeStruct((nb, SSD_STATE, D_SSD), F32),
        scratch_shapes=[pltpu.VMEM((SSD_STATE, D_SSD), F32)],
        compiler_params=_cparams(("arbitrary", "arbitrary"), 40),
        name="ssd_state_bwd" if rev else "ssd_state_fwd",
    )(xs, bc, dt, alog_row, tri, ee)


def _ssd_scan(xs, bc, dt, alog_row, dskip_row, tables_f, tables_b, h0f, h0b, nb):
    t = xs.shape[0]
    nc = t // nb // SSD_CHUNK
    fmap = _scan_rowmap(nc, False)
    bmap = _scan_rowmap(nc, True)
    const2 = lambda b, k: (0, 0)
    chunk = lambda width, m: pl.BlockSpec((SSD_CHUNK, width), m)
    tri_spec = pl.BlockSpec((SSD_CHUNK, SSD_CHUNK), const2)
    ee_spec = pl.BlockSpec((LANE, D_SSD), const2)
    state = pl.BlockSpec((1, SSD_STATE, D_SSD), lambda b, k: (b, 0, 0))
    return pl.pallas_call(
        _ssd_scan_kernel,
        grid=(nb, nc),
        in_specs=[chunk(D_SSD, fmap), chunk(2 * D_BC, fmap), chunk(LANE, fmap),
                  chunk(D_SSD, bmap), chunk(2 * D_BC, bmap), chunk(LANE, bmap),
                  pl.BlockSpec((1, LANE), const2),
                  pl.BlockSpec((1, D_SSD), const2),
                  tri_spec, ee_spec, tri_spec, ee_spec, state, state],
        out_specs=[chunk(D_SSD, fmap), chunk(D_SSD, bmap)],
        out_shape=[jax.ShapeDtypeStruct((t, D_SSD), BF16)] * 2,
        scratch_shapes=[pltpu.VMEM((SSD_STATE, D_SSD), F32)] * 2,
        compiler_params=_cparams(("arbitrary", "arbitrary"), 48),
        name="ssd_scan",
    )(xs, bc, dt, xs, bc, dt, alog_row, dskip_row, *tables_f, *tables_b, h0f, h0b)


def _headnorm_cols(src_ref, w_ref, g, dst_ref, scale, transposed=False):
    for c in range(D_NA // LANE):
        cs = slice(c * LANE, (c + 1) * LANE)
        x = src_ref[:, cs].astype(F32)
        ms = _dot((x * x).astype(BF16), g)
        y = x * lax.rsqrt(ms + EPS) * w_ref[:, cs]
        if scale is not None:
            y = y * scale
        if transposed:
            dst_ref[cs, :] = y.T.astype(BF16)
        else:
            dst_ref[:, cs] = y.astype(BF16)


def _na_prep_qk_kernel(q_ref, k_ref, qw_ref, kw_ref, g_ref, qo_ref, ko_ref):
    g = g_ref[...]
    _headnorm_cols(q_ref, qw_ref, g, qo_ref, NA_HEADDIM ** -0.5)
    _headnorm_cols(k_ref, kw_ref, g, ko_ref, None)


def _na_prep_k_kernel(k_ref, kw_ref, g_ref, ko_ref):
    _headnorm_cols(k_ref, kw_ref, g_ref[...], ko_ref, None, transposed=True)


def _na_prep(big, qcol, kcol, qw, kw, gmat, tm):
    t = big.shape[0]
    blk = lambda cidx: pl.BlockSpec((tm, D_NA), lambda i: (i, cidx))
    row = pl.BlockSpec((1, D_NA), lambda i: (0, 0))
    gspec = pl.BlockSpec((LANE, LANE), lambda i: (0, 0))
    out = pl.BlockSpec((tm, D_NA), lambda i: (i, 0))
    if qcol is None:
        return pl.pallas_call(
            _na_prep_k_kernel, grid=(t // tm,),
            in_specs=[blk(kcol), row, gspec], out_specs=pl.BlockSpec((D_NA, tm), lambda i: (i, 0)),
            out_shape=jax.ShapeDtypeStruct((t // tm * D_NA, tm), BF16),
            compiler_params=_cparams(("arbitrary",), 32), name="na_prep_ctx",
        )(big, kw, gmat)
    return pl.pallas_call(
        _na_prep_qk_kernel, grid=(t // tm,),
        in_specs=[blk(qcol), blk(kcol), row, row, gspec], out_specs=[out, out],
        out_shape=[jax.ShapeDtypeStruct((t, D_NA), BF16)] * 2,
        compiler_params=_cparams(("arbitrary",), 32), name="na_prep",
    )(big, big, qw, kw, gmat)


NA_LOOKAHEAD = 1


NA_ROWS_PER_STEP = 4


def _na_row_stages(r, qrow, q_ref, k_ref, v_ref, kc_ref, vc_ref, bias_ref, o_ref, rows):
    rs = jnp.clip(r - NA_KH // 2, 0, rows - NA_KH)
    start = pl.multiple_of(rs * GRID_W, GRID_W)
    nk = NA_KH * GRID_W
    lane = lax.broadcasted_iota(I32, (GRID_W, LANE), 1)
    first_head = lane < NA_HEADDIM
    npair = NA_HEADS // 2
    qrows = slice(qrow, qrow + GRID_W)

    def scores(j):
        cs = slice(j * LANE, (j + 1) * LANE)
        qp = q_ref[qrows, cs]
        zero = jnp.zeros_like(qp)
        qs = jnp.concatenate([jnp.where(first_head, qp, zero), jnp.where(first_head, zero, qp)], axis=0)
        kb = k_ref[pl.ds(start, nk), cs]
        s_loc = _dot_nt(qs, kb) + bias_ref[0, j * LANE:(j + 1) * LANE, :]
        s_ctx = _dot(qs, kc_ref[cs, :])
        return s_loc, s_ctx

    pending = [scores(j) for j in range(NA_LOOKAHEAD)]
    for j in range(npair):
        yield
        cs = slice(j * LANE, (j + 1) * LANE)
        s_loc, s_ctx = pending.pop(0)
        if j + NA_LOOKAHEAD < npair:
            pending.append(scores(j + NA_LOOKAHEAD))
        vb = v_ref[pl.ds(start, nk), cs]
        m = jnp.maximum(jnp.max(s_loc, axis=-1, keepdims=True), jnp.max(s_ctx, axis=-1, keepdims=True))
        p_loc = jnp.exp(s_loc - m)
        p_ctx = jnp.exp(s_ctx - m)
        den = jnp.sum(p_loc, axis=-1, keepdims=True) + jnp.sum(p_ctx, axis=-1, keepdims=True)
        o = _dot(p_loc.astype(BF16), vb) + _dot(p_ctx.astype(BF16), vc_ref[:, cs])
        o = o / den
        o_ref[qrows, cs] = jnp.where(first_head, o[:GRID_W], o[GRID_W:]).astype(BF16)


def _na_kernel(q_ref, k_ref, v_ref, kc_ref, vc_ref, *rest, rows):
    bias_refs, o_ref = rest[:NA_ROWS_PER_STEP], rest[NA_ROWS_PER_STEP]
    r0 = pl.program_id(1) * NA_ROWS_PER_STEP
    chains = [_na_row_stages(r0 + u, u * GRID_W, q_ref, k_ref, v_ref, kc_ref, vc_ref, bias_refs[u], o_ref, rows)
              for u in range(NA_ROWS_PER_STEP)]
    while chains:
        for ch in list(chains):
            try:
                next(ch)
            except StopIteration:
                chains.remove(ch)


def _neigh_attention(qn, kn, big, kcn, bigc, bias_tab, nb, seq_len, ctx_len, vcol, vccol):
    t = qn.shape[0]
    rows = seq_len // GRID_W
    rps = NA_ROWS_PER_STEP
    steps = rows // rps
    kern = functools.partial(_na_kernel, rows=rows)

    def pat(u):
        def index_map(b, s):
            r = s * rps + u
            return (r - jnp.clip(r - NA_KH // 2, 0, rows - NA_KH), 0, 0)
        return index_map

    bias_specs = [pl.BlockSpec((1, NA_HEADS * GRID_W, NA_KH * GRID_W), pat(u)) for u in range(rps)]
    return pl.pallas_call(
        kern,
        grid=(nb, steps),
        in_specs=[pl.BlockSpec((rps * GRID_W, D_NA), lambda b, s: (b * steps + s, 0)),
                  pl.BlockSpec((seq_len, D_NA), lambda b, s: (b, 0), pipeline_mode=pl.Buffered(1)),
                  pl.BlockSpec((seq_len, D_NA), lambda b, s: (b, vcol), pipeline_mode=pl.Buffered(1)),
                  pl.BlockSpec((D_NA, ctx_len), lambda b, s: (b, 0), pipeline_mode=pl.Buffered(1)),
                  pl.BlockSpec((ctx_len, D_NA), lambda b, s: (b, vccol), pipeline_mode=pl.Buffered(1))] + bias_specs,
        out_specs=pl.BlockSpec((rps * GRID_W, D_NA), lambda b, s: (b * steps + s, 0)),
        out_shape=jax.ShapeDtypeStruct((t, D_NA), BF16),
        compiler_params=_cparams(("arbitrary", "arbitrary"), 48),
        name="neigh_attention",
    )(qn, kn, big, kcn, bigc, *([bias_tab] * rps))


def _na_bias_kernel(rpb_ref, o_ref):
    rp = rpb_ref[0]
    r64 = pltpu.roll(rp, GRID_W, 1)
    c = lax.broadcasted_iota(I32, (GRID_W, LANE), 0)
    kc = lax.broadcasted_iota(I32, (GRID_W, LANE), 1) % GRID_W
    cs = jnp.clip(c - NA_KW // 2, 0, GRID_W - NA_KW)
    valid = (kc >= cs) & (kc < cs + NA_KW)
    pair = []
    for d in range(2 * NA_KH - 2):
        vec = rp[d:d + 1, :] + r64[d + 1:d + 2, :]
        w = pltpu.roll(jnp.broadcast_to(vec, (GRID_W, LANE)), LANE - (NA_KW - 1), 1, stride=1, stride_axis=0)
        pair.append(jnp.where(valid, w, NEG_BIG))
    for p in range(NA_KH):
        for ii in range(NA_KH // 2):
            o_ref[p, :, ii * LANE:(ii + 1) * LANE] = pair[2 * ii - p + NA_KH - 1]


def _na_bias_table(rpb):
    rp = jnp.pad(rpb, ((0, 0), (0, 1), (0, LANE - (2 * NA_KW - 1))))
    return pl.pallas_call(
        _na_bias_kernel,
        grid=(NA_HEADS,),
        in_specs=[pl.BlockSpec((1, 2 * NA_KH, LANE), lambda h: (h, 0, 0))],
        out_specs=pl.BlockSpec((NA_KH, GRID_W, NA_KH * GRID_W), lambda h: (0, h, 0)),
        out_shape=jax.ShapeDtypeStruct((NA_KH, NA_HEADS * GRID_W, NA_KH * GRID_W), F32),
        compiler_params=_cparams(("arbitrary",), 32),
        name="na_bias",
    )(rp)


MERGE_SUB = 128


def _merge_kernel(yf_ref, yb_ref, z_ref, gt_ref, yna_ref, x_ref, g1_ref, sh2_ref, sc2_ref, snw_ref, n2w_ref,
                  wbs_ref, wbn_ref, wo_ref, wrh_ref, wrl_ref, br_ref, x1_ref, h2p_ref, lg_ref, xbz_ref):
    xbz_ref[...] = jnp.zeros_like(xbz_ref)
    tm = x_ref.shape[0]
    subs = [slice(r * MERGE_SUB, (r + 1) * MERGE_SUB) for r in range(tm // MERGE_SUB)]
    half = D_MODEL // 2
    yn = []
    for rs in subs:
        z = z_ref[rs, :].astype(F32)
        y = (yf_ref[rs, :] + yb_ref[rs, :]).astype(F32) * _silu(z)
        ms = jnp.mean(y * y, axis=-1, keepdims=True)
        yn.append((y * lax.rsqrt(ms + EPS) * snw_ref[...]).astype(BF16))
    ab = [(_dot(yn[r], wbs_ref[...]), _dot(yna_ref[rs, :], wbn_ref[...])) for r, rs in enumerate(subs)]
    merged = []
    for r, rs in enumerate(subs):
        g_ssd = gt_ref[rs, :D_MODEL].astype(F32)
        g_na = gt_ref[rs, D_MODEL:].astype(F32)
        merged.append((_sigmoid(g_ssd) * ab[r][0] + _sigmoid(g_na) * ab[r][1]).astype(BF16))
    mo = [_dot(m, wo_ref[...]) for m in merged]
    hs = []
    for r, rs in enumerate(subs):
        x1 = x_ref[rs, :] + g1_ref[0] * mo[r]
        x1_ref[rs, :] = x1
        ms2 = jnp.mean(x1 * x1, axis=-1, keepdims=True)
        h2 = x1 * lax.rsqrt(ms2 + EPS) * n2w_ref[...] * (1.0 + sc2_ref[0]) + sh2_ref[0]
        h_hi, h_lo = _split2(h2)
        hs.append((h_hi, h_lo))
        h2p_ref[rs, :] = _pack_bf16_pairs(h2)
    for r, rs in enumerate(subs):
        h_hi, h_lo = hs[r]
        lg_ref[rs, :] = (_dot(h_hi, wrh_ref[...]) + _dot(h_lo, wrh_ref[...]) + _dot(h_hi, wrl_ref[...])) + br_ref[...]


def _merge(yf, yb, big, yna, x2, mod3, snw, n2w, wbs, wbn, wo, wrh, wrl, br, seq_len, tm, zcol, gcol, n_rows):
    t = x2.shape[0]
    tiles_per_seq = seq_len // tm
    zrows = n_rows // (t // tm)
    assert zrows * (t // tm) == n_rows and zrows % 8 == 0
    modspec = lambda kidx: pl.BlockSpec((1, 1, D_MODEL), lambda i: ((i // tiles_per_seq) * N_MOD + kidx, 0, 0))
    full = lambda shp: pl.BlockSpec(shp, lambda i: (0,) * len(shp), pipeline_mode=pl.Buffered(1))
    return pl.pallas_call(
        _merge_kernel,
        grid=(t // tm,),
        in_specs=[pl.BlockSpec((tm, D_SSD), lambda i: (i, 0)),
                  pl.BlockSpec((tm, D_SSD), lambda i: (i, 0)),
                  pl.BlockSpec((tm, D_SSD), lambda i: (i, zcol)),
                  pl.BlockSpec((tm, 2 * D_MODEL), lambda i: (i, gcol)),
                  pl.BlockSpec((tm, D_NA), lambda i: (i, 0)),
                  pl.BlockSpec((tm, D_MODEL), lambda i: (i, 0)),
                  modspec(2), modspec(3), modspec(4),
                  full((1, D_SSD)), full((1, D_MODEL)),
                  full((D_SSD, D_MODEL)), full((D_NA, D_MODEL)), full((D_MODEL, D_MODEL)),
                  full((D_MODEL, LANE)), full((D_MODEL, LANE)), full((1, LANE))],
        out_specs=[pl.BlockSpec((tm, D_MODEL), lambda i: (i, 0)),
                   pl.BlockSpec((tm, D_MODEL // 2), lambda i: (i, 0)),
                   pl.BlockSpec((tm, LANE), lambda i: (i, 0)),
                   pl.BlockSpec((zrows, D_MODEL // 2), lambda i: (i, 0))],
        out_shape=[jax.ShapeDtypeStruct((t, D_MODEL), F32),
                   jax.ShapeDtypeStruct((t, D_MODEL // 2), U32),
                   jax.ShapeDtypeStruct((t, LANE), F32),
                   jax.ShapeDtypeStruct((n_rows, D_MODEL // 2), U32)],
        compiler_params=_cparams(("arbitrary",), 60),
        name="merge",
    )(yf, yb, big, big, yna, x2, mod3, mod3, mod3, snw, n2w, wbs, wbn, wo, wrh, wrl, br)


ROUTE_TM = 512
GRP_LANE0 = N_EXPERTS


def _route_topk(lg):
    tm = lg.shape[0]
    lane = lax.broadcasted_iota(I32, (tm, LANE), 1)
    neg_inf = jnp.float32(-jnp.inf)
    big_lane = jnp.int32(4 * LANE)
    is_grp = (lane >= GRP_LANE0) & (lane < GRP_LANE0 + N_GROUPS)
    gl = jnp.where(is_grp, lg, neg_inf)
    gmax = jnp.max(gl, axis=-1, keepdims=True)
    grp = jnp.min(jnp.where(gl == gmax, lane, big_lane), axis=-1, keepdims=True) - GRP_LANE0
    psum = jnp.sum(jnp.where(is_grp, jnp.exp(lg - gmax), 0.0), axis=-1, keepdims=True)
    p_grp = 1.0 / psum
    in_g = (lane < N_EXPERTS) & ((lane // EXPERTS_PER_GROUP) == grp)
    el = jnp.where(in_g, lg, neg_inf)
    v1 = jnp.max(el, axis=-1, keepdims=True)
    i1 = jnp.min(jnp.where(el == v1, lane, big_lane), axis=-1, keepdims=True)
    el2 = jnp.where(lane == i1, neg_inf, el)
    v2 = jnp.max(el2, axis=-1, keepdims=True)
    i2 = jnp.min(jnp.where(el2 == v2, lane, big_lane), axis=-1, keepdims=True)
    tt = jnp.exp(v2 - v1)
    den = 1.0 + tt
    ga = p_grp / den
    gb = p_grp * tt / den

    return lane, i1, i2, ga, gb


def _route_kernel(lg_ref, stri_ref, utri_ref, gs_ref, rt_ref, cnt_ref, run_scr, slab_scr):
    p = pl.program_id(0)
    i = pl.program_id(1)
    tm = ROUTE_TM
    rows = pl.ds(pl.multiple_of(i * tm, tm), tm)
    lane = lax.broadcasted_iota(I32, (tm, LANE), 1)

    @pl.when((p == 0) & (i == 0))
    def _():
        run_scr[...] = jnp.zeros_like(run_scr)

    @pl.when(p == 0)
    def _():
        _, i1, i2, ga, gb = _route_topk(lg_ref[...])
        onehot = jnp.where((lane == i1) | (lane == i2), 1.0, 0.0)
        run_scr[...] = run_scr[...] + jnp.sum(onehot, axis=0, keepdims=True)
        cnt_ref[...] = run_scr[...].astype(I32)
        slab = jnp.where(lane == 0, ga, 0.0)
        slab = jnp.where(lane == 1, gb, slab)
        slab = jnp.where(lane == 2, i1.astype(F32), slab)
        slab = jnp.where(lane == 3, i2.astype(F32), slab)
        slab_scr[rows, :] = slab

    @pl.when((p == 1) & (i == 0))
    def _():
        lane1 = lax.broadcasted_iota(I32, (8, LANE), 1)
        cnt = jnp.broadcast_to(run_scr[...], (8, LANE))
        blocks = jnp.where(lane1 < N_EXPERTS, jnp.floor((cnt + (MOE_BLOCK - 1)) * (1.0 / MOE_BLOCK)), 0.0)
        ends = _dot(blocks.astype(BF16), utri_ref[...])
        run_scr[...] = ((ends - blocks) * MOE_BLOCK)[0:1, :]

    @pl.when(p == 1)
    def _():
        slab = slab_scr[rows, :]
        lane_f = lane.astype(F32)
        sel1 = lane_f == slab[:, 2:3]
        sel2 = lane_f == slab[:, 3:4]
        onehot = jnp.where(sel1 | sel2, 1.0, 0.0)
        pos = _dot(stri_ref[...], onehot.astype(BF16)) + run_scr[...]
        d1 = jnp.sum(jnp.where(sel1, pos, 0.0), axis=-1, keepdims=True)
        d2 = jnp.sum(jnp.where(sel2, pos, 0.0), axis=-1, keepdims=True)
        run_scr[...] = run_scr[...] + jnp.sum(onehot, axis=0, keepdims=True)
        out = jnp.where(lane == 2, d1, jnp.where(lane == 3, d2, slab))
        gs_ref[...] = out
        for q in range(tm // LANE):
            blk_t = out[q * LANE:(q + 1) * LANE, :].T
            rt_ref[:, q * LANE:(q + 1) * LANE] = blk_t[0:8, :].astype(I32)


def _route(logits, stri, utri):
    t = logits.shape[0]
    tm = ROUTE_TM
    nt = t // tm
    return pl.pallas_call(
        _route_kernel,
        grid=(2, nt),
        in_specs=[pl.BlockSpec((tm, LANE), lambda p, i: (i * (1 - p) + (nt - 1) * p, 0)),
                  pl.BlockSpec((tm, tm), lambda p, i: (0, 0)),
                  pl.BlockSpec((LANE, LANE), lambda p, i: (0, 0))],
        out_specs=[pl.BlockSpec((tm, LANE), lambda p, i: (p * i, 0)),
                   pl.BlockSpec((8, tm), lambda p, i: (0, p * i)),
                   pl.BlockSpec((1, LANE), lambda p, i: (0, 0))],
        out_shape=[jax.ShapeDtypeStruct((t, LANE), F32),
                   jax.ShapeDtypeStruct((8, t), I32),
                   jax.ShapeDtypeStruct((1, LANE), I32)],
        scratch_shapes=[pltpu.VMEM((1, LANE), F32), pltpu.VMEM((t, LANE), F32)],
        compiler_params=_cparams(("arbitrary", "arbitrary"), 40),
        name="route",
    )(logits, stri, utri)


MOE_BLOCK = 512
DISPATCH_TM = 1024
COMBINE_TM = 512


ROW_UNROLL = 16


DISPATCH_BUFS = 3


def _dispatch_kernel(d1_ref, d2_ref, h_hbm, xb_in_hbm, xb_hbm, hbuf, load_sem, row_sem):
    del xb_in_hbm
    tm = DISPATCH_TM
    i = pl.program_id(0)
    last = pl.num_programs(0) - 1
    slot = i % DISPATCH_BUFS
    par = i % 2

    def load(tile, s):
        return pltpu.make_async_copy(h_hbm.at[pl.ds(tile * tm, tm)], hbuf.at[s], load_sem.at[s])

    def wait_rows(p):
        for _ in range(2):
            pltpu.make_async_copy(hbuf.at[0], xb_hbm.at[pl.ds(0, tm)], row_sem.at[p]).wait()

    @pl.when(i == 0)
    def _():
        load(0, 0).start()

    @pl.when(i < last)
    def _():
        load(i + 1, (i + 1) % DISPATCH_BUFS).start()

    load(i, slot).wait()

    def body(g, carry):
        for u in range(ROW_UNROLL):
            tt = g * ROW_UNROLL + u
            src = hbuf.at[slot, pl.ds(tt, 1)]
            pltpu.make_async_copy(src, xb_hbm.at[pl.ds(d1_ref[0, 0, tt], 1)], row_sem.at[par]).start()
            pltpu.make_async_copy(src, xb_hbm.at[pl.ds(d2_ref[0, 0, tt], 1)], row_sem.at[par]).start(priority=1)
        return carry

    lax.fori_loop(0, tm // ROW_UNROLL, body, 0)

    @pl.when(i > 0)
    def _():
        wait_rows(1 - par)

    @pl.when(i == last)
    def _():
        wait_rows(par)


def _dispatch(d1, d2, h2p, xb0):
    t = h2p.shape[0]
    tm = DISPATCH_TM
    n_rows, w = xb0.shape
    smem = pl.BlockSpec((1, 1, tm), lambda i: (i, 0, 0), memory_space=pltpu.SMEM)
    anyspec = pl.BlockSpec(memory_space=pl.ANY)
    return pl.pallas_call(
        _dispatch_kernel,
        grid=(t // tm,),
        in_specs=[smem, smem, anyspec, anyspec],
        out_specs=anyspec,
        out_shape=jax.ShapeDtypeStruct((n_rows, w), U32),
        scratch_shapes=[pltpu.VMEM((DISPATCH_BUFS, tm, w), U32),
                        pltpu.SemaphoreType.DMA((DISPATCH_BUFS,)), pltpu.SemaphoreType.DMA((2,))],
        input_output_aliases={3: 0},
        compiler_params=pltpu.CompilerParams(dimension_semantics=("arbitrary",), has_side_effects=True),
        name="moe_dispatch",
    )(d1, d2, h2p, xb0)


def _ffn_kernel(be_ref, nu_ref, seg_ref, nxt_ref, xb_ref, w1_hbm, w3_hbm, w2_hbm, yb_ref,
                w1_f32, w3_f32, w2_f32, w13_scr, w2_scr, sem):
    i = pl.program_id(0)
    e = be_ref[i]
    prev = be_ref[jnp.maximum(i - 1, 0)]
    used = i < nu_ref[0]
    first = used & ((i == 0) | (e != prev))
    slot = seg_ref[i] % 2

    def weight_copies(expert, s):
        return (pltpu.make_async_copy(w1_hbm.at[expert], w1_f32.at[s], sem.at[s]),
                pltpu.make_async_copy(w3_hbm.at[expert], w3_f32.at[s], sem.at[s]),
                pltpu.make_async_copy(w2_hbm.at[expert], w2_f32.at[s], sem.at[s]))

    @pl.when(i == 0)
    def _():
        for cp in weight_copies(e, 0):
            cp.start()

    @pl.when(first)
    def _():
        nxt = nxt_ref[i]

        @pl.when(nxt != e)
        def _():
            for cp in weight_copies(nxt, 1 - slot):
                cp.start()

        for cp in weight_copies(e, slot):
            cp.wait()
        w13_scr[:, :D_EXPERT] = w1_f32[slot].astype(BF16)
        w13_scr[:, D_EXPERT:] = w3_f32[slot].astype(BF16)
        w2_scr[...] = w2_f32[slot].astype(BF16)

    @pl.when(used)
    def _():
        x = jnp.concatenate(_unpack_bf16_pairs(xb_ref[...]), axis=1).astype(BF16)
        h = _dot(x, w13_scr[...])
        h1 = h[:, :D_EXPERT]
        h3 = h[:, D_EXPERT:]
        a = (_silu(h1) * h3).astype(BF16)
        yb_ref[...] = _pack_bf16_pairs(_dot(a, w2_scr[...]))

    @pl.when(jnp.logical_not(used))
    def _():
        yb_ref[...] = jnp.zeros_like(yb_ref)


def _ffn(blk_exp, n_used, seg, nxt, xb, w1, w3, w2):
    n_rows = xb.shape[0]
    nblk = n_rows // MOE_BLOCK
    rowblk = lambda i, be, nu, sg, nx: (i, 0)
    anyspec = pl.BlockSpec(memory_space=pl.ANY)
    grid_spec = pltpu.PrefetchScalarGridSpec(
        num_scalar_prefetch=4, grid=(nblk,),
        in_specs=[pl.BlockSpec((MOE_BLOCK, D_MODEL // 2), rowblk), anyspec, anyspec, anyspec],
        out_specs=pl.BlockSpec((MOE_BLOCK, D_MODEL // 2), rowblk),
        scratch_shapes=[pltpu.VMEM((2, D_MODEL, D_EXPERT), F32),
                        pltpu.VMEM((2, D_MODEL, D_EXPERT), F32),
                        pltpu.VMEM((2, D_EXPERT, D_MODEL), F32),
                        pltpu.VMEM((D_MODEL, 2 * D_EXPERT), BF16),
                        pltpu.VMEM((D_EXPERT, D_MODEL), BF16),
                        pltpu.SemaphoreType.DMA((2,))])
    return pl.pallas_call(
        _ffn_kernel, grid_spec=grid_spec,
        out_shape=jax.ShapeDtypeStruct((n_rows, D_MODEL // 2), U32),
        compiler_params=_cparams(("arbitrary",), 40),
        name="moe_ffn",
    )(blk_exp, n_used, seg, nxt, xb, w1, w3, w2)


def _combine_kernel(d1_ref, d2_ref, d1n_ref, d2n_ref, yb_hbm, gs_ref, x1_ref, g2_ref, o_ref,
                    ya_scr, yb_scr, sem):
    tm = COMBINE_TM
    i = pl.program_id(0)
    slot = i % 2

    def issue(da_ref, db_ref, s):
        def body(g, carry):
            for u in range(ROW_UNROLL):
                tt = g * ROW_UNROLL + u
                pltpu.make_async_copy(yb_hbm.at[pl.ds(da_ref[0, 0, tt], 1)],
                                      ya_scr.at[s, pl.ds(tt, 1)], sem.at[s]).start()
                pltpu.make_async_copy(yb_hbm.at[pl.ds(db_ref[0, 0, tt], 1)],
                                      yb_scr.at[s, pl.ds(tt, 1)], sem.at[s]).start(priority=1)
            return carry

        lax.fori_loop(0, tm // ROW_UNROLL, body, 0)

    @pl.when(i == 0)
    def _():
        issue(d1_ref, d2_ref, 0)

    @pl.when(i + 1 < pl.num_programs(0))
    def _():
        issue(d1n_ref, d2n_ref, 1 - slot)

    pltpu.make_async_copy(yb_hbm.at[pl.ds(0, tm)], ya_scr.at[slot], sem.at[slot]).wait()
    pltpu.make_async_copy(yb_hbm.at[pl.ds(0, tm)], yb_scr.at[slot], sem.at[slot]).wait()
    ga = gs_ref[:, 0:1]
    gb = gs_ref[:, 1:2]
    a_lo, a_hi = _unpack_bf16_pairs(ya_scr[slot])
    b_lo, b_hi = _unpack_bf16_pairs(yb_scr[slot])
    half = D_MODEL // 2
    o_ref[:, :half] = x1_ref[:, :half] + g2_ref[0][:, :half] * (ga * a_lo + gb * b_lo)
    o_ref[:, half:] = x1_ref[:, half:] + g2_ref[0][:, half:] * (ga * a_hi + gb * b_hi)


def _combine(d1, d2, ybuf, gs, x1, mod3, seq_len):
    t = x1.shape[0]
    tm = COMBINE_TM
    nt = t // tm
    tiles_per_seq = seq_len // tm
    smem = pl.BlockSpec((1, 1, tm), lambda i: (i, 0, 0), memory_space=pltpu.SMEM)
    smem_next = pl.BlockSpec((1, 1, tm), lambda i: (jnp.minimum(i + 1, nt - 1), 0, 0), memory_space=pltpu.SMEM)
    return pl.pallas_call(
        _combine_kernel,
        grid=(nt,),
        in_specs=[smem, smem, smem_next, smem_next,
                  pl.BlockSpec(memory_space=pl.ANY),
                  pl.BlockSpec((tm, LANE), lambda i: (i, 0)),
                  pl.BlockSpec((tm, D_MODEL), lambda i: (i, 0)),
                  pl.BlockSpec((1, 1, D_MODEL), lambda i: ((i // tiles_per_seq) * N_MOD + 5, 0, 0))],
        out_specs=pl.BlockSpec((tm, D_MODEL), lambda i: (i, 0)),
        out_shape=jax.ShapeDtypeStruct((t, D_MODEL), F32),
        scratch_shapes=[pltpu.VMEM((2, tm, D_MODEL // 2), U32), pltpu.VMEM((2, tm, D_MODEL // 2), U32),
                        pltpu.SemaphoreType.DMA((2,))],
        compiler_params=_cparams(("arbitrary",), 32),
        name="moe_combine",
    )(d1, d2, d1, d2, ybuf, gs, x1, mod3)


def _rope_tables(seq_len):
    t = np.arange(seq_len)
    row = (t // GRID_W).astype(np.float32)
    colp = (t % GRID_W).astype(np.float32)
    half = SSD_STATE // 2
    inv = (ROPE_THETA ** (-np.arange(0, half, 2, dtype=np.float32) / half)).astype(np.float32)
    ar = (row[:, None] * inv).astype(np.float64)
    ac = (colp[:, None] * inv).astype(np.float64)
    cos_t = np.concatenate([np.cos(ar), np.cos(ar), np.cos(ac), np.cos(ac)], axis=-1)
    sin_t = np.concatenate([-np.sin(ar), np.sin(ar), -np.sin(ac), np.sin(ac)], axis=-1)
    return jnp.asarray(cos_t, F32), jnp.asarray(sin_t, F32)


def _scan_tables(rev):
    li = np.arange(SSD_CHUNK)[:, None]
    ui = np.arange(SSD_CHUNK)[None, :]
    tri = (ui >= li) if rev else (ui <= li)
    col = SSD_HEADS if rev else 0
    j = np.arange(LANE)[:, None]
    c = np.arange(D_SSD)[None, :]
    ee = j == col + c // SSD_HEADDIM
    return jnp.asarray(tri, BF16), jnp.asarray(ee, BF16)


def _pad_lanes(v, width=LANE):
    v = v.reshape(1, -1)
    return jnp.pad(v, ((0, 0), (0, width - v.shape[1])))


def kernel(x, c, ctx, c_ctx, w_mod, b_mod, norm1_w, w_in, conv_w, conv_b, a_log_f, a_log_b, dt_bias_f, dt_bias_b, d_skip, ssd_norm_w, q_norm_w, k_norm_w, rpb, w_br_ssd, w_br_na, w_out, norm2_w, w_grp, b_grp, w_rt, b_rt, w1, w3, w2):
    nb, seq_len, d = x.shape
    ctx_len = ctx.shape[1]
    t = nb * seq_len
    tc = nb * ctx_len
    assert w_mod.shape[0] == 1 and d == D_MODEL and nb <= 7
    assert seq_len % 256 == 0 and ctx_len % SSD_CHUNK == 0 and seq_len // GRID_W >= NA_KH

    cin = jnp.concatenate([c, c_ctx[None, :], jnp.zeros((8 - nb - 1, d), F32)], axis=0)
    mod = _modulation(cin, w_mod[0], b_mod[0])
    mod3 = mod.reshape(8 * N_MOD, 1, D_MODEL)

    o_dt = D_SSD + D_XBC
    o_qkv = o_dt + 2 * SSD_HEADS
    wi = w_in[0].astype(BF16)
    w_a = wi
    w_b = wi[:, o_qkv:]
    w_dtp = jnp.pad(wi[:, o_dt:o_qkv], ((0, 0), (0, LANE - 2 * SSD_HEADS)))
    n1w = norm1_w[0].reshape(1, D_MODEL)

    x2 = x.reshape(t, D_MODEL)
    ctx2 = ctx.reshape(tc, D_MODEL)
    tm_in = 2048 if seq_len % 2048 == 0 else 256
    tiles = seq_len // tm_in
    tn_in = 1024
    z_t = list(range(0, D_SSD // tn_in))
    xbc_t = list(range(D_SSD // tn_in, o_dt // tn_in))
    q_t, k_t, v_t = [0], [1], [2]
    g_t = [3, 4]
    big, dt_raw = _inproj(x2, mod3, lambda i: i // tiles, n1w, w_a, w_b, w_dtp, tm_in, tn_in,
                          xbc_t + z_t, g_t + q_t + k_t + v_t)
    tm_c = 1024 if tc % 1024 == 0 else SSD_CHUNK
    bigc, dtc_raw = _inproj(ctx2, mod3, lambda i: nb, n1w, w_a, w_b, w_dtp, tm_c, tn_in, xbc_t, k_t + v_t)
    ZCOL, GCOL, QCOL, KCOL, VCOL = 2, 3, 8, 9, 10
    KC_COL, VC_COL = 4, 5

    dtbias = _pad_lanes(jnp.concatenate([dt_bias_f[0], dt_bias_b[0]]))
    cw = conv_w[0]
    cbias = conv_b[0].reshape(1, D_XBC)
    cos_t, sin_t = _rope_tables(seq_len)
    xs, bc, dts = _ssd_prep(big, dt_raw, cw, cbias, dtbias, cos_t, sin_t, seq_len, 256)
    ctl = 256 if ctx_len % 256 == 0 else SSD_CHUNK
    ones_t = jnp.ones((ctx_len, LANE), F32)
    zeros_t = jnp.zeros((ctx_len, LANE), F32)
    xsc, bcc, dtsc = _ssd_prep(bigc, dtc_raw, cw, cbias, dtbias, ones_t, zeros_t, ctx_len, ctl)

    alog = _pad_lanes(jnp.concatenate([a_log_f[0], a_log_b[0]]))
    dskip = jnp.repeat(d_skip[0], SSD_HEADDIM).reshape(1, D_SSD)
    tri_f, ee_f = _scan_tables(False)
    tri_b, ee_b = _scan_tables(True)
    hcf = _ssd_state(xsc, bcc, dtsc, alog, tri_f, ee_f, nb, False)
    hcb = _ssd_state(xsc, bcc, dtsc, alog, tri_b, ee_b, nb, True)
    yf, yb = _ssd_scan(xs, bc, dts, alog, dskip, (tri_f, ee_f), (tri_b, ee_b), hcf, hcb, nb)

    qw = jnp.tile(q_norm_w[0], NA_HEADS).reshape(1, D_NA)
    kw = jnp.tile(k_norm_w[0], NA_HEADS).reshape(1, D_NA)
    gi = np.arange(LANE)
    gmat = jnp.asarray(((gi[:, None] // NA_HEADDIM) == (gi[None, :] // NA_HEADDIM)) * (1.0 / NA_HEADDIM), BF16)
    qn, kn = _na_prep(big, QCOL, KCOL, qw, kw, gmat, 512 if t % 512 == 0 else 256)
    kcn = _na_prep(bigc, None, KC_COL, qw, kw, gmat, ctx_len)
    bias_tab = _na_bias_table(rpb[0])
    y_na = _neigh_attention(qn, kn, big, kcn, bigc, bias_tab, nb, seq_len, ctx_len, VCOL, VC_COL)

    w_r = jnp.pad(jnp.concatenate([w_rt[0], w_grp[0]], axis=1), ((0, 0), (0, LANE - N_EXPERTS - N_GROUPS)))
    wrh = w_r.astype(BF16)
    wrl = (w_r - wrh.astype(F32)).astype(BF16)
    br = _pad_lanes(jnp.concatenate([b_rt[0], b_grp[0]]))
    nblk = (2 * t + N_EXPERTS * (MOE_BLOCK - 1) + MOE_BLOCK - 1) // MOE_BLOCK
    x1, h2p, logits, xb_zero = _merge(yf, yb, big, y_na, x2, mod3,
                             ssd_norm_w[0].reshape(1, D_SSD), norm2_w[0].reshape(1, D_MODEL),
                             w_br_ssd[0].astype(BF16), w_br_na[0].astype(BF16), w_out[0].astype(BF16),
                             wrh, wrl, br, seq_len, 512 if seq_len % 512 == 0 else 256, ZCOL, GCOL,
                                      nblk * MOE_BLOCK)

    si = np.arange(ROUTE_TM)
    stri = jnp.asarray(si[None, :] < si[:, None], BF16)
    ui = np.arange(LANE)
    utri = jnp.asarray(ui[:, None] <= ui[None, :], BF16)
    gs, rt, cnt = _route(logits, stri, utri)

    counts = cnt[0, :N_EXPERTS]
    pend = jnp.cumsum((counts + MOE_BLOCK - 1) // MOE_BLOCK * MOE_BLOCK)
    n_used = (pend[-1] // MOE_BLOCK).astype(I32).reshape(1)
    blk_row0 = jnp.arange(nblk, dtype=I32) * MOE_BLOCK
    blk_exp = jnp.minimum(jnp.sum((pend[None, :] <= blk_row0[:, None]).astype(I32), axis=1), N_EXPERTS - 1)
    blk_used = jnp.arange(nblk, dtype=I32) < n_used[0]
    changed = jnp.concatenate([jnp.zeros((1,), I32), (blk_exp[1:] != blk_exp[:-1]).astype(I32)])
    blk_seg = jnp.cumsum(changed).astype(I32)
    eid = jnp.arange(N_EXPERTS, dtype=I32)
    later = (eid[None, :] > eid[:, None]) & (counts[None, :] > 0)
    nxt_e = jnp.min(jnp.where(later, eid[None, :], N_EXPERTS), axis=1)
    nxt_e = jnp.where(nxt_e == N_EXPERTS, eid, nxt_e).astype(I32)
    blk_nxt = jnp.sum(jnp.where(blk_exp[:, None] == eid[None, :], nxt_e[None, :], 0), axis=1).astype(I32)
    blk_nxt = jnp.where(blk_used, blk_nxt, blk_exp)

    def tok_tiles(row, tm):
        return rt[row].reshape(t // tm, 1, tm)

    xb = _dispatch(tok_tiles(2, DISPATCH_TM), tok_tiles(3, DISPATCH_TM), h2p, xb_zero)
    ybuf = _ffn(blk_exp, n_used, blk_seg, blk_nxt, xb, w1[0], w3[0], w2[0])
    out = _combine(tok_tiles(2, COMBINE_TM), tok_tiles(3, COMBINE_TM), ybuf, gs, x1, mod3, seq_len)
    return out.reshape(nb, seq_len, D_MODEL)
```
